```python
import math
import numpy as np
import jax
import jax.numpy as jnp
from jax import lax

D_MODEL = 1024
BATCH = 2
SEQ = 8192
DEPTH = 2

CTX_LEN = 256
GRID_W = 64
N_MOD = 6
BRANCH_WIDTH = 512
N_BRANCHES = 4
S5_GROUP = 16
S5_GROUPS = BRANCH_WIDTH // S5_GROUP
S5_STATE = 64
S5_DT_MIN = 0.001
S5_DT_MAX = 0.1
NA_HEADS = 8
NA_HEAD_DIM = BRANCH_WIDTH // NA_HEADS
NA_ROWS = 8
NA_COLS = 16
ML_HEADS = 4
ML_HEAD_DIM = BRANCH_WIDTH // ML_HEADS
ML_CHUNK = 128
GQ_HEADS = 8
GQ_KV_HEADS = 2
GQ_HEAD_DIM = BRANCH_WIDTH // GQ_HEADS
GQ_KV_WIDTH = GQ_KV_HEADS * GQ_HEAD_DIM
GQ_BLOCK = 128
ROPE_THETA = 10000.0
N_EXPERTS = 32
N_EXPERT_GROUPS = 8
EXPERTS_PER_GROUP = N_EXPERTS // N_EXPERT_GROUPS
TOP_K = 2
D_EXPERT = 512
MOE_BLOCK = 128
EPS = 1e-6
NEG_INIT = -1e30
F32 = jnp.float32
PROJ_SIZES = (BRANCH_WIDTH,) * 8 + (4 * ML_HEADS, BRANCH_WIDTH, GQ_KV_WIDTH, GQ_KV_WIDTH)
PROJ_WIDTH = 9 * BRANCH_WIDTH + 4 * ML_HEADS + 2 * GQ_KV_WIDTH

kernel_name = 'hybrid_prefix_diffusion_block'


def rms_norm(x, g):
    x32 = x.astype(F32)
    y = x32 * lax.rsqrt(jnp.mean(x32 * x32, axis=-1, keepdims=True) + EPS)
    return (y * g.astype(F32)).astype(x.dtype)


def modulate(x, shift, scale):
    return x * (1 + scale) + shift


def flip_seq(t, rev):
    return t[:, ::-1] if rev else t


def split_proj(p):
    idx = np.cumsum(PROJ_SIZES)[:-1].tolist()
    return jnp.split(p, idx, axis=-1)


def dense_attn(q, k, v):
    b, nq, h, dh = q.shape
    hkv = k.shape[2]
    qg = q.reshape(b, nq, hkv, h // hkv, dh)
    s = jnp.einsum('bqkgd,bskd->bkgqs', qg, k).astype(F32) * (dh ** -0.5)
    p = jax.nn.softmax(s, axis=-1)
    o = jnp.einsum('bkgqs,bskd->bqkgd', p, v.astype(F32))
    return o.reshape(b, nq, h * dh).astype(q.dtype)


def rope_1d(x, pos):
    half = x.shape[-1] // 2
    freqs = ROPE_THETA ** (-jnp.arange(half, dtype=F32) / half)
    ang = pos.astype(F32)[:, None] * freqs
    cos = jnp.cos(ang)[:, None, :]
    sin = jnp.sin(ang)[:, None, :]
    x = x.astype(F32)
    x1, x2 = x[..., :half], x[..., half:]
    return jnp.concatenate([x1 * cos - x2 * sin, x1 * sin + x2 * cos], axis=-1)


def rope_2d(x, rows, cols):
    half = x.shape[-1] // 2
    out = jnp.concatenate([rope_1d(x[..., :half], rows), rope_1d(x[..., half:], cols)], axis=-1)
    return out.astype(x.dtype)


def s5_discretize(lam_re, lam_im, log_dt, b_re, b_im):
    lam_re, lam_im = lam_re.astype(F32), lam_im.astype(F32)
    b_re, b_im = b_re.astype(F32), b_im.astype(F32)
    dt = jnp.exp(log_dt.astype(F32))[:, None]
    mag = jnp.exp(lam_re * dt)
    a_re = mag * jnp.cos(lam_im * dt)
    a_im = mag * jnp.sin(lam_im * dt)
    den = lam_re * lam_re + lam_im * lam_im
    nr = a_re - 1.0
    f_re = (nr * lam_re + a_im * lam_im) / den
    f_im = (a_im * lam_re - nr * lam_im) / den
    bb_re = f_re[..., None] * b_re - f_im[..., None] * b_im
    bb_im = f_re[..., None] * b_im + f_im[..., None] * b_re
    return a_re, a_im, bb_re, bb_im


def complex_affine_combine(e1, e2):
    a1r, a1i, b1r, b1i = e1
    a2r, a2i, b2r, b2i = e2
    return (a2r * a1r - a2i * a1i, a2r * a1i + a2i * a1r,
            a2r * b1r - a2i * b1i + b2r, a2r * b1i + a2i * b1r + b2i)


def s5_scan(u, a_re, a_im, bb_re, bb_im, h0_re, h0_im):
    xr = jnp.einsum('gpc,bngc->bngp', bb_re, u)
    xi = jnp.einsum('gpc,bngc->bngp', bb_im, u)
    xr = xr.at[:, 0].add(a_re * h0_re - a_im * h0_im)
    xi = xi.at[:, 0].add(a_re * h0_im + a_im * h0_re)
    n = u.shape[1]
    ar = jnp.broadcast_to(a_re, (1, n) + a_re.shape)
    ai = jnp.broadcast_to(a_im, (1, n) + a_im.shape)
    _, _, hr, hi = lax.associative_scan(complex_affine_combine, (ar, ai, xr, xi), axis=1)
    return hr, hi


def s5_readout(hr, hi, c_re, c_im):
    y = (jnp.einsum('gcp,bngp->bngc', c_re.astype(F32), hr)
         - jnp.einsum('gcp,bngp->bngc', c_im.astype(F32), hi))
    return y.reshape(y.shape[0], y.shape[1], -1)


def s5_glu(y, w_glu, b_glu):
    z = jax.nn.gelu(y)
    return z * jax.nn.sigmoid(z @ w_glu.astype(F32) + b_glu.astype(F32))


def s5_branch(u, uc, lam_re, lam_im, log_dt, b_re, b_im, c_re, c_im, d_skip, w_glu, b_glu, with_ctx):
    b, n_lat, _ = u.shape
    u32, uc32 = u.astype(F32), uc.astype(F32)
    ug = u32.reshape(b, n_lat, S5_GROUPS, S5_GROUP)
    ucg = uc32.reshape(b, uc.shape[1], S5_GROUPS, S5_GROUP)
    zero = jnp.zeros((b, S5_GROUPS, S5_STATE), F32)
    d32 = d_skip.astype(F32)
    y = d32 * u32
    yc = d32 * uc32 if with_ctx else None
    for d in range(2):
        rev = d == 1
        a_re, a_im, bb_re, bb_im = s5_discretize(lam_re[d], lam_im[d], log_dt[d], b_re[d], b_im[d])
        hcr, hci = s5_scan(flip_seq(ucg, rev), a_re, a_im, bb_re, bb_im, zero, zero)
        hr, hi = s5_scan(flip_seq(ug, rev), a_re, a_im, bb_re, bb_im, hcr[:, -1], hci[:, -1])
        y = y + flip_seq(s5_readout(hr, hi, c_re[d], c_im[d]), rev)
        if with_ctx:
            yc = yc + flip_seq(s5_readout(hcr, hci, c_re[d], c_im[d]), rev)
    out = s5_glu(y, w_glu, b_glu).astype(u.dtype)
    out_c = s5_glu(yc, w_glu, b_glu).astype(u.dtype) if with_ctx else None
    return out, out_c


def na_branch(q, k, v, qc, kc, vc, rpb, with_ctx):
    b, n_lat, _ = q.shape
    n_rows = n_lat // GRID_W
    kr = min(NA_ROWS, n_rows)
    shp = (b, n_rows, GRID_W, NA_HEADS, NA_HEAD_DIM)
    qg, kg, vg = q.reshape(shp), k.reshape(shp), v.reshape(shp)
    kcn = kc.reshape(b, kc.shape[1], NA_HEADS, NA_HEAD_DIM)
    vcn = vc.reshape(b, vc.shape[1], NA_HEADS, NA_HEAD_DIM)
    cols = jnp.arange(GRID_W)
    cstart = jnp.clip(cols - NA_COLS // 2, 0, GRID_W - NA_COLS)
    col_ok = (cols[None, :] >= cstart[:, None]) & (cols[None, :] < cstart[:, None] + NA_COLS)
    dc = jnp.clip(cols[None, :] - cols[:, None] + NA_COLS - 1, 0, 2 * NA_COLS - 2)
    rpb_c = rpb.astype(F32)[:, :, dc]
    scale = NA_HEAD_DIM ** -0.5
    n_win = kr * GRID_W

    def one_row(args):
        r, q_r = args
        rs = jnp.clip(r - kr // 2, 0, n_rows - kr)
        k_w = lax.dynamic_slice_in_dim(kg, rs, kr, axis=1)
        v_w = lax.dynamic_slice_in_dim(vg, rs, kr, axis=1)
        dr = rs + jnp.arange(kr) - r + NA_ROWS - 1
        bias = rpb_c[:, dr].transpose(0, 2, 1, 3)
        s_lat = jnp.einsum('bqhd,bkchd->bhqkc', q_r, k_w).astype(F32) * scale + bias[None]
        s_lat = jnp.where(col_ok[:, None, :], s_lat, -jnp.inf)
        s_ctx = jnp.einsum('bqhd,bjhd->bhqj', q_r, kcn).astype(F32) * scale
        s = jnp.concatenate([s_lat.reshape(b, NA_HEADS, GRID_W, n_win), s_ctx], axis=-1)
        p = jax.nn.softmax(s, axis=-1)
        p_lat = p[..., :n_win].reshape(b, NA_HEADS, GRID_W, kr, GRID_W)
        o = (jnp.einsum('bhqkc,bkchd->bqhd', p_lat, v_w.astype(F32))
             + jnp.einsum('bhqj,bjhd->bqhd', p[..., n_win:], vcn.astype(F32)))
        return o.astype(q.dtype)

    ob = lax.map(one_row, (jnp.arange(n_rows), jnp.moveaxis(qg, 1, 0)))
    y = jnp.moveaxis(ob, 0, 1).reshape(b, n_lat, BRANCH_WIDTH)
    if not with_ctx:
        return y, None
    qcn = qc.reshape(b, qc.shape[1], NA_HEADS, NA_HEAD_DIM)
    return y, dense_attn(qcn, kcn, vcn)


def mlstm_chunked(q, k, v, log_i, log_f, state):
    b, n_tok, nh, d = q.shape
    nc = n_tok // ML_CHUNK

    def chunks(t):
        return jnp.moveaxis(t.reshape((b, nc, ML_CHUNK) + t.shape[2:]), 1, 0)

    tri = jnp.tril(jnp.ones((ML_CHUNK, ML_CHUNK), dtype=bool))

    def step(carry, xs):
        c_mat, n_vec, m = carry
        qb, kb, vb, li, lf = xs
        bt = jnp.cumsum(lf, axis=1).transpose(0, 2, 1)
        lt = li.transpose(0, 2, 1)
        dmat = jnp.where(tri, bt[..., :, None] - bt[..., None, :] + lt[..., None, :], -jnp.inf)
        inter = bt + m[..., None]
        mt = jnp.maximum(inter, jnp.max(dmat, axis=-1))
        s = jnp.einsum('bthd,bshd->bhts', qb, kb) * jnp.exp(dmat - mt[..., None])
        w_inter = jnp.exp(inter - mt)
        num = (jnp.einsum('bhts,bshv->bhtv', s, vb)
               + w_inter[..., None] * jnp.einsum('bhvd,bthd->bhtv', c_mat, qb))
        den = jnp.sum(s, axis=-1) + w_inter * jnp.einsum('bhd,bthd->bht', n_vec, qb)
        h_out = num / jnp.maximum(jnp.abs(den), jnp.exp(-mt))[..., None]
        b_last = bt[..., -1]
        g = b_last[..., None] - bt + lt
        m_new = jnp.maximum(b_last + m, jnp.max(g, axis=-1))
        wg = jnp.exp(g - m_new[..., None])
        decay = jnp.exp(b_last + m - m_new)
        c_new = decay[..., None, None] * c_mat + jnp.einsum('bhs,bshv,bshd->bhvd', wg, vb, kb)
        n_new = decay[..., None] * n_vec + jnp.einsum('bhs,bshd->bhd', wg, kb)
        return (c_new, n_new, m_new), h_out.transpose(0, 2, 1, 3)

    final, hs = lax.scan(step, state, tuple(chunks(t) for t in (q, k, v, log_i, log_f)))
    return jnp.moveaxis(hs, 0, 1).reshape(b, n_tok, nh, d), final


def mlstm_branch(q, k, v, o, gt, qc, kc, vc, oc, gtc, b_gates, g_norm, with_ctx):
    def heads(t):
        return t.reshape(t.shape[0], t.shape[1], ML_HEADS, ML_HEAD_DIM).astype(F32)

    def gates(t):
        return (t.astype(F32) + b_gates.astype(F32)).reshape(t.shape[0], t.shape[1], 2, 2, ML_HEADS)

    scale = ML_HEAD_DIM ** -0.5
    lat = (heads(q), heads(k) * scale, heads(v))
    con = (heads(qc), heads(kc) * scale, heads(vc))
    g_lat, g_con = gates(gt), gates(gtc)
    b = q.shape[0]
    h_lat, h_con = [], []
    for d in range(2):
        rev = d == 1
        state0 = (jnp.zeros((b, ML_HEADS, ML_HEAD_DIM, ML_HEAD_DIM), F32),
                  jnp.zeros((b, ML_HEADS, ML_HEAD_DIM), F32),
                  jnp.full((b, ML_HEADS), NEG_INIT, F32))
        hc_d, st = mlstm_chunked(*(flip_seq(t, rev) for t in con),
                                 flip_seq(g_con[:, :, d, 0], rev),
                                 flip_seq(jax.nn.log_sigmoid(g_con[:, :, d, 1]), rev), state0)
        hl_d, _ = mlstm_chunked(*(flip_seq(t, rev) for t in lat),
                                flip_seq(g_lat[:, :, d, 0], rev),
                                flip_seq(jax.nn.log_sigmoid(g_lat[:, :, d, 1]), rev), st)
        h_lat.append(flip_seq(hl_d, rev))
        h_con.append(flip_seq(hc_d, rev))

    def readout(hs, og):
        y = rms_norm(hs[0] + hs[1], g_norm.reshape(ML_HEADS, ML_HEAD_DIM))
        y = y.reshape(y.shape[0], y.shape[1], BRANCH_WIDTH) * jax.nn.sigmoid(og.astype(F32))
        return y.astype(og.dtype)

    y = readout(h_lat, o)
    yc = readout(h_con, oc) if with_ctx else None
    return y, yc


def gqa_branch(q, k, v, qc, kc, vc, g_q, g_k, rows, cols, with_ctx):
    b, n_lat, _ = q.shape

    def split_heads(t, nh):
        return t.reshape(t.shape[0], t.shape[1], nh, GQ_HEAD_DIM)

    ql = rope_2d(rms_norm(split_heads(q, GQ_HEADS), g_q), rows, cols)
    kl = rope_2d(rms_norm(split_heads(k, GQ_KV_HEADS), g_k), rows, cols)
    kcn = rms_norm(split_heads(kc, GQ_KV_HEADS), g_k)
    vcn = split_heads(vc, GQ_KV_HEADS)
    k_all = jnp.concatenate([kl, kcn], axis=1)
    v_all = jnp.concatenate([split_heads(v, GQ_KV_HEADS), vcn], axis=1)
    nb = n_lat // GQ_BLOCK
    qb = jnp.moveaxis(ql.reshape(b, nb, GQ_BLOCK, GQ_HEADS, GQ_HEAD_DIM), 1, 0)
    ob = lax.map(lambda blk: dense_attn(blk, k_all, v_all), qb)
    y = jnp.moveaxis(ob, 0, 1).reshape(b, n_lat, BRANCH_WIDTH)
    if not with_ctx:
        return y, None
    qcn = rms_norm(split_heads(qc, GQ_HEADS), g_q)
    return y, dense_attn(qcn, kcn, vcn)


def merge_branches(h, ys, w_branch, w_gate, b_gate):
    gate = jax.nn.sigmoid((h @ w_gate + b_gate).astype(F32)).astype(h.dtype)
    gate = gate.reshape(h.shape[:-1] + (N_BRANCHES, h.shape[-1]))
    out = gate[..., 0, :] * (ys[0] @ w_branch[0])
    for i in range(1, N_BRANCHES):
        out = out + gate[..., i, :] * (ys[i] @ w_branch[i])
    return out


def mixer_layer(h, hc, rows, cols, w_in, lam_re, lam_im, log_dt, b_re, b_im, c_re, c_im, d_skip,
                w_glu, b_glu, rpb, ml_bg, ml_g, gq_gq, gq_gk, w_branch, w_gate, b_gate, w_out, with_ctx):
    (s5_u, na_q, na_k, na_v, ml_q, ml_k, ml_v, ml_o, ml_gt, gq_q, gq_k, gq_v) = split_proj(h @ w_in)
    (s5_uc, na_qc, na_kc, na_vc, ml_qc, ml_kc, ml_vc, ml_oc, ml_gtc, gq_qc, gq_kc, gq_vc) = split_proj(hc @ w_in)
    ya, ya_c = s5_branch(s5_u, s5_uc, lam_re, lam_im, log_dt, b_re, b_im, c_re, c_im, d_skip,
                         w_glu, b_glu, with_ctx)
    yb, yb_c = na_branch(na_q, na_k, na_v, na_qc, na_kc, na_vc, rpb, with_ctx)
    ym, ym_c = mlstm_branch(ml_q, ml_k, ml_v, ml_o, ml_gt, ml_qc, ml_kc, ml_vc, ml_oc, ml_gtc,
                            ml_bg, ml_g, with_ctx)
    yd, yd_c = gqa_branch(gq_q, gq_k, gq_v, gq_qc, gq_kc, gq_vc, gq_gq, gq_gk, rows, cols, with_ctx)
    y = merge_branches(h, (ya, yb, ym, yd), w_branch, w_gate, b_gate) @ w_out
    if not with_ctx:
        return y, None
    y_ctx = merge_branches(hc, (ya_c, yb_c, ym_c, yd_c), w_branch, w_gate, b_gate) @ w_out
    return y, y_ctx


def moe_ffn(h, w_router, b_router, w1, w3, w2):
    n_tok, d_model = h.shape
    s = jax.nn.sigmoid((h @ w_router).astype(F32))
    sel = s + b_router.astype(F32)
    sel_g = sel.reshape(n_tok, N_EXPERT_GROUPS, EXPERTS_PER_GROUP)
    grp_score = jnp.sum(lax.top_k(sel_g, TOP_K)[0], axis=-1)
    grp = jnp.argmax(grp_score, axis=-1)
    in_grp = jnp.take_along_axis(sel_g, grp[:, None, None], axis=1)[:, 0]
    _, local = lax.top_k(in_grp, TOP_K)
    expert = grp[:, None] * EXPERTS_PER_GROUP + local
    wts = jnp.take_along_axis(s, expert, axis=1)
    wts = wts / jnp.sum(wts, axis=-1, keepdims=True)
    n_asg = n_tok * TOP_K
    e_flat = expert.reshape(-1)
    tok = jnp.repeat(jnp.arange(n_tok), TOP_K)
    order = jnp.argsort(e_flat)
    e_s, tok_s, w_s = e_flat[order], tok[order], wts.reshape(-1)[order]
    counts = jnp.bincount(e_flat, length=N_EXPERTS)
    padded = (counts + MOE_BLOCK - 1) // MOE_BLOCK * MOE_BLOCK
    starts = jnp.cumsum(counts) - counts
    pend = jnp.cumsum(padded)
    pstarts = pend - padded
    dest = pstarts[e_s] + jnp.arange(n_asg) - starts[e_s]
    n_blocks = -(-n_asg // MOE_BLOCK) + N_EXPERTS
    buf = jnp.zeros((n_blocks * MOE_BLOCK, d_model), h.dtype).at[dest].set(h[tok_s])
    blk_expert = jnp.minimum(jnp.searchsorted(pend, jnp.arange(n_blocks) * MOE_BLOCK, side='right'),
                             N_EXPERTS - 1)

    def expert_block(args):
        xb, e = args
        return (jax.nn.silu(xb @ w1[e]) * (xb @ w3[e])) @ w2[e]

    yb = lax.map(expert_block, (buf.reshape(n_blocks, MOE_BLOCK, d_model), blk_expert))
    y_s = yb.reshape(-1, d_model)[dest]
    return jnp.zeros((n_tok, d_model), h.dtype).at[tok_s].add(w_s[:, None].astype(h.dtype) * y_s)


def setup_inputs(seed: int = 0) -> dict:
    key = jax.random.key(seed)
    ks = jax.random.split(key, 40)
    dm, bw, gs, ps = D_MODEL, BRANCH_WIDTH, S5_GROUPS, S5_STATE

    def nrm(i, shape, scale):
        return jax.random.normal(ks[i], shape, F32) * scale

    i_bias = nrm(20, (DEPTH, 2, ML_HEADS), 0.1)
    f_bias = jnp.linspace(3.0, 6.0, ML_HEADS, dtype=F32) + nrm(21, (DEPTH, 2, ML_HEADS), 0.1)
    return {
        'x': nrm(0, (BATCH, SEQ, dm), 1.0),
        'c': nrm(1, (BATCH, dm), 1.0),
        'ctx': nrm(2, (BATCH, CTX_LEN, dm), 1.0),
        'c_ctx': nrm(3, (dm,), 1.0),
        'w_mod': nrm(4, (DEPTH, dm, N_MOD * dm), 0.5 * dm ** -0.5),
        'b_mod': nrm(5, (DEPTH, N_MOD * dm), 0.01),
        'g_norm1': 1.0 + nrm(6, (DEPTH, dm), 0.02),
        'g_norm2': 1.0 + nrm(7, (DEPTH, dm), 0.02),
        'w_in': nrm(8, (DEPTH, dm, PROJ_WIDTH), dm ** -0.5),
        's5_lam_re': -0.5 + nrm(9, (DEPTH, 2, gs, ps), 0.01),
        's5_lam_im': math.pi * jnp.arange(ps, dtype=F32) + nrm(10, (DEPTH, 2, gs, ps), 0.01),
        's5_log_dt': jax.random.uniform(ks[11], (DEPTH, 2, gs), F32,
                                        math.log(S5_DT_MIN), math.log(S5_DT_MAX)),
        's5_b_re': nrm(12, (DEPTH, 2, gs, ps, S5_GROUP), (2 * S5_GROUP) ** -0.5),
        's5_b_im': nrm(13, (DEPTH, 2, gs, ps, S5_GROUP), (2 * S5_GROUP) ** -0.5),
        's5_c_re': nrm(14, (DEPTH, 2, gs, S5_GROUP, ps), ps ** -0.5),
        's5_c_im': nrm(15, (DEPTH, 2, gs, S5_GROUP, ps), ps ** -0.5),
        's5_d': nrm(16, (DEPTH, bw), 1.0),
        's5_w_glu': nrm(17, (DEPTH, bw, bw), bw ** -0.5),
        's5_b_glu': nrm(18, (DEPTH, bw), 0.01),
        'na_rpb': nrm(19, (DEPTH, NA_HEADS, 2 * NA_ROWS - 1, 2 * NA_COLS - 1), 0.1),
        'ml_b_gates': jnp.stack([i_bias, f_bias], axis=2).reshape(DEPTH, 4 * ML_HEADS),
        'ml_norm': 1.0 + nrm(22, (DEPTH, bw), 0.02),
        'gq_qnorm': 1.0 + nrm(23, (DEPTH, GQ_HEAD_DIM), 0.02),
        'gq_knorm': 1.0 + nrm(24, (DEPTH, GQ_HEAD_DIM), 0.02),
        'w_branch': nrm(25, (DEPTH, N_BRANCHES, bw, dm), bw ** -0.5),
        'w_gate': nrm(26, (DEPTH, dm, N_BRANCHES * dm), dm ** -0.5),
        'b_gate': nrm(27, (DEPTH, N_BRANCHES * dm), 0.01),
        'w_out': nrm(28, (DEPTH, dm, dm), dm ** -0.5),
        'w_router': nrm(29, (dm, N_EXPERTS), dm ** -0.5),
        'b_router': nrm(30, (N_EXPERTS,), 0.01),
        'moe_w1': nrm(31, (DEPTH, N_EXPERTS, dm, D_EXPERT), dm ** -0.5),
        'moe_w3': nrm(32, (DEPTH, N_EXPERTS, dm, D_EXPERT), dm ** -0.5),
        'moe_w2': nrm(33, (DEPTH, N_EXPERTS, D_EXPERT, dm), D_EXPERT ** -0.5),
        'g_final': 1.0 + nrm(34, (dm,), 0.02),
    }


def reference(x, c, ctx, c_ctx, w_mod, b_mod, g_norm1, g_norm2, w_in,
              s5_lam_re, s5_lam_im, s5_log_dt, s5_b_re, s5_b_im, s5_c_re, s5_c_im, s5_d, s5_w_glu, s5_b_glu,
              na_rpb, ml_b_gates, ml_norm, gq_qnorm, gq_knorm,
              w_branch, w_gate, b_gate, w_out, w_router, b_router, moe_w1, moe_w3, moe_w2, g_final):
    b, n_lat, dm = x.shape
    n_ctx = ctx.shape[1]
    t = jnp.arange(n_lat)
    rows = t // GRID_W
    cols = t % GRID_W
    for l in range(DEPTH):
        last = l == DEPTH - 1
        mod = (jax.nn.silu(c) @ w_mod[l] + b_mod[l]).reshape(b, N_MOD, 1, dm)
        modc = (jax.nn.silu(c_ctx) @ w_mod[l] + b_mod[l]).reshape(N_MOD, dm)
        h = modulate(rms_norm(x, g_norm1[l]), mod[:, 0], mod[:, 1])
        hc = modulate(rms_norm(ctx, g_norm1[l]), modc[0], modc[1])
        y, y_ctx = mixer_layer(h, hc, rows, cols, w_in[l],
                               s5_lam_re[l], s5_lam_im[l], s5_log_dt[l], s5_b_re[l], s5_b_im[l],
                               s5_c_re[l], s5_c_im[l], s5_d[l], s5_w_glu[l], s5_b_glu[l],
                               na_rpb[l], ml_b_gates[l], ml_norm[l], gq_qnorm[l], gq_knorm[l],
                               w_branch[l], w_gate[l], b_gate[l], w_out[l], not last)
        x = x + mod[:, 2] * y
        h2 = modulate(rms_norm(x, g_norm2[l]), mod[:, 3], mod[:, 4])
        if last:
            f = moe_ffn(h2.reshape(-1, dm), w_router, b_router, moe_w1[l], moe_w3[l], moe_w2[l])
            x = x + mod[:, 5] * f.reshape(b, n_lat, dm)
        else:
            ctx = ctx + modc[2] * y_ctx
            h2c = modulate(rms_norm(ctx, g_norm2[l]), modc[3], modc[4])
            f_all = moe_ffn(jnp.concatenate([h2.reshape(-1, dm), h2c.reshape(-1, dm)], axis=0),
                            w_router, b_router, moe_w1[l], moe_w3[l], moe_w2[l])
            x = x + mod[:, 5] * f_all[:b * n_lat].reshape(b, n_lat, dm)
            ctx = ctx + modc[5] * f_all[b * n_lat:].reshape(b, n_ctx, dm)
    return rms_norm(x, g_final)
```

```python
import functools
import math

import numpy as np
import jax
import jax.numpy as jnp
from jax import lax
from jax.experimental import pallas as pl
from jax.experimental.pallas import tpu as pltpu

F32 = jnp.float32
BF16 = jnp.bfloat16
I32 = jnp.int32

GRID_W = 64
N_MOD = 6
BRANCH_WIDTH = 512
N_BRANCHES = 4
S5_GROUP = 16
S5_GROUPS = BRANCH_WIDTH // S5_GROUP
S5_STATE = 64
NA_HEADS = 8
NA_HEAD_DIM = 64
NA_ROWS = 8
NA_COLS = 16
ML_HEADS = 4
ML_HEAD_DIM = 128
ML_CHUNK = 128
GQ_HEADS = 8
GQ_KV_HEADS = 2
GQ_HEAD_DIM = 64
ROPE_THETA = 10000.0
N_EXPERTS = 32
N_EXPERT_GROUPS = 8
EXPERTS_PER_GROUP = 4
TOP_K = 2
D_EXPERT = 512
MOE_BLOCK = 128
EPS = 1e-6
NEG_INIT = -1e30
MASK_NEG = -1e30

LANES = 128
ROW_TILE = 256
S5_CHUNK = 32
NA_TILE_ROWS = 4
VMEM_LIMIT = 56 * 1024 * 1024

HIGHEST = lax.Precision.HIGHEST


def _cp(sem, vmem=None):
    return pltpu.CompilerParams(dimension_semantics=sem, vmem_limit_bytes=vmem)


def _dot(a, b):
    return jnp.dot(a, b, preferred_element_type=F32)


def _dot_nt(a, b):
    return lax.dot_general(a, b, (((1,), (1,)), ((), ())), preferred_element_type=F32)


def _dot_tn(a, b):
    return lax.dot_general(a, b, (((0,), (0,)), ((), ())), preferred_element_type=F32)


def _rms_mod(x, g, shift, scale):
    ms = jnp.mean(x * x, axis=-1, keepdims=True)
    y = x * lax.rsqrt(ms + EPS) * g
    return y * (1.0 + scale) + shift


def _mod_body(c_ref, w_ref, b_ref, o_ref):
    c = c_ref[...]
    a = (c * jax.nn.sigmoid(c)).astype(BF16)
    o_ref[...] = _dot(a, w_ref[...].astype(BF16)) + b_ref[...]


def mod_vectors(c_all, w_mod, b_mod):
    d = c_all.shape[1]
    return pl.pallas_call(
        _mod_body,
        grid=(N_MOD,),
        in_specs=[pl.BlockSpec((8, d), lambda j: (0, 0)),
                  pl.BlockSpec((d, d), lambda j: (0, j)),
                  pl.BlockSpec((1, d), lambda j: (0, j))],
        out_specs=pl.BlockSpec((8, d), lambda j: (0, j)),
        out_shape=jax.ShapeDtypeStruct((8, N_MOD * d), F32),
        compiler_params=_cp(("arbitrary",)),
        name="mod_vectors",
    )(c_all, w_mod, b_mod.reshape(1, -1))


def _inproj_body(x_ref, g_ref, mod_ref, wa_ref, wb_ref, wg_ref, bg_ref, oa_ref, ob_ref, og_ref):
    mod = mod_ref[0]
    h = _rms_mod(x_ref[...], g_ref[...], mod[0:1], mod[1:2]).astype(BF16)
    oa_ref[...] = _dot(h, wa_ref[...])
    ob_ref[...] = _dot(h, wb_ref[...]).astype(BF16)
    og_ref[...] = jax.nn.sigmoid(_dot(h, wg_ref[...]) + bg_ref[...]).astype(BF16)


def in_projection(x_all, g, modtab, wa, wb, wg, bg, n_lat_rows, n_batch):
    r, d = x_all.shape
    tm = ROW_TILE
    per_b = n_lat_rows // n_batch // tm

    def mod_idx(i):
        return (jnp.minimum(i // per_b, n_batch), 0, 0)

    const = lambda i: (0, 0)
    return pl.pallas_call(
        _inproj_body,
        grid=(r // tm,),
        in_specs=[pl.BlockSpec((tm, d), lambda i: (i, 0)),
                  pl.BlockSpec((1, d), const),
                  pl.BlockSpec((1, N_MOD, d), mod_idx),
                  pl.BlockSpec(wa.shape, const, pipeline_mode=pl.Buffered(1)),
                  pl.BlockSpec(wb.shape, const, pipeline_mode=pl.Buffered(1)),
                  pl.BlockSpec(wg.shape, const, pipeline_mode=pl.Buffered(1)),
                  pl.BlockSpec((1, wg.shape[1]), const)],
        out_specs=[pl.BlockSpec((tm, wa.shape[1]), lambda i: (i, 0)),
                   pl.BlockSpec((tm, wb.shape[1]), lambda i: (i, 0)),
                   pl.BlockSpec((tm, wg.shape[1]), lambda i: (i, 0))],
        out_shape=[jax.ShapeDtypeStruct((r, wa.shape[1]), F32),
                   jax.ShapeDtypeStruct((r, wb.shape[1]), BF16),
                   jax.ShapeDtypeStruct((r, wg.shape[1]), BF16)],
        compiler_params=_cp(("arbitrary",), VMEM_LIMIT),
        name="in_projection",
    )(x_all, g.reshape(1, d), modtab, wa, wb, wg, bg.reshape(1, -1))


def s5_tables(lam_re, lam_im, log_dt, b_re, b_im, c_re, c_im, d_skip):
    ell, g_n, p_n, c_n = S5_CHUNK, S5_GROUPS, S5_STATE, S5_GROUP
    lam_re, lam_im = lam_re.astype(F32), lam_im.astype(F32)
    dt = jnp.exp(log_dt.astype(F32))[..., None]
    mag = jnp.exp(lam_re * dt)
    a_re = mag * jnp.cos(lam_im * dt)
    a_im = mag * jnp.sin(lam_im * dt)
    den = lam_re * lam_re + lam_im * lam_im
    nr = a_re - 1.0
    f_re = (nr * lam_re + a_im * lam_im) / den
    f_im = (a_im * lam_re - nr * lam_im) / den
    bb_re = f_re[..., None] * b_re - f_im[..., None] * b_im
    bb_im = f_re[..., None] * b_im + f_im[..., None] * b_re
    k = jnp.arange(ell + 1, dtype=F32)[:, None, None, None]
    pmag = jnp.exp(k * (lam_re * dt)[None])
    ang = k * (lam_im * dt)[None]
    pr, pi = pmag * jnp.cos(ang), pmag * jnp.sin(ang)
    ab_re = pr[..., None] * bb_re[None] - pi[..., None] * bb_im[None]
    ab_im = pr[..., None] * bb_im[None] + pi[..., None] * bb_re[None]
    c_re, c_im = c_re.astype(F32), c_im.astype(F32)
    kk = (jnp.einsum('dgcp,kdgpe->dgkce', c_re, ab_re, precision=HIGHEST)
          - jnp.einsum('dgcp,kdgpe->dgkce', c_im, ab_im, precision=HIGHEST))
    s_i = np.arange(ell)[:, None]
    t_i = np.arange(ell)[None, :]
    t0 = jnp.where((t_i >= s_i)[None, :, :, None, None], kk[0][:, np.clip(t_i - s_i, 0, ell)], 0.0)
    t1 = jnp.where((s_i >= t_i)[None, :, :, None, None], kk[1][:, np.clip(s_i - t_i, 0, ell)], 0.0)
    eye_c = jnp.eye(c_n, dtype=F32)
    skip = (jnp.asarray(s_i == t_i, F32)[None, :, :, None, None]
            * (d_skip.astype(F32).reshape(g_n, 1, 1, c_n, 1) * eye_c[None, None, None]))
    tsum = (t0 + t1 + skip).transpose(0, 1, 4, 2, 3).reshape(g_n, ell * c_n, ell * c_n)

    parity = jnp.asarray(np.arange(g_n)[:, None] % 2 == np.arange(2)[None, :], F32)
    k_end = [np.arange(ell)[::-1].copy(), np.arange(ell)]
    cols = []
    for d in range(2):
        for part in (ab_re, ab_im):
            cols.append(part[k_end[d], d].transpose(1, 0, 3, 2).reshape(g_n, ell * c_n, p_n))
    x = jnp.stack(cols, axis=2)
    mend = (x[:, :, :, None, :] * parity[:, None, None, :, None]).reshape(g_n, ell * c_n, 8 * p_n)

    k_out = [np.arange(ell) + 1, ell - np.arange(ell)]
    rows = []
    for d in range(2):
        prd, pid = pr[k_out[d], d], pi[k_out[d], d]
        w_re = (c_re[d][None] * prd[:, :, None, :] - c_im[d][None] * pid[:, :, None, :])
        w_im = (-c_re[d][None] * pid[:, :, None, :] - c_im[d][None] * prd[:, :, None, :])
        rows += [w_re.transpose(1, 3, 0, 2).reshape(g_n, p_n, ell * c_n),
                 w_im.transpose(1, 3, 0, 2).reshape(g_n, p_n, ell * c_n)]
    w = jnp.stack(rows, axis=1)
    wst = (w[:, :, None] * parity[:, None, :, None, None]).reshape(g_n, 8 * p_n, ell * c_n)

    al = jnp.stack([pr[ell, 0], pi[ell, 0], pr[ell, 1], pi[ell, 1]], axis=1)
    a_chunk = al.reshape(g_n // 2, 2, 4, p_n).transpose(0, 2, 1, 3).reshape(g_n // 2, 8 * p_n)
    return tsum.astype(BF16), mend.astype(BF16), wst.astype(BF16), a_chunk


def _s5_end_body(u_ref, m_ref, o_ref):
    o_ref[...] = _dot(u_ref[0], m_ref[0]) + _dot(u_ref[1], m_ref[1])


def _s5_scan_body(n_ctx_chunks, e_ref, a_ref, o_ref):
    n_chunks = e_ref.shape[0]
    a = a_ref[...]
    a0r, a0i, a1r, a1i = (a[:, j * LANES:(j + 1) * LANES] for j in range(4))

    def step(i, carry):
        s0r, s0i, s1r, s1i = carry
        r0 = i
        r1 = jnp.where(i < n_ctx_chunks, n_ctx_chunks - 1 - i, n_chunks + n_ctx_chunks - 1 - i)
        o_ref[r0, :, 0 * LANES:1 * LANES] = s0r
        o_ref[r0, :, 1 * LANES:2 * LANES] = s0i
        o_ref[r1, :, 2 * LANES:3 * LANES] = s1r
        o_ref[r1, :, 3 * LANES:4 * LANES] = s1i
        e0r = e_ref[r0, :, 0 * LANES:1 * LANES]
        e0i = e_ref[r0, :, 1 * LANES:2 * LANES]
        e1r = e_ref[r1, :, 2 * LANES:3 * LANES]
        e1i = e_ref[r1, :, 3 * LANES:4 * LANES]
        return (a0r * s0r - a0i * s0i + e0r, a0r * s0i + a0i * s0r + e0i,
                a1r * s1r - a1i * s1i + e1r, a1r * s1i + a1i * s1r + e1i)

    z = jnp.zeros(a0r.shape, F32)
    lax.fori_loop(0, n_chunks, step, (z, z, z, z))


def _s5_out_body(u_ref, t_ref, s_ref, w_ref, o_ref):
    o_ref[0] = _dot(u_ref[0], t_ref[0]) + _dot(s_ref[...].astype(BF16), w_ref[0])


def s5_mixer(u_seq, tables, n_ctx):
    tsum, mend, wst, a_chunk = tables
    b, s, _ = u_seq.shape
    ell, g_n, c_n = S5_CHUNK, S5_GROUPS, S5_GROUP
    n_chunks = s // ell
    r5 = b * n_chunks
    width = ell * c_n
    u_g = (u_seq.astype(BF16).reshape(b, n_chunks, ell, g_n, c_n)
           .transpose(3, 0, 1, 2, 4).reshape(g_n, r5, width))
    ends = pl.pallas_call(
        _s5_end_body,
        grid=(g_n // 2,),
        in_specs=[pl.BlockSpec((2, r5, width), lambda p: (p, 0, 0)),
                  pl.BlockSpec((2, width, width), lambda p: (p, 0, 0))],
        out_specs=pl.BlockSpec((r5, width), lambda p: (0, p)),
        out_shape=jax.ShapeDtypeStruct((r5, g_n // 2 * width), F32),
        compiler_params=_cp(("arbitrary",)),
        name="s5_chunk_ends",
    )(u_g, mend)
    half = g_n // 4
    states = pl.pallas_call(
        functools.partial(_s5_scan_body, n_ctx // ell),
        grid=(b, 2),
        in_specs=[pl.BlockSpec((n_chunks, half, width), lambda i, j: (i, j, 0)),
                  pl.BlockSpec((half, width), lambda i, j: (j, 0))],
        out_specs=pl.BlockSpec((n_chunks, half, width), lambda i, j: (i, j, 0)),
        out_shape=jax.ShapeDtypeStruct((r5, g_n // 2, width), F32),
        compiler_params=_cp(("arbitrary", "arbitrary"), VMEM_LIMIT),
        name="s5_state_scan",
    )(ends.reshape(r5, g_n // 2, width), a_chunk)
    y_g = pl.pallas_call(
        _s5_out_body,
        grid=(g_n,),
        in_specs=[pl.BlockSpec((1, r5, width), lambda g: (g, 0, 0)),
                  pl.BlockSpec((1, width, width), lambda g: (g, 0, 0)),
                  pl.BlockSpec((r5, width), lambda g: (0, g // 2)),
                  pl.BlockSpec((1, width, width), lambda g: (g, 0, 0))],
        out_specs=pl.BlockSpec((1, r5, width), lambda g: (g, 0, 0)),
        out_shape=jax.ShapeDtypeStruct((g_n, r5, width), F32),
        compiler_params=_cp(("arbitrary",)),
        name="s5_outputs",
    )(u_g, tsum, states.reshape(r5, g_n // 2 * width), wst)
    return (y_g.reshape(g_n, b, n_chunks, ell, c_n).transpose(1, 2, 3, 0, 4).reshape(b, s, g_n * c_n))


def rope_tables(n_lat):
    half = GQ_HEAD_DIM // 2
    quarter = half // 2
    t = np.arange(n_lat)
    freqs = ROPE_THETA ** (-np.arange(quarter, dtype=np.float64) / quarter)
    ang_r = (t // GRID_W)[:, None] * freqs
    ang_c = (t % GRID_W)[:, None] * freqs
    ang = np.concatenate([ang_r, ang_r, ang_c, ang_c], axis=1)
    sign = np.concatenate([-np.ones(quarter), np.ones(quarter)] * 2)
    cos = np.concatenate([np.cos(ang), np.ones((ROW_TILE, GQ_HEAD_DIM))], axis=0)
    sin = np.concatenate([np.sin(ang) * sign, np.zeros((ROW_TILE, GQ_HEAD_DIM))], axis=0)
    tab = np.concatenate([cos, cos, sin, sin], axis=1)
    return jnp.asarray(tab, F32)


def _group_ones(width, group):
    i = np.arange(width)
    return jnp.asarray((i[:, None] // group) == (i[None, :] // group), BF16)


def _group_mean_sq(x, ones_blk, group):
    sq = x * x
    hi = sq.astype(BF16)
    lo = (sq - hi.astype(F32)).astype(BF16)
    return (_dot(hi, ones_blk) + _dot(lo, ones_blk)) * (1.0 / group)


def _rope(x, cos, sin):
    w = x.shape[-1]
    q = GQ_HEAD_DIM // 4
    lane = lax.broadcasted_iota(I32, x.shape, 1)
    first = (lane % (2 * q)) < q
    partner = jnp.where(first, pltpu.roll(x, w - q, 1), pltpu.roll(x, q, 1))
    return x * cos + partner * sin


def _gq_prep_body(q_ref, k_ref, v_ref, cs_ref, gq_ref, gk_ref, oq_ref, ok_ref, qm_ref, kr_ref, va_ref):
    cs = cs_ref[...]
    cos1, sin1 = cs[:, :LANES], cs[:, LANES:]
    q = q_ref[...].astype(F32)
    qn = q * lax.rsqrt(_group_mean_sq(q, oq_ref[...], GQ_HEAD_DIM) + EPS) * gq_ref[...]
    n_pairs = q.shape[1] // LANES
    qr = _rope(qn, jnp.concatenate([cos1] * n_pairs, axis=1), jnp.concatenate([sin1] * n_pairs, axis=1))
    qr = (qr * (GQ_HEAD_DIM ** -0.5)).astype(BF16)
    lane = lax.broadcasted_iota(I32, (q.shape[0], LANES), 1)
    heads_per_kv = GQ_HEADS // GQ_KV_HEADS
    for h in range(GQ_HEADS):
        pair = qr[:, (h // 2) * LANES:(h // 2 + 1) * LANES]
        kv = h // heads_per_kv
        if h % 2 != kv:
            pair = pltpu.roll(pair, GQ_HEAD_DIM, 1)
        keep = (lane >= kv * GQ_HEAD_DIM) & (lane < (kv + 1) * GQ_HEAD_DIM)
        qm_ref[h] = jnp.where(keep, pair, jnp.zeros_like(pair))
    k = k_ref[...].astype(F32)
    kn = k * lax.rsqrt(_group_mean_sq(k, ok_ref[...], GQ_HEAD_DIM) + EPS) * gk_ref[...]
    kr_ref[...] = _rope(kn, cos1, sin1).astype(BF16)
    va_ref[...] = jnp.concatenate([v_ref[...], jnp.ones(v_ref.shape, BF16)], axis=1)


def gq_prepare(pb, col_q, col_k, col_v, cs_tab, g_q, g_k, n_lat, n_ctx, n_batch):
    r = pb.shape[0]
    tm = ROW_TILE
    assert n_ctx == tm and n_lat % tm == 0
    nb = n_lat // tm
    n_lat_tiles = n_batch * nb

    def tab_idx(i):
        return (jnp.where(i < n_lat_tiles, i % nb, nb), 0)

    def kv_idx(i):
        lat = (i // nb) * (nb + 1) + i % nb
        ctx = (i - n_lat_tiles) * (nb + 1) + nb
        return (jnp.where(i < n_lat_tiles, lat, ctx), 0)

    qw = GQ_HEADS * GQ_HEAD_DIM
    const = lambda i: (0, 0)
    gq = jnp.tile(g_q.astype(F32), GQ_HEADS).reshape(1, qw)
    gk = jnp.tile(g_k.astype(F32), GQ_KV_HEADS).reshape(1, LANES)
    n_keys = n_batch * (n_lat + n_ctx)
    return pl.pallas_call(
        _gq_prep_body,
        grid=(r // tm,),
        in_specs=[pl.BlockSpec((tm, qw), lambda i: (i, col_q // qw)),
                  pl.BlockSpec((tm, LANES), lambda i: (i, col_k // LANES)),
                  pl.BlockSpec((tm, LANES), lambda i: (i, col_v // LANES)),
                  pl.BlockSpec((tm, 2 * LANES), tab_idx),
                  pl.BlockSpec((1, qw), const),
                  pl.BlockSpec((1, LANES), const),
                  pl.BlockSpec((qw, qw), const),
                  pl.BlockSpec((LANES, LANES), const)],
        out_specs=[pl.BlockSpec((GQ_HEADS, tm, LANES), lambda i: (0, i, 0)),
                   pl.BlockSpec((tm, LANES), kv_idx),
                   pl.BlockSpec((tm, 2 * LANES), kv_idx)],
        out_shape=[jax.ShapeDtypeStruct((GQ_HEADS, r, LANES), BF16),
                   jax.ShapeDtypeStruct((n_keys, LANES), BF16),
                   jax.ShapeDtypeStruct((n_keys, 2 * LANES), BF16)],
        compiler_params=_cp(("arbitrary",)),
        name="gq_prepare",
    )(pb, pb, pb, cs_tab, gq, gk, _group_ones(qw, GQ_HEAD_DIM), _group_ones(LANES, GQ_HEAD_DIM))


def _gq_flash_body(q_ref, k_ref, v_ref, o_ref, m_sc, acc_sc):
    kj = pl.program_id(2)
    n_h, tq, _ = q_ref.shape

    @pl.when(kj == 0)
    def _():
        m_sc[...] = jnp.full(m_sc.shape, -jnp.inf, F32)
        acc_sc[...] = jnp.zeros(acc_sc.shape, F32)

    q = q_ref[...].reshape(n_h * tq, LANES)
    s = _dot_nt(q, k_ref[...])
    m_prev = m_sc[...]
    m_new = jnp.maximum(m_prev, jnp.max(s, axis=-1, keepdims=True))
    p = jnp.exp(s - m_new)
    acc_sc[...] = jnp.exp(m_prev - m_new) * acc_sc[...] + _dot(p.astype(BF16), v_ref[...])
    m_sc[...] = m_new

    @pl.when(kj == pl.num_programs(2) - 1)
    def _():
        lane = lax.broadcasted_iota(I32, (tq, LANES), 1)
        heads_per_kv = n_h // GQ_KV_HEADS
        for j in range(n_h // 2):
            kv = (2 * j) // heads_per_kv
            halves = []
            for h in (2 * j, 2 * j + 1):
                a = acc_sc[h * tq:(h + 1) * tq, :]
                halves.append(a[:, :LANES] / a[:, LANES:LANES + 1])
            lo, hi = halves
            if kv == 0:
                hi = pltpu.roll(hi, GQ_HEAD_DIM, 1)
            else:
                lo = pltpu.roll(lo, GQ_HEAD_DIM, 1)
            o_ref[:, j * LANES:(j + 1) * LANES] = jnp.where(lane < GQ_HEAD_DIM, lo, hi).astype(o_ref.dtype)


def gq_attention(qm, keys, vals, n_rows_out, tq, tk, q_blk, k_blk, n_q, n_k, n_batch):
    n_h = qm.shape[0]
    return pl.pallas_call(
        _gq_flash_body,
        grid=(n_batch, n_q, n_k),
        in_specs=[pl.BlockSpec((n_h, tq, LANES), lambda b, i, j: (0, q_blk(b, i), 0)),
                  pl.BlockSpec((tk, LANES), lambda b, i, j: (k_blk(b, j), 0)),
                  pl.BlockSpec((tk, 2 * LANES), lambda b, i, j: (k_blk(b, j), 0))],
        out_specs=pl.BlockSpec((tq, n_h * GQ_HEAD_DIM), lambda b, i, j: (b * n_q + i, 0)),
        out_shape=jax.ShapeDtypeStruct((n_rows_out, n_h * GQ_HEAD_DIM), BF16),
        scratch_shapes=[pltpu.VMEM((n_h * tq, 1), F32), pltpu.VMEM((n_h * tq, 2 * LANES), F32)],
        compiler_params=_cp(("arbitrary", "arbitrary", "arbitrary"), VMEM_LIMIT),
        name="gq_attention",
    )(qm, keys, vals)


def _largest_divisor(n, cap):
    return max(d for d in range(1, cap + 1) if n % d == 0)


def gq_mixer(pb, col_q, col_k, col_v, cs_tab, g_q, g_k, n_lat, n_ctx, n_batch, with_ctx):
    qm, keys, vals = gq_prepare(pb, col_q, col_k, col_v, cs_tab, g_q, g_k, n_lat, n_ctx, n_batch)
    tq = 128
    kb = (n_lat + n_ctx) // ROW_TILE
    tk = ROW_TILE * _largest_divisor(kb, 3)
    n_q = n_lat // tq
    per_b = (n_lat + n_ctx) // tk
    y_lat = gq_attention(qm, keys, vals, n_batch * n_lat, tq, tk,
                         lambda b, i: b * n_q + i, lambda b, j: b * per_b + j, n_q, per_b, n_batch)
    if not with_ctx:
        return y_lat
    tc = n_ctx
    y_ctx = gq_attention(qm, keys, vals, n_batch * n_ctx, tc, tc,
                         lambda b, i: n_batch * n_lat // tc + b,
                         lambda b, j: b * ((n_lat + n_ctx) // tc) + n_lat // tc, 1, 1, n_batch)
    return jnp.concatenate([y_lat, y_ctx], axis=0)


def na_bias_tables(rpb, n_img_rows, n_ctx):
    tr = NA_TILE_ROWS
    nt = n_img_rows // tr
    assert nt >= 4
    kr = min(NA_ROWS, n_img_rows)
    classes = []
    for i in (0, 1, nt - 1):
        wb = int(np.clip(i - 1, 0, nt - 3))
        qrow = (i * tr + np.arange(tr))[:, None, None, None]
        qcol = np.arange(GRID_W)[None, :, None, None]
        krow = (wb * tr + np.arange(3 * tr))[None, None, :, None]
        kcol = np.arange(GRID_W)[None, None, None, :]
        rs = np.clip(qrow - kr // 2, 0, n_img_rows - kr)
        cstart = np.clip(qcol - NA_COLS // 2, 0, GRID_W - NA_COLS)
        ok = (krow >= rs) & (krow < rs + kr) & (kcol >= cstart) & (kcol < cstart + NA_COLS)
        dr = np.clip(krow - qrow + NA_ROWS - 1, 0, 2 * NA_ROWS - 2) + 0 * kcol + 0 * qcol
        dc = np.clip(kcol - qcol + NA_COLS - 1, 0, 2 * NA_COLS - 2) + 0 * krow + 0 * qrow
        shape = (tr * GRID_W, 3 * tr * GRID_W)
        ok = np.broadcast_to(ok, (tr, GRID_W, 3 * tr, GRID_W)).reshape(shape)
        bias = rpb.astype(F32)[:, dr.reshape(shape), dc.reshape(shape)]
        bias = jnp.where(ok[None], bias, MASK_NEG)
        classes.append(jnp.concatenate([bias, jnp.zeros((rpb.shape[0], shape[0], n_ctx), F32)], axis=-1))
    return jnp.stack(classes, axis=0)


def _pair_attention(q_pair, k_pair, v_pair, bias_fn):
    lane = lax.broadcasted_iota(I32, q_pair.shape, 1)
    out = None
    for hh in range(2):
        mine = (lane >= hh * NA_HEAD_DIM) & (lane < (hh + 1) * NA_HEAD_DIM)
        qm = jnp.where(mine, q_pair, jnp.zeros_like(q_pair))
        s = _dot_nt(qm, k_pair)
        b = bias_fn(hh)
        if b is not None:
            s = s + b
        m = jnp.max(s, axis=-1, keepdims=True)
        p = jnp.exp(s - m)
        l = jnp.sum(p, axis=-1, keepdims=True)
        o = _dot(p.astype(BF16), v_pair) / l
        out = o if out is None else jnp.where(mine, o, out)
    return out


def _na_body(q_ref, k0_ref, k1_ref, k2_ref, kc_ref, v0_ref, v1_ref, v2_ref, vc_ref, b_ref, o_ref):
    scale = NA_HEAD_DIM ** -0.5
    for j in range(NA_HEADS // 2):
        sl = slice(j * LANES, (j + 1) * LANES)
        q_pair = q_ref[:, sl] * scale
        k_pair = jnp.concatenate([k0_ref[:, sl], k1_ref[:, sl], k2_ref[:, sl], kc_ref[:, sl]], axis=0)
        v_pair = jnp.concatenate([v0_ref[:, sl], v1_ref[:, sl], v2_ref[:, sl], vc_ref[:, sl]], axis=0)
        o = _pair_attention(q_pair, k_pair, v_pair, lambda hh: b_ref[0, 2 * j + hh])
        o_ref[:, sl] = o.astype(o_ref.dtype)


def na_mixer(pb, col_q, col_k, col_v, bias_tab, n_lat, n_ctx, n_batch):
    tm = NA_TILE_ROWS * GRID_W
    assert n_ctx == tm
    w = NA_HEADS * NA_HEAD_DIM
    nt = n_lat // tm
    n_keys = 3 * tm + n_ctx
    cq, ck, cv = col_q // w, col_k // w, col_v // w
    ctx0 = n_batch * nt

    def win(o):
        return lambda b, i: (b * nt + jnp.clip(i - 1, 0, nt - 3) + o)

    def cls(b, i):
        return (jnp.where(i == 0, 0, jnp.where(i == nt - 1, 2, 1)), 0, 0, 0)

    kspecs = [pl.BlockSpec((tm, w), (lambda b, i, f=win(o): (f(b, i), ck))) for o in range(3)]
    vspecs = [pl.BlockSpec((tm, w), (lambda b, i, f=win(o): (f(b, i), cv))) for o in range(3)]
    return pl.pallas_call(
        _na_body,
        grid=(n_batch, nt),
        in_specs=[pl.BlockSpec((tm, w), lambda b, i: (b * nt + i, cq))] + kspecs
        + [pl.BlockSpec((tm, w), lambda b, i: (ctx0 + b, ck))] + vspecs
        + [pl.BlockSpec((tm, w), lambda b, i: (ctx0 + b, cv)),
           pl.BlockSpec((1, NA_HEADS, tm, n_keys), cls)],
        out_specs=pl.BlockSpec((tm, w), lambda b, i: (b * nt + i, 0)),
        out_shape=jax.ShapeDtypeStruct((n_batch * n_lat, w), BF16),
        compiler_params=_cp(("arbitrary", "arbitrary"), VMEM_LIMIT),
        name="na_attention",
    )(pb, pb, pb, pb, pb, pb, pb, pb, pb, bias_tab)


def _mlstm_chunk(rev, q, k, v, g, lf_cum, g_t, lf_cum_t, ci, cf, c_ref, n_ref, m_ref, idx):
    tok = lax.broadcasted_iota(I32, (ML_CHUNK, ML_CHUNK), 0)
    src = lax.broadcasted_iota(I32, (ML_CHUNK, ML_CHUNK), 1)
    causal = (src >= tok) if rev else (src <= tok)
    bt_col = lf_cum[:, cf:cf + 1]
    bt_row = lf_cum_t[cf:cf + 1, :]
    li_col = g[:, ci:ci + 1]
    li_row = g_t[ci:ci + 1, :]
    m_prev = m_ref[idx][:, 0:1]
    dmat = jnp.where(causal, bt_col - bt_row + li_row, -jnp.inf)
    inter = bt_col + m_prev
    mt = jnp.maximum(inter, jnp.max(dmat, axis=-1, keepdims=True))
    s = _dot_nt(q, k) * jnp.exp(dmat - mt)
    w_inter = jnp.exp(inter - mt)
    c_prev = c_ref[idx]
    n_prev = n_ref[idx]
    num = _dot(s.astype(BF16), v) + w_inter * _dot_nt(q, c_prev.astype(BF16))
    qn = jnp.sum(q.astype(F32) * n_prev, axis=-1, keepdims=True)
    den = jnp.sum(s, axis=-1, keepdims=True) + w_inter * qn
    h_out = num / jnp.maximum(jnp.abs(den), jnp.exp(-mt))
    b_last = bt_col[0:1, :] if rev else bt_col[ML_CHUNK - 1:ML_CHUNK, :]
    g_col = b_last - bt_col + li_col
    m_new = jnp.maximum(b_last + m_prev, jnp.max(g_col, axis=0, keepdims=True))
    wg = jnp.exp(g_col - m_new)
    decay = jnp.exp(b_last + m_prev - m_new)
    c_ref[idx] = decay * c_prev + _dot_tn((wg * v.astype(F32)).astype(BF16), k)
    n_ref[idx] = decay * n_prev + jnp.sum(wg * k.astype(F32), axis=0, keepdims=True)
    m_ref[idx] = jnp.broadcast_to(m_new, (1, LANES))
    return h_out


def _mlstm_body(qf_ref, kf_ref, vf_ref, gf_ref, qb_ref, kb_ref, vb_ref, gb_ref, bias_ref,
                hf_ref, hb_ref, c_ref, n_ref, m_ref):
    @pl.when(pl.program_id(1) == 0)
    def _():
        c_ref[...] = jnp.zeros(c_ref.shape, F32)
        n_ref[...] = jnp.zeros(n_ref.shape, F32)
        m_ref[...] = jnp.full(m_ref.shape, NEG_INIT, F32)

    tok = lax.broadcasted_iota(I32, (ML_CHUNK, ML_CHUNK), 0)
    src = lax.broadcasted_iota(I32, (ML_CHUNK, ML_CHUNK), 1)
    for d, (q_ref, k_ref, v_ref, g_ref, h_ref) in enumerate(
            ((qf_ref, kf_ref, vf_ref, gf_ref, hf_ref), (qb_ref, kb_ref, vb_ref, gb_ref, hb_ref))):
        rev = d == 1
        g = g_ref[...] + bias_ref[...]
        tri = ((src >= tok) if rev else (src <= tok)).astype(F32)
        lf_cum = jnp.dot(tri, jax.nn.log_sigmoid(g), precision=HIGHEST, preferred_element_type=F32)
        g_t = g.T
        lf_cum_t = lf_cum.T
        for h in range(ML_HEADS):
            sl = slice(h * ML_HEAD_DIM, (h + 1) * ML_HEAD_DIM)
            h_out = _mlstm_chunk(rev, q_ref[:, sl], k_ref[:, sl], v_ref[:, sl], g, lf_cum, g_t, lf_cum_t,
                                 d * 2 * ML_HEADS + h, d * 2 * ML_HEADS + ML_HEADS + h,
                                 c_ref, n_ref, m_ref, d * ML_HEADS + h)
            h_ref[:, sl] = h_out


def mlstm_mixer(pb, pa, col_q, col_k, col_v, col_g, bias, n_lat, n_ctx, n_batch):
    r = pb.shape[0]
    w = ML_HEADS * ML_HEAD_DIM
    tc = ML_CHUNK
    nl, nc = n_lat // tc, n_ctx // tc

    def fwd(b, i):
        return jnp.where(i < nc, n_batch * nl + b * nc + i, b * nl + i - nc)

    def bwd(b, i):
        return jnp.where(i < nc, n_batch * nl + b * nc + nc - 1 - i, b * nl + nl - 1 - (i - nc))

    def specs(f):
        return [pl.BlockSpec((tc, w), lambda b, i, c=c: (f(b, i), c // w)) for c in (col_q, col_k, col_v)] + [
            pl.BlockSpec((tc, LANES), lambda b, i: (f(b, i), col_g // LANES))]

    n_st = 2 * ML_HEADS
    return pl.pallas_call(
        _mlstm_body,
        grid=(n_batch, nl + nc),
        in_specs=specs(fwd) + specs(bwd) + [pl.BlockSpec((1, LANES), lambda b, i: (0, 0))],
        out_specs=[pl.BlockSpec((tc, w), lambda b, i: (fwd(b, i), 0)),
                   pl.BlockSpec((tc, w), lambda b, i: (bwd(b, i), 0))],
        out_shape=[jax.ShapeDtypeStruct((r, w), F32), jax.ShapeDtypeStruct((r, w), F32)],
        scratch_shapes=[pltpu.VMEM((n_st, ML_HEAD_DIM, ML_HEAD_DIM), F32),
                        pltpu.VMEM((n_st, 1, ML_HEAD_DIM), F32),
                        pltpu.VMEM((n_st, 1, LANES), F32)],
        compiler_params=_cp(("arbitrary", "arbitrary")),
        name="mlstm_chunks",
    )(pb, pb, pb, pa, pb, pb, pb, pa, bias)


def _merge_body(y5_ref, yb_ref, hf_ref, hb_ref, o_ref, yd_ref, gate_ref, x_ref, mod_ref,
                wglu_ref, bglu_ref, mlg_ref, wbr_ref, wout_ref, g2_ref, wr_ref,
                xo_ref, h2_ref, st_ref):
    d = x_ref.shape[1]
    z = jax.nn.gelu(y5_ref[...])
    ya = z * jax.nn.sigmoid(_dot(z.astype(BF16), wglu_ref[...]) + bglu_ref[...])
    hs = hf_ref[...] + hb_ref[...]
    segs = []
    for h in range(ML_HEADS):
        seg = hs[:, h * ML_HEAD_DIM:(h + 1) * ML_HEAD_DIM]
        segs.append(seg * lax.rsqrt(jnp.mean(seg * seg, axis=-1, keepdims=True) + EPS))
    ym = jnp.concatenate(segs, axis=1) * mlg_ref[...] * jax.nn.sigmoid(o_ref[...].astype(F32))
    ys = (ya.astype(BF16), yb_ref[...], ym.astype(BF16), yd_ref[...])
    merged = None
    for i in range(N_BRANCHES):
        term = gate_ref[:, i * d:(i + 1) * d].astype(F32) * _dot(ys[i], wbr_ref[i])
        merged = term if merged is None else merged + term
    y = _dot(merged.astype(BF16), wout_ref[...])
    mod = mod_ref[0]
    x_new = x_ref[...] + mod[2:3] * y
    xo_ref[...] = x_new
    h2 = _rms_mod(x_new, g2_ref[...], mod[3:4], mod[4:5])
    h2_ref[...] = h2
    st_ref[...] = jax.nn.sigmoid(_dot_nt(wr_ref[...], h2.astype(BF16)))


def merge_layer(y5, yb, hf, hb, pb, col_o, yd, gate, x_all, modtab, wglu, bglu, mlg, wbr, wout, g2, wr_t,
                n_rows, n_lat_rows, n_batch):
    d = x_all.shape[1]
    tm = ROW_TILE
    w = BRANCH_WIDTH
    per_b = n_lat_rows // n_batch // tm
    row = lambda i: (i, 0)
    const2 = lambda i: (0, 0)
    return pl.pallas_call(
        _merge_body,
        grid=(n_rows // tm,),
        in_specs=[pl.BlockSpec((tm, w), row), pl.BlockSpec((tm, w), row), pl.BlockSpec((tm, w), row),
                  pl.BlockSpec((tm, w), row), pl.BlockSpec((tm, w), lambda i: (i, col_o // w)),
                  pl.BlockSpec((tm, w), row), pl.BlockSpec((tm, N_BRANCHES * d), row),
                  pl.BlockSpec((tm, d), row),
                  pl.BlockSpec((1, N_MOD, d), lambda i: (jnp.minimum(i // per_b, n_batch), 0, 0)),
                  pl.BlockSpec((w, w), const2), pl.BlockSpec((1, w), const2), pl.BlockSpec((1, w), const2),
                  pl.BlockSpec((N_BRANCHES, w, d), lambda i: (0, 0, 0)),
                  pl.BlockSpec((d, d), const2), pl.BlockSpec((1, d), const2),
                  pl.BlockSpec((LANES, d), const2)],
        out_specs=[pl.BlockSpec((tm, d), row), pl.BlockSpec((tm, d), row),
                   pl.BlockSpec((LANES, tm), lambda i: (0, i))],
        out_shape=[jax.ShapeDtypeStruct((n_rows, d), F32), jax.ShapeDtypeStruct((n_rows, d), F32),
                   jax.ShapeDtypeStruct((LANES, n_rows), F32)],
        compiler_params=_cp(("arbitrary",), VMEM_LIMIT),
        name="merge_layer",
    )(y5, yb, hf, hb, pb, yd, gate, x_all, modtab, wglu, bglu, mlg, wbr, wout, g2, wr_t)


def _router_body(s_ref, b_ref, e_ref, w_ref, rank_ref, cnt_ref, base_sc):
    @pl.when(pl.program_id(0) == 0)
    def _():
        base_sc[...] = jnp.zeros(base_sc.shape, F32)

    tm = s_ref.shape[1]
    s = s_ref[0:N_EXPERTS, :]
    sel = s + b_ref[0:N_EXPERTS, :]
    row = lambda a, e: a[e:e + 1, :]
    best, grp = None, None
    for g in range(N_EXPERT_GROUPS):
        v = [row(sel, EXPERTS_PER_GROUP * g + k) for k in range(EXPERTS_PER_GROUP)]
        gs = None
        for a in range(EXPERTS_PER_GROUP):
            for c in range(a + 1, EXPERTS_PER_GROUP):
                gs = v[a] + v[c] if gs is None else jnp.maximum(gs, v[a] + v[c])
        if best is None:
            best, grp = gs, jnp.zeros((1, tm), I32)
        else:
            better = gs > best
            grp = jnp.where(better, g, grp)
            best = jnp.where(better, gs, best)
    vals, affs = [], []
    for k in range(EXPERTS_PER_GROUP):
        vk, sk = row(sel, k), row(s, k)
        for g in range(1, N_EXPERT_GROUPS):
            hit = grp == g
            vk = jnp.where(hit, row(sel, EXPERTS_PER_GROUP * g + k), vk)
            sk = jnp.where(hit, row(s, EXPERTS_PER_GROUP * g + k), sk)
        vals.append(vk)
        affs.append(sk)
    i1, b1, w1 = jnp.zeros((1, tm), I32), vals[0], affs[0]
    for k in range(1, EXPERTS_PER_GROUP):
        better = vals[k] > b1
        i1 = jnp.where(better, k, i1)
        w1 = jnp.where(better, affs[k], w1)
        b1 = jnp.where(better, vals[k], b1)
    i2 = jnp.zeros((1, tm), I32)
    b2 = jnp.full((1, tm), -jnp.inf, F32)
    w2 = jnp.zeros((1, tm), F32)
    for k in range(EXPERTS_PER_GROUP):
        cand = (i1 != k) & (vals[k] > b2)
        i2 = jnp.where(cand, k, i2)
        w2 = jnp.where(cand, affs[k], w2)
        b2 = jnp.where(cand, vals[k], b2)
    e1 = grp * EXPERTS_PER_GROUP + i1
    e2 = grp * EXPERTS_PER_GROUP + i2
    tot = w1 + w2
    e_ref[...] = jnp.concatenate([e1, e2], axis=0)
    wpad = jnp.concatenate([w1 / tot, w2 / tot, jnp.zeros((6, tm), F32)], axis=0)
    w_ref[...] = wpad.T
    ids = lax.broadcasted_iota(I32, (N_EXPERTS, tm), 0)
    oh1 = (ids == e1).astype(F32)
    oh2 = (ids == e2).astype(F32)
    oh = oh1 + oh2
    before = (lax.broadcasted_iota(I32, (tm, tm), 0) < lax.broadcasted_iota(I32, (tm, tm), 1)).astype(BF16)
    prior = _dot(oh.astype(BF16), before) + base_sc[...]
    r1 = jnp.sum(oh1 * prior, axis=0, keepdims=True)
    r2 = jnp.sum(oh2 * prior, axis=0, keepdims=True)
    rank_ref[...] = jnp.concatenate([r1, r2], axis=0).astype(I32)
    base = base_sc[...] + jnp.sum(oh, axis=1, keepdims=True)
    base_sc[...] = base
    cnt_ref[...] = jnp.broadcast_to(base, cnt_ref.shape).astype(I32)


def route(s_t, b_router):
    n_rows = s_t.shape[1]
    tm = ROW_TILE
    b_col = jnp.pad(b_router.astype(F32), (0, LANES - N_EXPERTS)).reshape(LANES, 1)
    return pl.pallas_call(
        _router_body,
        grid=(n_rows // tm,),
        in_specs=[pl.BlockSpec((LANES, tm), lambda i: (0, i)), pl.BlockSpec((LANES, 1), lambda i: (0, 0))],
        out_specs=[pl.BlockSpec((TOP_K, tm), lambda i: (0, i)), pl.BlockSpec((tm, 8), lambda i: (i, 0)),
                   pl.BlockSpec((TOP_K, tm), lambda i: (0, i)), pl.BlockSpec((N_EXPERTS, LANES), lambda i: (0, 0))],
        out_shape=[jax.ShapeDtypeStruct((TOP_K, n_rows), I32), jax.ShapeDtypeStruct((n_rows, 8), F32),
                   jax.ShapeDtypeStruct((TOP_K, n_rows), I32), jax.ShapeDtypeStruct((N_EXPERTS, LANES), I32)],
        scratch_shapes=[pltpu.VMEM((N_EXPERTS, 1), F32)],
        compiler_params=_cp(("arbitrary",)),
        name="moe_router",
    )(s_t, b_col)


def _row_copy(src_ref, src_row, dst_ref, dst_row, sem):
    return pltpu.make_async_copy(src_ref.at[pl.ds(src_row, 1), :], dst_ref.at[pl.ds(dst_row, 1), :], sem)


def _dispatch_body(dest_ref, h_ref, buf_in_ref, buf_ref, sem):
    del buf_in_ref
    tm = h_ref.shape[0]

    def issue(t, carry):
        for k in range(TOP_K):
            _row_copy(h_ref, t, buf_ref, dest_ref[0, k, t], sem).start()
        return carry

    def drain(t, carry):
        for k in range(TOP_K):
            _row_copy(h_ref, 0, buf_ref, 0, sem).wait()
        return carry

    lax.fori_loop(0, tm, issue, 0)
    lax.fori_loop(0, tm, drain, 0)


def moe_dispatch(h2, dest3, buf0):
    n_rows, d = h2.shape
    tm = ROW_TILE
    return pl.pallas_call(
        _dispatch_body,
        grid=(n_rows // tm,),
        in_specs=[pl.BlockSpec((1, TOP_K, tm), lambda i: (i, 0, 0), memory_space=pltpu.SMEM),
                  pl.BlockSpec((tm, d), lambda i: (i, 0)),
                  pl.BlockSpec(memory_space=pl.ANY)],
        out_specs=pl.BlockSpec(memory_space=pl.ANY),
        out_shape=jax.ShapeDtypeStruct(buf0.shape, buf0.dtype),
        scratch_shapes=[pltpu.SemaphoreType.DMA(())],
        input_output_aliases={2: 0},
        compiler_params=_cp(("arbitrary",)),
        name="moe_dispatch",
    )(dest3, h2, buf0)


def _expert_body(be_ref, x_ref, w1_ref, w3_ref, w2_ref, o_ref, w1_sc, w3_sc, w2_sc):
    i = pl.program_id(0)
    changed = jnp.logical_or(i == 0, be_ref[i] != be_ref[jnp.maximum(i - 1, 0)])

    @pl.when(changed)
    def _():
        w1_sc[...] = w1_ref[0].astype(BF16)
        w3_sc[...] = w3_ref[0].astype(BF16)
        w2_sc[...] = w2_ref[0].astype(BF16)

    x = x_ref[...].astype(BF16)
    a = _dot(x, w1_sc[...])
    mid = (a * jax.nn.sigmoid(a)) * _dot(x, w3_sc[...])
    o_ref[...] = _dot(mid.astype(BF16), w2_sc[...])


def moe_experts(buf, blk_expert, w1, w3, w2, blk):
    n_slots, d = buf.shape
    de = w1.shape[2]
    grid_spec = pltpu.PrefetchScalarGridSpec(
        num_scalar_prefetch=1,
        grid=(n_slots // blk,),
        in_specs=[pl.BlockSpec((blk, d), lambda i, be: (i, 0)),
                  pl.BlockSpec((1, d, de), lambda i, be: (be[i], 0, 0)),
                  pl.BlockSpec((1, d, de), lambda i, be: (be[i], 0, 0)),
                  pl.BlockSpec((1, de, d), lambda i, be: (be[i], 0, 0))],
        out_specs=pl.BlockSpec((blk, d), lambda i, be: (i, 0)),
        scratch_shapes=[pltpu.VMEM((d, de), BF16), pltpu.VMEM((d, de), BF16), pltpu.VMEM((de, d), BF16)],
    )
    return pl.pallas_call(
        _expert_body,
        grid_spec=grid_spec,
        out_shape=jax.ShapeDtypeStruct((n_slots, d), F32),
        compiler_params=_cp(("arbitrary",), VMEM_LIMIT),
        name="moe_experts",
    )(blk_expert, buf, w1, w3, w2)


def _combine_body(final, dest_ref, x_ref, w_ref, mod_ref, gf_ref, y_hbm, o_ref, y0_sc, y1_sc, sem):
    tm = x_ref.shape[0]
    bufs = (y0_sc, y1_sc)

    def issue(t, carry):
        for k in range(TOP_K):
            _row_copy(y_hbm, dest_ref[0, k, t], bufs[k], t, sem).start()
        return carry

    def drain(t, carry):
        for k in range(TOP_K):
            _row_copy(y_hbm, 0, bufs[k], 0, sem).wait()
        return carry

    lax.fori_loop(0, tm, issue, 0)
    lax.fori_loop(0, tm, drain, 0)
    w = w_ref[...]
    f = w[:, 0:1] * y0_sc[...] + w[:, 1:2] * y1_sc[...]
    x_new = x_ref[...] + mod_ref[0][5:6] * f
    if final:
        ms = jnp.mean(x_new * x_new, axis=-1, keepdims=True)
        x_new = x_new * lax.rsqrt(ms + EPS) * gf_ref[...]
    o_ref[...] = x_new


def moe_combine(dest3, x_rows, wts, modtab, g_final, y_slots, n_lat_rows, n_batch, final):
    n_rows, d = x_rows.shape
    tm = ROW_TILE
    per_b = n_lat_rows // n_batch // tm
    return pl.pallas_call(
        functools.partial(_combine_body, final),
        grid=(n_rows // tm,),
        in_specs=[pl.BlockSpec((1, TOP_K, tm), lambda i: (i, 0, 0), memory_space=pltpu.SMEM),
                  pl.BlockSpec((tm, d), lambda i: (i, 0)),
                  pl.BlockSpec((tm, 8), lambda i: (i, 0)),
                  pl.BlockSpec((1, N_MOD, d), lambda i: (jnp.minimum(i // per_b, n_batch), 0, 0)),
                  pl.BlockSpec((1, d), lambda i: (0, 0)),
                  pl.BlockSpec(memory_space=pl.ANY)],
        out_specs=pl.BlockSpec((tm, d), lambda i: (i, 0)),
        out_shape=jax.ShapeDtypeStruct((n_rows, d), F32),
        scratch_shapes=[pltpu.VMEM((tm, d), F32), pltpu.VMEM((tm, d), F32), pltpu.SemaphoreType.DMA(())],
        compiler_params=_cp(("arbitrary",)),
        name="moe_combine",
    )(dest3, x_rows, wts, modtab, g_final.reshape(1, d), y_slots)


def moe_layer(x_rows, h2, s_t, b_router, w1, w3, w2, modtab, g_final, n_lat_rows, n_batch, final):
    n_rows, d = h2.shape
    blk = 2 * MOE_BLOCK
    experts, wts, rank, counts = route(s_t, b_router)
    cnt = counts[:, 0]
    padded = (cnt + blk - 1) // blk * blk
    pend = jnp.cumsum(padded)
    pstart = pend - padded
    dest = pstart[experts] + rank
    n_blocks = -(-(n_rows * TOP_K) // blk) + N_EXPERTS
    blk_expert = jnp.minimum(jnp.searchsorted(pend, jnp.arange(n_blocks, dtype=I32) * blk, side='right'),
                             N_EXPERTS - 1).astype(I32)
    dest3 = dest.reshape(TOP_K, n_rows // ROW_TILE, ROW_TILE).transpose(1, 0, 2)
    buf = moe_dispatch(h2, dest3, jnp.zeros((n_blocks * blk, d), F32))
    y_slots = moe_experts(buf, blk_expert, w1, w3, w2, blk)
    return moe_combine(dest3, x_rows, wts, modtab, g_final, y_slots, n_lat_rows, n_batch, final)


def _ctx_mha_body(q_ref, k_ref, v_ref, o_ref):
    scale = NA_HEAD_DIM ** -0.5
    for j in range(NA_HEADS // 2):
        sl = slice(j * LANES, (j + 1) * LANES)
        o = _pair_attention(q_ref[:, sl] * scale, k_ref[:, sl], v_ref[:, sl], lambda hh: None)
        o_ref[:, sl] = o.astype(o_ref.dtype)


def na_ctx_attention(pb, col_q, col_k, col_v, n_lat, n_ctx, n_batch):
    w = NA_HEADS * NA_HEAD_DIM
    ctx0 = n_batch * n_lat // n_ctx
    spec = lambda c: pl.BlockSpec((n_ctx, w), lambda b: (ctx0 + b, c // w))
    return pl.pallas_call(
        _ctx_mha_body,
        grid=(n_batch,),
        in_specs=[spec(col_q), spec(col_k), spec(col_v)],
        out_specs=pl.BlockSpec((n_ctx, w), lambda b: (b, 0)),
        out_shape=jax.ShapeDtypeStruct((n_batch * n_ctx, w), BF16),
        compiler_params=_cp(("arbitrary",)),
        name="na_ctx_attention",
    )(pb, pb, pb)


_COL = dict(na_q=0, na_k=512, na_v=1024, ml_q=1536, ml_k=2048, ml_v=2560, ml_o=3072,
            gq_q=3584, gq_k=4096, gq_v=4224)
_COL_GATES = BRANCH_WIDTH


def _split_w_in(w_in):
    sizes = (BRANCH_WIDTH,) * 8 + (4 * ML_HEADS, BRANCH_WIDTH, GQ_KV_HEADS * GQ_HEAD_DIM, GQ_KV_HEADS * GQ_HEAD_DIM)
    idx = np.cumsum(sizes)[:-1].tolist()
    (s5_u, na_q, na_k, na_v, ml_q, ml_k, ml_v, ml_o, ml_gt, gq_q, gq_k, gq_v) = jnp.split(w_in, idx, axis=-1)
    wa = jnp.concatenate([s5_u, jnp.pad(ml_gt, ((0, 0), (0, LANES - 4 * ML_HEADS)))], axis=1)
    wb = jnp.concatenate([na_q, na_k, na_v, ml_q, ml_k * (ML_HEAD_DIM ** -0.5), ml_v, ml_o, gq_q, gq_k, gq_v], axis=1)
    return wa.astype(BF16), wb.astype(BF16)


def kernel(x, c, ctx, c_ctx, w_mod, b_mod, g_norm1, g_norm2, w_in, s5_lam_re, s5_lam_im, s5_log_dt, s5_b_re,
           s5_b_im, s5_c_re, s5_c_im, s5_d, s5_w_glu, s5_b_glu, na_rpb, ml_b_gates, ml_norm, gq_qnorm, gq_knorm,
           w_branch, w_gate, b_gate, w_out, w_router, b_router, moe_w1, moe_w3, moe_w2, g_final):
    b, n_lat, dm = x.shape
    n_ctx = ctx.shape[1]
    depth = w_in.shape[0]
    bn, bc = b * n_lat, b * n_ctx
    x_all = jnp.concatenate([x.reshape(bn, dm), ctx.reshape(bc, dm)], axis=0).astype(F32)
    c_all = jnp.concatenate([c.astype(F32), c_ctx.astype(F32)[None], jnp.zeros((8 - b - 1, dm), F32)], axis=0)
    cs_tab = rope_tables(n_lat)
    wr_t = jnp.pad(w_router.astype(BF16).T, ((0, LANES - N_EXPERTS), (0, 0)))
    out = None
    for l in range(depth):
        last = l == depth - 1
        with_ctx = not last
        modtab = mod_vectors(c_all, w_mod[l], b_mod[l])[:b + 1].reshape(b + 1, N_MOD, dm)
        wa, wb = _split_w_in(w_in[l])
        pa, pb, gate = in_projection(x_all, g_norm1[l], modtab, wa, wb, w_gate[l].astype(BF16), b_gate[l], bn, b)

        tables = s5_tables(s5_lam_re[l], s5_lam_im[l], s5_log_dt[l], s5_b_re[l], s5_b_im[l],
                           s5_c_re[l], s5_c_im[l], s5_d[l])
        u_seq = jnp.concatenate([pa[bn:, :BRANCH_WIDTH].reshape(b, n_ctx, BRANCH_WIDTH),
                                 pa[:bn, :BRANCH_WIDTH].reshape(b, n_lat, BRANCH_WIDTH)], axis=1)
        y_seq = s5_mixer(u_seq, tables, n_ctx)
        y5 = y_seq[:, n_ctx:].reshape(bn, BRANCH_WIDTH)

        bias_tab = na_bias_tables(na_rpb[l], n_lat // GRID_W, n_ctx)
        yb = na_mixer(pb, _COL['na_q'], _COL['na_k'], _COL['na_v'], bias_tab, n_lat, n_ctx, b)

        ml_bias = jnp.pad(ml_b_gates[l].astype(F32), (0, LANES - 4 * ML_HEADS)).reshape(1, LANES)
        hf, hb = mlstm_mixer(pb, pa, _COL['ml_q'], _COL['ml_k'], _COL['ml_v'], _COL_GATES, ml_bias, n_lat, n_ctx, b)

        yd = gq_mixer(pb, _COL['gq_q'], _COL['gq_k'], _COL['gq_v'], cs_tab, gq_qnorm[l], gq_knorm[l],
                      n_lat, n_ctx, b, with_ctx)
        if with_ctx:
            y5 = jnp.concatenate([y5, y_seq[:, :n_ctx].reshape(bc, BRANCH_WIDTH)], axis=0)
            yb = jnp.concatenate([yb, na_ctx_attention(pb, _COL['na_q'], _COL['na_k'], _COL['na_v'],
                                                       n_lat, n_ctx, b)], axis=0)
        n_rows = bn + bc if with_ctx else bn
        x_mid, h2, s_t = merge_layer(
            y5, yb, hf, hb, pb, _COL['ml_o'], yd, gate, x_all, modtab,
            s5_w_glu[l].astype(BF16), s5_b_glu[l].astype(F32).reshape(1, -1), ml_norm[l].astype(F32).reshape(1, -1),
            w_branch[l].astype(BF16), w_out[l].astype(BF16), g_norm2[l].astype(F32).reshape(1, -1), wr_t,
            n_rows, bn, b)
        x_next = moe_layer(x_mid, h2, s_t, b_router, moe_w1[l], moe_w3[l], moe_w2[l], modtab, g_final,
                           bn, b, last)
        if last:
            out = x_next.reshape(b, n_lat, dm).astype(x.dtype)
        else:
            x_all = x_next
    return out
```

```python
import functools
import math

import numpy as np
import jax
import jax.numpy as jnp
from jax import lax
from jax.experimental import pallas as pl
from jax.experimental.pallas import tpu as pltpu

F32 = jnp.float32
BF16 = jnp.bfloat16
I32 = jnp.int32

GRID_W = 64
N_MOD = 6
BRANCH_WIDTH = 512
N_BRANCHES = 4
S5_GROUP = 16
S5_GROUPS = BRANCH_WIDTH // S5_GROUP
S5_STATE = 64
NA_HEADS = 8
NA_HEAD_DIM = 64
NA_ROWS = 8
NA_COLS = 16
ML_HEADS = 4
ML_HEAD_DIM = 128
ML_CHUNK = 128
GQ_HEADS = 8
GQ_KV_HEADS = 2
GQ_HEAD_DIM = 64
ROPE_THETA = 10000.0
N_EXPERTS = 32
N_EXPERT_GROUPS = 8
EXPERTS_PER_GROUP = 4
TOP_K = 2
D_EXPERT = 512
MOE_BLOCK = 128
EPS = 1e-6
NEG_INIT = -1e30
MASK_NEG = -1e30
LOG2E = 1.4426950408889634

LANES = 128
ROW_TILE = 256
S5_CHUNK = 32
NA_TILE_ROWS = 4
VMEM_LIMIT = 56 * 1024 * 1024

HIGHEST = lax.Precision.HIGHEST


def _cp(sem, vmem=None):
    return pltpu.CompilerParams(dimension_semantics=sem, vmem_limit_bytes=vmem)


def _dot(a, b):
    return jnp.dot(a, b, preferred_element_type=F32)


def _dot_nt(a, b):
    return lax.dot_general(a, b, (((1,), (1,)), ((), ())), preferred_element_type=F32)


def _dot_tn(a, b):
    return lax.dot_general(a, b, (((0,), (0,)), ((), ())), preferred_element_type=F32)


def _rms_mod(x, g, shift, scale):
    ms = jnp.mean(x * x, axis=-1, keepdims=True)
    y = x * lax.rsqrt(ms + EPS) * g
    return y * (1.0 + scale) + shift


def _round_up(n, m):
    return -(-n // m) * m


def _mod_body(c_ref, w_ref, b_ref, o_ref):
    c = c_ref[...]
    a = (c * jax.nn.sigmoid(c)).astype(BF16)
    o_ref[...] = _dot(a, w_ref[...].astype(BF16)) + b_ref[...]


def mod_vectors(c_all, w_mod, b_mod):
    d = c_all.shape[1]
    return pl.pallas_call(
        _mod_body,
        grid=(N_MOD,),
        in_specs=[pl.BlockSpec((8, d), lambda j: (0, 0)),
                  pl.BlockSpec((d, d), lambda j: (0, j)),
                  pl.BlockSpec((1, d), lambda j: (0, j))],
        out_specs=pl.BlockSpec((8, d), lambda j: (0, j)),
        out_shape=jax.ShapeDtypeStruct((8, N_MOD * d), F32),
        compiler_params=_cp(("arbitrary",)),
        name="mod_vectors",
    )(c_all, w_mod, b_mod.reshape(1, -1))


def _inproj_body(x_ref, g_ref, mod_ref, wa_ref, wb_ref, wg_ref, bg_ref, oa_ref, ob_ref, og_ref):
    mod = mod_ref[0]
    h = _rms_mod(x_ref[...], g_ref[...], mod[0:1], mod[1:2]).astype(BF16)
    oa_ref[...] = _dot(h, wa_ref[...])
    ob_ref[...] = _dot(h, wb_ref[...]).astype(BF16)
    og_ref[...] = jax.nn.sigmoid(_dot(h, wg_ref[...]) + bg_ref[...]).astype(BF16)


def in_projection(x_all, g, modtab, wa, wb, wg, bg, n_lat_rows, n_batch):
    r, d = x_all.shape
    tm = ROW_TILE
    per_b = n_lat_rows // n_batch // tm

    def mod_idx(i):
        return (jnp.minimum(i // per_b, n_batch), 0, 0)

    const = lambda i: (0, 0)
    return pl.pallas_call(
        _inproj_body,
        grid=(r // tm,),
        in_specs=[pl.BlockSpec((tm, d), lambda i: (i, 0)),
                  pl.BlockSpec((1, d), const),
                  pl.BlockSpec((1, N_MOD, d), mod_idx),
                  pl.BlockSpec(wa.shape, const, pipeline_mode=pl.Buffered(1)),
                  pl.BlockSpec(wb.shape, const, pipeline_mode=pl.Buffered(1)),
                  pl.BlockSpec(wg.shape, const, pipeline_mode=pl.Buffered(1)),
                  pl.BlockSpec((1, wg.shape[1]), const)],
        out_specs=[pl.BlockSpec((tm, wa.shape[1]), lambda i: (i, 0)),
                   pl.BlockSpec((tm, wb.shape[1]), lambda i: (i, 0)),
                   pl.BlockSpec((tm, wg.shape[1]), lambda i: (i, 0))],
        out_shape=[jax.ShapeDtypeStruct((r, wa.shape[1]), F32),
                   jax.ShapeDtypeStruct((r, wb.shape[1]), BF16),
                   jax.ShapeDtypeStruct((r, wg.shape[1]), BF16)],
        compiler_params=_cp(("arbitrary",), VMEM_LIMIT),
        name="in_projection",
    )(x_all, g.reshape(1, d), modtab, wa, wb, wg, bg.reshape(1, -1))


def s5_tables(lam_re, lam_im, log_dt, b_re, b_im, c_re, c_im, d_skip):
    ell, g_n, p_n, c_n = S5_CHUNK, S5_GROUPS, S5_STATE, S5_GROUP
    lam_re, lam_im = lam_re.astype(F32), lam_im.astype(F32)
    b_re, b_im, c_re, c_im = (t.astype(F32) for t in (b_re, b_im, c_re, c_im))
    dt = jnp.exp(log_dt.astype(F32))[..., None]
    mag = jnp.exp(lam_re * dt)
    a_re = mag * jnp.cos(lam_im * dt)
    a_im = mag * jnp.sin(lam_im * dt)
    den = lam_re * lam_re + lam_im * lam_im
    nr = a_re - 1.0
    f_re = (nr * lam_re + a_im * lam_im) / den
    f_im = (a_im * lam_re - nr * lam_im) / den
    bb_re = f_re[..., None] * b_re - f_im[..., None] * b_im
    bb_im = f_re[..., None] * b_im + f_im[..., None] * b_re
    k = jnp.arange(ell + 1, dtype=F32)
    pmag = jnp.exp((lam_re * dt)[..., None] * k)
    ang = (lam_im * dt)[..., None] * k
    pr, pi = pmag * jnp.cos(ang), pmag * jnp.sin(ang)
    ab_re = pr[..., None] * bb_re[:, :, :, None, :] - pi[..., None] * bb_im[:, :, :, None, :]
    ab_im = pr[..., None] * bb_im[:, :, :, None, :] + pi[..., None] * bb_re[:, :, :, None, :]
    kk = (jnp.einsum('dgcp,dgpke->dgcke', c_re, ab_re, precision=HIGHEST)
          - jnp.einsum('dgcp,dgpke->dgcke', c_im, ab_im, precision=HIGHEST))
    centre = kk[0][:, :, 0] + kk[1][:, :, 0] + d_skip.astype(F32).reshape(g_n, c_n, 1) * jnp.eye(c_n, dtype=F32)
    w = jnp.concatenate([kk[0][:, :, ell - 1:0:-1], centre[:, :, None], kk[1][:, :, 1:ell]], axis=2)
    wf = jnp.concatenate([w.reshape(g_n, c_n, (2 * ell - 1) * c_n), jnp.zeros((g_n, c_n, c_n), F32)], axis=2)
    toe = jnp.tile(wf, (1, 1, ell))[:, :, :ell * (2 * ell - 1) * c_n]
    toe = toe.reshape(g_n, c_n, ell, (2 * ell - 1) * c_n)[..., (ell - 1) * c_n:]
    tsum_t = toe.transpose(0, 2, 1, 3).reshape(g_n, ell * c_n, ell * c_n)

    parity = jnp.asarray(np.arange(g_n)[:, None] % 2 == np.arange(2)[None, :], F32)
    parts = []
    for d in range(2):
        for ab in (ab_re, ab_im):
            sel = ab[d][:, :, :ell]
            if d == 0:
                sel = sel[:, :, ::-1]
            parts.append(sel.reshape(g_n, p_n, ell * c_n))
    x = jnp.stack(parts, axis=1)
    mend_t = (x[:, :, None] * parity[:, None, :, None, None]).reshape(g_n, 8 * p_n, ell * c_n)

    rows = []
    for d in range(2):
        prk, pik = pr[d][:, :, 1:ell + 1], pi[d][:, :, 1:ell + 1]
        if d == 1:
            prk, pik = prk[:, :, ::-1], pik[:, :, ::-1]
        prk = prk.transpose(0, 2, 1)[:, :, None, :]
        pik = pik.transpose(0, 2, 1)[:, :, None, :]
        cr, ci = c_re[d][:, None], c_im[d][:, None]
        rows += [cr * prk - ci * pik, -cr * pik - ci * prk]
    w4 = jnp.stack(rows, axis=3)
    wst_t = (w4[:, :, :, :, None, :] * parity[:, None, None, None, :, None]).reshape(g_n, ell * c_n, 8 * p_n)

    al = jnp.stack([pr[0][:, :, ell], pi[0][:, :, ell], pr[1][:, :, ell], pi[1][:, :, ell]], axis=1)
    a_chunk = al.reshape(g_n // 2, 2, 4, p_n).transpose(0, 2, 1, 3).reshape(g_n // 2, 8 * p_n)
    return tsum_t.astype(BF16), mend_t.astype(BF16), wst_t.astype(BF16), a_chunk


def _s5_proj_body(n_batch, xl_ref, xc_ref, g_ref, mod_ref, w_ref, o_ref):
    n_lat = xl_ref.shape[0]
    per_b = n_lat // n_batch
    n_pad = o_ref.shape[1] - LANES
    g = g_ref[...]
    parts = [_rms_mod(xl_ref[b * per_b:(b + 1) * per_b, :], g, mod_ref[b, 0:1], mod_ref[b, 1:2]).astype(BF16)
             for b in range(n_batch)]
    if n_pad > n_lat:
        parts.append(jnp.zeros((n_pad - n_lat, xl_ref.shape[1]), BF16))
    o_ref[:, 0:n_pad] = _dot_nt(w_ref[...], jnp.concatenate(parts, axis=0)).astype(BF16)
    hc = _rms_mod(xc_ref[...], g, mod_ref[n_batch, 0:1], mod_ref[n_batch, 1:2]).astype(BF16)
    hc = jnp.concatenate([hc, jnp.zeros((LANES - hc.shape[0], hc.shape[1]), BF16)], axis=0)
    o_ref[:, n_pad:] = _dot_nt(w_ref[...], hc).astype(BF16)


def _s5_end_body(u_ref, m_ref, o_ref):
    width = m_ref.shape[1]
    acc = None
    for q in range(2):
        u = u_ref[:, q].reshape(width, u_ref.shape[3])
        term = _dot(m_ref[q], u)
        acc = term if acc is None else acc + term
    o_ref[...] = acc.T


def _s5_scan_body(n_batch, per_b, n_ctx_chunks, lat_pad, e_ref, a_ref, o_ref):
    n_pairs = e_ref.shape[0]
    o_ref[...] = jnp.zeros(o_ref.shape, F32)
    n_steps = per_b + n_ctx_chunks
    first = pl.program_id(0) * n_pairs
    coef = [[a_ref[first + q, j:j + 1, :] for j in range(4)] for q in range(n_pairs)]

    def step(i, carry):
        new = []
        for b in range(n_batch):
            fwd = jnp.where(i < n_ctx_chunks, lat_pad + b * n_ctx_chunks + i, b * per_b + i - n_ctx_chunks)
            bwd = jnp.where(i < n_ctx_chunks, lat_pad + b * n_ctx_chunks + n_ctx_chunks - 1 - i,
                            b * per_b + per_b - 1 - (i - n_ctx_chunks))
            for q in range(n_pairs):
                for d, row in enumerate((fwd, bwd)):
                    sr, si = carry[((b * n_pairs + q) * 2 + d) * 2:((b * n_pairs + q) * 2 + d) * 2 + 2]
                    ar, ai = coef[q][2 * d], coef[q][2 * d + 1]
                    o_ref[q, row, 2 * d:2 * d + 1, :] = sr
                    o_ref[q, row, 2 * d + 1:2 * d + 2, :] = si
                    er = e_ref[q, row, 2 * d:2 * d + 1, :]
                    ei = e_ref[q, row, 2 * d + 1:2 * d + 2, :]
                    new += [ar * sr - ai * si + er, ar * si + ai * sr + ei]
        return tuple(new)

    z = jnp.zeros((1, LANES), F32)
    lax.fori_loop(0, n_steps, step, tuple(z for _ in range(n_batch * n_pairs * 4)))


def _s5_out_body(u_ref, t_ref, s_ref, w_ref, o_ref):
    width = t_ref.shape[0]
    u = u_ref[...].reshape(width, u_ref.shape[2])
    y = _dot(t_ref[...], u) + _dot_nt(w_ref[...], s_ref[...].astype(BF16))
    o_ref[...] = y.reshape(o_ref.shape)


def _s5_glu_body(n_lat_chunks, n_ctx_chunks, lat_pad, y_ref, w_ref, b_ref, o_ref):
    y = y_ref[...].T
    if lat_pad == n_lat_chunks:
        y = y[:n_lat_chunks + n_ctx_chunks]
    else:
        y = jnp.concatenate([y[:n_lat_chunks], y[lat_pad:lat_pad + n_ctx_chunks]], axis=0)
    z = jax.nn.gelu(y)
    o_ref[...] = (z * jax.nn.sigmoid(_dot(z.astype(BF16), w_ref[...]) + b_ref[...])).astype(o_ref.dtype)


def s5_mixer(x_all, g, modtab, w_u_t, tables, w_glu, b_glu, n_lat, n_ctx, n_batch):
    tsum_t, mend_t, wst_t, a_chunk = tables
    r, d = x_all.shape
    ell, g_n, c_n = S5_CHUNK, S5_GROUPS, S5_GROUP
    width = ell * c_n
    bw = g_n * c_n
    n_lat_chunks = n_batch * n_lat // ell
    n_ctx_chunks = n_batch * n_ctx // ell
    assert n_lat_chunks % n_ctx_chunks == 0 and n_ctx_chunks % 16 == 0 and n_ctx_chunks <= LANES
    lat_pad = _round_up(n_lat_chunks, LANES)
    nch = lat_pad + LANES
    x2 = x_all.reshape(r // ell, ell * d)
    const = lambda t: (0, 0)
    u_t = pl.pallas_call(
        functools.partial(_s5_proj_body, n_batch),
        grid=(ell,),
        in_specs=[pl.BlockSpec((n_lat_chunks, d), lambda t: (0, t)),
                  pl.BlockSpec((n_ctx_chunks, d), lambda t: (n_lat_chunks // n_ctx_chunks, t)),
                  pl.BlockSpec((1, d), const),
                  pl.BlockSpec(modtab.shape, lambda t: (0, 0, 0)),
                  pl.BlockSpec((bw, d), const)],
        out_specs=pl.BlockSpec((None, bw, nch), lambda t: (t, 0, 0)),
        out_shape=jax.ShapeDtypeStruct((ell, bw, nch), BF16),
        compiler_params=_cp(("arbitrary",)),
        name="s5_projection",
    )(x2, x2, g.reshape(1, d), modtab, w_u_t)
    u4 = u_t.reshape(ell, g_n, c_n, nch)
    ends = pl.pallas_call(
        _s5_end_body,
        grid=(g_n // 2,),
        in_specs=[pl.BlockSpec((ell, 2, c_n, nch), lambda p: (0, p, 0, 0)),
                  pl.BlockSpec((2, width, width), lambda p: (p, 0, 0))],
        out_specs=pl.BlockSpec((None, nch, width), lambda p: (p, 0, 0)),
        out_shape=jax.ShapeDtypeStruct((g_n // 2, nch, width), F32),
        compiler_params=_cp(("arbitrary",)),
        name="s5_chunk_ends",
    )(u4, mend_t)
    pairs_per_step = 2
    quad = (g_n // 2, nch, 4, LANES)
    states = pl.pallas_call(
        functools.partial(_s5_scan_body, n_batch, n_lat // ell, n_ctx // ell, lat_pad),
        grid=(g_n // 2 // pairs_per_step,),
        in_specs=[pl.BlockSpec((pairs_per_step, nch, 4, LANES), lambda j: (j, 0, 0, 0)),
                  pl.BlockSpec((g_n // 2, 4, LANES), lambda j: (0, 0, 0))],
        out_specs=pl.BlockSpec((pairs_per_step, nch, 4, LANES), lambda j: (j, 0, 0, 0)),
        out_shape=jax.ShapeDtypeStruct(quad, F32),
        compiler_params=_cp(("arbitrary",), VMEM_LIMIT),
        name="s5_state_scan",
    )(ends.reshape(quad), a_chunk.reshape(g_n // 2, 4, LANES)).reshape(g_n // 2, nch, width)
    y_t = pl.pallas_call(
        _s5_out_body,
        grid=(g_n,),
        in_specs=[pl.BlockSpec((ell, None, c_n, nch), lambda gi: (0, gi, 0, 0)),
                  pl.BlockSpec((None, width, width), lambda gi: (gi, 0, 0)),
                  pl.BlockSpec((None, nch, width), lambda gi: (gi // 2, 0, 0)),
                  pl.BlockSpec((None, width, width), lambda gi: (gi, 0, 0))],
        out_specs=pl.BlockSpec((ell, None, c_n, nch), lambda gi: (0, gi, 0, 0)),
        out_shape=jax.ShapeDtypeStruct((ell, g_n, c_n, nch), F32),
        compiler_params=_cp(("arbitrary",)),
        name="s5_outputs",
    )(u4, tsum_t, states, wst_t)
    n_chunks = n_lat_chunks + n_ctx_chunks
    ya = pl.pallas_call(
        functools.partial(_s5_glu_body, n_lat_chunks, n_ctx_chunks, lat_pad),
        grid=(ell,),
        in_specs=[pl.BlockSpec((None, bw, nch), lambda t: (t, 0, 0)),
                  pl.BlockSpec((bw, bw), const),
                  pl.BlockSpec((1, bw), const)],
        out_specs=pl.BlockSpec((n_chunks, bw), lambda t: (0, t)),
        out_shape=jax.ShapeDtypeStruct((n_chunks, ell * bw), BF16),
        compiler_params=_cp(("arbitrary",)),
        name="s5_glu",
    )(y_t.reshape(ell, bw, nch), w_glu, b_glu)
    return ya.reshape(r, bw)


def rope_tables(n_lat):
    half = GQ_HEAD_DIM // 2
    quarter = half // 2
    t = np.arange(n_lat)
    freqs = ROPE_THETA ** (-np.arange(quarter, dtype=np.float64) / quarter)
    ang_r = (t // GRID_W)[:, None] * freqs
    ang_c = (t % GRID_W)[:, None] * freqs
    ang = np.concatenate([ang_r, ang_r, ang_c, ang_c], axis=1)
    sign = np.concatenate([-np.ones(quarter), np.ones(quarter)] * 2)
    cos = np.concatenate([np.cos(ang), np.ones((ROW_TILE, GQ_HEAD_DIM))], axis=0)
    sin = np.concatenate([np.sin(ang) * sign, np.zeros((ROW_TILE, GQ_HEAD_DIM))], axis=0)
    tab = np.concatenate([cos, cos, sin, sin], axis=1)
    return jnp.asarray(tab, F32)


def _group_ones(width, group):
    i = np.arange(width)
    return jnp.asarray((i[:, None] // group) == (i[None, :] // group), BF16)


def _group_mean_sq(x, ones_blk, group):
    sq = x * x
    hi = sq.astype(BF16)
    lo = (sq - hi.astype(F32)).astype(BF16)
    return (_dot(hi, ones_blk) + _dot(lo, ones_blk)) * (1.0 / group)


def _rope(x, cos, sin):
    w = x.shape[-1]
    q = GQ_HEAD_DIM // 4
    lane = lax.broadcasted_iota(I32, x.shape, 1)
    first = (lane % (2 * q)) < q
    partner = jnp.where(first, pltpu.roll(x, w - q, 1), pltpu.roll(x, q, 1))
    return x * cos + partner * sin


def _gq_prep_body(q_ref, k_ref, v_ref, cs_ref, gq_ref, gk_ref, oq_ref, ok_ref, qm_ref, kr_ref, va_ref):
    cs = cs_ref[...]
    cos1, sin1 = cs[:, :LANES], cs[:, LANES:]
    q = q_ref[...].astype(F32)
    qn = q * lax.rsqrt(_group_mean_sq(q, oq_ref[...], GQ_HEAD_DIM) + EPS) * gq_ref[...]
    n_pairs = q.shape[1] // LANES
    qr = _rope(qn, jnp.concatenate([cos1] * n_pairs, axis=1), jnp.concatenate([sin1] * n_pairs, axis=1))
    qr = (qr * (GQ_HEAD_DIM ** -0.5 * LOG2E)).astype(BF16)
    lane = lax.broadcasted_iota(I32, (q.shape[0], LANES), 1)
    heads_per_kv = GQ_HEADS // GQ_KV_HEADS
    for h in range(GQ_HEADS):
        pair = qr[:, (h // 2) * LANES:(h // 2 + 1) * LANES]
        kv = h // heads_per_kv
        if h % 2 != kv:
            pair = pltpu.roll(pair, GQ_HEAD_DIM, 1)
        keep = (lane >= kv * GQ_HEAD_DIM) & (lane < (kv + 1) * GQ_HEAD_DIM)
        qm_ref[h] = jnp.where(keep, pair, jnp.zeros_like(pair))
    k = k_ref[...].astype(F32)
    kn = k * lax.rsqrt(_group_mean_sq(k, ok_ref[...], GQ_HEAD_DIM) + EPS) * gk_ref[...]
    kr_ref[...] = _rope(kn, cos1, sin1).astype(BF16)
    va_ref[...] = jnp.concatenate([v_ref[...], jnp.ones(v_ref.shape, BF16)], axis=1)


def gq_prepare(pb, col_q, col_k, col_v, cs_tab, g_q, g_k, n_lat, n_ctx, n_batch):
    r = pb.shape[0]
    tm = ROW_TILE
    assert n_ctx == tm and n_lat % tm == 0
    nb = n_lat // tm
    n_lat_tiles = n_batch * nb

    def tab_idx(i):
        return (jnp.where(i < n_lat_tiles, i % nb, nb), 0)

    def kv_idx(i):
        lat = (i // nb) * (nb + 1) + i % nb
        ctx = (i - n_lat_tiles) * (nb + 1) + nb
        return (jnp.where(i < n_lat_tiles, lat, ctx), 0)

    qw = GQ_HEADS * GQ_HEAD_DIM
    const = lambda i: (0, 0)
    gq = jnp.tile(g_q.astype(F32), GQ_HEADS).reshape(1, qw)
    gk = jnp.tile(g_k.astype(F32), GQ_KV_HEADS).reshape(1, LANES)
    n_keys = n_batch * (n_lat + n_ctx)
    return pl.pallas_call(
        _gq_prep_body,
        grid=(r // tm,),
        in_specs=[pl.BlockSpec((tm, qw), lambda i: (i, col_q // qw)),
                  pl.BlockSpec((tm, LANES), lambda i: (i, col_k // LANES)),
                  pl.BlockSpec((tm, LANES), lambda i: (i, col_v // LANES)),
                  pl.BlockSpec((tm, 2 * LANES), tab_idx),
                  pl.BlockSpec((1, qw), const),
                  pl.BlockSpec((1, LANES), const),
                  pl.BlockSpec((qw, qw), const),
                  pl.BlockSpec((LANES, LANES), const)],
        out_specs=[pl.BlockSpec((GQ_HEADS, tm, LANES), lambda i: (0, i, 0)),
                   pl.BlockSpec((tm, LANES), kv_idx),
                   pl.BlockSpec((tm, 2 * LANES), kv_idx)],
        out_shape=[jax.ShapeDtypeStruct((GQ_HEADS, r, LANES), BF16),
                   jax.ShapeDtypeStruct((n_keys, LANES), BF16),
                   jax.ShapeDtypeStruct((n_keys, 2 * LANES), BF16)],
        compiler_params=_cp(("arbitrary",)),
        name="gq_prepare",
    )(pb, pb, pb, cs_tab, gq, gk, _group_ones(qw, GQ_HEAD_DIM), _group_ones(LANES, GQ_HEAD_DIM))


def _gq_flash_body(q_ref, k_ref, v_ref, o_ref, m_sc, acc_sc):
    kj = pl.program_id(2)
    n_h, tq, _ = q_ref.shape

    @pl.when(kj == 0)
    def _():
        m_sc[...] = jnp.full(m_sc.shape, -jnp.inf, F32)
        acc_sc[...] = jnp.zeros(acc_sc.shape, F32)

    q = q_ref[...].reshape(n_h * tq, LANES)
    s = _dot_nt(q, k_ref[...])
    m_prev = m_sc[...]
    m_new = jnp.maximum(m_prev, jnp.max(s, axis=-1, keepdims=True))
    p = jnp.exp2(s - m_new)
    acc_sc[...] = jnp.exp2(m_prev - m_new) * acc_sc[...] + _dot(p.astype(BF16), v_ref[...])
    m_sc[...] = m_new

    @pl.when(kj == pl.num_programs(2) - 1)
    def _():
        lane = lax.broadcasted_iota(I32, (tq, LANES), 1)
        heads_per_kv = n_h // GQ_KV_HEADS
        for j in range(n_h // 2):
            kv = (2 * j) // heads_per_kv
            halves = []
            for h in (2 * j, 2 * j + 1):
                a = acc_sc[h * tq:(h + 1) * tq, :]
                halves.append(a[:, :LANES] / a[:, LANES:LANES + 1])
            lo, hi = halves
            if kv == 0:
                hi = pltpu.roll(hi, GQ_HEAD_DIM, 1)
            else:
                lo = pltpu.roll(lo, GQ_HEAD_DIM, 1)
            o_ref[:, j * LANES:(j + 1) * LANES] = jnp.where(lane < GQ_HEAD_DIM, lo, hi).astype(o_ref.dtype)


def gq_attention(qm, keys, vals, n_rows_out, tq, tk, q_blk, k_blk, n_q, n_k, n_batch):
    n_h = qm.shape[0]
    return pl.pallas_call(
        _gq_flash_body,
        grid=(n_batch, n_q, n_k),
        in_specs=[pl.BlockSpec((n_h, tq, LANES), lambda b, i, j: (0, q_blk(b, i), 0)),
                  pl.BlockSpec((tk, LANES), lambda b, i, j: (k_blk(b, j), 0)),
                  pl.BlockSpec((tk, 2 * LANES), lambda b, i, j: (k_blk(b, j), 0))],
        out_specs=pl.BlockSpec((tq, n_h * GQ_HEAD_DIM), lambda b, i, j: (b * n_q + i, 0)),
        out_shape=jax.ShapeDtypeStruct((n_rows_out, n_h * GQ_HEAD_DIM), BF16),
        scratch_shapes=[pltpu.VMEM((n_h * tq, 1), F32), pltpu.VMEM((n_h * tq, 2 * LANES), F32)],
        compiler_params=_cp(("arbitrary", "arbitrary", "arbitrary"), VMEM_LIMIT),
        name="gq_attention",
    )(qm, keys, vals)


def _largest_divisor(n, cap):
    return max(d for d in range(1, cap + 1) if n % d == 0)


def gq_mixer(pb, col_q, col_k, col_v, cs_tab, g_q, g_k, n_lat, n_ctx, n_batch, with_ctx):
    qm, keys, vals = gq_prepare(pb, col_q, col_k, col_v, cs_tab, g_q, g_k, n_lat, n_ctx, n_batch)
    tq = 128
    kb = (n_lat + n_ctx) // ROW_TILE
    tk = ROW_TILE * _largest_divisor(kb, 3)
    n_q = n_lat // tq
    per_b = (n_lat + n_ctx) // tk
    y_lat = gq_attention(qm, keys, vals, n_batch * n_lat, tq, tk,
                         lambda b, i: b * n_q + i, lambda b, j: b * per_b + j, n_q, per_b, n_batch)
    if not with_ctx:
        return y_lat
    tc = n_ctx
    y_ctx = gq_attention(qm, keys, vals, n_batch * n_ctx, tc, tc,
                         lambda b, i: n_batch * n_lat // tc + b,
                         lambda b, j: b * ((n_lat + n_ctx) // tc) + n_lat // tc, 1, 1, n_batch)
    return jnp.concatenate([y_lat, y_ctx], axis=0)


def na_bias_tables(rpb, n_img_rows, n_ctx):
    tr = NA_TILE_ROWS
    nt = n_img_rows // tr
    assert nt >= 4
    kr = min(NA_ROWS, n_img_rows)
    n_heads = rpb.shape[0]
    qcol = np.arange(GRID_W)[:, None]
    kcol = np.arange(GRID_W)[None, :]
    dc = np.clip(kcol - qcol + NA_COLS - 1, 0, 2 * NA_COLS - 2)
    oh_c = (dc[None] == np.arange(2 * NA_COLS - 1)[:, None, None]).astype(np.float32)
    cstart = np.clip(qcol - NA_COLS // 2, 0, GRID_W - NA_COLS)
    col_ok = (kcol >= cstart) & (kcol < cstart + NA_COLS)
    by_col = jnp.einsum('hrd,dqk->hrqk', rpb.astype(F32), jnp.asarray(oh_c), precision=HIGHEST)
    classes = []
    for i in (0, 1, nt - 1):
        wb = int(np.clip(i - 1, 0, nt - 3))
        qrow = (i * tr + np.arange(tr))[:, None]
        krow = (wb * tr + np.arange(3 * tr))[None, :]
        rs = np.clip(qrow - kr // 2, 0, n_img_rows - kr)
        row_ok = (krow >= rs) & (krow < rs + kr)
        dr = np.clip(krow - qrow + NA_ROWS - 1, 0, 2 * NA_ROWS - 2)
        oh_r = ((dr[None] == np.arange(2 * NA_ROWS - 1)[:, None, None]) & row_ok[None]).astype(np.float32)
        bias = jnp.einsum('hrqk,rab->haqbk', by_col, jnp.asarray(oh_r), precision=HIGHEST)
        ok = row_ok[:, None, :, None] & col_ok[None, :, None, :]
        bias = jnp.where(jnp.asarray(ok)[None], bias, MASK_NEG).reshape(n_heads, tr * GRID_W, 3 * tr * GRID_W)
        classes.append(jnp.concatenate([bias, jnp.zeros((n_heads, tr * GRID_W, n_ctx), F32)], axis=-1))
    return jnp.stack(classes, axis=0)


def _pair_attention(q_pair, k_pair, v_pair, bias_fn):
    lane = lax.broadcasted_iota(I32, q_pair.shape, 1)
    out = None
    for hh in range(2):
        mine = (lane >= hh * NA_HEAD_DIM) & (lane < (hh + 1) * NA_HEAD_DIM)
        qm = jnp.where(mine, q_pair, jnp.zeros_like(q_pair))
        s = _dot_nt(qm, k_pair)
        b = bias_fn(hh)
        if b is not None:
            s = s + b
        m = jnp.max(s, axis=-1, keepdims=True)
        p = jnp.exp(s - m)
        l = jnp.sum(p, axis=-1, keepdims=True)
        o = _dot(p.astype(BF16), v_pair) / l
        out = o if out is None else jnp.where(mine, o, out)
    return out


def _na_body(q_ref, k0_ref, k1_ref, k2_ref, kc_ref, v0_ref, v1_ref, v2_ref, vc_ref, b_ref, o_ref):
    scale = NA_HEAD_DIM ** -0.5
    for j in range(NA_HEADS // 2):
        sl = slice(j * LANES, (j + 1) * LANES)
        q_pair = q_ref[:, sl] * scale
        k_pair = jnp.concatenate([k0_ref[:, sl], k1_ref[:, sl], k2_ref[:, sl], kc_ref[:, sl]], axis=0)
        v_pair = jnp.concatenate([v0_ref[:, sl], v1_ref[:, sl], v2_ref[:, sl], vc_ref[:, sl]], axis=0)
        o = _pair_attention(q_pair, k_pair, v_pair, lambda hh: b_ref[0, 2 * j + hh])
        o_ref[:, sl] = o.astype(o_ref.dtype)


def na_mixer(pb, col_q, col_k, col_v, bias_tab, n_lat, n_ctx, n_batch):
    tm = NA_TILE_ROWS * GRID_W
    assert n_ctx == tm
    w = NA_HEADS * NA_HEAD_DIM
    nt = n_lat // tm
    n_keys = 3 * tm + n_ctx
    cq, ck, cv = col_q // w, col_k // w, col_v // w
    ctx0 = n_batch * nt

    def win(o):
        return lambda b, i: (b * nt + jnp.clip(i - 1, 0, nt - 3) + o)

    def cls(b, i):
        return (jnp.where(i == 0, 0, jnp.where(i == nt - 1, 2, 1)), 0, 0, 0)

    kspecs = [pl.BlockSpec((tm, w), (lambda b, i, f=win(o): (f(b, i), ck))) for o in range(3)]
    vspecs = [pl.BlockSpec((tm, w), (lambda b, i, f=win(o): (f(b, i), cv))) for o in range(3)]
    return pl.pallas_call(
        _na_body,
        grid=(n_batch, nt),
        in_specs=[pl.BlockSpec((tm, w), lambda b, i: (b * nt + i, cq))] + kspecs
        + [pl.BlockSpec((tm, w), lambda b, i: (ctx0 + b, ck))] + vspecs
        + [pl.BlockSpec((tm, w), lambda b, i: (ctx0 + b, cv)),
           pl.BlockSpec((1, NA_HEADS, tm, n_keys), cls)],
        out_specs=pl.BlockSpec((tm, w), lambda b, i: (b * nt + i, 0)),
        out_shape=jax.ShapeDtypeStruct((n_batch * n_lat, w), BF16),
        compiler_params=_cp(("arbitrary", "arbitrary"), VMEM_LIMIT),
        name="na_attention",
    )(pb, pb, pb, pb, pb, pb, pb, pb, pb, bias_tab)


def _ctx_mha_body(q_ref, k_ref, v_ref, o_ref):
    scale = NA_HEAD_DIM ** -0.5
    for j in range(NA_HEADS // 2):
        sl = slice(j * LANES, (j + 1) * LANES)
        o = _pair_attention(q_ref[:, sl] * scale, k_ref[:, sl], v_ref[:, sl], lambda hh: None)
        o_ref[:, sl] = o.astype(o_ref.dtype)


def na_ctx_attention(pb, col_q, col_k, col_v, n_lat, n_ctx, n_batch):
    w = NA_HEADS * NA_HEAD_DIM
    ctx0 = n_batch * n_lat // n_ctx
    spec = lambda c: pl.BlockSpec((n_ctx, w), lambda b: (ctx0 + b, c // w))
    return pl.pallas_call(
        _ctx_mha_body,
        grid=(n_batch,),
        in_specs=[spec(col_q), spec(col_k), spec(col_v)],
        out_specs=pl.BlockSpec((n_ctx, w), lambda b: (b, 0)),
        out_shape=jax.ShapeDtypeStruct((n_batch * n_ctx, w), BF16),
        compiler_params=_cp(("arbitrary",)),
        name="na_ctx_attention",
    )(pb, pb, pb)


def _mlstm_chunk(rev, q, k, v, g, lf_cum, g_t, lf_cum_t, ci, cf, c_ref, n_ref, m_ref, idx):
    tok = lax.broadcasted_iota(I32, (ML_CHUNK, ML_CHUNK), 0)
    src = lax.broadcasted_iota(I32, (ML_CHUNK, ML_CHUNK), 1)
    causal = (src >= tok) if rev else (src <= tok)
    bt_col = lf_cum[:, cf:cf + 1]
    bt_row = lf_cum_t[cf:cf + 1, :]
    li_col = g[:, ci:ci + 1]
    li_row = g_t[ci:ci + 1, :]
    m_prev = m_ref[idx][:, 0:1]
    dmat = jnp.where(causal, bt_col - bt_row + li_row, -jnp.inf)
    inter = bt_col + m_prev
    mt = jnp.maximum(inter, jnp.max(dmat, axis=-1, keepdims=True))
    s = _dot_nt(q, k) * jnp.exp(dmat - mt)
    w_inter = jnp.exp(inter - mt)
    c_prev = c_ref[idx]
    n_prev = n_ref[idx]
    num = _dot(s.astype(BF16), v) + w_inter * _dot_nt(q, c_prev.astype(BF16))
    qn = jnp.sum(q.astype(F32) * n_prev, axis=-1, keepdims=True)
    den = jnp.sum(s, axis=-1, keepdims=True) + w_inter * qn
    h_out = num / jnp.maximum(jnp.abs(den), jnp.exp(-mt))
    b_last = bt_col[0:1, :] if rev else bt_col[ML_CHUNK - 1:ML_CHUNK, :]
    g_col = b_last - bt_col + li_col
    m_new = jnp.maximum(b_last + m_prev, jnp.max(g_col, axis=0, keepdims=True))
    wg = jnp.exp(g_col - m_new)
    decay = jnp.exp(b_last + m_prev - m_new)
    c_ref[idx] = decay * c_prev + _dot_tn((wg * v.astype(F32)).astype(BF16), k)
    n_ref[idx] = decay * n_prev + jnp.sum(wg * k.astype(F32), axis=0, keepdims=True)
    m_ref[idx] = jnp.broadcast_to(m_new, (1, LANES))
    return h_out


def _mlstm_body(qf_ref, kf_ref, vf_ref, gf_ref, qb_ref, kb_ref, vb_ref, gb_ref, bias_ref,
                hf_ref, hb_ref, c_ref, n_ref, m_ref):
    @pl.when(pl.program_id(1) == 0)
    def _():
        c_ref[...] = jnp.zeros(c_ref.shape, F32)
        n_ref[...] = jnp.zeros(n_ref.shape, F32)
        m_ref[...] = jnp.full(m_ref.shape, NEG_INIT, F32)

    tok = lax.broadcasted_iota(I32, (ML_CHUNK, ML_CHUNK), 0)
    src = lax.broadcasted_iota(I32, (ML_CHUNK, ML_CHUNK), 1)
    for d, (q_ref, k_ref, v_ref, g_ref, h_ref) in enumerate(
            ((qf_ref, kf_ref, vf_ref, gf_ref, hf_ref), (qb_ref, kb_ref, vb_ref, gb_ref, hb_ref))):
        rev = d == 1
        g = g_ref[...] + bias_ref[...]
        tri = ((src >= tok) if rev else (src <= tok)).astype(F32)
        lf_cum = jnp.dot(tri, jax.nn.log_sigmoid(g), precision=HIGHEST, preferred_element_type=F32)
        g_t = g.T
        lf_cum_t = lf_cum.T
        for h in range(ML_HEADS):
            sl = slice(h * ML_HEAD_DIM, (h + 1) * ML_HEAD_DIM)
            h_out = _mlstm_chunk(rev, q_ref[:, sl], k_ref[:, sl], v_ref[:, sl], g, lf_cum, g_t, lf_cum_t,
                                 d * 2 * ML_HEADS + h, d * 2 * ML_HEADS + ML_HEADS + h,
                                 c_ref, n_ref, m_ref, d * ML_HEADS + h)
            h_ref[:, sl] = h_out


def mlstm_mixer(pb, pa, col_q, col_k, col_v, col_g, bias, n_lat, n_ctx, n_batch):
    r = pb.shape[0]
    w = ML_HEADS * ML_HEAD_DIM
    tc = ML_CHUNK
    nl, nc = n_lat // tc, n_ctx // tc

    def fwd(b, i):
        return jnp.where(i < nc, n_batch * nl + b * nc + i, b * nl + i - nc)

    def bwd(b, i):
        return jnp.where(i < nc, n_batch * nl + b * nc + nc - 1 - i, b * nl + nl - 1 - (i - nc))

    def specs(f):
        return [pl.BlockSpec((tc, w), lambda b, i, c=c: (f(b, i), c // w)) for c in (col_q, col_k, col_v)] + [
            pl.BlockSpec((tc, LANES), lambda b, i: (f(b, i), col_g // LANES))]

    n_st = 2 * ML_HEADS
    return pl.pallas_call(
        _mlstm_body,
        grid=(n_batch, nl + nc),
        in_specs=specs(fwd) + specs(bwd) + [pl.BlockSpec((1, LANES), lambda b, i: (0, 0))],
        out_specs=[pl.BlockSpec((tc, w), lambda b, i: (fwd(b, i), 0)),
                   pl.BlockSpec((tc, w), lambda b, i: (bwd(b, i), 0))],
        out_shape=[jax.ShapeDtypeStruct((r, w), F32), jax.ShapeDtypeStruct((r, w), F32)],
        scratch_shapes=[pltpu.VMEM((n_st, ML_HEAD_DIM, ML_HEAD_DIM), F32),
                        pltpu.VMEM((n_st, 1, ML_HEAD_DIM), F32),
                        pltpu.VMEM((n_st, 1, LANES), F32)],
        compiler_params=_cp(("arbitrary", "arbitrary")),
        name="mlstm_chunks",
    )(pb, pb, pb, pa, pb, pb, pb, pa, bias)


def _merge_body(ya_ref, yb_ref, hf_ref, hb_ref, o_ref, yd_ref, gate_ref, x_ref, mod_ref,
                mlg_ref, wbr_ref, wout_ref, g2_ref, wr_ref, xo_ref, h2_ref, st_ref):
    d = x_ref.shape[1]
    hs = hf_ref[...] + hb_ref[...]
    segs = []
    for h in range(ML_HEADS):
        seg = hs[:, h * ML_HEAD_DIM:(h + 1) * ML_HEAD_DIM]
        segs.append(seg * lax.rsqrt(jnp.mean(seg * seg, axis=-1, keepdims=True) + EPS))
    ym = jnp.concatenate(segs, axis=1) * mlg_ref[...] * jax.nn.sigmoid(o_ref[...].astype(F32))
    ys = (ya_ref[...], yb_ref[...], ym.astype(BF16), yd_ref[...])
    merged = None
    for i in range(N_BRANCHES):
        term = gate_ref[:, i * d:(i + 1) * d].astype(F32) * _dot(ys[i], wbr_ref[i])
        merged = term if merged is None else merged + term
    y = _dot(merged.astype(BF16), wout_ref[...])
    mod = mod_ref[0]
    x_new = x_ref[...] + mod[2:3] * y
    xo_ref[...] = x_new
    h2 = _rms_mod(x_new, g2_ref[...], mod[3:4], mod[4:5])
    h2_ref[...] = h2
    st_ref[...] = jax.nn.sigmoid(_dot_nt(wr_ref[...], h2.astype(BF16)))


def merge_layer(ya, yb, hf, hb, pb, col_o, yd, gate, x_all, modtab, mlg, wbr, wout, g2, wr_t,
                n_rows, n_lat_rows, n_batch):
    d = x_all.shape[1]
    tm = ROW_TILE
    w = BRANCH_WIDTH
    per_b = n_lat_rows // n_batch // tm
    row = lambda i: (i, 0)
    const2 = lambda i: (0, 0)
    return pl.pallas_call(
        _merge_body,
        grid=(n_rows // tm,),
        in_specs=[pl.BlockSpec((tm, w), row), pl.BlockSpec((tm, w), row), pl.BlockSpec((tm, w), row),
                  pl.BlockSpec((tm, w), row), pl.BlockSpec((tm, w), lambda i: (i, col_o // w)),
                  pl.BlockSpec((tm, w), row), pl.BlockSpec((tm, N_BRANCHES * d), row),
                  pl.BlockSpec((tm, d), row),
                  pl.BlockSpec((1, N_MOD, d), lambda i: (jnp.minimum(i // per_b, n_batch), 0, 0)),
                  pl.BlockSpec((1, w), const2),
                  pl.BlockSpec((N_BRANCHES, w, d), lambda i: (0, 0, 0)),
                  pl.BlockSpec((d, d), const2), pl.BlockSpec((1, d), const2),
                  pl.BlockSpec((LANES, d), const2)],
        out_specs=[pl.BlockSpec((tm, d), row), pl.BlockSpec((tm, d), row),
                   pl.BlockSpec((LANES, tm), lambda i: (0, i))],
        out_shape=[jax.ShapeDtypeStruct((n_rows, d), F32), jax.ShapeDtypeStruct((n_rows, d), F32),
                   jax.ShapeDtypeStruct((LANES, n_rows), F32)],
        compiler_params=_cp(("arbitrary",), VMEM_LIMIT),
        name="merge_layer",
    )(ya, yb, hf, hb, pb, yd, gate, x_all, modtab, mlg, wbr, wout, g2, wr_t)


def _router_body(s_ref, b_ref, e_ref, w_ref, rank_ref, cnt_ref, base_sc):
    @pl.when(pl.program_id(0) == 0)
    def _():
        base_sc[...] = jnp.zeros(base_sc.shape, F32)

    tm = s_ref.shape[1]
    s = s_ref[0:N_EXPERTS, :]
    sel = s + b_ref[0:N_EXPERTS, :]
    row = lambda a, e: a[e:e + 1, :]
    best, grp = None, None
    for g in range(N_EXPERT_GROUPS):
        v = [row(sel, EXPERTS_PER_GROUP * g + k) for k in range(EXPERTS_PER_GROUP)]
        gs = None
        for a in range(EXPERTS_PER_GROUP):
            for c in range(a + 1, EXPERTS_PER_GROUP):
                gs = v[a] + v[c] if gs is None else jnp.maximum(gs, v[a] + v[c])
        if best is None:
            best, grp = gs, jnp.zeros((1, tm), I32)
        else:
            better = gs > best
            grp = jnp.where(better, g, grp)
            best = jnp.where(better, gs, best)
    vals, affs = [], []
    for k in range(EXPERTS_PER_GROUP):
        vk, sk = row(sel, k), row(s, k)
        for g in range(1, N_EXPERT_GROUPS):
            hit = grp == g
            vk = jnp.where(hit, row(sel, EXPERTS_PER_GROUP * g + k), vk)
            sk = jnp.where(hit, row(s, EXPERTS_PER_GROUP * g + k), sk)
        vals.append(vk)
        affs.append(sk)
    i1, b1, w1 = jnp.zeros((1, tm), I32), vals[0], affs[0]
    for k in range(1, EXPERTS_PER_GROUP):
        better = vals[k] > b1
        i1 = jnp.where(better, k, i1)
        w1 = jnp.where(better, affs[k], w1)
        b1 = jnp.where(better, vals[k], b1)
    i2 = jnp.zeros((1, tm), I32)
    b2 = jnp.full((1, tm), -jnp.inf, F32)
    w2 = jnp.zeros((1, tm), F32)
    for k in range(EXPERTS_PER_GROUP):
        cand = (i1 != k) & (vals[k] > b2)
        i2 = jnp.where(cand, k, i2)
        w2 = jnp.where(cand, affs[k], w2)
        b2 = jnp.where(cand, vals[k], b2)
    e1 = grp * EXPERTS_PER_GROUP + i1
    e2 = grp * EXPERTS_PER_GROUP + i2
    tot = w1 + w2
    e_ref[...] = jnp.concatenate([e1, e2], axis=0)
    wpad = jnp.concatenate([w1 / tot, w2 / tot, jnp.zeros((6, tm), F32)], axis=0)
    w_ref[...] = wpad.T
    ids = lax.broadcasted_iota(I32, (N_EXPERTS, tm), 0)
    oh1 = (ids == e1).astype(F32)
    oh2 = (ids == e2).astype(F32)
    oh = oh1 + oh2
    before = (lax.broadcasted_iota(I32, (tm, tm), 0) < lax.broadcasted_iota(I32, (tm, tm), 1)).astype(BF16)
    prior = _dot(oh.astype(BF16), before) + base_sc[...]
    r1 = jnp.sum(oh1 * prior, axis=0, keepdims=True)
    r2 = jnp.sum(oh2 * prior, axis=0, keepdims=True)
    rank_ref[...] = jnp.concatenate([r1, r2], axis=0).astype(I32)
    base = base_sc[...] + jnp.sum(oh, axis=1, keepdims=True)
    base_sc[...] = base
    cnt_ref[...] = jnp.broadcast_to(base, cnt_ref.shape).astype(I32)


def route(s_t, b_router):
    n_rows = s_t.shape[1]
    tm = ROW_TILE
    b_col = jnp.pad(b_router.astype(F32), (0, LANES - N_EXPERTS)).reshape(LANES, 1)
    return pl.pallas_call(
        _router_body,
        grid=(n_rows // tm,),
        in_specs=[pl.BlockSpec((LANES, tm), lambda i: (0, i)), pl.BlockSpec((LANES, 1), lambda i: (0, 0))],
        out_specs=[pl.BlockSpec((TOP_K, tm), lambda i: (0, i)), pl.BlockSpec((tm, 8), lambda i: (i, 0)),
                   pl.BlockSpec((TOP_K, tm), lambda i: (0, i)), pl.BlockSpec((N_EXPERTS, LANES), lambda i: (0, 0))],
        out_shape=[jax.ShapeDtypeStruct((TOP_K, n_rows), I32), jax.ShapeDtypeStruct((n_rows, 8), F32),
                   jax.ShapeDtypeStruct((TOP_K, n_rows), I32), jax.ShapeDtypeStruct((N_EXPERTS, LANES), I32)],
        scratch_shapes=[pltpu.VMEM((N_EXPERTS, 1), F32)],
        compiler_params=_cp(("arbitrary",)),
        name="moe_router",
    )(s_t, b_col)


def _row_copy(src_ref, src_row, dst_ref, dst_row, sem):
    return pltpu.make_async_copy(src_ref.at[pl.ds(src_row, 1), :], dst_ref.at[pl.ds(dst_row, 1), :], sem)


def _dispatch_body(dest_ref, h_ref, buf_in_ref, buf_ref, sem):
    del buf_in_ref
    tm = h_ref.shape[0]

    def issue(t, carry):
        for k in range(TOP_K):
            _row_copy(h_ref, t, buf_ref, dest_ref[0, k, t], sem).start()
        return carry

    def drain(t, carry):
        for k in range(TOP_K):
            _row_copy(h_ref, 0, buf_ref, 0, sem).wait()
        return carry

    lax.fori_loop(0, tm, issue, 0)
    lax.fori_loop(0, tm, drain, 0)


def moe_dispatch(h2, dest3, buf0):
    n_rows, d = h2.shape
    tm = ROW_TILE
    return pl.pallas_call(
        _dispatch_body,
        grid=(n_rows // tm,),
        in_specs=[pl.BlockSpec((1, TOP_K, tm), lambda i: (i, 0, 0), memory_space=pltpu.SMEM),
                  pl.BlockSpec((tm, d), lambda i: (i, 0)),
                  pl.BlockSpec(memory_space=pl.ANY)],
        out_specs=pl.BlockSpec(memory_space=pl.ANY),
        out_shape=jax.ShapeDtypeStruct(buf0.shape, buf0.dtype),
        scratch_shapes=[pltpu.SemaphoreType.DMA(())],
        input_output_aliases={2: 0},
        compiler_params=_cp(("arbitrary",)),
        name="moe_dispatch",
    )(dest3, h2, buf0)


def _expert_body(be_ref, x_ref, w1_ref, w3_ref, w2_ref, o_ref, w1_sc, w3_sc, w2_sc):
    i = pl.program_id(0)
    changed = jnp.logical_or(i == 0, be_ref[i] != be_ref[jnp.maximum(i - 1, 0)])

    @pl.when(changed)
    def _():
        w1_sc[...] = w1_ref[0].astype(BF16)
        w3_sc[...] = w3_ref[0].astype(BF16)
        w2_sc[...] = w2_ref[0].astype(BF16)

    x = x_ref[...].astype(BF16)
    a = _dot(x, w1_sc[...])
    mid = (a * jax.nn.sigmoid(a)) * _dot(x, w3_sc[...])
    o_ref[...] = _dot(mid.astype(BF16), w2_sc[...])


def moe_experts(buf, blk_expert, w1, w3, w2, blk):
    n_slots, d = buf.shape
    de = w1.shape[2]
    grid_spec = pltpu.PrefetchScalarGridSpec(
        num_scalar_prefetch=1,
        grid=(n_slots // blk,),
        in_specs=[pl.BlockSpec((blk, d), lambda i, be: (i, 0)),
                  pl.BlockSpec((1, d, de), lambda i, be: (be[i], 0, 0)),
                  pl.BlockSpec((1, d, de), lambda i, be: (be[i], 0, 0)),
                  pl.BlockSpec((1, de, d), lambda i, be: (be[i], 0, 0))],
        out_specs=pl.BlockSpec((blk, d), lambda i, be: (i, 0)),
        scratch_shapes=[pltpu.VMEM((d, de), BF16), pltpu.VMEM((d, de), BF16), pltpu.VMEM((de, d), BF16)],
    )
    return pl.pallas_call(
        _expert_body,
        grid_spec=grid_spec,
        out_shape=jax.ShapeDtypeStruct((n_slots, d), F32),
        compiler_params=_cp(("arbitrary",), VMEM_LIMIT),
        name="moe_experts",
    )(blk_expert, buf, w1, w3, w2)


def _combine_body(final, dest_ref, x_ref, w_ref, mod_ref, gf_ref, y_hbm, o_ref, y0_sc, y1_sc, sem):
    tm = x_ref.shape[0]
    bufs = (y0_sc, y1_sc)

    def issue(t, carry):
        for k in range(TOP_K):
            _row_copy(y_hbm, dest_ref[0, k, t], bufs[k], t, sem).start()
        return carry

    def drain(t, carry):
        for k in range(TOP_K):
            _row_copy(y_hbm, 0, bufs[k], 0, sem).wait()
        return carry

    lax.fori_loop(0, tm, issue, 0)
    lax.fori_loop(0, tm, drain, 0)
    w = w_ref[...]
    f = w[:, 0:1] * y0_sc[...] + w[:, 1:2] * y1_sc[...]
    x_new = x_ref[...] + mod_ref[0][5:6] * f
    if final:
        ms = jnp.mean(x_new * x_new, axis=-1, keepdims=True)
        x_new = x_new * lax.rsqrt(ms + EPS) * gf_ref[...]
    o_ref[...] = x_new


def moe_combine(dest3, x_rows, wts, modtab, g_final, y_slots, n_lat_rows, n_batch, final):
    n_rows, d = x_rows.shape
    tm = ROW_TILE
    per_b = n_lat_rows // n_batch // tm
    return pl.pallas_call(
        functools.partial(_combine_body, final),
        grid=(n_rows // tm,),
        in_specs=[pl.BlockSpec((1, TOP_K, tm), lambda i: (i, 0, 0), memory_space=pltpu.SMEM),
                  pl.BlockSpec((tm, d), lambda i: (i, 0)),
                  pl.BlockSpec((tm, 8), lambda i: (i, 0)),
                  pl.BlockSpec((1, N_MOD, d), lambda i: (jnp.minimum(i // per_b, n_batch), 0, 0)),
                  pl.BlockSpec((1, d), lambda i: (0, 0)),
                  pl.BlockSpec(memory_space=pl.ANY)],
        out_specs=pl.BlockSpec((tm, d), lambda i: (i, 0)),
        out_shape=jax.ShapeDtypeStruct((n_rows, d), F32),
        scratch_shapes=[pltpu.VMEM((tm, d), F32), pltpu.VMEM((tm, d), F32), pltpu.SemaphoreType.DMA(())],
        compiler_params=_cp(("arbitrary",)),
        name="moe_combine",
    )(dest3, x_rows, wts, modtab, g_final.reshape(1, d), y_slots)


def moe_layer(x_rows, h2, s_t, b_router, w1, w3, w2, modtab, g_final, n_lat_rows, n_batch, final):
    n_rows, d = h2.shape
    blk = 2 * MOE_BLOCK
    experts, wts, rank, counts = route(s_t, b_router)
    cnt = counts[:, 0]
    padded = (cnt + blk - 1) // blk * blk
    pend = jnp.cumsum(padded)
    pstart = pend - padded
    dest = pstart[experts] + rank
    n_blocks = -(-(n_rows * TOP_K) // blk) + N_EXPERTS
    blk_start = jnp.arange(n_blocks, dtype=I32) * blk
    blk_expert = jnp.minimum(jnp.sum((pend[None, :] <= blk_start[:, None]).astype(I32), axis=1), N_EXPERTS - 1)
    dest3 = dest.reshape(TOP_K, n_rows // ROW_TILE, ROW_TILE).transpose(1, 0, 2)
    buf = moe_dispatch(h2, dest3, jnp.zeros((n_blocks * blk, d), F32))
    y_slots = moe_experts(buf, blk_expert, w1, w3, w2, blk)
    return moe_combine(dest3, x_rows, wts, modtab, g_final, y_slots, n_lat_rows, n_batch, final)


_COL = dict(na_q=0, na_k=512, na_v=1024, ml_q=1536, ml_k=2048, ml_v=2560, ml_o=3072,
            gq_q=3584, gq_k=4096, gq_v=4224)


def _split_w_in(w_in):
    sizes = (BRANCH_WIDTH,) * 8 + (4 * ML_HEADS, BRANCH_WIDTH, GQ_KV_HEADS * GQ_HEAD_DIM, GQ_KV_HEADS * GQ_HEAD_DIM)
    idx = np.cumsum(sizes)[:-1].tolist()
    (s5_u, na_q, na_k, na_v, ml_q, ml_k, ml_v, ml_o, ml_gt, gq_q, gq_k, gq_v) = jnp.split(w_in, idx, axis=-1)
    wa = jnp.pad(ml_gt, ((0, 0), (0, LANES - 4 * ML_HEADS)))
    wb = jnp.concatenate([na_q, na_k, na_v, ml_q, ml_k * (ML_HEAD_DIM ** -0.5), ml_v, ml_o, gq_q, gq_k, gq_v], axis=1)
    return s5_u.T.astype(BF16), wa.astype(BF16), wb.astype(BF16)


def kernel(x, c, ctx, c_ctx, w_mod, b_mod, g_norm1, g_norm2, w_in, s5_lam_re, s5_lam_im, s5_log_dt, s5_b_re,
           s5_b_im, s5_c_re, s5_c_im, s5_d, s5_w_glu, s5_b_glu, na_rpb, ml_b_gates, ml_norm, gq_qnorm, gq_knorm,
           w_branch, w_gate, b_gate, w_out, w_router, b_router, moe_w1, moe_w3, moe_w2, g_final):
    b, n_lat, dm = x.shape
    n_ctx = ctx.shape[1]
    depth = w_in.shape[0]
    bn, bc = b * n_lat, b * n_ctx
    x_all = jnp.concatenate([x.reshape(bn, dm), ctx.reshape(bc, dm)], axis=0).astype(F32)
    c_all = jnp.concatenate([c.astype(F32), c_ctx.astype(F32)[None], jnp.zeros((8 - b - 1, dm), F32)], axis=0)
    cs_tab = rope_tables(n_lat)
    wr_t = jnp.pad(w_router.astype(BF16).T, ((0, LANES - N_EXPERTS), (0, 0)))
    out = None
    for l in range(depth):
        last = l == depth - 1
        with_ctx = not last
        modtab = mod_vectors(c_all, w_mod[l], b_mod[l])[:b + 1].reshape(b + 1, N_MOD, dm)
        w_u_t, wa, wb = _split_w_in(w_in[l])
        pa, pb, gate = in_projection(x_all, g_norm1[l], modtab, wa, wb, w_gate[l].astype(BF16), b_gate[l], bn, b)

        tables = s5_tables(s5_lam_re[l], s5_lam_im[l], s5_log_dt[l], s5_b_re[l], s5_b_im[l],
                           s5_c_re[l], s5_c_im[l], s5_d[l])
        ya = s5_mixer(x_all, g_norm1[l], modtab, w_u_t, tables, s5_w_glu[l].astype(BF16),
                      s5_b_glu[l].astype(F32).reshape(1, -1), n_lat, n_ctx, b)

        bias_tab = na_bias_tables(na_rpb[l], n_lat // GRID_W, n_ctx)
        yb = na_mixer(pb, _COL['na_q'], _COL['na_k'], _COL['na_v'], bias_tab, n_lat, n_ctx, b)

        ml_bias = jnp.pad(ml_b_gates[l].astype(F32), (0, LANES - 4 * ML_HEADS)).reshape(1, LANES)
        hf, hb = mlstm_mixer(pb, pa, _COL['ml_q'], _COL['ml_k'], _COL['ml_v'], 0, ml_bias, n_lat, n_ctx, b)

        yd = gq_mixer(pb, _COL['gq_q'], _COL['gq_k'], _COL['gq_v'], cs_tab, gq_qnorm[l], gq_knorm[l],
                      n_lat, n_ctx, b, with_ctx)
        if with_ctx:
            yb = jnp.concatenate([yb, na_ctx_attention(pb, _COL['na_q'], _COL['na_k'], _COL['na_v'],
                                                       n_lat, n_ctx, b)], axis=0)
        n_rows = bn + bc if with_ctx else bn
        x_mid, h2, s_t = merge_layer(
            ya, yb, hf, hb, pb, _COL['ml_o'], yd, gate, x_all, modtab, ml_norm[l].astype(F32).reshape(1, -1),
            w_branch[l].astype(BF16), w_out[l].astype(BF16), g_norm2[l].astype(F32).reshape(1, -1), wr_t,
            n_rows, bn, b)
        x_next = moe_layer(x_mid, h2, s_t, b_router, moe_w1[l], moe_w3[l], moe_w2[l], modtab, g_final,
                           bn, b, last)
        if last:
            out = x_next.reshape(b, n_lat, dm).astype(x.dtype)
        else:
            x_all = x_next
    return out
```

```python
import functools
import math

import numpy as np
import jax
import jax.numpy as jnp
from jax import lax
from jax.experimental import pallas as pl
from jax.experimental.pallas import tpu as pltpu

F32 = jnp.float32
BF16 = jnp.bfloat16
I32 = jnp.int32

GRID_W = 64
N_MOD = 6
BRANCH_WIDTH = 512
N_BRANCHES = 4
S5_GROUP = 16
S5_GROUPS = BRANCH_WIDTH // S5_GROUP
S5_STATE = 64
NA_HEADS = 8
NA_HEAD_DIM = 64
NA_ROWS = 8
NA_COLS = 16
ML_HEADS = 4
ML_HEAD_DIM = 128
ML_CHUNK = 128
GQ_HEADS = 8
GQ_KV_HEADS = 2
GQ_HEAD_DIM = 64
ROPE_THETA = 10000.0
N_EXPERTS = 32
N_EXPERT_GROUPS = 8
EXPERTS_PER_GROUP = 4
TOP_K = 2
D_EXPERT = 512
MOE_BLOCK = 128
EPS = 1e-6
NEG_INIT = -1e30
MASK_NEG = -1e30
LOG2E = 1.4426950408889634

LANES = 128
ROW_TILE = 256
S5_CHUNK = 32
NA_TILE_ROWS = 4
VMEM_LIMIT = 56 * 1024 * 1024

HIGHEST = lax.Precision.HIGHEST


def _cp(sem, vmem=None):
    return pltpu.CompilerParams(dimension_semantics=sem, vmem_limit_bytes=vmem)


def _dot(a, b):
    return jnp.dot(a, b, preferred_element_type=F32)


def _dot_nt(a, b):
    return lax.dot_general(a, b, (((1,), (1,)), ((), ())), preferred_element_type=F32)


def _dot_tn(a, b):
    return lax.dot_general(a, b, (((0,), (0,)), ((), ())), preferred_element_type=F32)


def _rms_mod(x, g, shift, scale):
    ms = jnp.mean(x * x, axis=-1, keepdims=True)
    y = x * lax.rsqrt(ms + EPS) * g
    return y * (1.0 + scale) + shift


def _round_up(n, m):
    return -(-n // m) * m


def _mod_body(c_ref, w_ref, b_ref, o_ref):
    c = c_ref[...]
    a = (c * jax.nn.sigmoid(c)).astype(BF16)
    o_ref[...] = _dot(a, w_ref[...].astype(BF16)) + b_ref[...]


def mod_vectors(c_all, w_mod, b_mod):
    d = c_all.shape[1]
    return pl.pallas_call(
        _mod_body,
        grid=(N_MOD,),
        in_specs=[pl.BlockSpec((8, d), lambda j: (0, 0)),
                  pl.BlockSpec((d, d), lambda j: (0, j)),
                  pl.BlockSpec((1, d), lambda j: (0, j))],
        out_specs=pl.BlockSpec((8, d), lambda j: (0, j)),
        out_shape=jax.ShapeDtypeStruct((8, N_MOD * d), F32),
        compiler_params=_cp(("arbitrary",)),
        name="mod_vectors",
    )(c_all, w_mod, b_mod.reshape(1, -1))


def _inproj_body(x_ref, g_ref, mod_ref, wa_ref, wb_ref, wg_ref, bg_ref, oa_ref, ob_ref, og_ref):
    mod = mod_ref[0]
    h = _rms_mod(x_ref[...], g_ref[...], mod[0:1], mod[1:2]).astype(BF16)
    oa_ref[...] = _dot(h, wa_ref[...])
    ob_ref[...] = _dot(h, wb_ref[...]).astype(BF16)
    og_ref[...] = jax.nn.sigmoid(_dot(h, wg_ref[...]) + bg_ref[...]).astype(BF16)


def in_projection(x_all, g, modtab, wa, wb, wg, bg, n_lat_rows, n_batch):
    r, d = x_all.shape
    tm = ROW_TILE
    per_b = n_lat_rows // n_batch // tm

    def mod_idx(i):
        return (jnp.minimum(i // per_b, n_batch), 0, 0)

    const = lambda i: (0, 0)
    return pl.pallas_call(
        _inproj_body,
        grid=(r // tm,),
        in_specs=[pl.BlockSpec((tm, d), lambda i: (i, 0)),
                  pl.BlockSpec((1, d), const),
                  pl.BlockSpec((1, N_MOD, d), mod_idx),
                  pl.BlockSpec(wa.shape, const, pipeline_mode=pl.Buffered(1)),
                  pl.BlockSpec(wb.shape, const, pipeline_mode=pl.Buffered(1)),
                  pl.BlockSpec(wg.shape, const, pipeline_mode=pl.Buffered(1)),
                  pl.BlockSpec((1, wg.shape[1]), const)],
        out_specs=[pl.BlockSpec((tm, wa.shape[1]), lambda i: (i, 0)),
                   pl.BlockSpec((tm, wb.shape[1]), lambda i: (i, 0)),
                   pl.BlockSpec((tm, wg.shape[1]), lambda i: (i, 0))],
        out_shape=[jax.ShapeDtypeStruct((r, wa.shape[1]), F32),
                   jax.ShapeDtypeStruct((r, wb.shape[1]), BF16),
                   jax.ShapeDtypeStruct((r, wg.shape[1]), BF16)],
        compiler_params=_cp(("arbitrary",), VMEM_LIMIT),
        name="in_projection",
    )(x_all, g.reshape(1, d), modtab, wa, wb, wg, bg.reshape(1, -1))


def s5_tables(lam_re, lam_im, log_dt, b_re, b_im, c_re, c_im, d_skip):
    ell, g_n, p_n, c_n = S5_CHUNK, S5_GROUPS, S5_STATE, S5_GROUP
    lam_re, lam_im = lam_re.astype(F32), lam_im.astype(F32)
    b_re, b_im, c_re, c_im = (t.astype(F32) for t in (b_re, b_im, c_re, c_im))
    dt = jnp.exp(log_dt.astype(F32))[..., None]
    mag = jnp.exp(lam_re * dt)
    a_re = mag * jnp.cos(lam_im * dt)
    a_im = mag * jnp.sin(lam_im * dt)
    den = lam_re * lam_re + lam_im * lam_im
    nr = a_re - 1.0
    f_re = (nr * lam_re + a_im * lam_im) / den
    f_im = (a_im * lam_re - nr * lam_im) / den
    bb_re = f_re[..., None] * b_re - f_im[..., None] * b_im
    bb_im = f_re[..., None] * b_im + f_im[..., None] * b_re
    k = jnp.arange(ell + 1, dtype=F32)
    pmag = jnp.exp((lam_re * dt)[..., None] * k)
    ang = (lam_im * dt)[..., None] * k
    pr, pi = pmag * jnp.cos(ang), pmag * jnp.sin(ang)
    ab_re = pr[..., None] * bb_re[:, :, :, None, :] - pi[..., None] * bb_im[:, :, :, None, :]
    ab_im = pr[..., None] * bb_im[:, :, :, None, :] + pi[..., None] * bb_re[:, :, :, None, :]
    kk = (jnp.einsum('dgcp,dgpke->dgcke', c_re, ab_re, precision=HIGHEST)
          - jnp.einsum('dgcp,dgpke->dgcke', c_im, ab_im, precision=HIGHEST))
    centre = kk[0][:, :, 0] + kk[1][:, :, 0] + d_skip.astype(F32).reshape(g_n, c_n, 1) * jnp.eye(c_n, dtype=F32)
    w = jnp.concatenate([kk[0][:, :, ell - 1:0:-1], centre[:, :, None], kk[1][:, :, 1:ell]], axis=2)
    wf = w.reshape(g_n, c_n, (2 * ell - 1) * c_n)
    toe = jnp.stack([wf[:, :, (ell - 1 - t) * c_n:(2 * ell - 1 - t) * c_n] for t in range(ell)], axis=1)
    tsum_t = toe.astype(BF16).reshape(g_n, ell * c_n, ell * c_n)

    parity = jnp.asarray(np.arange(g_n)[:, None] % 2 == np.arange(2)[None, :], F32)
    parts = []
    for d in range(2):
        for ab in (ab_re, ab_im):
            sel = ab[d][:, :, :ell]
            if d == 0:
                sel = sel[:, :, ::-1]
            parts.append(sel.reshape(g_n, p_n, ell * c_n))
    x = jnp.stack(parts, axis=1)
    mend_t = (x[:, :, None] * parity[:, None, :, None, None]).reshape(g_n, 8 * p_n, ell * c_n)

    rows = []
    for d in range(2):
        prk, pik = pr[d][:, :, 1:ell + 1], pi[d][:, :, 1:ell + 1]
        if d == 1:
            prk, pik = prk[:, :, ::-1], pik[:, :, ::-1]
        prk = prk.transpose(0, 2, 1)[:, :, None, :]
        pik = pik.transpose(0, 2, 1)[:, :, None, :]
        cr, ci = c_re[d][:, None], c_im[d][:, None]
        rows += [cr * prk - ci * pik, -cr * pik - ci * prk]
    w4 = jnp.stack(rows, axis=3)
    wst_t = (w4[:, :, :, :, None, :] * parity[:, None, None, None, :, None]).reshape(g_n, ell * c_n, 8 * p_n)

    al = jnp.stack([pr[0][:, :, ell], pi[0][:, :, ell], pr[1][:, :, ell], pi[1][:, :, ell]], axis=1)
    a_chunk = al.reshape(g_n // 2, 2, 4, p_n).transpose(0, 2, 1, 3).reshape(g_n // 2, 8 * p_n)
    return tsum_t.astype(BF16), mend_t.astype(BF16), wst_t.astype(BF16), a_chunk


def _s5_proj_body(n_batch, xl_ref, xc_ref, g_ref, mod_ref, w_ref, o_ref):
    n_lat = xl_ref.shape[0]
    per_b = n_lat // n_batch
    n_pad = o_ref.shape[1] - LANES
    g = g_ref[...]
    parts = [_rms_mod(xl_ref[b * per_b:(b + 1) * per_b, :], g, mod_ref[b, 0:1], mod_ref[b, 1:2]).astype(BF16)
             for b in range(n_batch)]
    if n_pad > n_lat:
        parts.append(jnp.zeros((n_pad - n_lat, xl_ref.shape[1]), BF16))
    o_ref[:, 0:n_pad] = _dot_nt(w_ref[...], jnp.concatenate(parts, axis=0)).astype(BF16)
    hc = _rms_mod(xc_ref[...], g, mod_ref[n_batch, 0:1], mod_ref[n_batch, 1:2]).astype(BF16)
    hc = jnp.concatenate([hc, jnp.zeros((LANES - hc.shape[0], hc.shape[1]), BF16)], axis=0)
    o_ref[:, n_pad:] = _dot_nt(w_ref[...], hc).astype(BF16)


def _s5_end_body(u_ref, m_ref, o_ref):
    width = m_ref.shape[1]
    acc = None
    for q in range(2):
        u = u_ref[:, q].reshape(width, u_ref.shape[3])
        term = _dot(m_ref[q], u)
        acc = term if acc is None else acc + term
    o_ref[...] = acc.T


def _s5_scan_body(n_batch, per_b, n_ctx_chunks, lat_pad, e_ref, a_ref, o_ref):
    n_pairs = e_ref.shape[0]
    o_ref[...] = jnp.zeros(o_ref.shape, F32)
    n_steps = per_b + n_ctx_chunks
    first = pl.program_id(0) * n_pairs
    coef = [[a_ref[first + q, j:j + 1, :] for j in range(4)] for q in range(n_pairs)]

    def step(i, carry):
        new = []
        for b in range(n_batch):
            fwd = jnp.where(i < n_ctx_chunks, lat_pad + b * n_ctx_chunks + i, b * per_b + i - n_ctx_chunks)
            bwd = jnp.where(i < n_ctx_chunks, lat_pad + b * n_ctx_chunks + n_ctx_chunks - 1 - i,
                            b * per_b + per_b - 1 - (i - n_ctx_chunks))
            for q in range(n_pairs):
                for d, row in enumerate((fwd, bwd)):
                    sr, si = carry[((b * n_pairs + q) * 2 + d) * 2:((b * n_pairs + q) * 2 + d) * 2 + 2]
                    ar, ai = coef[q][2 * d], coef[q][2 * d + 1]
                    o_ref[q, row, 2 * d:2 * d + 1, :] = sr
                    o_ref[q, row, 2 * d + 1:2 * d + 2, :] = si
                    er = e_ref[q, row, 2 * d:2 * d + 1, :]
                    ei = e_ref[q, row, 2 * d + 1:2 * d + 2, :]
                    new += [ar * sr - ai * si + er, ar * si + ai * sr + ei]
        return tuple(new)

    z = jnp.zeros((1, LANES), F32)
    lax.fori_loop(0, n_steps, step, tuple(z for _ in range(n_batch * n_pairs * 4)))


def _s5_out_body(u_ref, t_ref, s_ref, w_ref, o_ref):
    width = t_ref.shape[0]
    u = u_ref[...].reshape(width, u_ref.shape[2])
    y = _dot(t_ref[...], u) + _dot_nt(w_ref[...], s_ref[...].astype(BF16))
    o_ref[...] = y.reshape(o_ref.shape)


def _s5_glu_body(n_lat_chunks, n_ctx_chunks, lat_pad, y_ref, w_ref, b_ref, o_ref):
    y = y_ref[...].T
    if lat_pad == n_lat_chunks:
        y = y[:n_lat_chunks + n_ctx_chunks]
    else:
        y = jnp.concatenate([y[:n_lat_chunks], y[lat_pad:lat_pad + n_ctx_chunks]], axis=0)
    z = jax.nn.gelu(y)
    o_ref[...] = (z * jax.nn.sigmoid(_dot(z.astype(BF16), w_ref[...]) + b_ref[...])).astype(o_ref.dtype)


def s5_mixer(x_all, g, modtab, w_u_t, tables, w_glu, b_glu, n_lat, n_ctx, n_batch):
    tsum_t, mend_t, wst_t, a_chunk = tables
    r, d = x_all.shape
    ell, g_n, c_n = S5_CHUNK, S5_GROUPS, S5_GROUP
    width = ell * c_n
    bw = g_n * c_n
    n_lat_chunks = n_batch * n_lat // ell
    n_ctx_chunks = n_batch * n_ctx // ell
    assert n_lat_chunks % n_ctx_chunks == 0 and n_ctx_chunks % 16 == 0 and n_ctx_chunks <= LANES
    lat_pad = _round_up(n_lat_chunks, LANES)
    nch = lat_pad + LANES
    x2 = x_all.reshape(r // ell, ell * d)
    const = lambda t: (0, 0)
    u_t = pl.pallas_call(
        functools.partial(_s5_proj_body, n_batch),
        grid=(ell,),
        in_specs=[pl.BlockSpec((n_lat_chunks, d), lambda t: (0, t)),
                  pl.BlockSpec((n_ctx_chunks, d), lambda t: (n_lat_chunks // n_ctx_chunks, t)),
                  pl.BlockSpec((1, d), const),
                  pl.BlockSpec(modtab.shape, lambda t: (0, 0, 0)),
                  pl.BlockSpec((bw, d), const)],
        out_specs=pl.BlockSpec((None, bw, nch), lambda t: (t, 0, 0)),
        out_shape=jax.ShapeDtypeStruct((ell, bw, nch), BF16),
        compiler_params=_cp(("arbitrary",)),
        name="s5_projection",
    )(x2, x2, g.reshape(1, d), modtab, w_u_t)
    u4 = u_t.reshape(ell, g_n, c_n, nch)
    ends = pl.pallas_call(
        _s5_end_body,
        grid=(g_n // 2,),
        in_specs=[pl.BlockSpec((ell, 2, c_n, nch), lambda p: (0, p, 0, 0)),
                  pl.BlockSpec((2, width, width), lambda p: (p, 0, 0))],
        out_specs=pl.BlockSpec((None, nch, width), lambda p: (p, 0, 0)),
        out_shape=jax.ShapeDtypeStruct((g_n // 2, nch, width), F32),
        compiler_params=_cp(("arbitrary",)),
        name="s5_chunk_ends",
    )(u4, mend_t)
    pairs_per_step = 2
    quad = (g_n // 2, nch, 4, LANES)
    states = pl.pallas_call(
        functools.partial(_s5_scan_body, n_batch, n_lat // ell, n_ctx // ell, lat_pad),
        grid=(g_n // 2 // pairs_per_step,),
        in_specs=[pl.BlockSpec((pairs_per_step, nch, 4, LANES), lambda j: (j, 0, 0, 0)),
                  pl.BlockSpec((g_n // 2, 4, LANES), lambda j: (0, 0, 0))],
        out_specs=pl.BlockSpec((pairs_per_step, nch, 4, LANES), lambda j: (j, 0, 0, 0)),
        out_shape=jax.ShapeDtypeStruct(quad, F32),
        compiler_params=_cp(("arbitrary",), VMEM_LIMIT),
        name="s5_state_scan",
    )(ends.reshape(quad), a_chunk.reshape(g_n // 2, 4, LANES)).reshape(g_n // 2, nch, width)
    y_t = pl.pallas_call(
        _s5_out_body,
        grid=(g_n,),
        in_specs=[pl.BlockSpec((ell, None, c_n, nch), lambda gi: (0, gi, 0, 0)),
                  pl.BlockSpec((None, width, width), lambda gi: (gi, 0, 0)),
                  pl.BlockSpec((None, nch, width), lambda gi: (gi // 2, 0, 0)),
                  pl.BlockSpec((None, width, width), lambda gi: (gi, 0, 0))],
        out_specs=pl.BlockSpec((ell, None, c_n, nch), lambda gi: (0, gi, 0, 0)),
        out_shape=jax.ShapeDtypeStruct((ell, g_n, c_n, nch), F32),
        compiler_params=_cp(("arbitrary",)),
        name="s5_outputs",
    )(u4, tsum_t, states, wst_t)
    n_chunks = n_lat_chunks + n_ctx_chunks
    ya = pl.pallas_call(
        functools.partial(_s5_glu_body, n_lat_chunks, n_ctx_chunks, lat_pad),
        grid=(ell,),
        in_specs=[pl.BlockSpec((None, bw, nch), lambda t: (t, 0, 0)),
                  pl.BlockSpec((bw, bw), const),
                  pl.BlockSpec((1, bw), const)],
        out_specs=pl.BlockSpec((n_chunks, bw), lambda t: (0, t)),
        out_shape=jax.ShapeDtypeStruct((n_chunks, ell * bw), BF16),
        compiler_params=_cp(("arbitrary",)),
        name="s5_glu",
    )(y_t.reshape(ell, bw, nch), w_glu, b_glu)
    return ya.reshape(r, bw)


def rope_tables(n_lat):
    half = GQ_HEAD_DIM // 2
    quarter = half // 2
    t = np.arange(n_lat)
    freqs = ROPE_THETA ** (-np.arange(quarter, dtype=np.float64) / quarter)
    ang_r = (t // GRID_W)[:, None] * freqs
    ang_c = (t % GRID_W)[:, None] * freqs
    ang = np.concatenate([ang_r, ang_r, ang_c, ang_c], axis=1)
    sign = np.concatenate([-np.ones(quarter), np.ones(quarter)] * 2)
    cos = np.concatenate([np.cos(ang), np.ones((ROW_TILE, GQ_HEAD_DIM))], axis=0)
    sin = np.concatenate([np.sin(ang) * sign, np.zeros((ROW_TILE, GQ_HEAD_DIM))], axis=0)
    tab = np.concatenate([cos, cos, sin, sin], axis=1)
    return jnp.asarray(tab, F32)


def _group_ones(width, group):
    i = np.arange(width)
    return jnp.asarray((i[:, None] // group) == (i[None, :] // group), BF16)


def _group_mean_sq(x, ones_blk, group):
    sq = x * x
    hi = sq.astype(BF16)
    lo = (sq - hi.astype(F32)).astype(BF16)
    return (_dot(hi, ones_blk) + _dot(lo, ones_blk)) * (1.0 / group)


def _rope(x, cos, sin):
    w = x.shape[-1]
    q = GQ_HEAD_DIM // 4
    lane = lax.broadcasted_iota(I32, x.shape, 1)
    first = (lane % (2 * q)) < q
    partner = jnp.where(first, pltpu.roll(x, w - q, 1), pltpu.roll(x, q, 1))
    return x * cos + partner * sin


def _gq_prep_body(q_ref, k_ref, v_ref, cs_ref, gq_ref, gk_ref, oq_ref, ok_ref, qm_ref, kr_ref, va_ref):
    cs = cs_ref[...]
    cos1, sin1 = cs[:, :LANES], cs[:, LANES:]
    q = q_ref[...].astype(F32)
    qn = q * lax.rsqrt(_group_mean_sq(q, oq_ref[...], GQ_HEAD_DIM) + EPS) * gq_ref[...]
    n_pairs = q.shape[1] // LANES
    qr = _rope(qn, jnp.concatenate([cos1] * n_pairs, axis=1), jnp.concatenate([sin1] * n_pairs, axis=1))
    qr = (qr * (GQ_HEAD_DIM ** -0.5 * LOG2E)).astype(BF16)
    lane = lax.broadcasted_iota(I32, (q.shape[0], LANES), 1)
    heads_per_kv = GQ_HEADS // GQ_KV_HEADS
    for h in range(GQ_HEADS):
        pair = qr[:, (h // 2) * LANES:(h // 2 + 1) * LANES]
        kv = h // heads_per_kv
        if h % 2 != kv:
            pair = pltpu.roll(pair, GQ_HEAD_DIM, 1)
        keep = (lane >= kv * GQ_HEAD_DIM) & (lane < (kv + 1) * GQ_HEAD_DIM)
        qm_ref[h] = jnp.where(keep, pair, jnp.zeros_like(pair))
    k = k_ref[...].astype(F32)
    kn = k * lax.rsqrt(_group_mean_sq(k, ok_ref[...], GQ_HEAD_DIM) + EPS) * gk_ref[...]
    kr_ref[...] = _rope(kn, cos1, sin1).astype(BF16)
    va_ref[...] = jnp.concatenate([v_ref[...], jnp.ones(v_ref.shape, BF16)], axis=1)


def gq_prepare(pb, col_q, col_k, col_v, cs_tab, g_q, g_k, n_lat, n_ctx, n_batch):
    r = pb.shape[0]
    tm = ROW_TILE
    assert n_ctx == tm and n_lat % tm == 0
    nb = n_lat // tm
    n_lat_tiles = n_batch * nb

    def tab_idx(i):
        return (jnp.where(i < n_lat_tiles, i % nb, nb), 0)

    def kv_idx(i):
        lat = (i // nb) * (nb + 1) + i % nb
        ctx = (i - n_lat_tiles) * (nb + 1) + nb
        return (jnp.where(i < n_lat_tiles, lat, ctx), 0)

    qw = GQ_HEADS * GQ_HEAD_DIM
    const = lambda i: (0, 0)
    gq = jnp.tile(g_q.astype(F32), GQ_HEADS).reshape(1, qw)
    gk = jnp.tile(g_k.astype(F32), GQ_KV_HEADS).reshape(1, LANES)
    n_keys = n_batch * (n_lat + n_ctx)
    return pl.pallas_call(
        _gq_prep_body,
        grid=(r // tm,),
        in_specs=[pl.BlockSpec((tm, qw), lambda i: (i, col_q // qw)),
                  pl.BlockSpec((tm, LANES), lambda i: (i, col_k // LANES)),
                  pl.BlockSpec((tm, LANES), lambda i: (i, col_v // LANES)),
                  pl.BlockSpec((tm, 2 * LANES), tab_idx),
                  pl.BlockSpec((1, qw), const),
                  pl.BlockSpec((1, LANES), const),
                  pl.BlockSpec((qw, qw), const),
                  pl.BlockSpec((LANES, LANES), const)],
        out_specs=[pl.BlockSpec((GQ_HEADS, tm, LANES), lambda i: (0, i, 0)),
                   pl.BlockSpec((tm, LANES), kv_idx),
                   pl.BlockSpec((tm, 2 * LANES), kv_idx)],
        out_shape=[jax.ShapeDtypeStruct((GQ_HEADS, r, LANES), BF16),
                   jax.ShapeDtypeStruct((n_keys, LANES), BF16),
                   jax.ShapeDtypeStruct((n_keys, 2 * LANES), BF16)],
        compiler_params=_cp(("arbitrary",)),
        name="gq_prepare",
    )(pb, pb, pb, cs_tab, gq, gk, _group_ones(qw, GQ_HEAD_DIM), _group_ones(LANES, GQ_HEAD_DIM))


def _gq_flash_body(q_ref, k_ref, v_ref, o_ref, m_sc, acc_sc):
    kj = pl.program_id(2)
    n_h, tq, _ = q_ref.shape

    @pl.when(kj == 0)
    def _():
        m_sc[...] = jnp.full(m_sc.shape, -jnp.inf, F32)
        acc_sc[...] = jnp.zeros(acc_sc.shape, F32)

    k = k_ref[...]
    v = v_ref[...]
    hp = 1
    for c in range(n_h // hp):
        rows = slice(c * hp * tq, (c + 1) * hp * tq)
        s = _dot_nt(q_ref[c * hp:(c + 1) * hp].reshape(hp * tq, LANES), k)
        m_prev = m_sc[rows, :]
        m_new = jnp.maximum(m_prev, jnp.max(s, axis=-1, keepdims=True))
        p = jnp.exp2(s - m_new)
        acc_sc[rows, :] = jnp.exp2(m_prev - m_new) * acc_sc[rows, :] + _dot(p.astype(BF16), v)
        m_sc[rows, :] = m_new

    @pl.when(kj == pl.num_programs(2) - 1)
    def _():
        lane = lax.broadcasted_iota(I32, (tq, LANES), 1)
        heads_per_kv = n_h // GQ_KV_HEADS
        for j in range(n_h // 2):
            kv = (2 * j) // heads_per_kv
            halves = []
            for h in (2 * j, 2 * j + 1):
                a = acc_sc[h * tq:(h + 1) * tq, :]
                halves.append(a[:, :LANES] / a[:, LANES:LANES + 1])
            lo, hi = halves
            if kv == 0:
                hi = pltpu.roll(hi, GQ_HEAD_DIM, 1)
            else:
                lo = pltpu.roll(lo, GQ_HEAD_DIM, 1)
            o_ref[:, j * LANES:(j + 1) * LANES] = jnp.where(lane < GQ_HEAD_DIM, lo, hi).astype(o_ref.dtype)


def gq_attention(qm, keys, vals, n_rows_out, tq, tk, q_blk, k_blk, n_q, n_k, n_batch):
    n_h = qm.shape[0]
    return pl.pallas_call(
        _gq_flash_body,
        grid=(n_batch, n_q, n_k),
        in_specs=[pl.BlockSpec((n_h, tq, LANES), lambda b, i, j: (0, q_blk(b, i), 0)),
                  pl.BlockSpec((tk, LANES), lambda b, i, j: (k_blk(b, j), 0)),
                  pl.BlockSpec((tk, 2 * LANES), lambda b, i, j: (k_blk(b, j), 0))],
        out_specs=pl.BlockSpec((tq, n_h * GQ_HEAD_DIM), lambda b, i, j: (b * n_q + i, 0)),
        out_shape=jax.ShapeDtypeStruct((n_rows_out, n_h * GQ_HEAD_DIM), BF16),
        scratch_shapes=[pltpu.VMEM((n_h * tq, 1), F32), pltpu.VMEM((n_h * tq, 2 * LANES), F32)],
        compiler_params=_cp(("arbitrary", "arbitrary", "arbitrary"), VMEM_LIMIT),
        name="gq_attention",
    )(qm, keys, vals)


def _largest_divisor(n, cap):
    return max(d for d in range(1, cap + 1) if n % d == 0)


def gq_mixer(pb, col_q, col_k, col_v, cs_tab, g_q, g_k, n_lat, n_ctx, n_batch, with_ctx):
    qm, keys, vals = gq_prepare(pb, col_q, col_k, col_v, cs_tab, g_q, g_k, n_lat, n_ctx, n_batch)
    tq = 256
    tk = LANES * _largest_divisor((n_lat + n_ctx) // LANES, 11)
    n_q = n_lat // tq
    per_b = (n_lat + n_ctx) // tk
    y_lat = gq_attention(qm, keys, vals, n_batch * n_lat, tq, tk,
                         lambda b, i: b * n_q + i, lambda b, j: b * per_b + j, n_q, per_b, n_batch)
    if not with_ctx:
        return y_lat
    tc = n_ctx
    y_ctx = gq_attention(qm, keys, vals, n_batch * n_ctx, tc, tc,
                         lambda b, i: n_batch * n_lat // tc + b,
                         lambda b, j: b * ((n_lat + n_ctx) // tc) + n_lat // tc, 1, 1, n_batch)
    return jnp.concatenate([y_lat, y_ctx], axis=0)


def na_bias_tables(rpb, n_img_rows, n_ctx):
    tr = NA_TILE_ROWS
    nt = n_img_rows // tr
    assert nt >= 4
    kr = min(NA_ROWS, n_img_rows)
    n_heads = rpb.shape[0]
    qcol = np.arange(GRID_W)[:, None]
    kcol = np.arange(GRID_W)[None, :]
    dc = np.clip(kcol - qcol + NA_COLS - 1, 0, 2 * NA_COLS - 2)
    oh_c = (dc[None] == np.arange(2 * NA_COLS - 1)[:, None, None]).astype(np.float32)
    cstart = np.clip(qcol - NA_COLS // 2, 0, GRID_W - NA_COLS)
    col_ok = (kcol >= cstart) & (kcol < cstart + NA_COLS)
    by_col = jnp.einsum('hrd,dqk->hrqk', rpb.astype(F32), jnp.asarray(oh_c), precision=HIGHEST)
    classes = []
    for i in (0, 1, nt - 1):
        wb = int(np.clip(i - 1, 0, nt - 3))
        qrow = (i * tr + np.arange(tr))[:, None]
        krow = (wb * tr + np.arange(3 * tr))[None, :]
        rs = np.clip(qrow - kr // 2, 0, n_img_rows - kr)
        row_ok = (krow >= rs) & (krow < rs + kr)
        dr = np.clip(krow - qrow + NA_ROWS - 1, 0, 2 * NA_ROWS - 2)
        oh_r = ((dr[None] == np.arange(2 * NA_ROWS - 1)[:, None, None]) & row_ok[None]).astype(np.float32)
        bias = jnp.einsum('hrqk,rab->haqbk', by_col, jnp.asarray(oh_r), precision=HIGHEST)
        ok = row_ok[:, None, :, None] & col_ok[None, :, None, :]
        bias = jnp.where(jnp.asarray(ok)[None], bias, MASK_NEG).reshape(n_heads, tr * GRID_W, 3 * tr * GRID_W)
        classes.append(jnp.concatenate([bias, jnp.zeros((n_heads, tr * GRID_W, n_ctx), F32)], axis=-1))
    return jnp.stack(classes, axis=0)


def _pair_attention(q_pair, k_pair, v_pair, bias_fn):
    lane = lax.broadcasted_iota(I32, q_pair.shape, 1)
    out = None
    for hh in range(2):
        mine = (lane >= hh * NA_HEAD_DIM) & (lane < (hh + 1) * NA_HEAD_DIM)
        qm = jnp.where(mine, q_pair, jnp.zeros_like(q_pair))
        s = _dot_nt(qm, k_pair)
        b = bias_fn(hh)
        if b is not None:
            s = s + b
        m = jnp.max(s, axis=-1, keepdims=True)
        p = jnp.exp(s - m)
        l = jnp.sum(p, axis=-1, keepdims=True)
        o = _dot(p.astype(BF16), v_pair) / l
        out = o if out is None else jnp.where(mine, o, out)
    return out


def _na_body(q_ref, k0_ref, k1_ref, k2_ref, kc_ref, v0_ref, v1_ref, v2_ref, vc_ref, b_ref, o_ref):
    scale = NA_HEAD_DIM ** -0.5
    for j in range(NA_HEADS // 2):
        sl = slice(j * LANES, (j + 1) * LANES)
        q_pair = q_ref[:, sl] * scale
        k_pair = jnp.concatenate([k0_ref[:, sl], k1_ref[:, sl], k2_ref[:, sl], kc_ref[:, sl]], axis=0)
        v_pair = jnp.concatenate([v0_ref[:, sl], v1_ref[:, sl], v2_ref[:, sl], vc_ref[:, sl]], axis=0)
        o = _pair_attention(q_pair, k_pair, v_pair, lambda hh: b_ref[0, 2 * j + hh])
        o_ref[:, sl] = o.astype(o_ref.dtype)


def na_mixer(pb, col_q, col_k, col_v, bias_tab, n_lat, n_ctx, n_batch):
    tm = NA_TILE_ROWS * GRID_W
    assert n_ctx == tm
    w = NA_HEADS * NA_HEAD_DIM
    nt = n_lat // tm
    n_keys = 3 * tm + n_ctx
    cq, ck, cv = col_q // w, col_k // w, col_v // w
    ctx0 = n_batch * nt

    def win(o):
        return lambda b, i: (b * nt + jnp.clip(i - 1, 0, nt - 3) + o)

    def cls(b, i):
        return (jnp.where(i == 0, 0, jnp.where(i == nt - 1, 2, 1)), 0, 0, 0)

    kspecs = [pl.BlockSpec((tm, w), (lambda b, i, f=win(o): (f(b, i), ck))) for o in range(3)]
    vspecs = [pl.BlockSpec((tm, w), (lambda b, i, f=win(o): (f(b, i), cv))) for o in range(3)]
    return pl.pallas_call(
        _na_body,
        grid=(n_batch, nt),
        in_specs=[pl.BlockSpec((tm, w), lambda b, i: (b * nt + i, cq))] + kspecs
        + [pl.BlockSpec((tm, w), lambda b, i: (ctx0 + b, ck))] + vspecs
        + [pl.BlockSpec((tm, w), lambda b, i: (ctx0 + b, cv)),
           pl.BlockSpec((1, NA_HEADS, tm, n_keys), cls)],
        out_specs=pl.BlockSpec((tm, w), lambda b, i: (b * nt + i, 0)),
        out_shape=jax.ShapeDtypeStruct((n_batch * n_lat, w), BF16),
        compiler_params=_cp(("arbitrary", "arbitrary"), VMEM_LIMIT),
        name="na_attention",
    )(pb, pb, pb, pb, pb, pb, pb, pb, pb, bias_tab)


def _ctx_mha_body(q_ref, k_ref, v_ref, o_ref):
    scale = NA_HEAD_DIM ** -0.5
    for j in range(NA_HEADS // 2):
        sl = slice(j * LANES, (j + 1) * LANES)
        o = _pair_attention(q_ref[:, sl] * scale, k_ref[:, sl], v_ref[:, sl], lambda hh: None)
        o_ref[:, sl] = o.astype(o_ref.dtype)


def na_ctx_attention(pb, col_q, col_k, col_v, n_lat, n_ctx, n_batch):
    w = NA_HEADS * NA_HEAD_DIM
    ctx0 = n_batch * n_lat // n_ctx
    spec = lambda c: pl.BlockSpec((n_ctx, w), lambda b: (ctx0 + b, c // w))
    return pl.pallas_call(
        _ctx_mha_body,
        grid=(n_batch,),
        in_specs=[spec(col_q), spec(col_k), spec(col_v)],
        out_specs=pl.BlockSpec((n_ctx, w), lambda b: (b, 0)),
        out_shape=jax.ShapeDtypeStruct((n_batch * n_ctx, w), BF16),
        compiler_params=_cp(("arbitrary",)),
        name="na_ctx_attention",
    )(pb, pb, pb)


def _mlstm_chunk(rev, q, k, v, g, lf_cum, g_t, lf_cum_t, ci, cf, c_ref, n_ref, m_ref, idx):
    tok = lax.broadcasted_iota(I32, (ML_CHUNK, ML_CHUNK), 0)
    src = lax.broadcasted_iota(I32, (ML_CHUNK, ML_CHUNK), 1)
    causal = (src >= tok) if rev else (src <= tok)
    bt_col = lf_cum[:, cf:cf + 1]
    bt_row = lf_cum_t[cf:cf + 1, :]
    li_col = g[:, ci:ci + 1]
    li_row = g_t[ci:ci + 1, :]
    m_prev = m_ref[idx][:, 0:1]
    dmat = jnp.where(causal, bt_col - bt_row + li_row, -jnp.inf)
    inter = bt_col + m_prev
    mt = jnp.maximum(inter, jnp.max(dmat, axis=-1, keepdims=True))
    s = _dot_nt(q, k) * jnp.exp(dmat - mt)
    w_inter = jnp.exp(inter - mt)
    c_prev = c_ref[idx]
    n_prev = n_ref[idx]
    num = _dot(s.astype(BF16), v) + w_inter * _dot_nt(q, c_prev.astype(BF16))
    qn = jnp.sum(q.astype(F32) * n_prev, axis=-1, keepdims=True)
    den = jnp.sum(s, axis=-1, keepdims=True) + w_inter * qn
    h_out = num / jnp.maximum(jnp.abs(den), jnp.exp(-mt))
    b_last = bt_col[0:1, :] if rev else bt_col[ML_CHUNK - 1:ML_CHUNK, :]
    g_col = b_last - bt_col + li_col
    m_new = jnp.maximum(b_last + m_prev, jnp.max(g_col, axis=0, keepdims=True))
    wg = jnp.exp(g_col - m_new)
    decay = jnp.exp(b_last + m_prev - m_new)
    c_ref[idx] = decay * c_prev + _dot_tn((wg * v.astype(F32)).astype(BF16), k)
    n_ref[idx] = decay * n_prev + jnp.sum(wg * k.astype(F32), axis=0, keepdims=True)
    m_ref[idx] = jnp.broadcast_to(m_new, (1, LANES))
    return h_out


def _mlstm_body(qf_ref, kf_ref, vf_ref, gf_ref, qb_ref, kb_ref, vb_ref, gb_ref, bias_ref,
                hf_ref, hb_ref, c_ref, n_ref, m_ref):
    @pl.when(pl.program_id(1) == 0)
    def _():
        c_ref[...] = jnp.zeros(c_ref.shape, F32)
        n_ref[...] = jnp.zeros(n_ref.shape, F32)
        m_ref[...] = jnp.full(m_ref.shape, NEG_INIT, F32)

    tok = lax.broadcasted_iota(I32, (ML_CHUNK, ML_CHUNK), 0)
    src = lax.broadcasted_iota(I32, (ML_CHUNK, ML_CHUNK), 1)
    for d, (q_ref, k_ref, v_ref, g_ref, h_ref) in enumerate(
            ((qf_ref, kf_ref, vf_ref, gf_ref, hf_ref), (qb_ref, kb_ref, vb_ref, gb_ref, hb_ref))):
        rev = d == 1
        g = g_ref[...] + bias_ref[...]
        tri = ((src >= tok) if rev else (src <= tok)).astype(F32)
        lf_cum = jnp.dot(tri, jax.nn.log_sigmoid(g), precision=HIGHEST, preferred_element_type=F32)
        g_t = g.T
        lf_cum_t = lf_cum.T
        for h in range(ML_HEADS):
            sl = slice(h * ML_HEAD_DIM, (h + 1) * ML_HEAD_DIM)
            h_out = _mlstm_chunk(rev, q_ref[:, sl], k_ref[:, sl], v_ref[:, sl], g, lf_cum, g_t, lf_cum_t,
                                 d * 2 * ML_HEADS + h, d * 2 * ML_HEADS + ML_HEADS + h,
                                 c_ref, n_ref, m_ref, d * ML_HEADS + h)
            h_ref[:, sl] = h_out


def mlstm_mixer(pb, pa, col_q, col_k, col_v, col_g, bias, n_lat, n_ctx, n_batch):
    r = pb.shape[0]
    w = ML_HEADS * ML_HEAD_DIM
    tc = ML_CHUNK
    nl, nc = n_lat // tc, n_ctx // tc

    def fwd(b, i):
        return jnp.where(i < nc, n_batch * nl + b * nc + i, b * nl + i - nc)

    def bwd(b, i):
        return jnp.where(i < nc, n_batch * nl + b * nc + nc - 1 - i, b * nl + nl - 1 - (i - nc))

    def specs(f):
        return [pl.BlockSpec((tc, w), lambda b, i, c=c: (f(b, i), c // w)) for c in (col_q, col_k, col_v)] + [
            pl.BlockSpec((tc, LANES), lambda b, i: (f(b, i), col_g // LANES))]

    n_st = 2 * ML_HEADS
    return pl.pallas_call(
        _mlstm_body,
        grid=(n_batch, nl + nc),
        in_specs=specs(fwd) + specs(bwd) + [pl.BlockSpec((1, LANES), lambda b, i: (0, 0))],
        out_specs=[pl.BlockSpec((tc, w), lambda b, i: (fwd(b, i), 0)),
                   pl.BlockSpec((tc, w), lambda b, i: (bwd(b, i), 0))],
        out_shape=[jax.ShapeDtypeStruct((r, w), F32), jax.ShapeDtypeStruct((r, w), F32)],
        scratch_shapes=[pltpu.VMEM((n_st, ML_HEAD_DIM, ML_HEAD_DIM), F32),
                        pltpu.VMEM((n_st, 1, ML_HEAD_DIM), F32),
                        pltpu.VMEM((n_st, 1, LANES), F32)],
        compiler_params=_cp(("arbitrary", "arbitrary")),
        name="mlstm_chunks",
    )(pb, pb, pb, pa, pb, pb, pb, pa, bias)


def _merge_body(ya_ref, yb_ref, hf_ref, hb_ref, o_ref, yd_ref, gate_ref, x_ref, mod_ref,
                mlg_ref, wbr_ref, wout_ref, g2_ref, wr_ref, xo_ref, h2_ref, st_ref):
    d = x_ref.shape[1]
    hs = hf_ref[...] + hb_ref[...]
    segs = []
    for h in range(ML_HEADS):
        seg = hs[:, h * ML_HEAD_DIM:(h + 1) * ML_HEAD_DIM]
        segs.append(seg * lax.rsqrt(jnp.mean(seg * seg, axis=-1, keepdims=True) + EPS))
    ym = jnp.concatenate(segs, axis=1) * mlg_ref[...] * jax.nn.sigmoid(o_ref[...].astype(F32))
    ys = (ya_ref[...], yb_ref[...], ym.astype(BF16), yd_ref[...])
    merged = None
    for i in range(N_BRANCHES):
        term = gate_ref[:, i * d:(i + 1) * d].astype(F32) * _dot(ys[i], wbr_ref[i])
        merged = term if merged is None else merged + term
    y = _dot(merged.astype(BF16), wout_ref[...])
    mod = mod_ref[0]
    x_new = x_ref[...] + mod[2:3] * y
    xo_ref[...] = x_new
    h2 = _rms_mod(x_new, g2_ref[...], mod[3:4], mod[4:5])
    h2_ref[...] = h2
    st_ref[...] = jax.nn.sigmoid(_dot_nt(wr_ref[...], h2.astype(BF16)))


def merge_layer(ya, yb, hf, hb, pb, col_o, yd, gate, x_all, modtab, mlg, wbr, wout, g2, wr_t,
                n_rows, n_lat_rows, n_batch):
    d = x_all.shape[1]
    tm = ROW_TILE
    w = BRANCH_WIDTH
    per_b = n_lat_rows // n_batch // tm
    row = lambda i: (i, 0)
    const2 = lambda i: (0, 0)
    return pl.pallas_call(
        _merge_body,
        grid=(n_rows // tm,),
        in_specs=[pl.BlockSpec((tm, w), row), pl.BlockSpec((tm, w), row), pl.BlockSpec((tm, w), row),
                  pl.BlockSpec((tm, w), row), pl.BlockSpec((tm, w), lambda i: (i, col_o // w)),
                  pl.BlockSpec((tm, w), row), pl.BlockSpec((tm, N_BRANCHES * d), row),
                  pl.BlockSpec((tm, d), row),
                  pl.BlockSpec((1, N_MOD, d), lambda i: (jnp.minimum(i // per_b, n_batch), 0, 0)),
                  pl.BlockSpec((1, w), const2),
                  pl.BlockSpec((N_BRANCHES, w, d), lambda i: (0, 0, 0)),
                  pl.BlockSpec((d, d), const2), pl.BlockSpec((1, d), const2),
                  pl.BlockSpec((LANES, d), const2)],
        out_specs=[pl.BlockSpec((tm, d), row), pl.BlockSpec((tm, d), row),
                   pl.BlockSpec((LANES, tm), lambda i: (0, i))],
        out_shape=[jax.ShapeDtypeStruct((n_rows, d), F32), jax.ShapeDtypeStruct((n_rows, d), F32),
                   jax.ShapeDtypeStruct((LANES, n_rows), F32)],
        compiler_params=_cp(("arbitrary",), VMEM_LIMIT),
        name="merge_layer",
    )(ya, yb, hf, hb, pb, yd, gate, x_all, modtab, mlg, wbr, wout, g2, wr_t)


def _router_body(s_ref, b_ref, e_ref, w_ref, rank_ref, cnt_ref, base_sc):
    @pl.when(pl.program_id(0) == 0)
    def _():
        base_sc[...] = jnp.zeros(base_sc.shape, F32)

    tm = s_ref.shape[1]
    s = s_ref[0:N_EXPERTS, :]
    sel = s + b_ref[0:N_EXPERTS, :]
    row = lambda a, e: a[e:e + 1, :]
    best, grp = None, None
    for g in range(N_EXPERT_GROUPS):
        v = [row(sel, EXPERTS_PER_GROUP * g + k) for k in range(EXPERTS_PER_GROUP)]
        gs = None
        for a in range(EXPERTS_PER_GROUP):
            for c in range(a + 1, EXPERTS_PER_GROUP):
                gs = v[a] + v[c] if gs is None else jnp.maximum(gs, v[a] + v[c])
        if best is None:
            best, grp = gs, jnp.zeros((1, tm), I32)
        else:
            better = gs > best
            grp = jnp.where(better, g, grp)
            best = jnp.where(better, gs, best)
    vals, affs = [], []
    for k in range(EXPERTS_PER_GROUP):
        vk, sk = row(sel, k), row(s, k)
        for g in range(1, N_EXPERT_GROUPS):
            hit = grp == g
            vk = jnp.where(hit, row(sel, EXPERTS_PER_GROUP * g + k), vk)
            sk = jnp.where(hit, row(s, EXPERTS_PER_GROUP * g + k), sk)
        vals.append(vk)
        affs.append(sk)
    i1, b1, w1 = jnp.zeros((1, tm), I32), vals[0], affs[0]
    for k in range(1, EXPERTS_PER_GROUP):
        better = vals[k] > b1
        i1 = jnp.where(better, k, i1)
        w1 = jnp.where(better, affs[k], w1)
        b1 = jnp.where(better, vals[k], b1)
    i2 = jnp.zeros((1, tm), I32)
    b2 = jnp.full((1, tm), -jnp.inf, F32)
    w2 = jnp.zeros((1, tm), F32)
    for k in range(EXPERTS_PER_GROUP):
        cand = (i1 != k) & (vals[k] > b2)
        i2 = jnp.where(cand, k, i2)
        w2 = jnp.where(cand, affs[k], w2)
        b2 = jnp.where(cand, vals[k], b2)
    e1 = grp * EXPERTS_PER_GROUP + i1
    e2 = grp * EXPERTS_PER_GROUP + i2
    tot = w1 + w2
    e_ref[...] = jnp.concatenate([e1, e2], axis=0)
    wpad = jnp.concatenate([w1 / tot, w2 / tot, jnp.zeros((6, tm), F32)], axis=0)
    w_ref[...] = wpad.T
    ids = lax.broadcasted_iota(I32, (N_EXPERTS, tm), 0)
    oh1 = (ids == e1).astype(F32)
    oh2 = (ids == e2).astype(F32)
    oh = oh1 + oh2
    before = (lax.broadcasted_iota(I32, (tm, tm), 0) < lax.broadcasted_iota(I32, (tm, tm), 1)).astype(BF16)
    prior = _dot(oh.astype(BF16), before) + base_sc[...]
    r1 = jnp.sum(oh1 * prior, axis=0, keepdims=True)
    r2 = jnp.sum(oh2 * prior, axis=0, keepdims=True)
    rank_ref[...] = jnp.concatenate([r1, r2], axis=0).astype(I32)
    base = base_sc[...] + jnp.sum(oh, axis=1, keepdims=True)
    base_sc[...] = base
    cnt_ref[...] = jnp.broadcast_to(base, cnt_ref.shape).astype(I32)


def route(s_t, b_router):
    n_rows = s_t.shape[1]
    tm = ROW_TILE
    b_col = jnp.pad(b_router.astype(F32), (0, LANES - N_EXPERTS)).reshape(LANES, 1)
    return pl.pallas_call(
        _router_body,
        grid=(n_rows // tm,),
        in_specs=[pl.BlockSpec((LANES, tm), lambda i: (0, i)), pl.BlockSpec((LANES, 1), lambda i: (0, 0))],
        out_specs=[pl.BlockSpec((TOP_K, tm), lambda i: (0, i)), pl.BlockSpec((tm, 8), lambda i: (i, 0)),
                   pl.BlockSpec((TOP_K, tm), lambda i: (0, i)), pl.BlockSpec((N_EXPERTS, LANES), lambda i: (0, 0))],
        out_shape=[jax.ShapeDtypeStruct((TOP_K, n_rows), I32), jax.ShapeDtypeStruct((n_rows, 8), F32),
                   jax.ShapeDtypeStruct((TOP_K, n_rows), I32), jax.ShapeDtypeStruct((N_EXPERTS, LANES), I32)],
        scratch_shapes=[pltpu.VMEM((N_EXPERTS, 1), F32)],
        compiler_params=_cp(("arbitrary",)),
        name="moe_router",
    )(s_t, b_col)


def _row_copy(src_ref, src_row, dst_ref, dst_row, sem):
    return pltpu.make_async_copy(src_ref.at[pl.ds(src_row, 1), :], dst_ref.at[pl.ds(dst_row, 1), :], sem)


def _dispatch_body(dest_ref, h_ref, buf_in_ref, buf_ref, sem):
    del buf_in_ref
    tm = h_ref.shape[0]

    def issue(t, carry):
        for k in range(TOP_K):
            _row_copy(h_ref, t, buf_ref, dest_ref[0, k, t], sem).start()
        return carry

    def drain(t, carry):
        for k in range(TOP_K):
            _row_copy(h_ref, 0, buf_ref, 0, sem).wait()
        return carry

    lax.fori_loop(0, tm, issue, 0)
    lax.fori_loop(0, tm, drain, 0)


def moe_dispatch(h2, dest3, buf0):
    n_rows, d = h2.shape
    tm = ROW_TILE
    return pl.pallas_call(
        _dispatch_body,
        grid=(n_rows // tm,),
        in_specs=[pl.BlockSpec((1, TOP_K, tm), lambda i: (i, 0, 0), memory_space=pltpu.SMEM),
                  pl.BlockSpec((tm, d), lambda i: (i, 0)),
                  pl.BlockSpec(memory_space=pl.ANY)],
        out_specs=pl.BlockSpec(memory_space=pl.ANY),
        out_shape=jax.ShapeDtypeStruct(buf0.shape, buf0.dtype),
        scratch_shapes=[pltpu.SemaphoreType.DMA(())],
        input_output_aliases={2: 0},
        compiler_params=_cp(("arbitrary",)),
        name="moe_dispatch",
    )(dest3, h2, buf0)


def _expert_body(be_ref, x_ref, w1_ref, w3_ref, w2_ref, o_ref, w1_sc, w3_sc, w2_sc):
    i = pl.program_id(0)
    changed = jnp.logical_or(i == 0, be_ref[i] != be_ref[jnp.maximum(i - 1, 0)])

    @pl.when(changed)
    def _():
        w1_sc[...] = w1_ref[0].astype(BF16)
        w3_sc[...] = w3_ref[0].astype(BF16)
        w2_sc[...] = w2_ref[0].astype(BF16)

    x = x_ref[...].astype(BF16)
    a = _dot(x, w1_sc[...])
    mid = (a * jax.nn.sigmoid(a)) * _dot(x, w3_sc[...])
    o_ref[...] = _dot(mid.astype(BF16), w2_sc[...])


def moe_experts(buf, blk_expert, w1, w3, w2, blk):
    n_slots, d = buf.shape
    de = w1.shape[2]
    grid_spec = pltpu.PrefetchScalarGridSpec(
        num_scalar_prefetch=1,
        grid=(n_slots // blk,),
        in_specs=[pl.BlockSpec((blk, d), lambda i, be: (i, 0)),
                  pl.BlockSpec((1, d, de), lambda i, be: (be[i], 0, 0)),
                  pl.BlockSpec((1, d, de), lambda i, be: (be[i], 0, 0)),
                  pl.BlockSpec((1, de, d), lambda i, be: (be[i], 0, 0))],
        out_specs=pl.BlockSpec((blk, d), lambda i, be: (i, 0)),
        scratch_shapes=[pltpu.VMEM((d, de), BF16), pltpu.VMEM((d, de), BF16), pltpu.VMEM((de, d), BF16)],
    )
    return pl.pallas_call(
        _expert_body,
        grid_spec=grid_spec,
        out_shape=jax.ShapeDtypeStruct((n_slots, d), F32),
        compiler_params=_cp(("arbitrary",), VMEM_LIMIT),
        name="moe_experts",
    )(blk_expert, buf, w1, w3, w2)


def _combine_body(final, dest_ref, x_ref, w_ref, mod_ref, gf_ref, y_hbm, o_ref, y0_sc, y1_sc, sem):
    tm = x_ref.shape[0]
    bufs = (y0_sc, y1_sc)

    def issue(t, carry):
        for k in range(TOP_K):
            _row_copy(y_hbm, dest_ref[0, k, t], bufs[k], t, sem).start()
        return carry

    def drain(t, carry):
        for k in range(TOP_K):
            _row_copy(y_hbm, 0, bufs[k], 0, sem).wait()
        return carry

    lax.fori_loop(0, tm, issue, 0)
    lax.fori_loop(0, tm, drain, 0)
    w = w_ref[...]
    f = w[:, 0:1] * y0_sc[...] + w[:, 1:2] * y1_sc[...]
    x_new = x_ref[...] + mod_ref[0][5:6] * f
    if final:
        ms = jnp.mean(x_new * x_new, axis=-1, keepdims=True)
        x_new = x_new * lax.rsqrt(ms + EPS) * gf_ref[...]
    o_ref[...] = x_new


def moe_combine(dest3, x_rows, wts, modtab, g_final, y_slots, n_lat_rows, n_batch, final):
    n_rows, d = x_rows.shape
    tm = ROW_TILE
    per_b = n_lat_rows // n_batch // tm
    return pl.pallas_call(
        functools.partial(_combine_body, final),
        grid=(n_rows // tm,),
        in_specs=[pl.BlockSpec((1, TOP_K, tm), lambda i: (i, 0, 0), memory_space=pltpu.SMEM),
                  pl.BlockSpec((tm, d), lambda i: (i, 0)),
                  pl.BlockSpec((tm, 8), lambda i: (i, 0)),
                  pl.BlockSpec((1, N_MOD, d), lambda i: (jnp.minimum(i // per_b, n_batch), 0, 0)),
                  pl.BlockSpec((1, d), lambda i: (0, 0)),
                  pl.BlockSpec(memory_space=pl.ANY)],
        out_specs=pl.BlockSpec((tm, d), lambda i: (i, 0)),
        out_shape=jax.ShapeDtypeStruct((n_rows, d), F32),
        scratch_shapes=[pltpu.VMEM((tm, d), F32), pltpu.VMEM((tm, d), F32), pltpu.SemaphoreType.DMA(())],
        compiler_params=_cp(("arbitrary",)),
        name="moe_combine",
    )(dest3, x_rows, wts, modtab, g_final.reshape(1, d), y_slots)


def moe_layer(x_rows, h2, s_t, b_router, w1, w3, w2, modtab, g_final, n_lat_rows, n_batch, final):
    n_rows, d = h2.shape
    blk = 2 * MOE_BLOCK
    experts, wts, rank, counts = route(s_t, b_router)
    cnt = counts[:, 0]
    padded = (cnt + blk - 1) // blk * blk
    pend = jnp.cumsum(padded)
    pstart = pend - padded
    hit = experts[..., None] == jnp.arange(N_EXPERTS, dtype=I32)
    dest = jnp.sum(jnp.where(hit, pstart.astype(I32), 0), axis=-1) + rank
    n_blocks = -(-(n_rows * TOP_K) // blk) + N_EXPERTS
    blk_start = jnp.arange(n_blocks, dtype=I32) * blk
    blk_expert = jnp.minimum(jnp.sum((pend[None, :] <= blk_start[:, None]).astype(I32), axis=1), N_EXPERTS - 1)
    dest3 = dest.reshape(TOP_K, n_rows // ROW_TILE, ROW_TILE).transpose(1, 0, 2)
    buf = moe_dispatch(h2, dest3, jnp.zeros((n_blocks * blk, d), F32))
    y_slots = moe_experts(buf, blk_expert, w1, w3, w2, blk)
    return moe_combine(dest3, x_rows, wts, modtab, g_final, y_slots, n_lat_rows, n_batch, final)


_COL = dict(na_q=0, na_k=512, na_v=1024, ml_q=1536, ml_k=2048, ml_v=2560, ml_o=3072,
            gq_q=3584, gq_k=4096, gq_v=4224)


def _split_w_in(w_in):
    sizes = (BRANCH_WIDTH,) * 8 + (4 * ML_HEADS, BRANCH_WIDTH, GQ_KV_HEADS * GQ_HEAD_DIM, GQ_KV_HEADS * GQ_HEAD_DIM)
    idx = np.cumsum(sizes)[:-1].tolist()
    (s5_u, na_q, na_k, na_v, ml_q, ml_k, ml_v, ml_o, ml_gt, gq_q, gq_k, gq_v) = jnp.split(w_in, idx, axis=-1)
    wa = jnp.pad(ml_gt, ((0, 0), (0, LANES - 4 * ML_HEADS)))
    wb = jnp.concatenate([na_q, na_k, na_v, ml_q, ml_k * (ML_HEAD_DIM ** -0.5), ml_v, ml_o, gq_q, gq_k, gq_v], axis=1)
    return s5_u.T.astype(BF16), wa.astype(BF16), wb.astype(BF16)


def kernel(x, c, ctx, c_ctx, w_mod, b_mod, g_norm1, g_norm2, w_in, s5_lam_re, s5_lam_im, s5_log_dt, s5_b_re,
           s5_b_im, s5_c_re, s5_c_im, s5_d, s5_w_glu, s5_b_glu, na_rpb, ml_b_gates, ml_norm, gq_qnorm, gq_knorm,
           w_branch, w_gate, b_gate, w_out, w_router, b_router, moe_w1, moe_w3, moe_w2, g_final):
    b, n_lat, dm = x.shape
    n_ctx = ctx.shape[1]
    depth = w_in.shape[0]
    bn, bc = b * n_lat, b * n_ctx
    x_all = jnp.concatenate([x.reshape(bn, dm), ctx.reshape(bc, dm)], axis=0).astype(F32)
    c_all = jnp.concatenate([c.astype(F32), c_ctx.astype(F32)[None], jnp.zeros((8 - b - 1, dm), F32)], axis=0)
    cs_tab = rope_tables(n_lat)
    wr_t = jnp.pad(w_router.astype(BF16).T, ((0, LANES - N_EXPERTS), (0, 0)))
    out = None
    for l in range(depth):
        last = l == depth - 1
        with_ctx = not last
        modtab = mod_vectors(c_all, w_mod[l], b_mod[l])[:b + 1].reshape(b + 1, N_MOD, dm)
        w_u_t, wa, wb = _split_w_in(w_in[l])
        pa, pb, gate = in_projection(x_all, g_norm1[l], modtab, wa, wb, w_gate[l].astype(BF16), b_gate[l], bn, b)

        tables = s5_tables(s5_lam_re[l], s5_lam_im[l], s5_log_dt[l], s5_b_re[l], s5_b_im[l],
                           s5_c_re[l], s5_c_im[l], s5_d[l])
        ya = s5_mixer(x_all, g_norm1[l], modtab, w_u_t, tables, s5_w_glu[l].astype(BF16),
                      s5_b_glu[l].astype(F32).reshape(1, -1), n_lat, n_ctx, b)

        bias_tab = na_bias_tables(na_rpb[l], n_lat // GRID_W, n_ctx)
        yb = na_mixer(pb, _COL['na_q'], _COL['na_k'], _COL['na_v'], bias_tab, n_lat, n_ctx, b)

        ml_bias = jnp.pad(ml_b_gates[l].astype(F32), (0, LANES - 4 * ML_HEADS)).reshape(1, LANES)
        hf, hb = mlstm_mixer(pb, pa, _COL['ml_q'], _COL['ml_k'], _COL['ml_v'], 0, ml_bias, n_lat, n_ctx, b)

        yd = gq_mixer(pb, _COL['gq_q'], _COL['gq_k'], _COL['gq_v'], cs_tab, gq_qnorm[l], gq_knorm[l],
                      n_lat, n_ctx, b, with_ctx)
        if with_ctx:
            yb = jnp.concatenate([yb, na_ctx_attention(pb, _COL['na_q'], _COL['na_k'], _COL['na_v'],
                                                       n_lat, n_ctx, b)], axis=0)
        n_rows = bn + bc if with_ctx else bn
        x_mid, h2, s_t = merge_layer(
            ya, yb, hf, hb, pb, _COL['ml_o'], yd, gate, x_all, modtab, ml_norm[l].astype(F32).reshape(1, -1),
            w_branch[l].astype(BF16), w_out[l].astype(BF16), g_norm2[l].astype(F32).reshape(1, -1), wr_t,
            n_rows, bn, b)
        x_next = moe_layer(x_mid, h2, s_t, b_router, moe_w1[l], moe_w3[l], moe_w2[l], modtab, g_final,
                           bn, b, last)
        if last:
            out = x_next.reshape(b, n_lat, dm).astype(x.dtype)
        else:
            x_all = x_next
    return out
```

```python
import functools
import math

import numpy as np
import jax
import jax.numpy as jnp
from jax import lax
from jax.experimental import pallas as pl
from jax.experimental.pallas import tpu as pltpu

F32 = jnp.float32
BF16 = jnp.bfloat16
I32 = jnp.int32

GRID_W = 64
N_MOD = 6
BRANCH_WIDTH = 512
N_BRANCHES = 4
S5_GROUP = 16
S5_GROUPS = BRANCH_WIDTH // S5_GROUP
S5_STATE = 64
NA_HEADS = 8
NA_HEAD_DIM = 64
NA_ROWS = 8
NA_COLS = 16
ML_HEADS = 4
ML_HEAD_DIM = 128
ML_CHUNK = 128
GQ_HEADS = 8
GQ_KV_HEADS = 2
GQ_HEAD_DIM = 64
ROPE_THETA = 10000.0
N_EXPERTS = 32
N_EXPERT_GROUPS = 8
EXPERTS_PER_GROUP = 4
TOP_K = 2
D_EXPERT = 512
MOE_BLOCK = 128
EPS = 1e-6
NEG_INIT = -1e30
MASK_NEG = -1e30
LOG2E = 1.4426950408889634

LANES = 128
ROW_TILE = 256
S5_CHUNK = 32
NA_TILE_ROWS = 4
VMEM_LIMIT = 56 * 1024 * 1024

HIGHEST = lax.Precision.HIGHEST


def _cp(sem, vmem=None):
    return pltpu.CompilerParams(dimension_semantics=sem, vmem_limit_bytes=vmem)


def _dot(a, b):
    return jnp.dot(a, b, preferred_element_type=F32)


def _dot_nt(a, b):
    return lax.dot_general(a, b, (((1,), (1,)), ((), ())), preferred_element_type=F32)


def _dot_tn(a, b):
    return lax.dot_general(a, b, (((0,), (0,)), ((), ())), preferred_element_type=F32)


def _rms_mod(x, g, shift, scale):
    ms = jnp.mean(x * x, axis=-1, keepdims=True)
    y = x * lax.rsqrt(ms + EPS) * g
    return y * (1.0 + scale) + shift


def _round_up(n, m):
    return -(-n // m) * m


def _mod_body(c_ref, w_ref, b_ref, o_ref):
    c = c_ref[...]
    a = (c * jax.nn.sigmoid(c)).astype(BF16)
    o_ref[...] = _dot(a, w_ref[...].astype(BF16)) + b_ref[...]


def mod_vectors(c_all, w_mod, b_mod, layer):
    d = c_all.shape[1]
    depth = w_mod.shape[0]
    return pl.pallas_call(
        _mod_body,
        grid=(N_MOD,),
        in_specs=[pl.BlockSpec((8, d), lambda j: (0, 0)),
                  pl.BlockSpec((None, d, d), lambda j: (layer, 0, j)),
                  pl.BlockSpec((None, 1, d), lambda j: (layer, 0, j))],
        out_specs=pl.BlockSpec((8, d), lambda j: (0, j)),
        out_shape=jax.ShapeDtypeStruct((8, N_MOD * d), F32),
        compiler_params=_cp(("arbitrary",)),
        name="mod_vectors",
    )(c_all, w_mod, b_mod.reshape(depth, 1, -1))


def _inproj_body(x_ref, g_ref, mod_ref, wa_ref, wb_ref, wg_ref, bg_ref, oa_ref, ob_ref, og_ref, oh_ref):
    mod = mod_ref[0]
    h = _rms_mod(x_ref[...], g_ref[...], mod[0:1], mod[1:2]).astype(BF16)
    oh_ref[...] = h
    oa_ref[...] = _dot(h, wa_ref[...])
    ob_ref[...] = _dot(h, wb_ref[...]).astype(BF16)
    og_ref[...] = jax.nn.sigmoid(_dot(h, wg_ref[...]) + bg_ref[...]).astype(BF16)


def in_projection(x_all, g, modtab, wa, wb, wg, bg, n_lat_rows, n_batch):
    r, d = x_all.shape
    tm = ROW_TILE
    per_b = n_lat_rows // n_batch // tm

    def mod_idx(i):
        return (jnp.minimum(i // per_b, n_batch), 0, 0)

    const = lambda i: (0, 0)
    return pl.pallas_call(
        _inproj_body,
        grid=(r // tm,),
        in_specs=[pl.BlockSpec((tm, d), lambda i: (i, 0)),
                  pl.BlockSpec((1, d), const),
                  pl.BlockSpec((1, N_MOD, d), mod_idx),
                  pl.BlockSpec(wa.shape, const, pipeline_mode=pl.Buffered(1)),
                  pl.BlockSpec(wb.shape, const, pipeline_mode=pl.Buffered(1)),
                  pl.BlockSpec(wg.shape, const, pipeline_mode=pl.Buffered(1)),
                  pl.BlockSpec((1, wg.shape[1]), const)],
        out_specs=[pl.BlockSpec((tm, wa.shape[1]), lambda i: (i, 0)),
                   pl.BlockSpec((tm, wb.shape[1]), lambda i: (i, 0)),
                   pl.BlockSpec((tm, wg.shape[1]), lambda i: (i, 0)),
                   pl.BlockSpec((tm, d), lambda i: (i, 0))],
        out_shape=[jax.ShapeDtypeStruct((r, wa.shape[1]), F32),
                   jax.ShapeDtypeStruct((r, wb.shape[1]), BF16),
                   jax.ShapeDtypeStruct((r, wg.shape[1]), BF16),
                   jax.ShapeDtypeStruct((r, d), BF16)],
        compiler_params=_cp(("arbitrary",), VMEM_LIMIT),
        name="in_projection",
    )(x_all, g.reshape(1, d), modtab, wa, wb, wg, bg.reshape(1, -1))


def s5_tables(lam_re, lam_im, log_dt, b_re, b_im, c_re, c_im, d_skip):
    ell, g_n, p_n, c_n = S5_CHUNK, S5_GROUPS, S5_STATE, S5_GROUP
    lam_re, lam_im = lam_re.astype(F32), lam_im.astype(F32)
    b_re, b_im, c_re, c_im = (t.astype(F32) for t in (b_re, b_im, c_re, c_im))
    dt = jnp.exp(log_dt.astype(F32))[..., None]
    mag = jnp.exp(lam_re * dt)
    a_re = mag * jnp.cos(lam_im * dt)
    a_im = mag * jnp.sin(lam_im * dt)
    den = lam_re * lam_re + lam_im * lam_im
    nr = a_re - 1.0
    f_re = (nr * lam_re + a_im * lam_im) / den
    f_im = (a_im * lam_re - nr * lam_im) / den
    bb_re = f_re[..., None] * b_re - f_im[..., None] * b_im
    bb_im = f_re[..., None] * b_im + f_im[..., None] * b_re
    k = jnp.arange(ell + 1, dtype=F32)
    pmag = jnp.exp((lam_re * dt)[..., None] * k)
    ang = (lam_im * dt)[..., None] * k
    pr, pi = pmag * jnp.cos(ang), pmag * jnp.sin(ang)
    ab_re = pr[..., None] * bb_re[:, :, :, None, :] - pi[..., None] * bb_im[:, :, :, None, :]
    ab_im = pr[..., None] * bb_im[:, :, :, None, :] + pi[..., None] * bb_re[:, :, :, None, :]
    flat = lambda t: t.reshape(2 * g_n, p_n, (ell + 1) * c_n)
    kk = (jnp.einsum('bcp,bpn->bcn', c_re.reshape(2 * g_n, c_n, p_n), flat(ab_re), precision=HIGHEST)
          - jnp.einsum('bcp,bpn->bcn', c_im.reshape(2 * g_n, c_n, p_n), flat(ab_im), precision=HIGHEST))
    kk = kk.reshape(2, g_n, c_n, ell + 1, c_n)
    centre = kk[0][:, :, 0] + kk[1][:, :, 0] + d_skip.astype(F32).reshape(g_n, c_n, 1) * jnp.eye(c_n, dtype=F32)
    w = jnp.concatenate([kk[0][:, :, ell - 1:0:-1], centre[:, :, None], kk[1][:, :, 1:ell]], axis=2)
    wf = w.reshape(g_n, c_n, (2 * ell - 1) * c_n)
    toe = jnp.stack([wf[:, :, (ell - 1 - t) * c_n:(2 * ell - 1 - t) * c_n] for t in range(ell)], axis=1)
    tsum_t = toe.astype(BF16).reshape(g_n, ell * c_n, ell * c_n)

    parity = [jnp.asarray(np.arange(g_n) % 2 == q, F32) for q in range(2)]
    parts = []
    for d in range(2):
        for ab in (ab_re, ab_im):
            sel = ab[d][:, :, :ell]
            if d == 0:
                sel = sel[:, :, ::-1]
            sel = sel.reshape(g_n, p_n, ell * c_n)
            parts += [sel * parity[q][:, None, None] for q in range(2)]
    mend_t = jnp.stack(parts, axis=1).reshape(g_n, 8 * p_n, ell * c_n)

    rows = []
    for d in range(2):
        prk, pik = pr[d][:, :, 1:ell + 1], pi[d][:, :, 1:ell + 1]
        if d == 1:
            prk, pik = prk[:, :, ::-1], pik[:, :, ::-1]
        prk = prk.transpose(0, 2, 1)[:, :, None, :]
        pik = pik.transpose(0, 2, 1)[:, :, None, :]
        cr, ci = c_re[d][:, None], c_im[d][:, None]
        for part in (cr * prk - ci * pik, -cr * pik - ci * prk):
            rows += [part * parity[q][:, None, None, None] for q in range(2)]
    wst_t = jnp.concatenate(rows, axis=-1).reshape(g_n, ell * c_n, 8 * p_n)

    al = jnp.stack([pr[0][:, :, ell], pi[0][:, :, ell], pr[1][:, :, ell], pi[1][:, :, ell]], axis=1)
    a_chunk = al.reshape(g_n // 2, 2, 4, p_n).transpose(0, 2, 1, 3).reshape(g_n // 2, 8 * p_n)
    return tsum_t.astype(BF16), mend_t.astype(BF16), wst_t.astype(BF16), a_chunk


def _s5_proj_body(hl_ref, hc_ref, w_ref, o_ref):
    n_lat = hl_ref.shape[0]
    n_pad = o_ref.shape[1] - LANES
    hl = hl_ref[...]
    if n_pad > n_lat:
        hl = jnp.concatenate([hl, jnp.zeros((n_pad - n_lat, hl.shape[1]), BF16)], axis=0)
    o_ref[:, 0:n_pad] = _dot_nt(w_ref[...], hl).astype(BF16)
    hc = hc_ref[...]
    hc = jnp.concatenate([hc, jnp.zeros((LANES - hc.shape[0], hc.shape[1]), BF16)], axis=0)
    o_ref[:, n_pad:] = _dot_nt(w_ref[...], hc).astype(BF16)


def _s5_end_body(u_ref, m_ref, o_ref):
    width = m_ref.shape[1]
    acc = None
    for q in range(2):
        u = u_ref[:, q].reshape(width, u_ref.shape[3])
        term = _dot(m_ref[q], u)
        acc = term if acc is None else acc + term
    o_ref[...] = acc.T


def _s5_scan_body(n_batch, per_b, n_ctx_chunks, lat_pad, e_ref, a_ref, o_ref):
    n_pairs = e_ref.shape[0]
    o_ref[...] = jnp.zeros(o_ref.shape, F32)
    n_steps = per_b + n_ctx_chunks
    first = pl.program_id(0) * n_pairs
    coef = [[a_ref[first + q, j:j + 1, :] for j in range(4)] for q in range(n_pairs)]

    def step(i, carry):
        new = []
        for b in range(n_batch):
            fwd = jnp.where(i < n_ctx_chunks, lat_pad + b * n_ctx_chunks + i, b * per_b + i - n_ctx_chunks)
            bwd = jnp.where(i < n_ctx_chunks, lat_pad + b * n_ctx_chunks + n_ctx_chunks - 1 - i,
                            b * per_b + per_b - 1 - (i - n_ctx_chunks))
            for q in range(n_pairs):
                for d, row in enumerate((fwd, bwd)):
                    sr, si = carry[((b * n_pairs + q) * 2 + d) * 2:((b * n_pairs + q) * 2 + d) * 2 + 2]
                    ar, ai = coef[q][2 * d], coef[q][2 * d + 1]
                    o_ref[q, row, 2 * d:2 * d + 1, :] = sr
                    o_ref[q, row, 2 * d + 1:2 * d + 2, :] = si
                    er = e_ref[q, row, 2 * d:2 * d + 1, :]
                    ei = e_ref[q, row, 2 * d + 1:2 * d + 2, :]
                    new += [ar * sr - ai * si + er, ar * si + ai * sr + ei]
        return tuple(new)

    z = jnp.zeros((1, LANES), F32)
    lax.fori_loop(0, n_steps, step, tuple(z for _ in range(n_batch * n_pairs * 4)))


def _s5_out_body(u_ref, t_ref, s_ref, w_ref, o_ref):
    width = t_ref.shape[0]
    u = u_ref[...].reshape(width, u_ref.shape[2])
    y = _dot(t_ref[...], u) + _dot_nt(w_ref[...], s_ref[...].astype(BF16))
    o_ref[...] = y.reshape(o_ref.shape)


def _s5_glu_body(n_lat_chunks, n_ctx_chunks, lat_pad, y_ref, w_ref, b_ref, o_ref):
    y = y_ref[...].T
    if lat_pad == n_lat_chunks:
        y = y[:n_lat_chunks + n_ctx_chunks]
    else:
        y = jnp.concatenate([y[:n_lat_chunks], y[lat_pad:lat_pad + n_ctx_chunks]], axis=0)
    z = jax.nn.gelu(y)
    o_ref[...] = (z * jax.nn.sigmoid(_dot(z.astype(BF16), w_ref[...]) + b_ref[...])).astype(o_ref.dtype)


def s5_mixer(h_all, w_u_t, tables, w_glu, b_glu, n_lat, n_ctx, n_batch):
    tsum_t, mend_t, wst_t, a_chunk = tables
    r, d = h_all.shape
    ell, g_n, c_n = S5_CHUNK, S5_GROUPS, S5_GROUP
    width = ell * c_n
    bw = g_n * c_n
    n_lat_chunks = n_batch * n_lat // ell
    n_ctx_chunks = n_batch * n_ctx // ell
    assert n_lat_chunks % n_ctx_chunks == 0 and n_ctx_chunks % 16 == 0 and n_ctx_chunks <= LANES
    lat_pad = _round_up(n_lat_chunks, LANES)
    nch = lat_pad + LANES
    h2 = h_all.reshape(r // ell, ell * d)
    const = lambda t: (0, 0)
    u_t = pl.pallas_call(
        _s5_proj_body,
        grid=(ell,),
        in_specs=[pl.BlockSpec((n_lat_chunks, d), lambda t: (0, t)),
                  pl.BlockSpec((n_ctx_chunks, d), lambda t: (n_lat_chunks // n_ctx_chunks, t)),
                  pl.BlockSpec((bw, d), const)],
        out_specs=pl.BlockSpec((None, bw, nch), lambda t: (t, 0, 0)),
        out_shape=jax.ShapeDtypeStruct((ell, bw, nch), BF16),
        compiler_params=_cp(("arbitrary",)),
        name="s5_projection",
    )(h2, h2, w_u_t)
    u4 = u_t.reshape(ell, g_n, c_n, nch)
    ends = pl.pallas_call(
        _s5_end_body,
        grid=(g_n // 2,),
        in_specs=[pl.BlockSpec((ell, 2, c_n, nch), lambda p: (0, p, 0, 0)),
                  pl.BlockSpec((2, width, width), lambda p: (p, 0, 0))],
        out_specs=pl.BlockSpec((None, nch, width), lambda p: (p, 0, 0)),
        out_shape=jax.ShapeDtypeStruct((g_n // 2, nch, width), F32),
        compiler_params=_cp(("arbitrary",)),
        name="s5_chunk_ends",
    )(u4, mend_t)
    pairs_per_step = 2
    quad = (g_n // 2, nch, 4, LANES)
    states = pl.pallas_call(
        functools.partial(_s5_scan_body, n_batch, n_lat // ell, n_ctx // ell, lat_pad),
        grid=(g_n // 2 // pairs_per_step,),
        in_specs=[pl.BlockSpec((pairs_per_step, nch, 4, LANES), lambda j: (j, 0, 0, 0)),
                  pl.BlockSpec((g_n // 2, 4, LANES), lambda j: (0, 0, 0))],
        out_specs=pl.BlockSpec((pairs_per_step, nch, 4, LANES), lambda j: (j, 0, 0, 0)),
        out_shape=jax.ShapeDtypeStruct(quad, F32),
        compiler_params=_cp(("arbitrary",), VMEM_LIMIT),
        name="s5_state_scan",
    )(ends.reshape(quad), a_chunk.reshape(g_n // 2, 4, LANES)).reshape(g_n // 2, nch, width)
    y_t = pl.pallas_call(
        _s5_out_body,
        grid=(g_n,),
        in_specs=[pl.BlockSpec((ell, None, c_n, nch), lambda gi: (0, gi, 0, 0)),
                  pl.BlockSpec((None, width, width), lambda gi: (gi, 0, 0)),
                  pl.BlockSpec((None, nch, width), lambda gi: (gi // 2, 0, 0)),
                  pl.BlockSpec((None, width, width), lambda gi: (gi, 0, 0))],
        out_specs=pl.BlockSpec((ell, None, c_n, nch), lambda gi: (0, gi, 0, 0)),
        out_shape=jax.ShapeDtypeStruct((ell, g_n, c_n, nch), F32),
        compiler_params=_cp(("arbitrary",)),
        name="s5_outputs",
    )(u4, tsum_t, states, wst_t)
    n_chunks = n_lat_chunks + n_ctx_chunks
    ya = pl.pallas_call(
        functools.partial(_s5_glu_body, n_lat_chunks, n_ctx_chunks, lat_pad),
        grid=(ell,),
        in_specs=[pl.BlockSpec((None, bw, nch), lambda t: (t, 0, 0)),
                  pl.BlockSpec((bw, bw), const),
                  pl.BlockSpec((1, bw), const)],
        out_specs=pl.BlockSpec((n_chunks, bw), lambda t: (0, t)),
        out_shape=jax.ShapeDtypeStruct((n_chunks, ell * bw), BF16),
        compiler_params=_cp(("arbitrary",)),
        name="s5_glu",
    )(y_t.reshape(ell, bw, nch), w_glu, b_glu)
    return ya.reshape(r, bw)


def rope_tables(n_lat):
    half = GQ_HEAD_DIM // 2
    quarter = half // 2
    t = np.arange(n_lat)
    freqs = ROPE_THETA ** (-np.arange(quarter, dtype=np.float64) / quarter)
    ang_r = (t // GRID_W)[:, None] * freqs
    ang_c = (t % GRID_W)[:, None] * freqs
    ang = np.concatenate([ang_r, ang_r, ang_c, ang_c], axis=1)
    sign = np.concatenate([-np.ones(quarter), np.ones(quarter)] * 2)
    cos = np.concatenate([np.cos(ang), np.ones((ROW_TILE, GQ_HEAD_DIM))], axis=0)
    sin = np.concatenate([np.sin(ang) * sign, np.zeros((ROW_TILE, GQ_HEAD_DIM))], axis=0)
    tab = np.concatenate([cos, cos, sin, sin], axis=1)
    return jnp.asarray(tab, F32)


def _group_ones(width, group):
    i = np.arange(width)
    return jnp.asarray((i[:, None] // group) == (i[None, :] // group), BF16)


def _group_mean_sq(x, ones_blk, group):
    sq = x * x
    hi = sq.astype(BF16)
    lo = (sq - hi.astype(F32)).astype(BF16)
    return (_dot(hi, ones_blk) + _dot(lo, ones_blk)) * (1.0 / group)


def _rope(x, cos, sin):
    w = x.shape[-1]
    q = GQ_HEAD_DIM // 4
    lane = lax.broadcasted_iota(I32, x.shape, 1)
    first = (lane % (2 * q)) < q
    partner = jnp.where(first, pltpu.roll(x, w - q, 1), pltpu.roll(x, q, 1))
    return x * cos + partner * sin


def _gq_prep_body(q_ref, k_ref, v_ref, cs_ref, gq_ref, gk_ref, oq_ref, ok_ref, qm_ref, kr_ref, va_ref):
    cs = cs_ref[...]
    cos1, sin1 = cs[:, :LANES], cs[:, LANES:]
    q = q_ref[...].astype(F32)
    qn = q * lax.rsqrt(_group_mean_sq(q, oq_ref[...], GQ_HEAD_DIM) + EPS) * gq_ref[...]
    n_pairs = q.shape[1] // LANES
    qr = _rope(qn, jnp.concatenate([cos1] * n_pairs, axis=1), jnp.concatenate([sin1] * n_pairs, axis=1))
    qr = (qr * (GQ_HEAD_DIM ** -0.5 * LOG2E)).astype(BF16)
    lane = lax.broadcasted_iota(I32, (q.shape[0], LANES), 1)
    heads_per_kv = GQ_HEADS // GQ_KV_HEADS
    for h in range(GQ_HEADS):
        pair = qr[:, (h // 2) * LANES:(h // 2 + 1) * LANES]
        kv = h // heads_per_kv
        if h % 2 != kv:
            pair = pltpu.roll(pair, GQ_HEAD_DIM, 1)
        keep = (lane >= kv * GQ_HEAD_DIM) & (lane < (kv + 1) * GQ_HEAD_DIM)
        qm_ref[h] = jnp.where(keep, pair, jnp.zeros_like(pair))
    k = k_ref[...].astype(F32)
    kn = k * lax.rsqrt(_group_mean_sq(k, ok_ref[...], GQ_HEAD_DIM) + EPS) * gk_ref[...]
    kr_ref[...] = _rope(kn, cos1, sin1).astype(BF16)
    va_ref[...] = jnp.concatenate([v_ref[...], jnp.ones(v_ref.shape, BF16)], axis=1)


def gq_prepare(pb, col_q, col_k, col_v, cs_tab, g_q, g_k, n_lat, n_ctx, n_batch):
    r = pb.shape[0]
    tm = ROW_TILE
    assert n_ctx == tm and n_lat % tm == 0
    nb = n_lat // tm
    n_lat_tiles = n_batch * nb

    def tab_idx(i):
        return (jnp.where(i < n_lat_tiles, i % nb, nb), 0)

    def kv_idx(i):
        lat = (i // nb) * (nb + 1) + i % nb
        ctx = (i - n_lat_tiles) * (nb + 1) + nb
        return (jnp.where(i < n_lat_tiles, lat, ctx), 0)

    qw = GQ_HEADS * GQ_HEAD_DIM
    const = lambda i: (0, 0)
    gq = jnp.tile(g_q.astype(F32), GQ_HEADS).reshape(1, qw)
    gk = jnp.tile(g_k.astype(F32), GQ_KV_HEADS).reshape(1, LANES)
    n_keys = n_batch * (n_lat + n_ctx)
    return pl.pallas_call(
        _gq_prep_body,
        grid=(r // tm,),
        in_specs=[pl.BlockSpec((tm, qw), lambda i: (i, col_q // qw)),
                  pl.BlockSpec((tm, LANES), lambda i: (i, col_k // LANES)),
                  pl.BlockSpec((tm, LANES), lambda i: (i, col_v // LANES)),
                  pl.BlockSpec((tm, 2 * LANES), tab_idx),
                  pl.BlockSpec((1, qw), const),
                  pl.BlockSpec((1, LANES), const),
                  pl.BlockSpec((qw, qw), const),
                  pl.BlockSpec((LANES, LANES), const)],
        out_specs=[pl.BlockSpec((GQ_HEADS, tm, LANES), lambda i: (0, i, 0)),
                   pl.BlockSpec((tm, LANES), kv_idx),
                   pl.BlockSpec((tm, 2 * LANES), kv_idx)],
        out_shape=[jax.ShapeDtypeStruct((GQ_HEADS, r, LANES), BF16),
                   jax.ShapeDtypeStruct((n_keys, LANES), BF16),
                   jax.ShapeDtypeStruct((n_keys, 2 * LANES), BF16)],
        compiler_params=_cp(("arbitrary",)),
        name="gq_prepare",
    )(pb, pb, pb, cs_tab, gq, gk, _group_ones(qw, GQ_HEAD_DIM), _group_ones(LANES, GQ_HEAD_DIM))


def _gq_flash_body(q_ref, k_ref, v_ref, o_ref, m_sc, acc_sc):
    kj = pl.program_id(2)
    n_h, tq, _ = q_ref.shape

    @pl.when(kj == 0)
    def _():
        m_sc[...] = jnp.full(m_sc.shape, -jnp.inf, F32)
        acc_sc[...] = jnp.zeros(acc_sc.shape, F32)

    k = k_ref[...]
    v = v_ref[...]
    hp = 1
    for c in range(n_h // hp):
        rows = slice(c * hp * tq, (c + 1) * hp * tq)
        s = _dot_nt(q_ref[c * hp:(c + 1) * hp].reshape(hp * tq, LANES), k)
        m_prev = m_sc[rows, :]
        m_new = jnp.maximum(m_prev, jnp.max(s, axis=-1, keepdims=True))
        p = jnp.exp2(s - m_new)
        acc_sc[rows, :] = jnp.exp2(m_prev - m_new) * acc_sc[rows, :] + _dot(p.astype(BF16), v)
        m_sc[rows, :] = m_new

    @pl.when(kj == pl.num_programs(2) - 1)
    def _():
        lane = lax.broadcasted_iota(I32, (tq, LANES), 1)
        heads_per_kv = n_h // GQ_KV_HEADS
        for j in range(n_h // 2):
            kv = (2 * j) // heads_per_kv
            halves = []
            for h in (2 * j, 2 * j + 1):
                a = acc_sc[h * tq:(h + 1) * tq, :]
                halves.append(a[:, :LANES] / a[:, LANES:LANES + 1])
            lo, hi = halves
            if kv == 0:
                hi = pltpu.roll(hi, GQ_HEAD_DIM, 1)
            else:
                lo = pltpu.roll(lo, GQ_HEAD_DIM, 1)
            o_ref[:, j * LANES:(j + 1) * LANES] = jnp.where(lane < GQ_HEAD_DIM, lo, hi).astype(o_ref.dtype)


def gq_attention(qm, keys, vals, n_rows_out, tq, tk, q_blk, k_blk, n_q, n_k, n_batch):
    n_h = qm.shape[0]
    return pl.pallas_call(
        _gq_flash_body,
        grid=(n_batch, n_q, n_k),
        in_specs=[pl.BlockSpec((n_h, tq, LANES), lambda b, i, j: (0, q_blk(b, i), 0)),
                  pl.BlockSpec((tk, LANES), lambda b, i, j: (k_blk(b, j), 0)),
                  pl.BlockSpec((tk, 2 * LANES), lambda b, i, j: (k_blk(b, j), 0))],
        out_specs=pl.BlockSpec((tq, n_h * GQ_HEAD_DIM), lambda b, i, j: (b * n_q + i, 0)),
        out_shape=jax.ShapeDtypeStruct((n_rows_out, n_h * GQ_HEAD_DIM), BF16),
        scratch_shapes=[pltpu.VMEM((n_h * tq, 1), F32), pltpu.VMEM((n_h * tq, 2 * LANES), F32)],
        compiler_params=_cp(("arbitrary", "arbitrary", "arbitrary"), VMEM_LIMIT),
        name="gq_attention",
    )(qm, keys, vals)


def _largest_divisor(n, cap):
    return max(d for d in range(1, cap + 1) if n % d == 0)


def gq_mixer(pb, col_q, col_k, col_v, cs_tab, g_q, g_k, n_lat, n_ctx, n_batch, with_ctx):
    qm, keys, vals = gq_prepare(pb, col_q, col_k, col_v, cs_tab, g_q, g_k, n_lat, n_ctx, n_batch)
    tq = 256
    tk = LANES * _largest_divisor((n_lat + n_ctx) // LANES, 22)
    n_q = n_lat // tq
    per_b = (n_lat + n_ctx) // tk
    y_lat = gq_attention(qm, keys, vals, n_batch * n_lat, tq, tk,
                         lambda b, i: b * n_q + i, lambda b, j: b * per_b + j, n_q, per_b, n_batch)
    if not with_ctx:
        return y_lat
    tc = n_ctx
    y_ctx = gq_attention(qm, keys, vals, n_batch * n_ctx, tc, tc,
                         lambda b, i: n_batch * n_lat // tc + b,
                         lambda b, j: b * ((n_lat + n_ctx) // tc) + n_lat // tc, 1, 1, n_batch)
    return jnp.concatenate([y_lat, y_ctx], axis=0)


def na_bias_tables(rpb, n_img_rows, n_ctx):
    tr = NA_TILE_ROWS
    nt = n_img_rows // tr
    assert nt >= 4
    kr = min(NA_ROWS, n_img_rows)
    n_heads = rpb.shape[0]
    qcol = np.arange(GRID_W)[:, None]
    kcol = np.arange(GRID_W)[None, :]
    dc = np.clip(kcol - qcol + NA_COLS - 1, 0, 2 * NA_COLS - 2)
    oh_c = (dc[None] == np.arange(2 * NA_COLS - 1)[:, None, None]).astype(np.float32)
    cstart = np.clip(qcol - NA_COLS // 2, 0, GRID_W - NA_COLS)
    col_ok = (kcol >= cstart) & (kcol < cstart + NA_COLS)
    by_col = jnp.einsum('hrd,dqk->hrqk', rpb.astype(F32), jnp.asarray(oh_c), precision=HIGHEST)
    classes = []
    for i in (0, 1, nt - 1):
        wb = int(np.clip(i - 1, 0, nt - 3))
        qrow = (i * tr + np.arange(tr))[:, None]
        krow = (wb * tr + np.arange(3 * tr))[None, :]
        rs = np.clip(qrow - kr // 2, 0, n_img_rows - kr)
        row_ok = (krow >= rs) & (krow < rs + kr)
        dr = np.clip(krow - qrow + NA_ROWS - 1, 0, 2 * NA_ROWS - 2)
        oh_r = ((dr[None] == np.arange(2 * NA_ROWS - 1)[:, None, None]) & row_ok[None]).astype(np.float32)
        bias = jnp.einsum('hrqk,rab->haqbk', by_col, jnp.asarray(oh_r), precision=HIGHEST)
        ok = row_ok[:, None, :, None] & col_ok[None, :, None, :]
        bias = jnp.where(jnp.asarray(ok)[None], bias, MASK_NEG).reshape(n_heads, tr * GRID_W, 3 * tr * GRID_W)
        classes.append(jnp.concatenate([bias, jnp.zeros((n_heads, tr * GRID_W, n_ctx), F32)], axis=-1))
    return jnp.stack(classes, axis=0)


def _pair_attention(q_pair, k_pair, v_pair, bias_fn):
    lane = lax.broadcasted_iota(I32, q_pair.shape, 1)
    out = None
    for hh in range(2):
        mine = (lane >= hh * NA_HEAD_DIM) & (lane < (hh + 1) * NA_HEAD_DIM)
        qm = jnp.where(mine, q_pair, jnp.zeros_like(q_pair))
        s = _dot_nt(qm, k_pair)
        b = bias_fn(hh)
        if b is not None:
            s = s + b
        m = jnp.max(s, axis=-1, keepdims=True)
        p = jnp.exp(s - m)
        l = jnp.sum(p, axis=-1, keepdims=True)
        o = _dot(p.astype(BF16), v_pair) / l
        out = o if out is None else jnp.where(mine, o, out)
    return out


def _na_body(q_ref, k0_ref, k1_ref, k2_ref, kc_ref, v0_ref, v1_ref, v2_ref, vc_ref, b_ref, o_ref):
    scale = NA_HEAD_DIM ** -0.5
    for j in range(NA_HEADS // 2):
        sl = slice(j * LANES, (j + 1) * LANES)
        q_pair = q_ref[:, sl] * scale
        k_pair = jnp.concatenate([k0_ref[:, sl], k1_ref[:, sl], k2_ref[:, sl], kc_ref[:, sl]], axis=0)
        v_pair = jnp.concatenate([v0_ref[:, sl], v1_ref[:, sl], v2_ref[:, sl], vc_ref[:, sl]], axis=0)
        o = _pair_attention(q_pair, k_pair, v_pair, lambda hh: b_ref[0, 2 * j + hh])
        o_ref[:, sl] = o.astype(o_ref.dtype)


def na_mixer(pb, col_q, col_k, col_v, bias_tab, n_lat, n_ctx, n_batch):
    tm = NA_TILE_ROWS * GRID_W
    assert n_ctx == tm
    w = NA_HEADS * NA_HEAD_DIM
    nt = n_lat // tm
    n_keys = 3 * tm + n_ctx
    cq, ck, cv = col_q // w, col_k // w, col_v // w
    ctx0 = n_batch * nt

    def win(o):
        return lambda b, i: (b * nt + jnp.clip(i - 1, 0, nt - 3) + o)

    def cls(b, i):
        return (jnp.where(i == 0, 0, jnp.where(i == nt - 1, 2, 1)), 0, 0, 0)

    kspecs = [pl.BlockSpec((tm, w), (lambda b, i, f=win(o): (f(b, i), ck))) for o in range(3)]
    vspecs = [pl.BlockSpec((tm, w), (lambda b, i, f=win(o): (f(b, i), cv))) for o in range(3)]
    return pl.pallas_call(
        _na_body,
        grid=(n_batch, nt),
        in_specs=[pl.BlockSpec((tm, w), lambda b, i: (b * nt + i, cq))] + kspecs
        + [pl.BlockSpec((tm, w), lambda b, i: (ctx0 + b, ck))] + vspecs
        + [pl.BlockSpec((tm, w), lambda b, i: (ctx0 + b, cv)),
           pl.BlockSpec((1, NA_HEADS, tm, n_keys), cls)],
        out_specs=pl.BlockSpec((tm, w), lambda b, i: (b * nt + i, 0)),
        out_shape=jax.ShapeDtypeStruct((n_batch * n_lat, w), BF16),
        compiler_params=_cp(("arbitrary", "arbitrary"), VMEM_LIMIT),
        name="na_attention",
    )(pb, pb, pb, pb, pb, pb, pb, pb, pb, bias_tab)


def _ctx_mha_body(q_ref, k_ref, v_ref, o_ref):
    scale = NA_HEAD_DIM ** -0.5
    for j in range(NA_HEADS // 2):
        sl = slice(j * LANES, (j + 1) * LANES)
        o = _pair_attention(q_ref[:, sl] * scale, k_ref[:, sl], v_ref[:, sl], lambda hh: None)
        o_ref[:, sl] = o.astype(o_ref.dtype)


def na_ctx_attention(pb, col_q, col_k, col_v, n_lat, n_ctx, n_batch):
    w = NA_HEADS * NA_HEAD_DIM
    ctx0 = n_batch * n_lat // n_ctx
    spec = lambda c: pl.BlockSpec((n_ctx, w), lambda b: (ctx0 + b, c // w))
    return pl.pallas_call(
        _ctx_mha_body,
        grid=(n_batch,),
        in_specs=[spec(col_q), spec(col_k), spec(col_v)],
        out_specs=pl.BlockSpec((n_ctx, w), lambda b: (b, 0)),
        out_shape=jax.ShapeDtypeStruct((n_batch * n_ctx, w), BF16),
        compiler_params=_cp(("arbitrary",)),
        name="na_ctx_attention",
    )(pb, pb, pb)


def _mlstm_chunk(rev, q, k, v, g, lf_cum, g_t, lf_cum_t, ci, cf, c_ref, n_ref, m_ref, idx):
    tok = lax.broadcasted_iota(I32, (ML_CHUNK, ML_CHUNK), 0)
    src = lax.broadcasted_iota(I32, (ML_CHUNK, ML_CHUNK), 1)
    causal = (src >= tok) if rev else (src <= tok)
    bt_col = lf_cum[:, cf:cf + 1]
    bt_row = lf_cum_t[cf:cf + 1, :]
    li_col = g[:, ci:ci + 1]
    li_row = g_t[ci:ci + 1, :]
    m_prev = m_ref[idx][:, 0:1]
    dmat = jnp.where(causal, bt_col - bt_row + li_row, -jnp.inf)
    inter = bt_col + m_prev
    mt = jnp.maximum(inter, jnp.max(dmat, axis=-1, keepdims=True))
    s = _dot_nt(q, k) * jnp.exp(dmat - mt)
    w_inter = jnp.exp(inter - mt)
    c_prev = c_ref[idx]
    n_prev = n_ref[idx]
    num = _dot(s.astype(BF16), v) + w_inter * _dot_nt(q, c_prev.astype(BF16))
    qn = jnp.sum(q.astype(F32) * n_prev, axis=-1, keepdims=True)
    den = jnp.sum(s, axis=-1, keepdims=True) + w_inter * qn
    h_out = num / jnp.maximum(jnp.abs(den), jnp.exp(-mt))
    b_last = bt_col[0:1, :] if rev else bt_col[ML_CHUNK - 1:ML_CHUNK, :]
    g_col = b_last - bt_col + li_col
    m_new = jnp.maximum(b_last + m_prev, jnp.max(g_col, axis=0, keepdims=True))
    wg = jnp.exp(g_col - m_new)
    decay = jnp.exp(b_last + m_prev - m_new)
    c_ref[idx] = decay * c_prev + _dot_tn((wg * v.astype(F32)).astype(BF16), k)
    n_ref[idx] = decay * n_prev + jnp.sum(wg * k.astype(F32), axis=0, keepdims=True)
    m_ref[idx] = jnp.broadcast_to(m_new, (1, LANES))
    return h_out


def _mlstm_body(qf_ref, kf_ref, vf_ref, gf_ref, qb_ref, kb_ref, vb_ref, gb_ref, bias_ref,
                hf_ref, hb_ref, c_ref, n_ref, m_ref):
    @pl.when(pl.program_id(1) == 0)
    def _():
        c_ref[...] = jnp.zeros(c_ref.shape, F32)
        n_ref[...] = jnp.zeros(n_ref.shape, F32)
        m_ref[...] = jnp.full(m_ref.shape, NEG_INIT, F32)

    tok = lax.broadcasted_iota(I32, (ML_CHUNK, ML_CHUNK), 0)
    src = lax.broadcasted_iota(I32, (ML_CHUNK, ML_CHUNK), 1)
    for d, (q_ref, k_ref, v_ref, g_ref, h_ref) in enumerate(
            ((qf_ref, kf_ref, vf_ref, gf_ref, hf_ref), (qb_ref, kb_ref, vb_ref, gb_ref, hb_ref))):
        rev = d == 1
        g = g_ref[...] + bias_ref[...]
        tri = ((src >= tok) if rev else (src <= tok)).astype(F32)
        lf_cum = jnp.dot(tri, jax.nn.log_sigmoid(g), precision=HIGHEST, preferred_element_type=F32)
        g_t = g.T
        lf_cum_t = lf_cum.T
        for h in range(ML_HEADS):
            sl = slice(h * ML_HEAD_DIM, (h + 1) * ML_HEAD_DIM)
            h_out = _mlstm_chunk(rev, q_ref[:, sl], k_ref[:, sl], v_ref[:, sl], g, lf_cum, g_t, lf_cum_t,
                                 d * 2 * ML_HEADS + h, d * 2 * ML_HEADS + ML_HEADS + h,
                                 c_ref, n_ref, m_ref, d * ML_HEADS + h)
            h_ref[:, sl] = h_out


def mlstm_mixer(pb, pa, col_q, col_k, col_v, col_g, bias, n_lat, n_ctx, n_batch):
    r = pb.shape[0]
    w = ML_HEADS * ML_HEAD_DIM
    tc = ML_CHUNK
    nl, nc = n_lat // tc, n_ctx // tc

    def fwd(b, i):
        return jnp.where(i < nc, n_batch * nl + b * nc + i, b * nl + i - nc)

    def bwd(b, i):
        return jnp.where(i < nc, n_batch * nl + b * nc + nc - 1 - i, b * nl + nl - 1 - (i - nc))

    def specs(f):
        return [pl.BlockSpec((tc, w), lambda b, i, c=c: (f(b, i), c // w)) for c in (col_q, col_k, col_v)] + [
            pl.BlockSpec((tc, LANES), lambda b, i: (f(b, i), col_g // LANES))]

    n_st = 2 * ML_HEADS
    return pl.pallas_call(
        _mlstm_body,
        grid=(n_batch, nl + nc),
        in_specs=specs(fwd) + specs(bwd) + [pl.BlockSpec((1, LANES), lambda b, i: (0, 0))],
        out_specs=[pl.BlockSpec((tc, w), lambda b, i: (fwd(b, i), 0)),
                   pl.BlockSpec((tc, w), lambda b, i: (bwd(b, i), 0))],
        out_shape=[jax.ShapeDtypeStruct((r, w), F32), jax.ShapeDtypeStruct((r, w), F32)],
        scratch_shapes=[pltpu.VMEM((n_st, ML_HEAD_DIM, ML_HEAD_DIM), F32),
                        pltpu.VMEM((n_st, 1, ML_HEAD_DIM), F32),
                        pltpu.VMEM((n_st, 1, LANES), F32)],
        compiler_params=_cp(("arbitrary", "arbitrary")),
        name="mlstm_chunks",
    )(pb, pb, pb, pa, pb, pb, pb, pa, bias)


def _merge_body(ya_ref, yb_ref, hf_ref, hb_ref, o_ref, yd_ref, gate_ref, x_ref, mod_ref,
                mlg_ref, wbr_ref, wout_ref, g2_ref, wr_ref, xo_ref, h2_ref, st_ref):
    d = x_ref.shape[1]
    hs = hf_ref[...] + hb_ref[...]
    segs = []
    for h in range(ML_HEADS):
        seg = hs[:, h * ML_HEAD_DIM:(h + 1) * ML_HEAD_DIM]
        segs.append(seg * lax.rsqrt(jnp.mean(seg * seg, axis=-1, keepdims=True) + EPS))
    ym = jnp.concatenate(segs, axis=1) * mlg_ref[...] * jax.nn.sigmoid(o_ref[...].astype(F32))
    ys = (ya_ref[...], yb_ref[...], ym.astype(BF16), yd_ref[...])
    merged = None
    for i in range(N_BRANCHES):
        term = gate_ref[:, i * d:(i + 1) * d].astype(F32) * _dot(ys[i], wbr_ref[i])
        merged = term if merged is None else merged + term
    y = _dot(merged.astype(BF16), wout_ref[...])
    mod = mod_ref[0]
    x_new = x_ref[...] + mod[2:3] * y
    xo_ref[...] = x_new
    h2 = _rms_mod(x_new, g2_ref[...], mod[3:4], mod[4:5])
    h2_ref[...] = h2
    st_ref[...] = jax.nn.sigmoid(_dot_nt(wr_ref[...], h2.astype(BF16)))


def merge_layer(ya, yb, hf, hb, pb, col_o, yd, gate, x_all, modtab, mlg, wbr, wout, g2, wr_t,
                n_rows, n_lat_rows, n_batch):
    d = x_all.shape[1]
    tm = ROW_TILE
    w = BRANCH_WIDTH
    per_b = n_lat_rows // n_batch // tm
    row = lambda i: (i, 0)
    const2 = lambda i: (0, 0)
    return pl.pallas_call(
        _merge_body,
        grid=(n_rows // tm,),
        in_specs=[pl.BlockSpec((tm, w), row), pl.BlockSpec((tm, w), row), pl.BlockSpec((tm, w), row),
                  pl.BlockSpec((tm, w), row), pl.BlockSpec((tm, w), lambda i: (i, col_o // w)),
                  pl.BlockSpec((tm, w), row), pl.BlockSpec((tm, N_BRANCHES * d), row),
                  pl.BlockSpec((tm, d), row),
                  pl.BlockSpec((1, N_MOD, d), lambda i: (jnp.minimum(i // per_b, n_batch), 0, 0)),
                  pl.BlockSpec((1, w), const2),
                  pl.BlockSpec((N_BRANCHES, w, d), lambda i: (0, 0, 0)),
                  pl.BlockSpec((d, d), const2), pl.BlockSpec((1, d), const2),
                  pl.BlockSpec((LANES, d), const2)],
        out_specs=[pl.BlockSpec((tm, d), row), pl.BlockSpec((tm, d), row),
                   pl.BlockSpec((LANES, tm), lambda i: (0, i))],
        out_shape=[jax.ShapeDtypeStruct((n_rows, d), F32), jax.ShapeDtypeStruct((n_rows, d), F32),
                   jax.ShapeDtypeStruct((LANES, n_rows), F32)],
        compiler_params=_cp(("arbitrary",), VMEM_LIMIT),
        name="merge_layer",
    )(ya, yb, hf, hb, pb, yd, gate, x_all, modtab, mlg, wbr, wout, g2, wr_t)


def _router_body(s_ref, b_ref, e_ref, w_ref, rank_ref, cnt_ref, base_sc):
    @pl.when(pl.program_id(0) == 0)
    def _():
        base_sc[...] = jnp.zeros(base_sc.shape, F32)

    tm = s_ref.shape[1]
    s = s_ref[0:N_EXPERTS, :]
    sel = s + b_ref[0:N_EXPERTS, :]
    row = lambda a, e: a[e:e + 1, :]
    best, grp = None, None
    for g in range(N_EXPERT_GROUPS):
        v = [row(sel, EXPERTS_PER_GROUP * g + k) for k in range(EXPERTS_PER_GROUP)]
        gs = None
        for a in range(EXPERTS_PER_GROUP):
            for c in range(a + 1, EXPERTS_PER_GROUP):
                gs = v[a] + v[c] if gs is None else jnp.maximum(gs, v[a] + v[c])
        if best is None:
            best, grp = gs, jnp.zeros((1, tm), I32)
        else:
            better = gs > best
            grp = jnp.where(better, g, grp)
            best = jnp.where(better, gs, best)
    vals, affs = [], []
    for k in range(EXPERTS_PER_GROUP):
        vk, sk = row(sel, k), row(s, k)
        for g in range(1, N_EXPERT_GROUPS):
            hit = grp == g
            vk = jnp.where(hit, row(sel, EXPERTS_PER_GROUP * g + k), vk)
            sk = jnp.where(hit, row(s, EXPERTS_PER_GROUP * g + k), sk)
        vals.append(vk)
        affs.append(sk)
    i1, b1, w1 = jnp.zeros((1, tm), I32), vals[0], affs[0]
    for k in range(1, EXPERTS_PER_GROUP):
        better = vals[k] > b1
        i1 = jnp.where(better, k, i1)
        w1 = jnp.where(better, affs[k], w1)
        b1 = jnp.where(better, vals[k], b1)
    i2 = jnp.zeros((1, tm), I32)
    b2 = jnp.full((1, tm), -jnp.inf, F32)
    w2 = jnp.zeros((1, tm), F32)
    for k in range(EXPERTS_PER_GROUP):
        cand = (i1 != k) & (vals[k] > b2)
        i2 = jnp.where(cand, k, i2)
        w2 = jnp.where(cand, affs[k], w2)
        b2 = jnp.where(cand, vals[k], b2)
    e1 = grp * EXPERTS_PER_GROUP + i1
    e2 = grp * EXPERTS_PER_GROUP + i2
    tot = w1 + w2
    e_ref[...] = jnp.concatenate([e1, e2], axis=0)
    wpad = jnp.concatenate([w1 / tot, w2 / tot, jnp.zeros((6, tm), F32)], axis=0)
    w_ref[...] = wpad.T
    ids = lax.broadcasted_iota(I32, (N_EXPERTS, tm), 0)
    oh1 = (ids == e1).astype(F32)
    oh2 = (ids == e2).astype(F32)
    oh = oh1 + oh2
    before = (lax.broadcasted_iota(I32, (tm, tm), 0) < lax.broadcasted_iota(I32, (tm, tm), 1)).astype(BF16)
    prior = _dot(oh.astype(BF16), before) + base_sc[...]
    r1 = jnp.sum(oh1 * prior, axis=0, keepdims=True)
    r2 = jnp.sum(oh2 * prior, axis=0, keepdims=True)
    rank_ref[...] = jnp.concatenate([r1, r2], axis=0).astype(I32)
    base = base_sc[...] + jnp.sum(oh, axis=1, keepdims=True)
    base_sc[...] = base
    cnt_ref[...] = jnp.broadcast_to(base, cnt_ref.shape).astype(I32)


def route(s_t, b_router):
    n_rows = s_t.shape[1]
    tm = ROW_TILE
    b_col = jnp.pad(b_router.astype(F32), (0, LANES - N_EXPERTS)).reshape(LANES, 1)
    return pl.pallas_call(
        _router_body,
        grid=(n_rows // tm,),
        in_specs=[pl.BlockSpec((LANES, tm), lambda i: (0, i)), pl.BlockSpec((LANES, 1), lambda i: (0, 0))],
        out_specs=[pl.BlockSpec((TOP_K, tm), lambda i: (0, i)), pl.BlockSpec((tm, 8), lambda i: (i, 0)),
                   pl.BlockSpec((TOP_K, tm), lambda i: (0, i)), pl.BlockSpec((N_EXPERTS, LANES), lambda i: (0, 0))],
        out_shape=[jax.ShapeDtypeStruct((TOP_K, n_rows), I32), jax.ShapeDtypeStruct((n_rows, 8), F32),
                   jax.ShapeDtypeStruct((TOP_K, n_rows), I32), jax.ShapeDtypeStruct((N_EXPERTS, LANES), I32)],
        scratch_shapes=[pltpu.VMEM((N_EXPERTS, 1), F32)],
        compiler_params=_cp(("arbitrary",)),
        name="moe_router",
    )(s_t, b_col)


def _row_copy(src_ref, src_row, dst_ref, dst_row, sem):
    return pltpu.make_async_copy(src_ref.at[pl.ds(src_row, 1), :], dst_ref.at[pl.ds(dst_row, 1), :], sem)


def _dispatch_body(dest_ref, h_ref, buf_in_ref, buf_ref, sem):
    del buf_in_ref
    tm = h_ref.shape[0]

    def issue(t, carry):
        for k in range(TOP_K):
            _row_copy(h_ref, t, buf_ref, dest_ref[0, k, t], sem).start()
        return carry

    lax.fori_loop(0, tm, issue, 0, unroll=8)
    for k in range(TOP_K):
        pltpu.make_async_copy(h_ref, buf_ref.at[pl.ds(0, tm), :], sem).wait()


def moe_dispatch(h2, dest3, buf0):
    n_rows, d = h2.shape
    tm = ROW_TILE
    return pl.pallas_call(
        _dispatch_body,
        grid=(n_rows // tm,),
        in_specs=[pl.BlockSpec((1, TOP_K, tm), lambda i: (i, 0, 0), memory_space=pltpu.SMEM),
                  pl.BlockSpec((tm, d), lambda i: (i, 0)),
                  pl.BlockSpec(memory_space=pl.ANY)],
        out_specs=pl.BlockSpec(memory_space=pl.ANY),
        out_shape=jax.ShapeDtypeStruct(buf0.shape, buf0.dtype),
        scratch_shapes=[pltpu.SemaphoreType.DMA(())],
        input_output_aliases={2: 0},
        compiler_params=_cp(("arbitrary",)),
        name="moe_dispatch",
    )(dest3, h2, buf0)


def _expert_body(be_ref, x_ref, w1_ref, w3_ref, w2_ref, o_ref, w1_sc, w3_sc, w2_sc):
    i = pl.program_id(0)
    changed = jnp.logical_or(i == 0, be_ref[i] != be_ref[jnp.maximum(i - 1, 0)])

    @pl.when(changed)
    def _():
        w1_sc[...] = w1_ref[0].astype(BF16)
        w3_sc[...] = w3_ref[0].astype(BF16)
        w2_sc[...] = w2_ref[0].astype(BF16)

    x = x_ref[...].astype(BF16)
    a = _dot(x, w1_sc[...])
    mid = (a * jax.nn.sigmoid(a)) * _dot(x, w3_sc[...])
    o_ref[...] = _dot(mid.astype(BF16), w2_sc[...])


def moe_experts(buf, blk_expert, w1, w3, w2, layer, blk):
    n_slots, d = buf.shape
    de = w1.shape[3]
    grid_spec = pltpu.PrefetchScalarGridSpec(
        num_scalar_prefetch=1,
        grid=(n_slots // blk,),
        in_specs=[pl.BlockSpec((blk, d), lambda i, be: (i, 0)),
                  pl.BlockSpec((None, 1, d, de), lambda i, be: (layer, be[i], 0, 0)),
                  pl.BlockSpec((None, 1, d, de), lambda i, be: (layer, be[i], 0, 0)),
                  pl.BlockSpec((None, 1, de, d), lambda i, be: (layer, be[i], 0, 0))],
        out_specs=pl.BlockSpec((blk, d), lambda i, be: (i, 0)),
        scratch_shapes=[pltpu.VMEM((d, de), BF16), pltpu.VMEM((d, de), BF16), pltpu.VMEM((de, d), BF16)],
    )
    return pl.pallas_call(
        _expert_body,
        grid_spec=grid_spec,
        out_shape=jax.ShapeDtypeStruct((n_slots, d), F32),
        compiler_params=_cp(("arbitrary",), VMEM_LIMIT),
        name="moe_experts",
    )(blk_expert, buf, w1, w3, w2)


def _combine_body(final, dest_ref, x_ref, w_ref, mod_ref, gf_ref, y_hbm, o_ref, y0_sc, y1_sc, sem):
    tm = x_ref.shape[0]
    bufs = (y0_sc, y1_sc)

    def issue(t, carry):
        for k in range(TOP_K):
            _row_copy(y_hbm, dest_ref[0, k, t], bufs[k], t, sem).start()
        return carry

    lax.fori_loop(0, tm, issue, 0, unroll=8)
    for k in range(TOP_K):
        pltpu.make_async_copy(y_hbm.at[pl.ds(0, tm), :], bufs[k], sem).wait()
    w = w_ref[...]
    f = w[:, 0:1] * y0_sc[...] + w[:, 1:2] * y1_sc[...]
    x_new = x_ref[...] + mod_ref[0][5:6] * f
    if final:
        ms = jnp.mean(x_new * x_new, axis=-1, keepdims=True)
        x_new = x_new * lax.rsqrt(ms + EPS) * gf_ref[...]
    o_ref[...] = x_new


def moe_combine(dest3, x_rows, wts, modtab, g_final, y_slots, n_lat_rows, n_batch, final):
    n_rows, d = x_rows.shape
    tm = ROW_TILE
    per_b = n_lat_rows // n_batch // tm
    return pl.pallas_call(
        functools.partial(_combine_body, final),
        grid=(n_rows // tm,),
        in_specs=[pl.BlockSpec((1, TOP_K, tm), lambda i: (i, 0, 0), memory_space=pltpu.SMEM),
                  pl.BlockSpec((tm, d), lambda i: (i, 0)),
                  pl.BlockSpec((tm, 8), lambda i: (i, 0)),
                  pl.BlockSpec((1, N_MOD, d), lambda i: (jnp.minimum(i // per_b, n_batch), 0, 0)),
                  pl.BlockSpec((1, d), lambda i: (0, 0)),
                  pl.BlockSpec(memory_space=pl.ANY)],
        out_specs=pl.BlockSpec((tm, d), lambda i: (i, 0)),
        out_shape=jax.ShapeDtypeStruct((n_rows, d), F32),
        scratch_shapes=[pltpu.VMEM((tm, d), F32), pltpu.VMEM((tm, d), F32), pltpu.SemaphoreType.DMA(())],
        compiler_params=_cp(("arbitrary",)),
        name="moe_combine",
    )(dest3, x_rows, wts, modtab, g_final.reshape(1, d), y_slots)


def moe_layer(x_rows, h2, s_t, b_router, w1, w3, w2, layer, modtab, g_final, n_lat_rows, n_batch, final):
    n_rows, d = h2.shape
    blk = 2 * MOE_BLOCK
    experts, wts, rank, counts = route(s_t, b_router)
    cnt = counts[:, 0]
    padded = (cnt + blk - 1) // blk * blk
    pend = jnp.cumsum(padded)
    pstart = pend - padded
    hit = experts[..., None] == jnp.arange(N_EXPERTS, dtype=I32)
    dest = jnp.sum(jnp.where(hit, pstart.astype(I32), 0), axis=-1) + rank
    n_blocks = -(-(n_rows * TOP_K) // blk) + N_EXPERTS
    blk_start = jnp.arange(n_blocks, dtype=I32) * blk
    blk_expert = jnp.minimum(jnp.sum((pend[None, :] <= blk_start[:, None]).astype(I32), axis=1), N_EXPERTS - 1)
    dest3 = dest.reshape(TOP_K, n_rows // ROW_TILE, ROW_TILE).transpose(1, 0, 2)
    buf = moe_dispatch(h2, dest3, jnp.zeros((n_blocks * blk, d), F32))
    y_slots = moe_experts(buf, blk_expert, w1, w3, w2, layer, blk)
    return moe_combine(dest3, x_rows, wts, modtab, g_final, y_slots, n_lat_rows, n_batch, final)


_COL = dict(na_q=0, na_k=512, na_v=1024, ml_q=1536, ml_k=2048, ml_v=2560, ml_o=3072,
            gq_q=3584, gq_k=4096, gq_v=4224)


def _split_w_in(w_in):
    sizes = (BRANCH_WIDTH,) * 8 + (4 * ML_HEADS, BRANCH_WIDTH, GQ_KV_HEADS * GQ_HEAD_DIM, GQ_KV_HEADS * GQ_HEAD_DIM)
    idx = np.cumsum(sizes)[:-1].tolist()
    (s5_u, na_q, na_k, na_v, ml_q, ml_k, ml_v, ml_o, ml_gt, gq_q, gq_k, gq_v) = jnp.split(w_in, idx, axis=-1)
    wa = jnp.pad(ml_gt, ((0, 0), (0, LANES - 4 * ML_HEADS)))
    wb = jnp.concatenate([na_q, na_k, na_v, ml_q, ml_k * (ML_HEAD_DIM ** -0.5), ml_v, ml_o, gq_q, gq_k, gq_v], axis=1)
    return s5_u.T.astype(BF16), wa.astype(BF16), wb.astype(BF16)


def kernel(x, c, ctx, c_ctx, w_mod, b_mod, g_norm1, g_norm2, w_in, s5_lam_re, s5_lam_im, s5_log_dt, s5_b_re,
           s5_b_im, s5_c_re, s5_c_im, s5_d, s5_w_glu, s5_b_glu, na_rpb, ml_b_gates, ml_norm, gq_qnorm, gq_knorm,
           w_branch, w_gate, b_gate, w_out, w_router, b_router, moe_w1, moe_w3, moe_w2, g_final):
    b, n_lat, dm = x.shape
    n_ctx = ctx.shape[1]
    depth = w_in.shape[0]
    bn, bc = b * n_lat, b * n_ctx
    x_all = jnp.concatenate([x.reshape(bn, dm), ctx.reshape(bc, dm)], axis=0).astype(F32)
    c_all = jnp.concatenate([c.astype(F32), c_ctx.astype(F32)[None], jnp.zeros((8 - b - 1, dm), F32)], axis=0)
    cs_tab = rope_tables(n_lat)
    wr_t = jnp.pad(w_router.astype(BF16).T, ((0, LANES - N_EXPERTS), (0, 0)))
    out = None
    for l in range(depth):
        last = l == depth - 1
        with_ctx = not last
        modtab = mod_vectors(c_all, w_mod, b_mod, l)[:b + 1].reshape(b + 1, N_MOD, dm)
        w_u_t, wa, wb = _split_w_in(w_in[l])
        pa, pb, gate, h_all = in_projection(x_all, g_norm1[l], modtab, wa, wb, w_gate[l].astype(BF16), b_gate[l],
                                            bn, b)

        tables = s5_tables(s5_lam_re[l], s5_lam_im[l], s5_log_dt[l], s5_b_re[l], s5_b_im[l],
                           s5_c_re[l], s5_c_im[l], s5_d[l])
        ya = s5_mixer(h_all, w_u_t, tables, s5_w_glu[l].astype(BF16), s5_b_glu[l].astype(F32).reshape(1, -1),
                      n_lat, n_ctx, b)

        bias_tab = na_bias_tables(na_rpb[l], n_lat // GRID_W, n_ctx)
        yb = na_mixer(pb, _COL['na_q'], _COL['na_k'], _COL['na_v'], bias_tab, n_lat, n_ctx, b)

        ml_bias = jnp.pad(ml_b_gates[l].astype(F32), (0, LANES - 4 * ML_HEADS)).reshape(1, LANES)
        hf, hb = mlstm_mixer(pb, pa, _COL['ml_q'], _COL['ml_k'], _COL['ml_v'], 0, ml_bias, n_lat, n_ctx, b)

        yd = gq_mixer(pb, _COL['gq_q'], _COL['gq_k'], _COL['gq_v'], cs_tab, gq_qnorm[l], gq_knorm[l],
                      n_lat, n_ctx, b, with_ctx)
        if with_ctx:
            yb = jnp.concatenate([yb, na_ctx_attention(pb, _COL['na_q'], _COL['na_k'], _COL['na_v'],
                                                       n_lat, n_ctx, b)], axis=0)
        n_rows = bn + bc if with_ctx else bn
        x_mid, h2, s_t = merge_layer(
            ya, yb, hf, hb, pb, _COL['ml_o'], yd, gate, x_all, modtab, ml_norm[l].astype(F32).reshape(1, -1),
            w_branch[l].astype(BF16), w_out[l].astype(BF16), g_norm2[l].astype(F32).reshape(1, -1), wr_t,
            n_rows, bn, b)
        x_next = moe_layer(x_mid, h2, s_t, b_router, moe_w1, moe_w3, moe_w2, l, modtab, g_final, bn, b, last)
        if last:
            out = x_next.reshape(b, n_lat, dm).astype(x.dtype)
        else:
            x_all = x_next
    return out
```

```python
import functools
import math

import numpy as np
import jax
import jax.numpy as jnp
from jax import lax
from jax.experimental import pallas as pl
from jax.experimental.pallas import tpu as pltpu

F32 = jnp.float32
BF16 = jnp.bfloat16
I32 = jnp.int32

GRID_W = 64
N_MOD = 6
BRANCH_WIDTH = 512
N_BRANCHES = 4
S5_GROUP = 16
S5_GROUPS = BRANCH_WIDTH // S5_GROUP
S5_STATE = 64
NA_HEADS = 8
NA_HEAD_DIM = 64
NA_ROWS = 8
NA_COLS = 16
ML_HEADS = 4
ML_HEAD_DIM = 128
ML_CHUNK = 128
GQ_HEADS = 8
GQ_KV_HEADS = 2
GQ_HEAD_DIM = 64
ROPE_THETA = 10000.0
N_EXPERTS = 32
N_EXPERT_GROUPS = 8
EXPERTS_PER_GROUP = 4
TOP_K = 2
D_EXPERT = 512
MOE_BLOCK = 128
EPS = 1e-6
NEG_INIT = -1e30
MASK_NEG = -1e30
LOG2E = 1.4426950408889634

LANES = 128
ROW_TILE = 256
S5_CHUNK = 32
NA_TILE_ROWS = 4
VMEM_LIMIT = 56 * 1024 * 1024

HIGHEST = lax.Precision.HIGHEST


def _cp(sem, vmem=None):
    return pltpu.CompilerParams(dimension_semantics=sem, vmem_limit_bytes=vmem)


def _dot(a, b):
    return jnp.dot(a, b, preferred_element_type=F32)


def _dot_nt(a, b):
    return lax.dot_general(a, b, (((1,), (1,)), ((), ())), preferred_element_type=F32)


def _dot_tn(a, b):
    return lax.dot_general(a, b, (((0,), (0,)), ((), ())), preferred_element_type=F32)


def _rms_mod(x, g, shift, scale):
    ms = jnp.mean(x * x, axis=-1, keepdims=True)
    y = x * lax.rsqrt(ms + EPS) * g
    return y * (1.0 + scale) + shift


def _round_up(n, m):
    return -(-n // m) * m


def _mod_body(c_ref, w_ref, b_ref, o_ref):
    c = c_ref[...]
    a = (c * jax.nn.sigmoid(c)).astype(BF16)
    o_ref[...] = _dot(a, w_ref[...].astype(BF16)) + b_ref[...]


def mod_vectors(c_all, w_mod, b_mod, layer):
    d = c_all.shape[1]
    depth = w_mod.shape[0]
    return pl.pallas_call(
        _mod_body,
        grid=(N_MOD,),
        in_specs=[pl.BlockSpec((8, d), lambda j: (0, 0)),
                  pl.BlockSpec((None, d, d), lambda j: (layer, 0, j)),
                  pl.BlockSpec((None, 1, d), lambda j: (layer, 0, j))],
        out_specs=pl.BlockSpec((8, d), lambda j: (0, j)),
        out_shape=jax.ShapeDtypeStruct((8, N_MOD * d), F32),
        compiler_params=_cp(("arbitrary",)),
        name="mod_vectors",
    )(c_all, w_mod, b_mod.reshape(depth, 1, -1))


def _inproj_body(x_ref, g_ref, mod_ref, wa_ref, wb_ref, wg_ref, bg_ref, oa_ref, ob_ref, og_ref, oh_ref):
    mod = mod_ref[0]
    h = _rms_mod(x_ref[...], g_ref[...], mod[0:1], mod[1:2]).astype(BF16)
    oh_ref[...] = h
    oa_ref[...] = _dot(h, wa_ref[...])
    ob_ref[...] = _dot(h, wb_ref[...]).astype(BF16)
    og_ref[...] = jax.nn.sigmoid(_dot(h, wg_ref[...]) + bg_ref[...]).astype(BF16)


def in_projection(x_all, g, modtab, wa, wb, wg, bg, n_lat_rows, n_batch):
    r, d = x_all.shape
    tm = ROW_TILE
    per_b = n_lat_rows // n_batch // tm

    def mod_idx(i):
        return (jnp.minimum(i // per_b, n_batch), 0, 0)

    const = lambda i: (0, 0)
    return pl.pallas_call(
        _inproj_body,
        grid=(r // tm,),
        in_specs=[pl.BlockSpec((tm, d), lambda i: (i, 0)),
                  pl.BlockSpec((1, d), const),
                  pl.BlockSpec((1, N_MOD, d), mod_idx),
                  pl.BlockSpec(wa.shape, const, pipeline_mode=pl.Buffered(1)),
                  pl.BlockSpec(wb.shape, const, pipeline_mode=pl.Buffered(1)),
                  pl.BlockSpec(wg.shape, const, pipeline_mode=pl.Buffered(1)),
                  pl.BlockSpec((1, wg.shape[1]), const)],
        out_specs=[pl.BlockSpec((tm, wa.shape[1]), lambda i: (i, 0)),
                   pl.BlockSpec((tm, wb.shape[1]), lambda i: (i, 0)),
                   pl.BlockSpec((tm, wg.shape[1]), lambda i: (i, 0)),
                   pl.BlockSpec((tm, d), lambda i: (i, 0))],
        out_shape=[jax.ShapeDtypeStruct((r, wa.shape[1]), F32),
                   jax.ShapeDtypeStruct((r, wb.shape[1]), BF16),
                   jax.ShapeDtypeStruct((r, wg.shape[1]), BF16),
                   jax.ShapeDtypeStruct((r, d), BF16)],
        compiler_params=_cp(("arbitrary",), VMEM_LIMIT),
        name="in_projection",
    )(x_all, g.reshape(1, d), modtab, wa, wb, wg, bg.reshape(1, -1))


def s5_tables(lam_re, lam_im, log_dt, b_re, b_im, c_re, c_im, d_skip):
    ell, g_n, p_n, c_n = S5_CHUNK, S5_GROUPS, S5_STATE, S5_GROUP
    lam_re, lam_im = lam_re.astype(F32), lam_im.astype(F32)
    b_re, b_im, c_re, c_im = (t.astype(F32) for t in (b_re, b_im, c_re, c_im))
    dt = jnp.exp(log_dt.astype(F32))[..., None]
    mag = jnp.exp(lam_re * dt)
    a_re = mag * jnp.cos(lam_im * dt)
    a_im = mag * jnp.sin(lam_im * dt)
    den = lam_re * lam_re + lam_im * lam_im
    nr = a_re - 1.0
    f_re = (nr * lam_re + a_im * lam_im) / den
    f_im = (a_im * lam_re - nr * lam_im) / den
    bb_re = f_re[..., None] * b_re - f_im[..., None] * b_im
    bb_im = f_re[..., None] * b_im + f_im[..., None] * b_re
    k = jnp.arange(ell + 1, dtype=F32)
    pmag = jnp.exp((lam_re * dt)[..., None] * k)
    ang = (lam_im * dt)[..., None] * k
    pr, pi = pmag * jnp.cos(ang), pmag * jnp.sin(ang)
    ab_re = pr[..., None] * bb_re[:, :, :, None, :] - pi[..., None] * bb_im[:, :, :, None, :]
    ab_im = pr[..., None] * bb_im[:, :, :, None, :] + pi[..., None] * bb_re[:, :, :, None, :]
    flat = lambda t: t.reshape(2 * g_n, p_n, (ell + 1) * c_n)
    kk = (jnp.einsum('bcp,bpn->bcn', c_re.reshape(2 * g_n, c_n, p_n), flat(ab_re), precision=HIGHEST)
          - jnp.einsum('bcp,bpn->bcn', c_im.reshape(2 * g_n, c_n, p_n), flat(ab_im), precision=HIGHEST))
    kk = kk.reshape(2, g_n, c_n, ell + 1, c_n)
    centre = kk[0][:, :, 0] + kk[1][:, :, 0] + d_skip.astype(F32).reshape(g_n, c_n, 1) * jnp.eye(c_n, dtype=F32)
    w = jnp.concatenate([kk[0][:, :, ell - 1:0:-1], centre[:, :, None], kk[1][:, :, 1:ell]], axis=2)
    wf = w.reshape(g_n, c_n, (2 * ell - 1) * c_n)
    toe = jnp.stack([wf[:, :, (ell - 1 - t) * c_n:(2 * ell - 1 - t) * c_n] for t in range(ell)], axis=1)
    tsum_t = toe.astype(BF16).reshape(g_n, ell * c_n, ell * c_n)

    parity = [jnp.asarray(np.arange(g_n) % 2 == q, F32) for q in range(2)]
    parts = []
    for d in range(2):
        for ab in (ab_re, ab_im):
            sel = ab[d][:, :, :ell]
            if d == 0:
                sel = sel[:, :, ::-1]
            sel = sel.reshape(g_n, p_n, ell * c_n)
            parts += [sel * parity[q][:, None, None] for q in range(2)]
    mend_t = jnp.stack(parts, axis=1).reshape(g_n, 8 * p_n, ell * c_n)

    rows = []
    for d in range(2):
        prk, pik = pr[d][:, :, 1:ell + 1], pi[d][:, :, 1:ell + 1]
        if d == 1:
            prk, pik = prk[:, :, ::-1], pik[:, :, ::-1]
        prk = prk.transpose(0, 2, 1)[:, :, None, :]
        pik = pik.transpose(0, 2, 1)[:, :, None, :]
        cr, ci = c_re[d][:, None], c_im[d][:, None]
        for part in (cr * prk - ci * pik, -cr * pik - ci * prk):
            rows += [part * parity[q][:, None, None, None] for q in range(2)]
    wst_t = jnp.concatenate(rows, axis=-1).reshape(g_n, ell * c_n, 8 * p_n)

    al = jnp.stack([pr[0][:, :, ell], pi[0][:, :, ell], pr[1][:, :, ell], pi[1][:, :, ell]], axis=1)
    a_chunk = al.reshape(g_n // 2, 2, 4, p_n).transpose(0, 2, 1, 3).reshape(g_n // 2, 8 * p_n)
    return tsum_t.astype(BF16), mend_t.astype(BF16), wst_t.astype(BF16), a_chunk


def _s5_proj_body(hl_ref, hc_ref, w_ref, o_ref):
    n_lat = hl_ref.shape[0]
    n_pad = o_ref.shape[1] - LANES
    hl = hl_ref[...]
    if n_pad > n_lat:
        hl = jnp.concatenate([hl, jnp.zeros((n_pad - n_lat, hl.shape[1]), BF16)], axis=0)
    o_ref[:, 0:n_pad] = _dot_nt(w_ref[...], hl).astype(BF16)
    hc = hc_ref[...]
    hc = jnp.concatenate([hc, jnp.zeros((LANES - hc.shape[0], hc.shape[1]), BF16)], axis=0)
    o_ref[:, n_pad:] = _dot_nt(w_ref[...], hc).astype(BF16)


def _s5_end_body(u_ref, m_ref, o_ref):
    width = m_ref.shape[1]
    acc = None
    for q in range(2):
        u = u_ref[:, q].reshape(width, u_ref.shape[3])
        term = _dot(m_ref[q], u)
        acc = term if acc is None else acc + term
    o_ref[...] = acc.T


def _s5_scan_body(n_batch, per_b, n_ctx_chunks, lat_pad, e_ref, a_ref, o_ref):
    n_pairs = e_ref.shape[0]
    o_ref[...] = jnp.zeros(o_ref.shape, F32)
    n_steps = per_b + n_ctx_chunks
    first = pl.program_id(0) * n_pairs
    coef = [[a_ref[first + q, j:j + 1, :] for j in range(4)] for q in range(n_pairs)]

    def step(i, carry):
        new = []
        for b in range(n_batch):
            fwd = jnp.where(i < n_ctx_chunks, lat_pad + b * n_ctx_chunks + i, b * per_b + i - n_ctx_chunks)
            bwd = jnp.where(i < n_ctx_chunks, lat_pad + b * n_ctx_chunks + n_ctx_chunks - 1 - i,
                            b * per_b + per_b - 1 - (i - n_ctx_chunks))
            for q in range(n_pairs):
                for d, row in enumerate((fwd, bwd)):
                    sr, si = carry[((b * n_pairs + q) * 2 + d) * 2:((b * n_pairs + q) * 2 + d) * 2 + 2]
                    ar, ai = coef[q][2 * d], coef[q][2 * d + 1]
                    o_ref[q, row, 2 * d:2 * d + 1, :] = sr
                    o_ref[q, row, 2 * d + 1:2 * d + 2, :] = si
                    er = e_ref[q, row, 2 * d:2 * d + 1, :]
                    ei = e_ref[q, row, 2 * d + 1:2 * d + 2, :]
                    new += [ar * sr - ai * si + er, ar * si + ai * sr + ei]
        return tuple(new)

    z = jnp.zeros((1, LANES), F32)
    lax.fori_loop(0, n_steps, step, tuple(z for _ in range(n_batch * n_pairs * 4)))


def _s5_out_body(u_ref, t_ref, s_ref, w_ref, o_ref):
    width = t_ref.shape[0]
    u = u_ref[...].reshape(width, u_ref.shape[2])
    y = _dot(t_ref[...], u) + _dot_nt(w_ref[...], s_ref[...].astype(BF16))
    o_ref[...] = y.reshape(o_ref.shape)


def _s5_glu_body(n_lat_chunks, n_ctx_chunks, lat_pad, y_ref, w_ref, b_ref, o_ref):
    y = y_ref[...].T
    if lat_pad == n_lat_chunks:
        y = y[:n_lat_chunks + n_ctx_chunks]
    else:
        y = jnp.concatenate([y[:n_lat_chunks], y[lat_pad:lat_pad + n_ctx_chunks]], axis=0)
    z = jax.nn.gelu(y)
    o_ref[...] = (z * jax.nn.sigmoid(_dot(z.astype(BF16), w_ref[...]) + b_ref[...])).astype(o_ref.dtype)


def s5_mixer(h_all, w_u_t, tables, w_glu, b_glu, n_lat, n_ctx, n_batch):
    tsum_t, mend_t, wst_t, a_chunk = tables
    r, d = h_all.shape
    ell, g_n, c_n = S5_CHUNK, S5_GROUPS, S5_GROUP
    width = ell * c_n
    bw = g_n * c_n
    n_lat_chunks = n_batch * n_lat // ell
    n_ctx_chunks = n_batch * n_ctx // ell
    assert n_lat_chunks % n_ctx_chunks == 0 and n_ctx_chunks % 16 == 0 and n_ctx_chunks <= LANES
    lat_pad = _round_up(n_lat_chunks, LANES)
    nch = lat_pad + LANES
    h2 = h_all.reshape(r // ell, ell * d)
    const = lambda t: (0, 0)
    u_t = pl.pallas_call(
        _s5_proj_body,
        grid=(ell,),
        in_specs=[pl.BlockSpec((n_lat_chunks, d), lambda t: (0, t)),
                  pl.BlockSpec((n_ctx_chunks, d), lambda t: (n_lat_chunks // n_ctx_chunks, t)),
                  pl.BlockSpec((bw, d), const)],
        out_specs=pl.BlockSpec((None, bw, nch), lambda t: (t, 0, 0)),
        out_shape=jax.ShapeDtypeStruct((ell, bw, nch), BF16),
        compiler_params=_cp(("arbitrary",)),
        name="s5_projection",
    )(h2, h2, w_u_t)
    u4 = u_t.reshape(ell, g_n, c_n, nch)
    ends = pl.pallas_call(
        _s5_end_body,
        grid=(g_n // 2,),
        in_specs=[pl.BlockSpec((ell, 2, c_n, nch), lambda p: (0, p, 0, 0)),
                  pl.BlockSpec((2, width, width), lambda p: (p, 0, 0))],
        out_specs=pl.BlockSpec((None, nch, width), lambda p: (p, 0, 0)),
        out_shape=jax.ShapeDtypeStruct((g_n // 2, nch, width), F32),
        compiler_params=_cp(("arbitrary",)),
        name="s5_chunk_ends",
    )(u4, mend_t)
    pairs_per_step = 2
    quad = (g_n // 2, nch, 4, LANES)
    states = pl.pallas_call(
        functools.partial(_s5_scan_body, n_batch, n_lat // ell, n_ctx // ell, lat_pad),
        grid=(g_n // 2 // pairs_per_step,),
        in_specs=[pl.BlockSpec((pairs_per_step, nch, 4, LANES), lambda j: (j, 0, 0, 0)),
                  pl.BlockSpec((g_n // 2, 4, LANES), lambda j: (0, 0, 0))],
        out_specs=pl.BlockSpec((pairs_per_step, nch, 4, LANES), lambda j: (j, 0, 0, 0)),
        out_shape=jax.ShapeDtypeStruct(quad, F32),
        compiler_params=_cp(("arbitrary",), VMEM_LIMIT),
        name="s5_state_scan",
    )(ends.reshape(quad), a_chunk.reshape(g_n // 2, 4, LANES)).reshape(g_n // 2, nch, width)
    y_t = pl.pallas_call(
        _s5_out_body,
        grid=(g_n,),
        in_specs=[pl.BlockSpec((ell, None, c_n, nch), lambda gi: (0, gi, 0, 0)),
                  pl.BlockSpec((None, width, width), lambda gi: (gi, 0, 0)),
                  pl.BlockSpec((None, nch, width), lambda gi: (gi // 2, 0, 0)),
                  pl.BlockSpec((None, width, width), lambda gi: (gi, 0, 0))],
        out_specs=pl.BlockSpec((ell, None, c_n, nch), lambda gi: (0, gi, 0, 0)),
        out_shape=jax.ShapeDtypeStruct((ell, g_n, c_n, nch), F32),
        compiler_params=_cp(("arbitrary",)),
        name="s5_outputs",
    )(u4, tsum_t, states, wst_t)
    n_chunks = n_lat_chunks + n_ctx_chunks
    ya = pl.pallas_call(
        functools.partial(_s5_glu_body, n_lat_chunks, n_ctx_chunks, lat_pad),
        grid=(ell,),
        in_specs=[pl.BlockSpec((None, bw, nch), lambda t: (t, 0, 0)),
                  pl.BlockSpec((bw, bw), const),
                  pl.BlockSpec((1, bw), const)],
        out_specs=pl.BlockSpec((n_chunks, bw), lambda t: (0, t)),
        out_shape=jax.ShapeDtypeStruct((n_chunks, ell * bw), BF16),
        compiler_params=_cp(("arbitrary",)),
        name="s5_glu",
    )(y_t.reshape(ell, bw, nch), w_glu, b_glu)
    return ya.reshape(r, bw)


def rope_tables(n_lat):
    half = GQ_HEAD_DIM // 2
    quarter = half // 2
    t = np.arange(n_lat)
    freqs = ROPE_THETA ** (-np.arange(quarter, dtype=np.float64) / quarter)
    ang_r = (t // GRID_W)[:, None] * freqs
    ang_c = (t % GRID_W)[:, None] * freqs
    ang = np.concatenate([ang_r, ang_r, ang_c, ang_c], axis=1)
    sign = np.concatenate([-np.ones(quarter), np.ones(quarter)] * 2)
    cos = np.concatenate([np.cos(ang), np.ones((ROW_TILE, GQ_HEAD_DIM))], axis=0)
    sin = np.concatenate([np.sin(ang) * sign, np.zeros((ROW_TILE, GQ_HEAD_DIM))], axis=0)
    tab = np.concatenate([cos, cos, sin, sin], axis=1)
    return jnp.asarray(tab, F32)


def _group_ones(width, group):
    i = np.arange(width)
    return jnp.asarray((i[:, None] // group) == (i[None, :] // group), BF16)


def _group_mean_sq(x, ones_blk, group):
    sq = x * x
    hi = sq.astype(BF16)
    lo = (sq - hi.astype(F32)).astype(BF16)
    return (_dot(hi, ones_blk) + _dot(lo, ones_blk)) * (1.0 / group)


def _rope(x, cos, sin):
    w = x.shape[-1]
    q = GQ_HEAD_DIM // 4
    lane = lax.broadcasted_iota(I32, x.shape, 1)
    first = (lane % (2 * q)) < q
    partner = jnp.where(first, pltpu.roll(x, w - q, 1), pltpu.roll(x, q, 1))
    return x * cos + partner * sin


def _gq_prep_body(q_ref, k_ref, v_ref, cs_ref, gq_ref, gk_ref, oq_ref, ok_ref, qm_ref, kr_ref, va_ref):
    cs = cs_ref[...]
    cos1, sin1 = cs[:, :LANES], cs[:, LANES:]
    q = q_ref[...].astype(F32)
    qn = q * lax.rsqrt(_group_mean_sq(q, oq_ref[...], GQ_HEAD_DIM) + EPS) * gq_ref[...]
    n_pairs = q.shape[1] // LANES
    qr = _rope(qn, jnp.concatenate([cos1] * n_pairs, axis=1), jnp.concatenate([sin1] * n_pairs, axis=1))
    qr = (qr * (GQ_HEAD_DIM ** -0.5 * LOG2E)).astype(BF16)
    lane = lax.broadcasted_iota(I32, (q.shape[0], LANES), 1)
    heads_per_kv = GQ_HEADS // GQ_KV_HEADS
    for h in range(GQ_HEADS):
        pair = qr[:, (h // 2) * LANES:(h // 2 + 1) * LANES]
        kv = h // heads_per_kv
        if h % 2 != kv:
            pair = pltpu.roll(pair, GQ_HEAD_DIM, 1)
        keep = (lane >= kv * GQ_HEAD_DIM) & (lane < (kv + 1) * GQ_HEAD_DIM)
        qm_ref[h] = jnp.where(keep, pair, jnp.zeros_like(pair))
    k = k_ref[...].astype(F32)
    kn = k * lax.rsqrt(_group_mean_sq(k, ok_ref[...], GQ_HEAD_DIM) + EPS) * gk_ref[...]
    kr_ref[...] = _rope(kn, cos1, sin1).astype(BF16)
    va_ref[...] = jnp.concatenate([v_ref[...], jnp.ones(v_ref.shape, BF16)], axis=1)


def gq_prepare(pb, col_q, col_k, col_v, cs_tab, g_q, g_k, n_lat, n_ctx, n_batch):
    r = pb.shape[0]
    tm = ROW_TILE
    assert n_ctx == tm and n_lat % tm == 0
    nb = n_lat // tm
    n_lat_tiles = n_batch * nb

    def tab_idx(i):
        return (jnp.where(i < n_lat_tiles, i % nb, nb), 0)

    def kv_idx(i):
        lat = (i // nb) * (nb + 1) + i % nb
        ctx = (i - n_lat_tiles) * (nb + 1) + nb
        return (jnp.where(i < n_lat_tiles, lat, ctx), 0)

    qw = GQ_HEADS * GQ_HEAD_DIM
    const = lambda i: (0, 0)
    gq = jnp.tile(g_q.astype(F32), GQ_HEADS).reshape(1, qw)
    gk = jnp.tile(g_k.astype(F32), GQ_KV_HEADS).reshape(1, LANES)
    n_keys = n_batch * (n_lat + n_ctx)
    return pl.pallas_call(
        _gq_prep_body,
        grid=(r // tm,),
        in_specs=[pl.BlockSpec((tm, qw), lambda i: (i, col_q // qw)),
                  pl.BlockSpec((tm, LANES), lambda i: (i, col_k // LANES)),
                  pl.BlockSpec((tm, LANES), lambda i: (i, col_v // LANES)),
                  pl.BlockSpec((tm, 2 * LANES), tab_idx),
                  pl.BlockSpec((1, qw), const),
                  pl.BlockSpec((1, LANES), const),
                  pl.BlockSpec((qw, qw), const),
                  pl.BlockSpec((LANES, LANES), const)],
        out_specs=[pl.BlockSpec((GQ_HEADS, tm, LANES), lambda i: (0, i, 0)),
                   pl.BlockSpec((tm, LANES), kv_idx),
                   pl.BlockSpec((tm, 2 * LANES), kv_idx)],
        out_shape=[jax.ShapeDtypeStruct((GQ_HEADS, r, LANES), BF16),
                   jax.ShapeDtypeStruct((n_keys, LANES), BF16),
                   jax.ShapeDtypeStruct((n_keys, 2 * LANES), BF16)],
        compiler_params=_cp(("arbitrary",)),
        name="gq_prepare",
    )(pb, pb, pb, cs_tab, gq, gk, _group_ones(qw, GQ_HEAD_DIM), _group_ones(LANES, GQ_HEAD_DIM))


def _gq_flash_body(q_ref, k_ref, v_ref, o_ref, m_sc, acc_sc):
    kj = pl.program_id(2)
    n_h, tq, _ = q_ref.shape

    @pl.when(kj == 0)
    def _():
        m_sc[...] = jnp.full(m_sc.shape, -jnp.inf, F32)
        acc_sc[...] = jnp.zeros(acc_sc.shape, F32)

    k = k_ref[...]
    v = v_ref[...]
    hp = 1
    for c in range(n_h // hp):
        rows = slice(c * hp * tq, (c + 1) * hp * tq)
        s = _dot_nt(q_ref[c * hp:(c + 1) * hp].reshape(hp * tq, LANES), k)
        m_prev = m_sc[rows, :]
        m_new = jnp.maximum(m_prev, jnp.max(s, axis=-1, keepdims=True))
        p = jnp.exp2(s - m_new)
        acc_sc[rows, :] = jnp.exp2(m_prev - m_new) * acc_sc[rows, :] + _dot(p.astype(BF16), v)
        m_sc[rows, :] = m_new

    @pl.when(kj == pl.num_programs(2) - 1)
    def _():
        lane = lax.broadcasted_iota(I32, (tq, LANES), 1)
        heads_per_kv = n_h // GQ_KV_HEADS
        for j in range(n_h // 2):
            kv = (2 * j) // heads_per_kv
            halves = []
            for h in (2 * j, 2 * j + 1):
                a = acc_sc[h * tq:(h + 1) * tq, :]
                halves.append(a[:, :LANES] / a[:, LANES:LANES + 1])
            lo, hi = halves
            if kv == 0:
                hi = pltpu.roll(hi, GQ_HEAD_DIM, 1)
            else:
                lo = pltpu.roll(lo, GQ_HEAD_DIM, 1)
            o_ref[:, j * LANES:(j + 1) * LANES] = jnp.where(lane < GQ_HEAD_DIM, lo, hi).astype(o_ref.dtype)


def gq_attention(qm, keys, vals, n_rows_out, tq, tk, q_blk, k_blk, o_blk, n_q, n_k, n_batch):
    n_h = qm.shape[0]
    return pl.pallas_call(
        _gq_flash_body,
        grid=(n_batch, n_q, n_k),
        in_specs=[pl.BlockSpec((n_h, tq, LANES), lambda b, i, j: (0, q_blk(b, i), 0)),
                  pl.BlockSpec((tk, LANES), lambda b, i, j: (k_blk(b, j), 0)),
                  pl.BlockSpec((tk, 2 * LANES), lambda b, i, j: (k_blk(b, j), 0))],
        out_specs=pl.BlockSpec((tq, n_h * GQ_HEAD_DIM), lambda b, i, j: (o_blk(b, i), 0)),
        out_shape=jax.ShapeDtypeStruct((n_rows_out, n_h * GQ_HEAD_DIM), BF16),
        scratch_shapes=[pltpu.VMEM((n_h * tq, 1), F32), pltpu.VMEM((n_h * tq, 2 * LANES), F32)],
        compiler_params=_cp(("arbitrary", "arbitrary", "arbitrary"), VMEM_LIMIT),
        name="gq_attention",
    )(qm, keys, vals)


def _largest_divisor(n, cap):
    return max(d for d in range(1, cap + 1) if n % d == 0)


def gq_mixer(pb, col_q, col_k, col_v, cs_tab, g_q, g_k, n_lat, n_ctx, n_batch, with_ctx):
    qm, keys, vals = gq_prepare(pb, col_q, col_k, col_v, cs_tab, g_q, g_k, n_lat, n_ctx, n_batch)
    tq = 256
    tk = LANES * _largest_divisor((n_lat + n_ctx) // LANES, 22)
    n_q = n_lat // tq
    per_b = (n_lat + n_ctx) // tk
    lat_blk = lambda b, i: b * n_q + i
    y_lat = gq_attention(qm, keys, vals, n_batch * n_lat, tq, tk, lat_blk, lambda b, j: b * per_b + j, lat_blk,
                         n_q, per_b, n_batch)
    if not with_ctx:
        return y_lat, None
    tc = n_ctx
    y_ctx = gq_attention(qm, keys, vals, n_batch * n_ctx, tc, tc, lambda b, i: n_batch * n_lat // tc + b,
                         lambda b, j: b * ((n_lat + n_ctx) // tc) + n_lat // tc, lambda b, i: b, 1, 1, n_batch)
    return y_lat, y_ctx


def na_bias_tables(rpb, n_img_rows, n_ctx):
    tr = NA_TILE_ROWS
    nt = n_img_rows // tr
    assert nt >= 4
    kr = min(NA_ROWS, n_img_rows)
    n_heads = rpb.shape[0]
    qcol = np.arange(GRID_W)[:, None]
    kcol = np.arange(GRID_W)[None, :]
    dc = np.clip(kcol - qcol + NA_COLS - 1, 0, 2 * NA_COLS - 2)
    oh_c = (dc[None] == np.arange(2 * NA_COLS - 1)[:, None, None]).astype(np.float32)
    cstart = np.clip(qcol - NA_COLS // 2, 0, GRID_W - NA_COLS)
    col_ok = (kcol >= cstart) & (kcol < cstart + NA_COLS)
    by_col = jnp.einsum('hrd,dqk->hrqk', rpb.astype(F32), jnp.asarray(oh_c), precision=HIGHEST)
    classes = []
    for i in (0, 1, nt - 1):
        wb = int(np.clip(i - 1, 0, nt - 3))
        qrow = (i * tr + np.arange(tr))[:, None]
        krow = (wb * tr + np.arange(3 * tr))[None, :]
        rs = np.clip(qrow - kr // 2, 0, n_img_rows - kr)
        row_ok = (krow >= rs) & (krow < rs + kr)
        dr = np.clip(krow - qrow + NA_ROWS - 1, 0, 2 * NA_ROWS - 2)
        oh_r = ((dr[None] == np.arange(2 * NA_ROWS - 1)[:, None, None]) & row_ok[None]).astype(np.float32)
        bias = jnp.einsum('hrqk,rab->haqbk', by_col, jnp.asarray(oh_r), precision=HIGHEST)
        ok = row_ok[:, None, :, None] & col_ok[None, :, None, :]
        bias = jnp.where(jnp.asarray(ok)[None], bias, MASK_NEG).reshape(n_heads, tr * GRID_W, 3 * tr * GRID_W)
        classes.append(jnp.concatenate([bias, jnp.zeros((n_heads, tr * GRID_W, n_ctx), F32)], axis=-1))
    return jnp.stack(classes, axis=0)


def _pair_attention(q_pair, k_pair, v_pair, bias_fn):
    lane = lax.broadcasted_iota(I32, q_pair.shape, 1)
    out = None
    for hh in range(2):
        mine = (lane >= hh * NA_HEAD_DIM) & (lane < (hh + 1) * NA_HEAD_DIM)
        qm = jnp.where(mine, q_pair, jnp.zeros_like(q_pair))
        s = _dot_nt(qm, k_pair)
        b = bias_fn(hh)
        if b is not None:
            s = s + b
        m = jnp.max(s, axis=-1, keepdims=True)
        p = jnp.exp(s - m)
        l = jnp.sum(p, axis=-1, keepdims=True)
        o = _dot(p.astype(BF16), v_pair) / l
        out = o if out is None else jnp.where(mine, o, out)
    return out


def _na_body(q_ref, k0_ref, k1_ref, k2_ref, kc_ref, v0_ref, v1_ref, v2_ref, vc_ref, b_ref, o_ref):
    scale = NA_HEAD_DIM ** -0.5
    for j in range(NA_HEADS // 2):
        sl = slice(j * LANES, (j + 1) * LANES)
        q_pair = q_ref[:, sl] * scale
        k_pair = jnp.concatenate([k0_ref[:, sl], k1_ref[:, sl], k2_ref[:, sl], kc_ref[:, sl]], axis=0)
        v_pair = jnp.concatenate([v0_ref[:, sl], v1_ref[:, sl], v2_ref[:, sl], vc_ref[:, sl]], axis=0)
        o = _pair_attention(q_pair, k_pair, v_pair, lambda hh: b_ref[0, 2 * j + hh])
        o_ref[:, sl] = o.astype(o_ref.dtype)


def na_mixer(pb, col_q, col_k, col_v, bias_tab, n_lat, n_ctx, n_batch):
    tm = NA_TILE_ROWS * GRID_W
    assert n_ctx == tm
    w = NA_HEADS * NA_HEAD_DIM
    nt = n_lat // tm
    n_keys = 3 * tm + n_ctx
    cq, ck, cv = col_q // w, col_k // w, col_v // w
    ctx0 = n_batch * nt

    def win(o):
        return lambda b, i: (b * nt + jnp.clip(i - 1, 0, nt - 3) + o)

    def cls(b, i):
        return (jnp.where(i == 0, 0, jnp.where(i == nt - 1, 2, 1)), 0, 0, 0)

    kspecs = [pl.BlockSpec((tm, w), (lambda b, i, f=win(o): (f(b, i), ck))) for o in range(3)]
    vspecs = [pl.BlockSpec((tm, w), (lambda b, i, f=win(o): (f(b, i), cv))) for o in range(3)]
    return pl.pallas_call(
        _na_body,
        grid=(n_batch, nt),
        in_specs=[pl.BlockSpec((tm, w), lambda b, i: (b * nt + i, cq))] + kspecs
        + [pl.BlockSpec((tm, w), lambda b, i: (ctx0 + b, ck))] + vspecs
        + [pl.BlockSpec((tm, w), lambda b, i: (ctx0 + b, cv)),
           pl.BlockSpec((1, NA_HEADS, tm, n_keys), cls)],
        out_specs=pl.BlockSpec((tm, w), lambda b, i: (b * nt + i, 0)),
        out_shape=jax.ShapeDtypeStruct((n_batch * n_lat, w), BF16),
        compiler_params=_cp(("arbitrary", "arbitrary"), VMEM_LIMIT),
        name="na_attention",
    )(pb, pb, pb, pb, pb, pb, pb, pb, pb, bias_tab)


def _ctx_mha_body(q_ref, k_ref, v_ref, o_ref):
    scale = NA_HEAD_DIM ** -0.5
    for j in range(NA_HEADS // 2):
        sl = slice(j * LANES, (j + 1) * LANES)
        o = _pair_attention(q_ref[:, sl] * scale, k_ref[:, sl], v_ref[:, sl], lambda hh: None)
        o_ref[:, sl] = o.astype(o_ref.dtype)


def na_ctx_attention(pb, col_q, col_k, col_v, n_lat, n_ctx, n_batch):
    w = NA_HEADS * NA_HEAD_DIM
    ctx0 = n_batch * n_lat // n_ctx
    spec = lambda c: pl.BlockSpec((n_ctx, w), lambda b: (ctx0 + b, c // w))
    return pl.pallas_call(
        _ctx_mha_body,
        grid=(n_batch,),
        in_specs=[spec(col_q), spec(col_k), spec(col_v)],
        out_specs=pl.BlockSpec((n_ctx, w), lambda b: (b, 0)),
        out_shape=jax.ShapeDtypeStruct((n_batch * n_ctx, w), BF16),
        compiler_params=_cp(("arbitrary",)),
        name="na_ctx_attention",
    )(pb, pb, pb)


def _mlstm_body(n_batch, *refs):
    n_in = 8 * n_batch
    ins, bias_ref = refs[:n_in], refs[n_in]
    hf_ref, hb_ref, c_ref, n_ref, m_ref = refs[n_in + 1:]

    @pl.when(pl.program_id(0) == 0)
    def _():
        c_ref[...] = jnp.zeros(c_ref.shape, F32)
        n_ref[...] = jnp.zeros(n_ref.shape, F32)
        m_ref[...] = jnp.full(m_ref.shape, NEG_INIT, F32)

    tok = lax.broadcasted_iota(I32, (ML_CHUNK, ML_CHUNK), 0)
    src = lax.broadcasted_iota(I32, (ML_CHUNK, ML_CHUNK), 1)
    masks = (src <= tok, src >= tok)
    gates = []
    for b in range(n_batch):
        for d in range(2):
            g = ins[(b * 2 + d) * 4 + 3][...] + bias_ref[...]
            lf_cum = jnp.dot(masks[d].astype(F32), jax.nn.log_sigmoid(g), precision=HIGHEST,
                             preferred_element_type=F32)
            gates.append((g, lf_cum))
    gates = [(g, lf_cum, g.T, lf_cum.T) for g, lf_cum in gates]
    chains = []
    for b in range(n_batch):
        for d in range(2):
            g, lf_cum, g_t, lf_cum_t = gates[b * 2 + d]
            for h in range(ML_HEADS):
                ci, cf = d * 2 * ML_HEADS + h, d * 2 * ML_HEADS + ML_HEADS + h
                idx = (b * 2 + d) * ML_HEADS + h
                chains.append(dict(
                    b=b, d=d, h=h, idx=idx, bt_col=lf_cum[:, cf:cf + 1], bt_row=lf_cum_t[cf:cf + 1, :],
                    li_col=g[:, ci:ci + 1], li_row=g_t[ci:ci + 1, :], m_prev=m_ref[idx][:, 0:1],
                    c_prev=c_ref[idx], n_prev=n_ref[idx]))
    for c in chains:
        c['dmat'] = jnp.where(masks[c['d']], c['bt_col'] - c['bt_row'] + c['li_row'], -jnp.inf)
        c['inter'] = c['bt_col'] + c['m_prev']
    for c in chains:
        c['mt'] = jnp.maximum(c['inter'], jnp.max(c['dmat'], axis=-1, keepdims=True))
    for c in chains:
        refs_c = ins[(c['b'] * 2 + c['d']) * 4:(c['b'] * 2 + c['d']) * 4 + 3]
        sl = slice(c['h'] * ML_HEAD_DIM, (c['h'] + 1) * ML_HEAD_DIM)
        c['q'], c['k'], c['v'] = (r[:, sl] for r in refs_c)
        c['s'] = _dot_nt(c['q'], c['k']) * jnp.exp(c['dmat'] - c['mt'])
        c['w_inter'] = jnp.exp(c['inter'] - c['mt'])
    for c in chains:
        num = _dot(c['s'].astype(BF16), c['v']) + c['w_inter'] * _dot_nt(c['q'], c['c_prev'].astype(BF16))
        qn = jnp.sum(c['q'].astype(F32) * c['n_prev'], axis=-1, keepdims=True)
        den = jnp.sum(c['s'], axis=-1, keepdims=True) + c['w_inter'] * qn
        h_out = num / jnp.maximum(jnp.abs(den), jnp.exp(-c['mt']))
        h_ref = hb_ref if c['d'] else hf_ref
        h_ref[c['b'], :, c['h'] * ML_HEAD_DIM:(c['h'] + 1) * ML_HEAD_DIM] = h_out
    new_state = []
    for c in chains:
        b_last = c['bt_col'][0:1, :] if c['d'] else c['bt_col'][ML_CHUNK - 1:ML_CHUNK, :]
        g_col = b_last - c['bt_col'] + c['li_col']
        m_new = jnp.maximum(b_last + c['m_prev'], jnp.max(g_col, axis=0, keepdims=True))
        wg = jnp.exp(g_col - m_new)
        decay = jnp.exp(b_last + c['m_prev'] - m_new)
        c_new = decay * c['c_prev'] + _dot_tn((wg * c['v'].astype(F32)).astype(BF16), c['k'])
        n_new = decay * c['n_prev'] + jnp.sum(wg * c['k'].astype(F32), axis=0, keepdims=True)
        new_state.append((c['idx'], c_new, n_new, m_new))
    for idx, c_new, n_new, m_new in new_state:
        c_ref[idx] = c_new
        n_ref[idx] = n_new
        m_ref[idx] = jnp.broadcast_to(m_new, (1, LANES))


def mlstm_mixer(pb, pa, col_q, col_k, col_v, col_g, bias, n_lat, n_ctx, n_batch):
    w = ML_HEADS * ML_HEAD_DIM
    tc = ML_CHUNK
    nl, nc = n_lat // tc, n_ctx // tc

    def fwd_pos(i):
        return jnp.where(i < nc, nl + i, i - nc)

    def bwd_pos(i):
        return jnp.where(i < nc, nl + nc - 1 - i, nl - 1 - (i - nc))

    def row_blk(b, pos):
        return jnp.where(pos < nl, b * nl + pos, n_batch * nl + b * nc + pos - nl)

    def specs(b, pos_fn):
        return [pl.BlockSpec((tc, w), lambda i, c=c: (row_blk(b, pos_fn(i)), c // w)) for c in (col_q, col_k, col_v)] + [
            pl.BlockSpec((tc, LANES), lambda i: (row_blk(b, pos_fn(i)), col_g // LANES))]

    in_specs, operands = [], []
    for b in range(n_batch):
        for pos_fn in (fwd_pos, bwd_pos):
            in_specs += specs(b, pos_fn)
            operands += [pb, pb, pb, pa]
    n_st = 2 * ML_HEADS * n_batch
    out_shape = jax.ShapeDtypeStruct((n_batch, n_lat + n_ctx, w), F32)
    return pl.pallas_call(
        functools.partial(_mlstm_body, n_batch),
        grid=(nl + nc,),
        in_specs=in_specs + [pl.BlockSpec((1, LANES), lambda i: (0, 0))],
        out_specs=[pl.BlockSpec((n_batch, tc, w), lambda i: (0, fwd_pos(i), 0)),
                   pl.BlockSpec((n_batch, tc, w), lambda i: (0, bwd_pos(i), 0))],
        out_shape=[out_shape, out_shape],
        scratch_shapes=[pltpu.VMEM((n_st, ML_HEAD_DIM, ML_HEAD_DIM), F32),
                        pltpu.VMEM((n_st, 1, ML_HEAD_DIM), F32),
                        pltpu.VMEM((n_st, 1, LANES), F32)],
        compiler_params=_cp(("arbitrary",)),
        name="mlstm_chunks",
    )(*operands, bias)


def _merge_body(n_lat_tiles, ya_ref, ybl_ref, ybc_ref, hf_ref, hb_ref, o_ref, ydl_ref, ydc_ref, gate_ref, x_ref,
                mod_ref, mlg_ref, wbr_ref, wout_ref, g2_ref, wr_ref, xo_ref, h2_ref, st_ref):
    d = x_ref.shape[1]
    is_ctx = pl.program_id(0) >= n_lat_tiles
    yb = jnp.where(is_ctx, ybc_ref[...], ybl_ref[...])
    yd = jnp.where(is_ctx, ydc_ref[...], ydl_ref[...])
    hs = hf_ref[...] + hb_ref[...]
    segs = []
    for h in range(ML_HEADS):
        seg = hs[:, h * ML_HEAD_DIM:(h + 1) * ML_HEAD_DIM]
        segs.append(seg * lax.rsqrt(jnp.mean(seg * seg, axis=-1, keepdims=True) + EPS))
    ym = jnp.concatenate(segs, axis=1) * mlg_ref[...] * jax.nn.sigmoid(o_ref[...].astype(F32))
    ys = (ya_ref[...], yb, ym.astype(BF16), yd)
    merged = None
    for i in range(N_BRANCHES):
        term = gate_ref[:, i * d:(i + 1) * d].astype(F32) * _dot(ys[i], wbr_ref[i])
        merged = term if merged is None else merged + term
    y = _dot(merged.astype(BF16), wout_ref[...])
    mod = mod_ref[0]
    x_new = x_ref[...] + mod[2:3] * y
    xo_ref[...] = x_new
    h2 = _rms_mod(x_new, g2_ref[...], mod[3:4], mod[4:5])
    h2_ref[...] = h2
    st_ref[...] = jax.nn.sigmoid(_dot_nt(wr_ref[...], h2.astype(BF16)))


def merge_layer(ya, yb, yd, hf, hb, pb, col_o, gate, x_all, modtab, mlg, wbr, wout, g2, wr_t,
                n_rows, n_lat_rows, n_batch):
    d = x_all.shape[1]
    tm = ROW_TILE
    w = BRANCH_WIDTH
    per_b = n_lat_rows // n_batch // tm
    row = lambda i: (i, 0)
    const2 = lambda i: (0, 0)
    n_lat_tiles = n_lat_rows // tm
    lat_row = lambda i: (jnp.minimum(i, n_lat_tiles - 1), 0)
    ctx_row = lambda i: (jnp.clip(i - n_lat_tiles, 0, n_batch - 1), 0)

    def seq(i):
        lat = i < n_lat_tiles
        return (jnp.where(lat, i // per_b, i - n_lat_tiles), jnp.where(lat, i % per_b, per_b), 0)

    (yb_lat, yb_ctx), (yd_lat, yd_ctx) = yb, yd
    if yb_ctx is None:
        yb_ctx, yd_ctx = yb_lat, yd_lat
    return pl.pallas_call(
        functools.partial(_merge_body, n_lat_tiles),
        grid=(n_rows // tm,),
        in_specs=[pl.BlockSpec((tm, w), row), pl.BlockSpec((tm, w), lat_row), pl.BlockSpec((tm, w), ctx_row),
                  pl.BlockSpec((None, tm, w), seq), pl.BlockSpec((None, tm, w), seq),
                  pl.BlockSpec((tm, w), lambda i: (i, col_o // w)),
                  pl.BlockSpec((tm, w), lat_row), pl.BlockSpec((tm, w), ctx_row),
                  pl.BlockSpec((tm, N_BRANCHES * d), row),
                  pl.BlockSpec((tm, d), row),
                  pl.BlockSpec((1, N_MOD, d), lambda i: (jnp.minimum(i // per_b, n_batch), 0, 0)),
                  pl.BlockSpec((1, w), const2),
                  pl.BlockSpec((N_BRANCHES, w, d), lambda i: (0, 0, 0)),
                  pl.BlockSpec((d, d), const2), pl.BlockSpec((1, d), const2),
                  pl.BlockSpec((LANES, d), const2)],
        out_specs=[pl.BlockSpec((tm, d), row), pl.BlockSpec((tm, d), row),
                   pl.BlockSpec((LANES, tm), lambda i: (0, i))],
        out_shape=[jax.ShapeDtypeStruct((n_rows, d), F32), jax.ShapeDtypeStruct((n_rows, d), F32),
                   jax.ShapeDtypeStruct((LANES, n_rows), F32)],
        compiler_params=_cp(("arbitrary",), VMEM_LIMIT),
        name="merge_layer",
    )(ya, yb_lat, yb_ctx, hf, hb, pb, yd_lat, yd_ctx, gate, x_all, modtab, mlg, wbr, wout, g2, wr_t)


def _router_body(s_ref, b_ref, e_ref, w_ref, rank_ref, cnt_ref, base_sc):
    @pl.when(pl.program_id(0) == 0)
    def _():
        base_sc[...] = jnp.zeros(base_sc.shape, F32)

    tm = s_ref.shape[1]
    s = s_ref[0:N_EXPERTS, :]
    sel = s + b_ref[0:N_EXPERTS, :]
    row = lambda a, e: a[e:e + 1, :]
    best, grp = None, None
    for g in range(N_EXPERT_GROUPS):
        v = [row(sel, EXPERTS_PER_GROUP * g + k) for k in range(EXPERTS_PER_GROUP)]
        gs = None
        for a in range(EXPERTS_PER_GROUP):
            for c in range(a + 1, EXPERTS_PER_GROUP):
                gs = v[a] + v[c] if gs is None else jnp.maximum(gs, v[a] + v[c])
        if best is None:
            best, grp = gs, jnp.zeros((1, tm), I32)
        else:
            better = gs > best
            grp = jnp.where(better, g, grp)
            best = jnp.where(better, gs, best)
    vals, affs = [], []
    for k in range(EXPERTS_PER_GROUP):
        vk, sk = row(sel, k), row(s, k)
        for g in range(1, N_EXPERT_GROUPS):
            hit = grp == g
            vk = jnp.where(hit, row(sel, EXPERTS_PER_GROUP * g + k), vk)
            sk = jnp.where(hit, row(s, EXPERTS_PER_GROUP * g + k), sk)
        vals.append(vk)
        affs.append(sk)
    i1, b1, w1 = jnp.zeros((1, tm), I32), vals[0], affs[0]
    for k in range(1, EXPERTS_PER_GROUP):
        better = vals[k] > b1
        i1 = jnp.where(better, k, i1)
        w1 = jnp.where(better, affs[k], w1)
        b1 = jnp.where(better, vals[k], b1)
    i2 = jnp.zeros((1, tm), I32)
    b2 = jnp.full((1, tm), -jnp.inf, F32)
    w2 = jnp.zeros((1, tm), F32)
    for k in range(EXPERTS_PER_GROUP):
        cand = (i1 != k) & (vals[k] > b2)
        i2 = jnp.where(cand, k, i2)
        w2 = jnp.where(cand, affs[k], w2)
        b2 = jnp.where(cand, vals[k], b2)
    e1 = grp * EXPERTS_PER_GROUP + i1
    e2 = grp * EXPERTS_PER_GROUP + i2
    tot = w1 + w2
    e_ref[...] = jnp.concatenate([e1, e2], axis=0)
    wpad = jnp.concatenate([w1 / tot, w2 / tot, jnp.zeros((6, tm), F32)], axis=0)
    w_ref[...] = wpad.T
    ids = lax.broadcasted_iota(I32, (N_EXPERTS, tm), 0)
    oh1 = (ids == e1).astype(F32)
    oh2 = (ids == e2).astype(F32)
    oh = oh1 + oh2
    before = (lax.broadcasted_iota(I32, (tm, tm), 0) < lax.broadcasted_iota(I32, (tm, tm), 1)).astype(BF16)
    prior = _dot(oh.astype(BF16), before) + base_sc[...]
    r1 = jnp.sum(oh1 * prior, axis=0, keepdims=True)
    r2 = jnp.sum(oh2 * prior, axis=0, keepdims=True)
    rank_ref[...] = jnp.concatenate([r1, r2], axis=0).astype(I32)
    base = base_sc[...] + jnp.sum(oh, axis=1, keepdims=True)
    base_sc[...] = base
    cnt_ref[...] = jnp.broadcast_to(base, cnt_ref.shape).astype(I32)


def route(s_t, b_router):
    n_rows = s_t.shape[1]
    tm = ROW_TILE
    b_col = jnp.pad(b_router.astype(F32), (0, LANES - N_EXPERTS)).reshape(LANES, 1)
    return pl.pallas_call(
        _router_body,
        grid=(n_rows // tm,),
        in_specs=[pl.BlockSpec((LANES, tm), lambda i: (0, i)), pl.BlockSpec((LANES, 1), lambda i: (0, 0))],
        out_specs=[pl.BlockSpec((TOP_K, tm), lambda i: (0, i)), pl.BlockSpec((tm, 8), lambda i: (i, 0)),
                   pl.BlockSpec((TOP_K, tm), lambda i: (0, i)), pl.BlockSpec((N_EXPERTS, LANES), lambda i: (0, 0))],
        out_shape=[jax.ShapeDtypeStruct((TOP_K, n_rows), I32), jax.ShapeDtypeStruct((n_rows, 8), F32),
                   jax.ShapeDtypeStruct((TOP_K, n_rows), I32), jax.ShapeDtypeStruct((N_EXPERTS, LANES), I32)],
        scratch_shapes=[pltpu.VMEM((N_EXPERTS, 1), F32)],
        compiler_params=_cp(("arbitrary",)),
        name="moe_router",
    )(s_t, b_col)


def _row_copy(src_ref, src_row, dst_ref, dst_row, sem):
    return pltpu.make_async_copy(src_ref.at[pl.ds(src_row, 1), :], dst_ref.at[pl.ds(dst_row, 1), :], sem)


def _dispatch_body(dest_ref, h_ref, buf_in_ref, buf_ref, sem):
    del buf_in_ref
    tm = h_ref.shape[0]

    def issue(t, carry):
        for k in range(TOP_K):
            _row_copy(h_ref, t, buf_ref, dest_ref[0, k, t], sem).start()
        return carry

    lax.fori_loop(0, tm, issue, 0, unroll=8)
    for k in range(TOP_K):
        pltpu.make_async_copy(h_ref, buf_ref.at[pl.ds(0, tm), :], sem).wait()


def moe_dispatch(h2, dest3, buf0):
    n_rows, d = h2.shape
    tm = ROW_TILE
    return pl.pallas_call(
        _dispatch_body,
        grid=(n_rows // tm,),
        in_specs=[pl.BlockSpec((1, TOP_K, tm), lambda i: (i, 0, 0), memory_space=pltpu.SMEM),
                  pl.BlockSpec((tm, d), lambda i: (i, 0)),
                  pl.BlockSpec(memory_space=pl.ANY)],
        out_specs=pl.BlockSpec(memory_space=pl.ANY),
        out_shape=jax.ShapeDtypeStruct(buf0.shape, buf0.dtype),
        scratch_shapes=[pltpu.SemaphoreType.DMA(())],
        input_output_aliases={2: 0},
        compiler_params=_cp(("arbitrary",)),
        name="moe_dispatch",
    )(dest3, h2, buf0)


def _expert_body(be_ref, x_ref, w1_ref, w3_ref, w2_ref, o_ref, w1_sc, w3_sc, w2_sc):
    i = pl.program_id(0)
    changed = jnp.logical_or(i == 0, be_ref[i] != be_ref[jnp.maximum(i - 1, 0)])

    @pl.when(changed)
    def _():
        w1_sc[...] = w1_ref[0].astype(BF16)
        w3_sc[...] = w3_ref[0].astype(BF16)
        w2_sc[...] = w2_ref[0].astype(BF16)

    x = x_ref[...].astype(BF16)
    a = _dot(x, w1_sc[...])
    mid = (a * jax.nn.sigmoid(a)) * _dot(x, w3_sc[...])
    o_ref[...] = _dot(mid.astype(BF16), w2_sc[...])


def moe_experts(buf, blk_expert, w1, w3, w2, layer, blk):
    n_slots, d = buf.shape
    de = w1.shape[3]
    grid_spec = pltpu.PrefetchScalarGridSpec(
        num_scalar_prefetch=1,
        grid=(n_slots // blk,),
        in_specs=[pl.BlockSpec((blk, d), lambda i, be: (i, 0)),
                  pl.BlockSpec((None, 1, d, de), lambda i, be: (layer, be[i], 0, 0)),
                  pl.BlockSpec((None, 1, d, de), lambda i, be: (layer, be[i], 0, 0)),
                  pl.BlockSpec((None, 1, de, d), lambda i, be: (layer, be[i], 0, 0))],
        out_specs=pl.BlockSpec((blk, d), lambda i, be: (i, 0)),
        scratch_shapes=[pltpu.VMEM((d, de), BF16), pltpu.VMEM((d, de), BF16), pltpu.VMEM((de, d), BF16)],
    )
    return pl.pallas_call(
        _expert_body,
        grid_spec=grid_spec,
        out_shape=jax.ShapeDtypeStruct((n_slots, d), F32),
        compiler_params=_cp(("arbitrary",), VMEM_LIMIT),
        name="moe_experts",
    )(blk_expert, buf, w1, w3, w2)


def _combine_body(final, dest_ref, x_ref, w_ref, mod_ref, gf_ref, y_hbm, o_ref, y0_sc, y1_sc, sem):
    tm = x_ref.shape[0]
    bufs = (y0_sc, y1_sc)

    def issue(t, carry):
        for k in range(TOP_K):
            _row_copy(y_hbm, dest_ref[0, k, t], bufs[k], t, sem).start()
        return carry

    lax.fori_loop(0, tm, issue, 0, unroll=8)
    for k in range(TOP_K):
        pltpu.make_async_copy(y_hbm.at[pl.ds(0, tm), :], bufs[k], sem).wait()
    w = w_ref[...]
    f = w[:, 0:1] * y0_sc[...] + w[:, 1:2] * y1_sc[...]
    x_new = x_ref[...] + mod_ref[0][5:6] * f
    if final:
        ms = jnp.mean(x_new * x_new, axis=-1, keepdims=True)
        x_new = x_new * lax.rsqrt(ms + EPS) * gf_ref[...]
    o_ref[...] = x_new


def moe_combine(dest3, x_rows, wts, modtab, g_final, y_slots, n_lat_rows, n_batch, final):
    n_rows, d = x_rows.shape
    tm = ROW_TILE
    per_b = n_lat_rows // n_batch // tm
    return pl.pallas_call(
        functools.partial(_combine_body, final),
        grid=(n_rows // tm,),
        in_specs=[pl.BlockSpec((1, TOP_K, tm), lambda i: (i, 0, 0), memory_space=pltpu.SMEM),
                  pl.BlockSpec((tm, d), lambda i: (i, 0)),
                  pl.BlockSpec((tm, 8), lambda i: (i, 0)),
                  pl.BlockSpec((1, N_MOD, d), lambda i: (jnp.minimum(i // per_b, n_batch), 0, 0)),
                  pl.BlockSpec((1, d), lambda i: (0, 0)),
                  pl.BlockSpec(memory_space=pl.ANY)],
        out_specs=pl.BlockSpec((tm, d), lambda i: (i, 0)),
        out_shape=jax.ShapeDtypeStruct((n_rows, d), F32),
        scratch_shapes=[pltpu.VMEM((tm, d), F32), pltpu.VMEM((tm, d), F32), pltpu.SemaphoreType.DMA(())],
        compiler_params=_cp(("arbitrary",)),
        name="moe_combine",
    )(dest3, x_rows, wts, modtab, g_final.reshape(1, d), y_slots)


def moe_layer(x_rows, h2, s_t, b_router, w1, w3, w2, layer, modtab, g_final, n_lat_rows, n_batch, final):
    n_rows, d = h2.shape
    blk = 2 * MOE_BLOCK
    experts, wts, rank, counts = route(s_t, b_router)
    cnt = counts[:, 0]
    padded = (cnt + blk - 1) // blk * blk
    pend = jnp.cumsum(padded)
    pstart = pend - padded
    hit = experts[..., None] == jnp.arange(N_EXPERTS, dtype=I32)
    dest = jnp.sum(jnp.where(hit, pstart.astype(I32), 0), axis=-1) + rank
    n_blocks = -(-(n_rows * TOP_K) // blk) + N_EXPERTS
    blk_start = jnp.arange(n_blocks, dtype=I32) * blk
    blk_expert = jnp.minimum(jnp.sum((pend[None, :] <= blk_start[:, None]).astype(I32), axis=1), N_EXPERTS - 1)
    dest3 = dest.reshape(TOP_K, n_rows // ROW_TILE, ROW_TILE).transpose(1, 0, 2)
    buf = moe_dispatch(h2, dest3, jnp.zeros((n_blocks * blk, d), F32))
    y_slots = moe_experts(buf, blk_expert, w1, w3, w2, layer, blk)
    return moe_combine(dest3, x_rows, wts, modtab, g_final, y_slots, n_lat_rows, n_batch, final)


_COL = dict(na_q=0, na_k=512, na_v=1024, ml_q=1536, ml_k=2048, ml_v=2560, ml_o=3072,
            gq_q=3584, gq_k=4096, gq_v=4224)


def _split_w_in(w_in):
    sizes = (BRANCH_WIDTH,) * 8 + (4 * ML_HEADS, BRANCH_WIDTH, GQ_KV_HEADS * GQ_HEAD_DIM, GQ_KV_HEADS * GQ_HEAD_DIM)
    idx = np.cumsum(sizes)[:-1].tolist()
    (s5_u, na_q, na_k, na_v, ml_q, ml_k, ml_v, ml_o, ml_gt, gq_q, gq_k, gq_v) = jnp.split(w_in, idx, axis=-1)
    wa = jnp.pad(ml_gt, ((0, 0), (0, LANES - 4 * ML_HEADS)))
    wb = jnp.concatenate([na_q, na_k, na_v, ml_q, ml_k * (ML_HEAD_DIM ** -0.5), ml_v, ml_o, gq_q, gq_k, gq_v], axis=1)
    return s5_u.T.astype(BF16), wa.astype(BF16), wb.astype(BF16)


def kernel(x, c, ctx, c_ctx, w_mod, b_mod, g_norm1, g_norm2, w_in, s5_lam_re, s5_lam_im, s5_log_dt, s5_b_re,
           s5_b_im, s5_c_re, s5_c_im, s5_d, s5_w_glu, s5_b_glu, na_rpb, ml_b_gates, ml_norm, gq_qnorm, gq_knorm,
           w_branch, w_gate, b_gate, w_out, w_router, b_router, moe_w1, moe_w3, moe_w2, g_final):
    b, n_lat, dm = x.shape
    n_ctx = ctx.shape[1]
    depth = w_in.shape[0]
    bn, bc = b * n_lat, b * n_ctx
    x_all = jnp.concatenate([x.reshape(bn, dm), ctx.reshape(bc, dm)], axis=0).astype(F32)
    c_all = jnp.concatenate([c.astype(F32), c_ctx.astype(F32)[None], jnp.zeros((8 - b - 1, dm), F32)], axis=0)
    cs_tab = rope_tables(n_lat)
    wr_t = jnp.pad(w_router.astype(BF16).T, ((0, LANES - N_EXPERTS), (0, 0)))
    out = None
    for l in range(depth):
        last = l == depth - 1
        with_ctx = not last
        modtab = mod_vectors(c_all, w_mod, b_mod, l)[:b + 1].reshape(b + 1, N_MOD, dm)
        w_u_t, wa, wb = _split_w_in(w_in[l])
        pa, pb, gate, h_all = in_projection(x_all, g_norm1[l], modtab, wa, wb, w_gate[l].astype(BF16), b_gate[l],
                                            bn, b)

        tables = s5_tables(s5_lam_re[l], s5_lam_im[l], s5_log_dt[l], s5_b_re[l], s5_b_im[l],
                           s5_c_re[l], s5_c_im[l], s5_d[l])
        ya = s5_mixer(h_all, w_u_t, tables, s5_w_glu[l].astype(BF16), s5_b_glu[l].astype(F32).reshape(1, -1),
                      n_lat, n_ctx, b)

        bias_tab = na_bias_tables(na_rpb[l], n_lat // GRID_W, n_ctx)
        n_rows = bn + bc if with_ctx else bn
        na_cols = (_COL['na_q'], _COL['na_k'], _COL['na_v'])
        yb = (na_mixer(pb, *na_cols, bias_tab, n_lat, n_ctx, b),
              na_ctx_attention(pb, *na_cols, n_lat, n_ctx, b) if with_ctx else None)

        ml_bias = jnp.pad(ml_b_gates[l].astype(F32), (0, LANES - 4 * ML_HEADS)).reshape(1, LANES)
        hf, hb = mlstm_mixer(pb, pa, _COL['ml_q'], _COL['ml_k'], _COL['ml_v'], 0, ml_bias, n_lat, n_ctx, b)

        yd = gq_mixer(pb, _COL['gq_q'], _COL['gq_k'], _COL['gq_v'], cs_tab, gq_qnorm[l], gq_knorm[l],
                      n_lat, n_ctx, b, with_ctx)
        x_mid, h2, s_t = merge_layer(
            ya, yb, yd, hf, hb, pb, _COL['ml_o'], gate, x_all, modtab, ml_norm[l].astype(F32).reshape(1, -1),
            w_branch[l].astype(BF16), w_out[l].astype(BF16), g_norm2[l].astype(F32).reshape(1, -1), wr_t,
            n_rows, bn, b)
        x_next = moe_layer(x_mid, h2, s_t, b_router, moe_w1, moe_w3, moe_w2, l, modtab, g_final, bn, b, last)
        if last:
            out = x_next.reshape(b, n_lat, dm).astype(x.dtype)
        else:
            x_all = x_next
    return out
```

```python
import functools
import math

import numpy as np
import jax
import jax.numpy as jnp
from jax import lax
from jax.experimental import pallas as pl
from jax.experimental.pallas import tpu as pltpu

F32 = jnp.float32
BF16 = jnp.bfloat16
I32 = jnp.int32

GRID_W = 64
N_MOD = 6
BRANCH_WIDTH = 512
N_BRANCHES = 4
S5_GROUP = 16
S5_GROUPS = BRANCH_WIDTH // S5_GROUP
S5_STATE = 64
NA_HEADS = 8
NA_HEAD_DIM = 64
NA_ROWS = 8
NA_COLS = 16
ML_HEADS = 4
ML_HEAD_DIM = 128
ML_CHUNK = 128
GQ_HEADS = 8
GQ_KV_HEADS = 2
GQ_HEAD_DIM = 64
ROPE_THETA = 10000.0
N_EXPERTS = 32
N_EXPERT_GROUPS = 8
EXPERTS_PER_GROUP = 4
TOP_K = 2
D_EXPERT = 512
MOE_BLOCK = 128
EPS = 1e-6
NEG_INIT = -1e30
MASK_NEG = -1e30
LOG2E = 1.4426950408889634

LANES = 128
ROW_TILE = 256
S5_CHUNK = 32
NA_TILE_ROWS = 4
VMEM_LIMIT = 56 * 1024 * 1024

HIGHEST = lax.Precision.HIGHEST


def _cp(sem, vmem=None):
    return pltpu.CompilerParams(dimension_semantics=sem, vmem_limit_bytes=vmem)


def _dot(a, b):
    return jnp.dot(a, b, preferred_element_type=F32)


def _dot_nt(a, b):
    return lax.dot_general(a, b, (((1,), (1,)), ((), ())), preferred_element_type=F32)


def _dot_tn(a, b):
    return lax.dot_general(a, b, (((0,), (0,)), ((), ())), preferred_element_type=F32)


def _rms_mod(x, g, shift, scale):
    ms = jnp.mean(x * x, axis=-1, keepdims=True)
    y = x * lax.rsqrt(ms + EPS) * g
    return y * (1.0 + scale) + shift


def _round_up(n, m):
    return -(-n // m) * m


def _mod_body(c_ref, w_ref, b_ref, o_ref):
    c = c_ref[...]
    a = (c * jax.nn.sigmoid(c)).astype(BF16)
    o_ref[...] = _dot(a, w_ref[...].astype(BF16)) + b_ref[...]


def mod_vectors(c_all, w_mod, b_mod, layer):
    d = c_all.shape[1]
    depth = w_mod.shape[0]
    return pl.pallas_call(
        _mod_body,
        grid=(N_MOD,),
        in_specs=[pl.BlockSpec((8, d), lambda j: (0, 0)),
                  pl.BlockSpec((None, d, d), lambda j: (layer, 0, j)),
                  pl.BlockSpec((None, 1, d), lambda j: (layer, 0, j))],
        out_specs=pl.BlockSpec((8, d), lambda j: (0, j)),
        out_shape=jax.ShapeDtypeStruct((8, N_MOD * d), F32),
        compiler_params=_cp(("arbitrary",)),
        name="mod_vectors",
    )(c_all, w_mod, b_mod.reshape(depth, 1, -1))


def _inproj_body(x_ref, g_ref, mod_ref, wa_ref, wb_ref, wg_ref, bg_ref, oa_ref, ob_ref, og_ref, oh_ref):
    mod = mod_ref[0]
    h = _rms_mod(x_ref[...], g_ref[...], mod[0:1], mod[1:2]).astype(BF16)
    oh_ref[...] = h
    oa_ref[...] = _dot(h, wa_ref[...])
    ob_ref[...] = _dot(h, wb_ref[...]).astype(BF16)
    og_ref[...] = jax.nn.sigmoid(_dot(h, wg_ref[...]) + bg_ref[...]).astype(BF16)


def in_projection(x_all, g, modtab, wa, wb, wg, bg, n_lat_rows, n_batch):
    r, d = x_all.shape
    tm = ROW_TILE
    per_b = n_lat_rows // n_batch // tm

    def mod_idx(i):
        return (jnp.minimum(i // per_b, n_batch), 0, 0)

    const = lambda i: (0, 0)
    return pl.pallas_call(
        _inproj_body,
        grid=(r // tm,),
        in_specs=[pl.BlockSpec((tm, d), lambda i: (i, 0)),
                  pl.BlockSpec((1, d), const),
                  pl.BlockSpec((1, N_MOD, d), mod_idx),
                  pl.BlockSpec(wa.shape, const, pipeline_mode=pl.Buffered(1)),
                  pl.BlockSpec(wb.shape, const, pipeline_mode=pl.Buffered(1)),
                  pl.BlockSpec(wg.shape, const, pipeline_mode=pl.Buffered(1)),
                  pl.BlockSpec((1, wg.shape[1]), const)],
        out_specs=[pl.BlockSpec((tm, wa.shape[1]), lambda i: (i, 0)),
                   pl.BlockSpec((tm, wb.shape[1]), lambda i: (i, 0)),
                   pl.BlockSpec((tm, wg.shape[1]), lambda i: (i, 0)),
                   pl.BlockSpec((tm, d), lambda i: (i, 0))],
        out_shape=[jax.ShapeDtypeStruct((r, wa.shape[1]), F32),
                   jax.ShapeDtypeStruct((r, wb.shape[1]), BF16),
                   jax.ShapeDtypeStruct((r, wg.shape[1]), BF16),
                   jax.ShapeDtypeStruct((r, d), BF16)],
        compiler_params=_cp(("arbitrary",), VMEM_LIMIT),
        name="in_projection",
    )(x_all, g.reshape(1, d), modtab, wa, wb, wg, bg.reshape(1, -1))


def s5_tables(lam_re, lam_im, log_dt, b_re, b_im, c_re, c_im, d_skip):
    ell, g_n, p_n, c_n = S5_CHUNK, S5_GROUPS, S5_STATE, S5_GROUP
    lam_re, lam_im = lam_re.astype(F32), lam_im.astype(F32)
    b_re, b_im, c_re, c_im = (t.astype(F32) for t in (b_re, b_im, c_re, c_im))
    dt = jnp.exp(log_dt.astype(F32))[..., None]
    mag = jnp.exp(lam_re * dt)
    a_re = mag * jnp.cos(lam_im * dt)
    a_im = mag * jnp.sin(lam_im * dt)
    den = lam_re * lam_re + lam_im * lam_im
    nr = a_re - 1.0
    f_re = (nr * lam_re + a_im * lam_im) / den
    f_im = (a_im * lam_re - nr * lam_im) / den
    bb_re = f_re[..., None] * b_re - f_im[..., None] * b_im
    bb_im = f_re[..., None] * b_im + f_im[..., None] * b_re
    k = jnp.arange(ell + 1, dtype=F32)
    pmag = jnp.exp((lam_re * dt)[..., None] * k)
    ang = (lam_im * dt)[..., None] * k
    pr, pi = pmag * jnp.cos(ang), pmag * jnp.sin(ang)
    ab_re = pr[..., None] * bb_re[:, :, :, None, :] - pi[..., None] * bb_im[:, :, :, None, :]
    ab_im = pr[..., None] * bb_im[:, :, :, None, :] + pi[..., None] * bb_re[:, :, :, None, :]
    flat = lambda t: t.reshape(2 * g_n, p_n, (ell + 1) * c_n)
    kk = (jnp.einsum('bcp,bpn->bcn', c_re.reshape(2 * g_n, c_n, p_n), flat(ab_re), precision=HIGHEST)
          - jnp.einsum('bcp,bpn->bcn', c_im.reshape(2 * g_n, c_n, p_n), flat(ab_im), precision=HIGHEST))
    kk = kk.reshape(2, g_n, c_n, ell + 1, c_n)
    centre = kk[0][:, :, 0] + kk[1][:, :, 0] + d_skip.astype(F32).reshape(g_n, c_n, 1) * jnp.eye(c_n, dtype=F32)
    w = jnp.concatenate([kk[0][:, :, ell - 1:0:-1], centre[:, :, None], kk[1][:, :, 1:ell]], axis=2)
    wf = w.reshape(g_n, c_n, (2 * ell - 1) * c_n)
    toe = jnp.stack([wf[:, :, (ell - 1 - t) * c_n:(2 * ell - 1 - t) * c_n] for t in range(ell)], axis=1)
    tsum_t = toe.astype(BF16).reshape(g_n, ell * c_n, ell * c_n)

    parity = [jnp.asarray(np.arange(g_n) % 2 == q, F32) for q in range(2)]
    parts = []
    for d in range(2):
        for ab in (ab_re, ab_im):
            sel = ab[d][:, :, :ell]
            if d == 0:
                sel = sel[:, :, ::-1]
            sel = sel.reshape(g_n, p_n, ell * c_n)
            parts += [sel * parity[q][:, None, None] for q in range(2)]
    mend_t = jnp.stack(parts, axis=1).reshape(g_n, 8 * p_n, ell * c_n)

    rows = []
    for d in range(2):
        prk, pik = pr[d][:, :, 1:ell + 1], pi[d][:, :, 1:ell + 1]
        if d == 1:
            prk, pik = prk[:, :, ::-1], pik[:, :, ::-1]
        prk = prk.transpose(0, 2, 1)[:, :, None, :]
        pik = pik.transpose(0, 2, 1)[:, :, None, :]
        cr, ci = c_re[d][:, None], c_im[d][:, None]
        for part in (cr * prk - ci * pik, -cr * pik - ci * prk):
            rows += [part * parity[q][:, None, None, None] for q in range(2)]
    wst_t = jnp.concatenate(rows, axis=-1).reshape(g_n, ell * c_n, 8 * p_n)

    al = jnp.stack([pr[0][:, :, ell], pi[0][:, :, ell], pr[1][:, :, ell], pi[1][:, :, ell]], axis=1)
    a_chunk = al.reshape(g_n // 2, 2, 4, p_n).transpose(0, 2, 1, 3).reshape(g_n // 2, 8 * p_n)
    return tsum_t.astype(BF16), mend_t.astype(BF16), wst_t.astype(BF16), a_chunk


def _s5_proj_body(hl_ref, hc_ref, w_ref, o_ref):
    n_lat = hl_ref.shape[0]
    n_pad = o_ref.shape[1] - LANES
    hl = hl_ref[...]
    if n_pad > n_lat:
        hl = jnp.concatenate([hl, jnp.zeros((n_pad - n_lat, hl.shape[1]), BF16)], axis=0)
    o_ref[:, 0:n_pad] = _dot_nt(w_ref[...], hl).astype(BF16)
    hc = hc_ref[...]
    hc = jnp.concatenate([hc, jnp.zeros((LANES - hc.shape[0], hc.shape[1]), BF16)], axis=0)
    o_ref[:, n_pad:] = _dot_nt(w_ref[...], hc).astype(BF16)


def _s5_end_body(u_ref, m_ref, o_ref):
    width = m_ref.shape[1]
    acc = None
    for q in range(2):
        u = u_ref[:, q].reshape(width, u_ref.shape[3])
        term = _dot(m_ref[q], u)
        acc = term if acc is None else acc + term
    o_ref[...] = acc.T


def _s5_scan_body(n_batch, per_b, n_ctx_chunks, lat_pad, e_ref, a_ref, o_ref):
    n_pairs = e_ref.shape[0]
    o_ref[...] = jnp.zeros(o_ref.shape, F32)
    n_steps = per_b + n_ctx_chunks
    first = pl.program_id(0) * n_pairs
    coef = [[a_ref[first + q, j:j + 1, :] for j in range(4)] for q in range(n_pairs)]

    def step(i, carry):
        new = []
        for b in range(n_batch):
            fwd = jnp.where(i < n_ctx_chunks, lat_pad + b * n_ctx_chunks + i, b * per_b + i - n_ctx_chunks)
            bwd = jnp.where(i < n_ctx_chunks, lat_pad + b * n_ctx_chunks + n_ctx_chunks - 1 - i,
                            b * per_b + per_b - 1 - (i - n_ctx_chunks))
            for q in range(n_pairs):
                for d, row in enumerate((fwd, bwd)):
                    sr, si = carry[((b * n_pairs + q) * 2 + d) * 2:((b * n_pairs + q) * 2 + d) * 2 + 2]
                    ar, ai = coef[q][2 * d], coef[q][2 * d + 1]
                    o_ref[q, row, 2 * d:2 * d + 1, :] = sr
                    o_ref[q, row, 2 * d + 1:2 * d + 2, :] = si
                    er = e_ref[q, row, 2 * d:2 * d + 1, :]
                    ei = e_ref[q, row, 2 * d + 1:2 * d + 2, :]
                    new += [ar * sr - ai * si + er, ar * si + ai * sr + ei]
        return tuple(new)

    z = jnp.zeros((1, LANES), F32)
    lax.fori_loop(0, n_steps, step, tuple(z for _ in range(n_batch * n_pairs * 4)))


def _s5_out_body(u_ref, t_ref, s_ref, w_ref, o_ref):
    width = t_ref.shape[0]
    u = u_ref[...].reshape(width, u_ref.shape[2])
    y = _dot(t_ref[...], u) + _dot_nt(w_ref[...], s_ref[...].astype(BF16))
    o_ref[...] = y.reshape(o_ref.shape)


def _s5_glu_body(n_lat_chunks, n_ctx_chunks, lat_pad, y_ref, w_ref, b_ref, o_ref):
    y = y_ref[...].T
    if lat_pad == n_lat_chunks:
        y = y[:n_lat_chunks + n_ctx_chunks]
    else:
        y = jnp.concatenate([y[:n_lat_chunks], y[lat_pad:lat_pad + n_ctx_chunks]], axis=0)
    z = jax.nn.gelu(y)
    o_ref[...] = (z * jax.nn.sigmoid(_dot(z.astype(BF16), w_ref[...]) + b_ref[...])).astype(o_ref.dtype)


def s5_mixer(h_all, w_u_t, tables, w_glu, b_glu, n_lat, n_ctx, n_batch):
    tsum_t, mend_t, wst_t, a_chunk = tables
    r, d = h_all.shape
    h2 = h_all.reshape(r // S5_CHUNK, S5_CHUNK * d)
    ell, g_n, c_n = S5_CHUNK, S5_GROUPS, S5_GROUP
    width = ell * c_n
    bw = g_n * c_n
    n_lat_chunks = n_batch * n_lat // ell
    n_ctx_chunks = n_batch * n_ctx // ell
    assert n_lat_chunks % n_ctx_chunks == 0 and n_ctx_chunks % 16 == 0 and n_ctx_chunks <= LANES
    lat_pad = _round_up(n_lat_chunks, LANES)
    nch = lat_pad + LANES
    const = lambda t: (0, 0)
    u_t = pl.pallas_call(
        _s5_proj_body,
        grid=(ell,),
        in_specs=[pl.BlockSpec((n_lat_chunks, d), lambda t: (0, t)),
                  pl.BlockSpec((n_ctx_chunks, d), lambda t: (n_lat_chunks // n_ctx_chunks, t)),
                  pl.BlockSpec((bw, d), const)],
        out_specs=pl.BlockSpec((None, bw, nch), lambda t: (t, 0, 0)),
        out_shape=jax.ShapeDtypeStruct((ell, bw, nch), BF16),
        compiler_params=_cp(("arbitrary",)),
        name="s5_projection",
    )(h2, h2, w_u_t)
    u4 = u_t.reshape(ell, g_n, c_n, nch)
    ends = pl.pallas_call(
        _s5_end_body,
        grid=(g_n // 2,),
        in_specs=[pl.BlockSpec((ell, 2, c_n, nch), lambda p: (0, p, 0, 0)),
                  pl.BlockSpec((2, width, width), lambda p: (p, 0, 0))],
        out_specs=pl.BlockSpec((None, nch, width), lambda p: (p, 0, 0)),
        out_shape=jax.ShapeDtypeStruct((g_n // 2, nch, width), F32),
        compiler_params=_cp(("arbitrary",)),
        name="s5_chunk_ends",
    )(u4, mend_t)
    pairs_per_step = 2
    quad = (g_n // 2, nch, 4, LANES)
    states = pl.pallas_call(
        functools.partial(_s5_scan_body, n_batch, n_lat // ell, n_ctx // ell, lat_pad),
        grid=(g_n // 2 // pairs_per_step,),
        in_specs=[pl.BlockSpec((pairs_per_step, nch, 4, LANES), lambda j: (j, 0, 0, 0)),
                  pl.BlockSpec((g_n // 2, 4, LANES), lambda j: (0, 0, 0))],
        out_specs=pl.BlockSpec((pairs_per_step, nch, 4, LANES), lambda j: (j, 0, 0, 0)),
        out_shape=jax.ShapeDtypeStruct(quad, F32),
        compiler_params=_cp(("arbitrary",), VMEM_LIMIT),
        name="s5_state_scan",
    )(ends.reshape(quad), a_chunk.reshape(g_n // 2, 4, LANES)).reshape(g_n // 2, nch, width)
    y_t = pl.pallas_call(
        _s5_out_body,
        grid=(g_n,),
        in_specs=[pl.BlockSpec((ell, None, c_n, nch), lambda gi: (0, gi, 0, 0)),
                  pl.BlockSpec((None, width, width), lambda gi: (gi, 0, 0)),
                  pl.BlockSpec((None, nch, width), lambda gi: (gi // 2, 0, 0)),
                  pl.BlockSpec((None, width, width), lambda gi: (gi, 0, 0))],
        out_specs=pl.BlockSpec((ell, None, c_n, nch), lambda gi: (0, gi, 0, 0)),
        out_shape=jax.ShapeDtypeStruct((ell, g_n, c_n, nch), F32),
        compiler_params=_cp(("arbitrary",)),
        name="s5_outputs",
    )(u4, tsum_t, states, wst_t)
    n_chunks = n_lat_chunks + n_ctx_chunks
    ya = pl.pallas_call(
        functools.partial(_s5_glu_body, n_lat_chunks, n_ctx_chunks, lat_pad),
        grid=(ell,),
        in_specs=[pl.BlockSpec((None, bw, nch), lambda t: (t, 0, 0)),
                  pl.BlockSpec((bw, bw), const),
                  pl.BlockSpec((1, bw), const)],
        out_specs=pl.BlockSpec((n_chunks, bw), lambda t: (0, t)),
        out_shape=jax.ShapeDtypeStruct((n_chunks, ell * bw), BF16),
        compiler_params=_cp(("arbitrary",)),
        name="s5_glu",
    )(y_t.reshape(ell, bw, nch), w_glu, b_glu)
    return ya.reshape(r, bw)


def rope_tables(n_lat):
    half = GQ_HEAD_DIM // 2
    quarter = half // 2
    t = np.arange(n_lat)
    freqs = ROPE_THETA ** (-np.arange(quarter, dtype=np.float64) / quarter)
    ang_r = (t // GRID_W)[:, None] * freqs
    ang_c = (t % GRID_W)[:, None] * freqs
    ang = np.concatenate([ang_r, ang_r, ang_c, ang_c], axis=1)
    sign = np.concatenate([-np.ones(quarter), np.ones(quarter)] * 2)
    cos = np.concatenate([np.cos(ang), np.ones((ROW_TILE, GQ_HEAD_DIM))], axis=0)
    sin = np.concatenate([np.sin(ang) * sign, np.zeros((ROW_TILE, GQ_HEAD_DIM))], axis=0)
    tab = np.concatenate([cos, cos, sin, sin], axis=1)
    return jnp.asarray(tab, F32)


def _group_ones(width, group):
    i = np.arange(width)
    return jnp.asarray((i[:, None] // group) == (i[None, :] // group), BF16)


def _group_mean_sq(x, ones_blk, group):
    sq = x * x
    hi = sq.astype(BF16)
    lo = (sq - hi.astype(F32)).astype(BF16)
    return (_dot(hi, ones_blk) + _dot(lo, ones_blk)) * (1.0 / group)


def _rope(x, cos, sin):
    w = x.shape[-1]
    q = GQ_HEAD_DIM // 4
    lane = lax.broadcasted_iota(I32, x.shape, 1)
    first = (lane % (2 * q)) < q
    partner = jnp.where(first, pltpu.roll(x, w - q, 1), pltpu.roll(x, q, 1))
    return x * cos + partner * sin


def _gq_prep_body(q_ref, k_ref, v_ref, cs_ref, gq_ref, gk_ref, oq_ref, ok_ref, qm_ref, kr_ref, va_ref):
    cs = cs_ref[...]
    cos1, sin1 = cs[:, :LANES], cs[:, LANES:]
    q = q_ref[...].astype(F32)
    qn = q * lax.rsqrt(_group_mean_sq(q, oq_ref[...], GQ_HEAD_DIM) + EPS) * gq_ref[...]
    n_pairs = q.shape[1] // LANES
    qr = _rope(qn, jnp.concatenate([cos1] * n_pairs, axis=1), jnp.concatenate([sin1] * n_pairs, axis=1))
    qr = (qr * (GQ_HEAD_DIM ** -0.5 * LOG2E)).astype(BF16)
    lane = lax.broadcasted_iota(I32, (q.shape[0], LANES), 1)
    heads_per_kv = GQ_HEADS // GQ_KV_HEADS
    for h in range(GQ_HEADS):
        pair = qr[:, (h // 2) * LANES:(h // 2 + 1) * LANES]
        kv = h // heads_per_kv
        if h % 2 != kv:
            pair = pltpu.roll(pair, GQ_HEAD_DIM, 1)
        keep = (lane >= kv * GQ_HEAD_DIM) & (lane < (kv + 1) * GQ_HEAD_DIM)
        qm_ref[h] = jnp.where(keep, pair, jnp.zeros_like(pair))
    k = k_ref[...].astype(F32)
    kn = k * lax.rsqrt(_group_mean_sq(k, ok_ref[...], GQ_HEAD_DIM) + EPS) * gk_ref[...]
    kr_ref[...] = _rope(kn, cos1, sin1).astype(BF16)
    va_ref[...] = jnp.concatenate([v_ref[...], jnp.ones(v_ref.shape, BF16)], axis=1)


def gq_prepare(pb, col_q, col_k, col_v, cs_tab, g_q, g_k, n_lat, n_ctx, n_batch):
    r = pb.shape[0]
    tm = ROW_TILE
    assert n_ctx == tm and n_lat % tm == 0
    nb = n_lat // tm
    n_lat_tiles = n_batch * nb

    def tab_idx(i):
        return (jnp.where(i < n_lat_tiles, i % nb, nb), 0)

    def kv_idx(i):
        lat = (i // nb) * (nb + 1) + i % nb
        ctx = (i - n_lat_tiles) * (nb + 1) + nb
        return (jnp.where(i < n_lat_tiles, lat, ctx), 0)

    qw = GQ_HEADS * GQ_HEAD_DIM
    const = lambda i: (0, 0)
    gq = jnp.tile(g_q.astype(F32), GQ_HEADS).reshape(1, qw)
    gk = jnp.tile(g_k.astype(F32), GQ_KV_HEADS).reshape(1, LANES)
    n_keys = n_batch * (n_lat + n_ctx)
    return pl.pallas_call(
        _gq_prep_body,
        grid=(r // tm,),
        in_specs=[pl.BlockSpec((tm, qw), lambda i: (i, col_q // qw)),
                  pl.BlockSpec((tm, LANES), lambda i: (i, col_k // LANES)),
                  pl.BlockSpec((tm, LANES), lambda i: (i, col_v // LANES)),
                  pl.BlockSpec((tm, 2 * LANES), tab_idx),
                  pl.BlockSpec((1, qw), const),
                  pl.BlockSpec((1, LANES), const),
                  pl.BlockSpec((qw, qw), const),
                  pl.BlockSpec((LANES, LANES), const)],
        out_specs=[pl.BlockSpec((GQ_HEADS, tm, LANES), lambda i: (0, i, 0)),
                   pl.BlockSpec((tm, LANES), kv_idx),
                   pl.BlockSpec((tm, 2 * LANES), kv_idx)],
        out_shape=[jax.ShapeDtypeStruct((GQ_HEADS, r, LANES), BF16),
                   jax.ShapeDtypeStruct((n_keys, LANES), BF16),
                   jax.ShapeDtypeStruct((n_keys, 2 * LANES), BF16)],
        compiler_params=_cp(("arbitrary",)),
        name="gq_prepare",
    )(pb, pb, pb, cs_tab, gq, gk, _group_ones(qw, GQ_HEAD_DIM), _group_ones(LANES, GQ_HEAD_DIM))


def _gq_flash_body(q_ref, k_ref, v_ref, o_ref, m_sc, acc_sc):
    kj = pl.program_id(2)
    n_h, tq, _ = q_ref.shape

    @pl.when(kj == 0)
    def _():
        m_sc[...] = jnp.full(m_sc.shape, -jnp.inf, F32)
        acc_sc[...] = jnp.zeros(acc_sc.shape, F32)

    k = k_ref[...]
    v = v_ref[...]
    hp = 1
    for c in range(n_h // hp):
        rows = slice(c * hp * tq, (c + 1) * hp * tq)
        s = _dot_nt(q_ref[c * hp:(c + 1) * hp].reshape(hp * tq, LANES), k)
        m_prev = m_sc[rows, :]
        m_new = jnp.maximum(m_prev, jnp.max(s, axis=-1, keepdims=True))
        p = jnp.exp2(s - m_new)
        acc_sc[rows, :] = jnp.exp2(m_prev - m_new) * acc_sc[rows, :] + _dot(p.astype(BF16), v)
        m_sc[rows, :] = m_new

    @pl.when(kj == pl.num_programs(2) - 1)
    def _():
        lane = lax.broadcasted_iota(I32, (tq, LANES), 1)
        heads_per_kv = n_h // GQ_KV_HEADS
        for j in range(n_h // 2):
            kv = (2 * j) // heads_per_kv
            halves = []
            for h in (2 * j, 2 * j + 1):
                a = acc_sc[h * tq:(h + 1) * tq, :]
                halves.append(a[:, :LANES] / a[:, LANES:LANES + 1])
            lo, hi = halves
            if kv == 0:
                hi = pltpu.roll(hi, GQ_HEAD_DIM, 1)
            else:
                lo = pltpu.roll(lo, GQ_HEAD_DIM, 1)
            o_ref[:, j * LANES:(j + 1) * LANES] = jnp.where(lane < GQ_HEAD_DIM, lo, hi).astype(o_ref.dtype)


def gq_attention(qm, keys, vals, n_rows_out, tq, tk, q_blk, k_blk, o_blk, n_q, n_k, n_batch):
    n_h = qm.shape[0]
    return pl.pallas_call(
        _gq_flash_body,
        grid=(n_batch, n_q, n_k),
        in_specs=[pl.BlockSpec((n_h, tq, LANES), lambda b, i, j: (0, q_blk(b, i), 0)),
                  pl.BlockSpec((tk, LANES), lambda b, i, j: (k_blk(b, j), 0)),
                  pl.BlockSpec((tk, 2 * LANES), lambda b, i, j: (k_blk(b, j), 0))],
        out_specs=pl.BlockSpec((tq, n_h * GQ_HEAD_DIM), lambda b, i, j: (o_blk(b, i), 0)),
        out_shape=jax.ShapeDtypeStruct((n_rows_out, n_h * GQ_HEAD_DIM), BF16),
        scratch_shapes=[pltpu.VMEM((n_h * tq, 1), F32), pltpu.VMEM((n_h * tq, 2 * LANES), F32)],
        compiler_params=_cp(("arbitrary", "arbitrary", "arbitrary"), VMEM_LIMIT),
        name="gq_attention",
    )(qm, keys, vals)


def _largest_divisor(n, cap):
    return max(d for d in range(1, cap + 1) if n % d == 0)


def gq_mixer(pb, col_q, col_k, col_v, cs_tab, g_q, g_k, n_lat, n_ctx, n_batch, with_ctx):
    qm, keys, vals = gq_prepare(pb, col_q, col_k, col_v, cs_tab, g_q, g_k, n_lat, n_ctx, n_batch)
    tq = 512
    tk = LANES * _largest_divisor((n_lat + n_ctx) // LANES, 22)
    n_q = n_lat // tq
    per_b = (n_lat + n_ctx) // tk
    lat_blk = lambda b, i: b * n_q + i
    y_lat = gq_attention(qm, keys, vals, n_batch * n_lat, tq, tk, lat_blk, lambda b, j: b * per_b + j, lat_blk,
                         n_q, per_b, n_batch)
    if not with_ctx:
        return y_lat, None
    tc = n_ctx
    y_ctx = gq_attention(qm, keys, vals, n_batch * n_ctx, tc, tc, lambda b, i: n_batch * n_lat // tc + b,
                         lambda b, j: b * ((n_lat + n_ctx) // tc) + n_lat // tc, lambda b, i: b, 1, 1, n_batch)
    return y_lat, y_ctx


def na_bias_tables(rpb, n_img_rows, n_ctx):
    tr = NA_TILE_ROWS
    nt = n_img_rows // tr
    assert nt >= 4
    kr = min(NA_ROWS, n_img_rows)
    n_heads = rpb.shape[0]
    qcol = np.arange(GRID_W)[:, None]
    kcol = np.arange(GRID_W)[None, :]
    dc = np.clip(kcol - qcol + NA_COLS - 1, 0, 2 * NA_COLS - 2)
    oh_c = (dc[None] == np.arange(2 * NA_COLS - 1)[:, None, None]).astype(np.float32)
    cstart = np.clip(qcol - NA_COLS // 2, 0, GRID_W - NA_COLS)
    col_ok = (kcol >= cstart) & (kcol < cstart + NA_COLS)
    by_col = jnp.einsum('hrd,dqk->hrqk', rpb.astype(F32), jnp.asarray(oh_c), precision=HIGHEST)
    classes = []
    for i in (0, 1, nt - 1):
        wb = int(np.clip(i - 1, 0, nt - 3))
        qrow = (i * tr + np.arange(tr))[:, None]
        krow = (wb * tr + np.arange(3 * tr))[None, :]
        rs = np.clip(qrow - kr // 2, 0, n_img_rows - kr)
        row_ok = (krow >= rs) & (krow < rs + kr)
        dr = np.clip(krow - qrow + NA_ROWS - 1, 0, 2 * NA_ROWS - 2)
        oh_r = ((dr[None] == np.arange(2 * NA_ROWS - 1)[:, None, None]) & row_ok[None]).astype(np.float32)
        bias = jnp.einsum('hrqk,rab->haqbk', by_col, jnp.asarray(oh_r), precision=HIGHEST)
        ok = row_ok[:, None, :, None] & col_ok[None, :, None, :]
        bias = jnp.where(jnp.asarray(ok)[None], bias, MASK_NEG).reshape(n_heads, tr * GRID_W, 3 * tr * GRID_W)
        classes.append(jnp.concatenate([bias, jnp.zeros((n_heads, tr * GRID_W, n_ctx), F32)], axis=-1))
    return jnp.stack(classes, axis=0)


def _pair_attention(q_pair, k_pair, v_pair, bias_fn):
    lane = lax.broadcasted_iota(I32, q_pair.shape, 1)
    out = None
    for hh in range(2):
        mine = (lane >= hh * NA_HEAD_DIM) & (lane < (hh + 1) * NA_HEAD_DIM)
        qm = jnp.where(mine, q_pair, jnp.zeros_like(q_pair))
        s = _dot_nt(qm, k_pair)
        b = bias_fn(hh)
        if b is not None:
            s = s + b
        m = jnp.max(s, axis=-1, keepdims=True)
        p = jnp.exp(s - m)
        l = jnp.sum(p, axis=-1, keepdims=True)
        o = _dot(p.astype(BF16), v_pair) / l
        out = o if out is None else jnp.where(mine, o, out)
    return out


def _na_body(q_ref, k0_ref, k1_ref, k2_ref, kc_ref, v0_ref, v1_ref, v2_ref, vc_ref, b_ref, o_ref):
    scale = NA_HEAD_DIM ** -0.5
    for j in range(NA_HEADS // 2):
        sl = slice(j * LANES, (j + 1) * LANES)
        q_pair = q_ref[:, sl] * scale
        k_pair = jnp.concatenate([k0_ref[:, sl], k1_ref[:, sl], k2_ref[:, sl], kc_ref[:, sl]], axis=0)
        v_pair = jnp.concatenate([v0_ref[:, sl], v1_ref[:, sl], v2_ref[:, sl], vc_ref[:, sl]], axis=0)
        o = _pair_attention(q_pair, k_pair, v_pair, lambda hh: b_ref[0, 2 * j + hh])
        o_ref[:, sl] = o.astype(o_ref.dtype)


def na_mixer(pb, col_q, col_k, col_v, bias_tab, n_lat, n_ctx, n_batch):
    tm = NA_TILE_ROWS * GRID_W
    assert n_ctx == tm
    w = NA_HEADS * NA_HEAD_DIM
    nt = n_lat // tm
    n_keys = 3 * tm + n_ctx
    cq, ck, cv = col_q // w, col_k // w, col_v // w
    ctx0 = n_batch * nt

    def win(o):
        return lambda b, i: (b * nt + jnp.clip(i - 1, 0, nt - 3) + o)

    def cls(b, i):
        return (jnp.where(i == 0, 0, jnp.where(i == nt - 1, 2, 1)), 0, 0, 0)

    kspecs = [pl.BlockSpec((tm, w), (lambda b, i, f=win(o): (f(b, i), ck))) for o in range(3)]
    vspecs = [pl.BlockSpec((tm, w), (lambda b, i, f=win(o): (f(b, i), cv))) for o in range(3)]
    return pl.pallas_call(
        _na_body,
        grid=(n_batch, nt),
        in_specs=[pl.BlockSpec((tm, w), lambda b, i: (b * nt + i, cq))] + kspecs
        + [pl.BlockSpec((tm, w), lambda b, i: (ctx0 + b, ck))] + vspecs
        + [pl.BlockSpec((tm, w), lambda b, i: (ctx0 + b, cv)),
           pl.BlockSpec((1, NA_HEADS, tm, n_keys), cls)],
        out_specs=pl.BlockSpec((tm, w), lambda b, i: (b * nt + i, 0)),
        out_shape=jax.ShapeDtypeStruct((n_batch * n_lat, w), BF16),
        compiler_params=_cp(("arbitrary", "arbitrary"), VMEM_LIMIT),
        name="na_attention",
    )(pb, pb, pb, pb, pb, pb, pb, pb, pb, bias_tab)


def _ctx_mha_body(q_ref, k_ref, v_ref, o_ref):
    scale = NA_HEAD_DIM ** -0.5
    for j in range(NA_HEADS // 2):
        sl = slice(j * LANES, (j + 1) * LANES)
        o = _pair_attention(q_ref[:, sl] * scale, k_ref[:, sl], v_ref[:, sl], lambda hh: None)
        o_ref[:, sl] = o.astype(o_ref.dtype)


def na_ctx_attention(pb, col_q, col_k, col_v, n_lat, n_ctx, n_batch):
    w = NA_HEADS * NA_HEAD_DIM
    ctx0 = n_batch * n_lat // n_ctx
    spec = lambda c: pl.BlockSpec((n_ctx, w), lambda b: (ctx0 + b, c // w))
    return pl.pallas_call(
        _ctx_mha_body,
        grid=(n_batch,),
        in_specs=[spec(col_q), spec(col_k), spec(col_v)],
        out_specs=pl.BlockSpec((n_ctx, w), lambda b: (b, 0)),
        out_shape=jax.ShapeDtypeStruct((n_batch * n_ctx, w), BF16),
        compiler_params=_cp(("arbitrary",)),
        name="na_ctx_attention",
    )(pb, pb, pb)


def _mlstm_body(n_batch, *refs):
    n_in = 8 * n_batch
    ins, bias_ref = refs[:n_in], refs[n_in]
    hf_ref, hb_ref, c_ref, n_ref, m_ref = refs[n_in + 1:]

    @pl.when(pl.program_id(0) == 0)
    def _():
        c_ref[...] = jnp.zeros(c_ref.shape, F32)
        n_ref[...] = jnp.zeros(n_ref.shape, F32)
        m_ref[...] = jnp.full(m_ref.shape, NEG_INIT, F32)

    tok = lax.broadcasted_iota(I32, (ML_CHUNK, ML_CHUNK), 0)
    src = lax.broadcasted_iota(I32, (ML_CHUNK, ML_CHUNK), 1)
    masks = (src <= tok, src >= tok)
    gates = []
    for b in range(n_batch):
        for d in range(2):
            g = ins[(b * 2 + d) * 4 + 3][...] + bias_ref[...]
            lf_cum = jnp.dot(masks[d].astype(F32), jax.nn.log_sigmoid(g), precision=HIGHEST,
                             preferred_element_type=F32)
            gates.append((g, lf_cum))
    gates = [(g, lf_cum, g.T, lf_cum.T) for g, lf_cum in gates]
    chains = []
    for b in range(n_batch):
        for d in range(2):
            g, lf_cum, g_t, lf_cum_t = gates[b * 2 + d]
            for h in range(ML_HEADS):
                ci, cf = d * 2 * ML_HEADS + h, d * 2 * ML_HEADS + ML_HEADS + h
                idx = (b * 2 + d) * ML_HEADS + h
                chains.append(dict(
                    b=b, d=d, h=h, idx=idx, bt_col=lf_cum[:, cf:cf + 1], bt_row=lf_cum_t[cf:cf + 1, :],
                    li_col=g[:, ci:ci + 1], li_row=g_t[ci:ci + 1, :], m_prev=m_ref[idx][:, 0:1],
                    c_prev=c_ref[idx], n_prev=n_ref[idx]))
    for c in chains:
        c['dmat'] = jnp.where(masks[c['d']], c['bt_col'] - c['bt_row'] + c['li_row'], -jnp.inf)
        c['inter'] = c['bt_col'] + c['m_prev']
    for c in chains:
        c['mt'] = jnp.maximum(c['inter'], jnp.max(c['dmat'], axis=-1, keepdims=True))
    for c in chains:
        refs_c = ins[(c['b'] * 2 + c['d']) * 4:(c['b'] * 2 + c['d']) * 4 + 3]
        sl = slice(c['h'] * ML_HEAD_DIM, (c['h'] + 1) * ML_HEAD_DIM)
        c['q'], c['k'], c['v'] = (r[:, sl] for r in refs_c)
        c['s'] = _dot_nt(c['q'], c['k']) * jnp.exp(c['dmat'] - c['mt'])
        c['w_inter'] = jnp.exp(c['inter'] - c['mt'])
    for c in chains:
        num = _dot(c['s'].astype(BF16), c['v']) + c['w_inter'] * _dot_nt(c['q'], c['c_prev'].astype(BF16))
        qn = jnp.sum(c['q'].astype(F32) * c['n_prev'], axis=-1, keepdims=True)
        den = jnp.sum(c['s'], axis=-1, keepdims=True) + c['w_inter'] * qn
        h_out = num / jnp.maximum(jnp.abs(den), jnp.exp(-c['mt']))
        h_ref = hb_ref if c['d'] else hf_ref
        h_ref[c['b'], :, c['h'] * ML_HEAD_DIM:(c['h'] + 1) * ML_HEAD_DIM] = h_out
    new_state = []
    for c in chains:
        b_last = c['bt_col'][0:1, :] if c['d'] else c['bt_col'][ML_CHUNK - 1:ML_CHUNK, :]
        g_col = b_last - c['bt_col'] + c['li_col']
        m_new = jnp.maximum(b_last + c['m_prev'], jnp.max(g_col, axis=0, keepdims=True))
        wg = jnp.exp(g_col - m_new)
        decay = jnp.exp(b_last + c['m_prev'] - m_new)
        c_new = decay * c['c_prev'] + _dot_tn((wg * c['v'].astype(F32)).astype(BF16), c['k'])
        n_new = decay * c['n_prev'] + jnp.sum(wg * c['k'].astype(F32), axis=0, keepdims=True)
        new_state.append((c['idx'], c_new, n_new, m_new))
    for idx, c_new, n_new, m_new in new_state:
        c_ref[idx] = c_new
        n_ref[idx] = n_new
        m_ref[idx] = jnp.broadcast_to(m_new, (1, LANES))


def mlstm_mixer(pb, pa, col_q, col_k, col_v, col_g, bias, n_lat, n_ctx, n_batch):
    w = ML_HEADS * ML_HEAD_DIM
    tc = ML_CHUNK
    nl, nc = n_lat // tc, n_ctx // tc

    def fwd_pos(i):
        return jnp.where(i < nc, nl + i, i - nc)

    def bwd_pos(i):
        return jnp.where(i < nc, nl + nc - 1 - i, nl - 1 - (i - nc))

    def row_blk(b, pos):
        return jnp.where(pos < nl, b * nl + pos, n_batch * nl + b * nc + pos - nl)

    def specs(b, pos_fn):
        return [pl.BlockSpec((tc, w), lambda i, c=c: (row_blk(b, pos_fn(i)), c // w)) for c in (col_q, col_k, col_v)] + [
            pl.BlockSpec((tc, LANES), lambda i: (row_blk(b, pos_fn(i)), col_g // LANES))]

    in_specs, operands = [], []
    for b in range(n_batch):
        for pos_fn in (fwd_pos, bwd_pos):
            in_specs += specs(b, pos_fn)
            operands += [pb, pb, pb, pa]
    n_st = 2 * ML_HEADS * n_batch
    out_shape = jax.ShapeDtypeStruct((n_batch, n_lat + n_ctx, w), F32)
    return pl.pallas_call(
        functools.partial(_mlstm_body, n_batch),
        grid=(nl + nc,),
        in_specs=in_specs + [pl.BlockSpec((1, LANES), lambda i: (0, 0))],
        out_specs=[pl.BlockSpec((n_batch, tc, w), lambda i: (0, fwd_pos(i), 0)),
                   pl.BlockSpec((n_batch, tc, w), lambda i: (0, bwd_pos(i), 0))],
        out_shape=[out_shape, out_shape],
        scratch_shapes=[pltpu.VMEM((n_st, ML_HEAD_DIM, ML_HEAD_DIM), F32),
                        pltpu.VMEM((n_st, 1, ML_HEAD_DIM), F32),
                        pltpu.VMEM((n_st, 1, LANES), F32)],
        compiler_params=_cp(("arbitrary",)),
        name="mlstm_chunks",
    )(*operands, bias)


def _merge_body(n_lat_tiles, ya_ref, ybl_ref, ybc_ref, hf_ref, hb_ref, o_ref, ydl_ref, ydc_ref, gate_ref, x_ref,
                mod_ref, mlg_ref, wbr_ref, wout_ref, g2_ref, wr_ref, xo_ref, h2_ref, st_ref):
    d = x_ref.shape[1]
    is_ctx = pl.program_id(0) >= n_lat_tiles
    yb = jnp.where(is_ctx, ybc_ref[...], ybl_ref[...])
    yd = jnp.where(is_ctx, ydc_ref[...], ydl_ref[...])
    hs = hf_ref[...] + hb_ref[...]
    segs = []
    for h in range(ML_HEADS):
        seg = hs[:, h * ML_HEAD_DIM:(h + 1) * ML_HEAD_DIM]
        segs.append(seg * lax.rsqrt(jnp.mean(seg * seg, axis=-1, keepdims=True) + EPS))
    ym = jnp.concatenate(segs, axis=1) * mlg_ref[...] * jax.nn.sigmoid(o_ref[...].astype(F32))
    ys = (ya_ref[...], yb, ym.astype(BF16), yd)
    merged = None
    for i in range(N_BRANCHES):
        term = gate_ref[:, i * d:(i + 1) * d].astype(F32) * _dot(ys[i], wbr_ref[i])
        merged = term if merged is None else merged + term
    y = _dot(merged.astype(BF16), wout_ref[...])
    mod = mod_ref[0]
    x_new = x_ref[...] + mod[2:3] * y
    xo_ref[...] = x_new
    h2 = _rms_mod(x_new, g2_ref[...], mod[3:4], mod[4:5])
    h2_ref[...] = h2
    st_ref[...] = jax.nn.sigmoid(_dot_nt(wr_ref[...], h2.astype(BF16)))


def merge_layer(ya, yb, yd, hf, hb, pb, col_o, gate, x_all, modtab, mlg, wbr, wout, g2, wr_t,
                n_rows, n_lat_rows, n_batch):
    d = x_all.shape[1]
    tm = ROW_TILE
    w = BRANCH_WIDTH
    per_b = n_lat_rows // n_batch // tm
    row = lambda i: (i, 0)
    const2 = lambda i: (0, 0)
    n_lat_tiles = n_lat_rows // tm
    lat_row = lambda i: (jnp.minimum(i, n_lat_tiles - 1), 0)
    ctx_row = lambda i: (jnp.clip(i - n_lat_tiles, 0, n_batch - 1), 0)

    def seq(i):
        lat = i < n_lat_tiles
        return (jnp.where(lat, i // per_b, i - n_lat_tiles), jnp.where(lat, i % per_b, per_b), 0)

    (yb_lat, yb_ctx), (yd_lat, yd_ctx) = yb, yd
    if yb_ctx is None:
        yb_ctx, yd_ctx = yb_lat, yd_lat
    return pl.pallas_call(
        functools.partial(_merge_body, n_lat_tiles),
        grid=(n_rows // tm,),
        in_specs=[pl.BlockSpec((tm, w), row), pl.BlockSpec((tm, w), lat_row), pl.BlockSpec((tm, w), ctx_row),
                  pl.BlockSpec((None, tm, w), seq), pl.BlockSpec((None, tm, w), seq),
                  pl.BlockSpec((tm, w), lambda i: (i, col_o // w)),
                  pl.BlockSpec((tm, w), lat_row), pl.BlockSpec((tm, w), ctx_row),
                  pl.BlockSpec((tm, N_BRANCHES * d), row),
                  pl.BlockSpec((tm, d), row),
                  pl.BlockSpec((1, N_MOD, d), lambda i: (jnp.minimum(i // per_b, n_batch), 0, 0)),
                  pl.BlockSpec((1, w), const2),
                  pl.BlockSpec((N_BRANCHES, w, d), lambda i: (0, 0, 0)),
                  pl.BlockSpec((d, d), const2), pl.BlockSpec((1, d), const2),
                  pl.BlockSpec((LANES, d), const2)],
        out_specs=[pl.BlockSpec((tm, d), row), pl.BlockSpec((tm, d), row),
                   pl.BlockSpec((LANES, tm), lambda i: (0, i))],
        out_shape=[jax.ShapeDtypeStruct((n_rows, d), F32), jax.ShapeDtypeStruct((n_rows, d), F32),
                   jax.ShapeDtypeStruct((LANES, n_rows), F32)],
        compiler_params=_cp(("arbitrary",), VMEM_LIMIT),
        name="merge_layer",
    )(ya, yb_lat, yb_ctx, hf, hb, pb, yd_lat, yd_ctx, gate, x_all, modtab, mlg, wbr, wout, g2, wr_t)


def _router_body(s_ref, b_ref, e_ref, w_ref, rank_ref, cnt_ref, base_sc):
    @pl.when(pl.program_id(0) == 0)
    def _():
        base_sc[...] = jnp.zeros(base_sc.shape, F32)

    tm = s_ref.shape[1]
    s = s_ref[0:N_EXPERTS, :]
    sel = s + b_ref[0:N_EXPERTS, :]
    row = lambda a, e: a[e:e + 1, :]
    best, grp = None, None
    for g in range(N_EXPERT_GROUPS):
        v = [row(sel, EXPERTS_PER_GROUP * g + k) for k in range(EXPERTS_PER_GROUP)]
        gs = None
        for a in range(EXPERTS_PER_GROUP):
            for c in range(a + 1, EXPERTS_PER_GROUP):
                gs = v[a] + v[c] if gs is None else jnp.maximum(gs, v[a] + v[c])
        if best is None:
            best, grp = gs, jnp.zeros((1, tm), I32)
        else:
            better = gs > best
            grp = jnp.where(better, g, grp)
            best = jnp.where(better, gs, best)
    vals, affs = [], []
    for k in range(EXPERTS_PER_GROUP):
        vk, sk = row(sel, k), row(s, k)
        for g in range(1, N_EXPERT_GROUPS):
            hit = grp == g
            vk = jnp.where(hit, row(sel, EXPERTS_PER_GROUP * g + k), vk)
            sk = jnp.where(hit, row(s, EXPERTS_PER_GROUP * g + k), sk)
        vals.append(vk)
        affs.append(sk)
    i1, b1, w1 = jnp.zeros((1, tm), I32), vals[0], affs[0]
    for k in range(1, EXPERTS_PER_GROUP):
        better = vals[k] > b1
        i1 = jnp.where(better, k, i1)
        w1 = jnp.where(better, affs[k], w1)
        b1 = jnp.where(better, vals[k], b1)
    i2 = jnp.zeros((1, tm), I32)
    b2 = jnp.full((1, tm), -jnp.inf, F32)
    w2 = jnp.zeros((1, tm), F32)
    for k in range(EXPERTS_PER_GROUP):
        cand = (i1 != k) & (vals[k] > b2)
        i2 = jnp.where(cand, k, i2)
        w2 = jnp.where(cand, affs[k], w2)
        b2 = jnp.where(cand, vals[k], b2)
    e1 = grp * EXPERTS_PER_GROUP + i1
    e2 = grp * EXPERTS_PER_GROUP + i2
    tot = w1 + w2
    e_ref[...] = jnp.concatenate([e1, e2], axis=0)
    wpad = jnp.concatenate([w1 / tot, w2 / tot, jnp.zeros((6, tm), F32)], axis=0)
    w_ref[...] = wpad.T
    ids = lax.broadcasted_iota(I32, (N_EXPERTS, tm), 0)
    oh1 = (ids == e1).astype(F32)
    oh2 = (ids == e2).astype(F32)
    oh = oh1 + oh2
    before = (lax.broadcasted_iota(I32, (tm, tm), 0) < lax.broadcasted_iota(I32, (tm, tm), 1)).astype(BF16)
    prior = _dot(oh.astype(BF16), before) + base_sc[...]
    r1 = jnp.sum(oh1 * prior, axis=0, keepdims=True)
    r2 = jnp.sum(oh2 * prior, axis=0, keepdims=True)
    rank_ref[...] = jnp.concatenate([r1, r2], axis=0).astype(I32)
    base = base_sc[...] + jnp.sum(oh, axis=1, keepdims=True)
    base_sc[...] = base
    cnt_ref[...] = jnp.broadcast_to(base, cnt_ref.shape).astype(I32)


def route(s_t, b_router):
    n_rows = s_t.shape[1]
    tm = ROW_TILE
    b_col = jnp.pad(b_router.astype(F32), (0, LANES - N_EXPERTS)).reshape(LANES, 1)
    return pl.pallas_call(
        _router_body,
        grid=(n_rows // tm,),
        in_specs=[pl.BlockSpec((LANES, tm), lambda i: (0, i)), pl.BlockSpec((LANES, 1), lambda i: (0, 0))],
        out_specs=[pl.BlockSpec((TOP_K, tm), lambda i: (0, i)), pl.BlockSpec((tm, 8), lambda i: (i, 0)),
                   pl.BlockSpec((TOP_K, tm), lambda i: (0, i)), pl.BlockSpec((N_EXPERTS, LANES), lambda i: (0, 0))],
        out_shape=[jax.ShapeDtypeStruct((TOP_K, n_rows), I32), jax.ShapeDtypeStruct((n_rows, 8), F32),
                   jax.ShapeDtypeStruct((TOP_K, n_rows), I32), jax.ShapeDtypeStruct((N_EXPERTS, LANES), I32)],
        scratch_shapes=[pltpu.VMEM((N_EXPERTS, 1), F32)],
        compiler_params=_cp(("arbitrary",)),
        name="moe_router",
    )(s_t, b_col)


def _row_copy(src_ref, src_row, dst_ref, dst_row, sem):
    return pltpu.make_async_copy(src_ref.at[pl.ds(src_row, 1), :], dst_ref.at[pl.ds(dst_row, 1), :], sem)


def _dispatch_body(dest_ref, h_ref, buf_in_ref, buf_ref, sem):
    del buf_in_ref
    tm = h_ref.shape[0]

    def issue(t, carry):
        for k in range(TOP_K):
            _row_copy(h_ref, t, buf_ref, dest_ref[0, k, t], sem).start(priority=k)
        return carry

    lax.fori_loop(0, tm, issue, 0, unroll=8)
    for k in range(TOP_K):
        pltpu.make_async_copy(h_ref, buf_ref.at[pl.ds(0, tm), :], sem).wait()


def moe_dispatch(h2, dest3, buf0):
    n_rows, d = h2.shape
    tm = ROW_TILE
    return pl.pallas_call(
        _dispatch_body,
        grid=(n_rows // tm,),
        in_specs=[pl.BlockSpec((1, TOP_K, tm), lambda i: (i, 0, 0), memory_space=pltpu.SMEM),
                  pl.BlockSpec((tm, d), lambda i: (i, 0)),
                  pl.BlockSpec(memory_space=pl.ANY)],
        out_specs=pl.BlockSpec(memory_space=pl.ANY),
        out_shape=jax.ShapeDtypeStruct(buf0.shape, buf0.dtype),
        scratch_shapes=[pltpu.SemaphoreType.DMA(())],
        input_output_aliases={2: 0},
        compiler_params=_cp(("arbitrary",)),
        name="moe_dispatch",
    )(dest3, h2, buf0)


def _expert_body(be_ref, x_ref, w1_ref, w3_ref, w2_ref, o_ref, w1_sc, w3_sc, w2_sc):
    i = pl.program_id(0)
    changed = jnp.logical_or(i == 0, be_ref[i] != be_ref[jnp.maximum(i - 1, 0)])

    @pl.when(changed)
    def _():
        w1_sc[...] = w1_ref[0].astype(BF16)
        w3_sc[...] = w3_ref[0].astype(BF16)
        w2_sc[...] = w2_ref[0].astype(BF16)

    x = x_ref[...].astype(BF16)
    a = _dot(x, w1_sc[...])
    mid = (a * jax.nn.sigmoid(a)) * _dot(x, w3_sc[...])
    o_ref[...] = _dot(mid.astype(BF16), w2_sc[...])


def moe_experts(buf, blk_expert, w1, w3, w2, layer, blk):
    n_slots, d = buf.shape
    de = w1.shape[3]
    grid_spec = pltpu.PrefetchScalarGridSpec(
        num_scalar_prefetch=1,
        grid=(n_slots // blk,),
        in_specs=[pl.BlockSpec((blk, d), lambda i, be: (i, 0)),
                  pl.BlockSpec((None, 1, d, de), lambda i, be: (layer, be[i], 0, 0)),
                  pl.BlockSpec((None, 1, d, de), lambda i, be: (layer, be[i], 0, 0)),
                  pl.BlockSpec((None, 1, de, d), lambda i, be: (layer, be[i], 0, 0))],
        out_specs=pl.BlockSpec((blk, d), lambda i, be: (i, 0)),
        scratch_shapes=[pltpu.VMEM((d, de), BF16), pltpu.VMEM((d, de), BF16), pltpu.VMEM((de, d), BF16)],
    )
    return pl.pallas_call(
        _expert_body,
        grid_spec=grid_spec,
        out_shape=jax.ShapeDtypeStruct((n_slots, d), F32),
        compiler_params=_cp(("arbitrary",), VMEM_LIMIT),
        name="moe_experts",
    )(blk_expert, buf, w1, w3, w2)


def _combine_body(final, dest_ref, x_ref, w_ref, mod_ref, gf_ref, y_hbm, o_ref, y0_sc, y1_sc, sem):
    tm = x_ref.shape[0]
    bufs = (y0_sc, y1_sc)

    def issue(t, carry):
        for k in range(TOP_K):
            _row_copy(y_hbm, dest_ref[0, k, t], bufs[k], t, sem).start(priority=k)
        return carry

    lax.fori_loop(0, tm, issue, 0, unroll=8)
    for k in range(TOP_K):
        pltpu.make_async_copy(y_hbm.at[pl.ds(0, tm), :], bufs[k], sem).wait()
    w = w_ref[...]
    f = w[:, 0:1] * y0_sc[...] + w[:, 1:2] * y1_sc[...]
    x_new = x_ref[...] + mod_ref[0][5:6] * f
    if final:
        ms = jnp.mean(x_new * x_new, axis=-1, keepdims=True)
        x_new = x_new * lax.rsqrt(ms + EPS) * gf_ref[...]
    o_ref[...] = x_new


def moe_combine(dest3, x_rows, wts, modtab, g_final, y_slots, n_lat_rows, n_batch, final):
    n_rows, d = x_rows.shape
    tm = ROW_TILE
    per_b = n_lat_rows // n_batch // tm
    return pl.pallas_call(
        functools.partial(_combine_body, final),
        grid=(n_rows // tm,),
        in_specs=[pl.BlockSpec((1, TOP_K, tm), lambda i: (i, 0, 0), memory_space=pltpu.SMEM),
                  pl.BlockSpec((tm, d), lambda i: (i, 0)),
                  pl.BlockSpec((tm, 8), lambda i: (i, 0)),
                  pl.BlockSpec((1, N_MOD, d), lambda i: (jnp.minimum(i // per_b, n_batch), 0, 0)),
                  pl.BlockSpec((1, d), lambda i: (0, 0)),
                  pl.BlockSpec(memory_space=pl.ANY)],
        out_specs=pl.BlockSpec((tm, d), lambda i: (i, 0)),
        out_shape=jax.ShapeDtypeStruct((n_rows, d), F32),
        scratch_shapes=[pltpu.VMEM((tm, d), F32), pltpu.VMEM((tm, d), F32), pltpu.SemaphoreType.DMA(())],
        compiler_params=_cp(("arbitrary",)),
        name="moe_combine",
    )(dest3, x_rows, wts, modtab, g_final.reshape(1, d), y_slots)


def moe_layer(x_rows, h2, s_t, b_router, w1, w3, w2, layer, modtab, g_final, n_lat_rows, n_batch, final):
    n_rows, d = h2.shape
    blk = 2 * MOE_BLOCK
    experts, wts, rank, counts = route(s_t, b_router)
    cnt = counts[:, 0]
    padded = (cnt + blk - 1) // blk * blk
    pend = jnp.cumsum(padded)
    pstart = pend - padded
    hit = experts[..., None] == jnp.arange(N_EXPERTS, dtype=I32)
    dest = jnp.sum(jnp.where(hit, pstart.astype(I32), 0), axis=-1) + rank
    n_blocks = -(-(n_rows * TOP_K) // blk) + N_EXPERTS
    blk_start = jnp.arange(n_blocks, dtype=I32) * blk
    blk_expert = jnp.minimum(jnp.sum((pend[None, :] <= blk_start[:, None]).astype(I32), axis=1), N_EXPERTS - 1)
    dest3 = dest.reshape(TOP_K, n_rows // ROW_TILE, ROW_TILE).transpose(1, 0, 2)
    buf = moe_dispatch(h2, dest3, jnp.zeros((n_blocks * blk, d), F32))
    y_slots = moe_experts(buf, blk_expert, w1, w3, w2, layer, blk)
    return moe_combine(dest3, x_rows, wts, modtab, g_final, y_slots, n_lat_rows, n_batch, final)


_COL = dict(na_q=0, na_k=512, na_v=1024, ml_q=1536, ml_k=2048, ml_v=2560, ml_o=3072,
            gq_q=3584, gq_k=4096, gq_v=4224)


def _split_w_in(w_in):
    sizes = (BRANCH_WIDTH,) * 8 + (4 * ML_HEADS, BRANCH_WIDTH, GQ_KV_HEADS * GQ_HEAD_DIM, GQ_KV_HEADS * GQ_HEAD_DIM)
    idx = np.cumsum(sizes)[:-1].tolist()
    (s5_u, na_q, na_k, na_v, ml_q, ml_k, ml_v, ml_o, ml_gt, gq_q, gq_k, gq_v) = jnp.split(w_in, idx, axis=-1)
    wa = jnp.pad(ml_gt, ((0, 0), (0, LANES - 4 * ML_HEADS)))
    wb = jnp.concatenate([na_q, na_k, na_v, ml_q, ml_k * (ML_HEAD_DIM ** -0.5), ml_v, ml_o, gq_q, gq_k, gq_v], axis=1)
    return s5_u.T.astype(BF16), wa.astype(BF16), wb.astype(BF16)


def kernel(x, c, ctx, c_ctx, w_mod, b_mod, g_norm1, g_norm2, w_in, s5_lam_re, s5_lam_im, s5_log_dt, s5_b_re,
           s5_b_im, s5_c_re, s5_c_im, s5_d, s5_w_glu, s5_b_glu, na_rpb, ml_b_gates, ml_norm, gq_qnorm, gq_knorm,
           w_branch, w_gate, b_gate, w_out, w_router, b_router, moe_w1, moe_w3, moe_w2, g_final):
    b, n_lat, dm = x.shape
    n_ctx = ctx.shape[1]
    depth = w_in.shape[0]
    bn, bc = b * n_lat, b * n_ctx
    x_all = jnp.concatenate([x.reshape(bn, dm), ctx.reshape(bc, dm)], axis=0).astype(F32)
    c_all = jnp.concatenate([c.astype(F32), c_ctx.astype(F32)[None], jnp.zeros((8 - b - 1, dm), F32)], axis=0)
    cs_tab = rope_tables(n_lat)
    wr_t = jnp.pad(w_router.astype(BF16).T, ((0, LANES - N_EXPERTS), (0, 0)))
    out = None
    for l in range(depth):
        last = l == depth - 1
        with_ctx = not last
        modtab = mod_vectors(c_all, w_mod, b_mod, l)[:b + 1].reshape(b + 1, N_MOD, dm)
        w_u_t, wa, wb = _split_w_in(w_in[l])
        pa, pb, gate, h_all = in_projection(x_all, g_norm1[l], modtab, wa, wb, w_gate[l].astype(BF16), b_gate[l],
                                            bn, b)

        tables = s5_tables(s5_lam_re[l], s5_lam_im[l], s5_log_dt[l], s5_b_re[l], s5_b_im[l],
                           s5_c_re[l], s5_c_im[l], s5_d[l])
        ya = s5_mixer(h_all, w_u_t, tables, s5_w_glu[l].astype(BF16), s5_b_glu[l].astype(F32).reshape(1, -1),
                      n_lat, n_ctx, b)

        bias_tab = na_bias_tables(na_rpb[l], n_lat // GRID_W, n_ctx)
        n_rows = bn + bc if with_ctx else bn
        na_cols = (_COL['na_q'], _COL['na_k'], _COL['na_v'])
        yb = (na_mixer(pb, *na_cols, bias_tab, n_lat, n_ctx, b),
              na_ctx_attention(pb, *na_cols, n_lat, n_ctx, b) if with_ctx else None)

        ml_bias = jnp.pad(ml_b_gates[l].astype(F32), (0, LANES - 4 * ML_HEADS)).reshape(1, LANES)
        hf, hb = mlstm_mixer(pb, pa, _COL['ml_q'], _COL['ml_k'], _COL['ml_v'], 0, ml_bias, n_lat, n_ctx, b)

        yd = gq_mixer(pb, _COL['gq_q'], _COL['gq_k'], _COL['gq_v'], cs_tab, gq_qnorm[l], gq_knorm[l],
                      n_lat, n_ctx, b, with_ctx)
        x_mid, h2, s_t = merge_layer(
            ya, yb, yd, hf, hb, pb, _COL['ml_o'], gate, x_all, modtab, ml_norm[l].astype(F32).reshape(1, -1),
            w_branch[l].astype(BF16), w_out[l].astype(BF16), g_norm2[l].astype(F32).reshape(1, -1), wr_t,
            n_rows, bn, b)
        x_next = moe_layer(x_mid, h2, s_t, b_router, moe_w1, moe_w3, moe_w2, l, modtab, g_final, bn, b, last)
        if last:
            out = x_next.reshape(b, n_lat, dm).astype(x.dtype)
        else:
            x_all = x_next
    return out
```

```python
import functools
import math

import numpy as np
import jax
import jax.numpy as jnp
from jax import lax
from jax.experimental import pallas as pl
from jax.experimental.pallas import tpu as pltpu

F32 = jnp.float32
BF16 = jnp.bfloat16
I32 = jnp.int32

GRID_W = 64
N_MOD = 6
BRANCH_WIDTH = 512
N_BRANCHES = 4
S5_GROUP = 16
S5_GROUPS = BRANCH_WIDTH // S5_GROUP
S5_STATE = 64
NA_HEADS = 8
NA_HEAD_DIM = 64
NA_ROWS = 8
NA_COLS = 16
ML_HEADS = 4
ML_HEAD_DIM = 128
ML_CHUNK = 128
GQ_HEADS = 8
GQ_KV_HEADS = 2
GQ_HEAD_DIM = 64
ROPE_THETA = 10000.0
N_EXPERTS = 32
N_EXPERT_GROUPS = 8
EXPERTS_PER_GROUP = 4
TOP_K = 2
D_EXPERT = 512
MOE_BLOCK = 128
EPS = 1e-6
NEG_INIT = -1e30
MASK_NEG = -1e30
LOG2E = 1.4426950408889634

LANES = 128
ROW_TILE = 256
S5_CHUNK = 32
NA_TILE_ROWS = 4
VMEM_LIMIT = 56 * 1024 * 1024

HIGHEST = lax.Precision.HIGHEST


def _cp(sem, vmem=None):
    return pltpu.CompilerParams(dimension_semantics=sem, vmem_limit_bytes=vmem)


def _dot(a, b):
    return jnp.dot(a, b, preferred_element_type=F32)


def _dot_nt(a, b):
    return lax.dot_general(a, b, (((1,), (1,)), ((), ())), preferred_element_type=F32)


def _dot_tn(a, b):
    return lax.dot_general(a, b, (((0,), (0,)), ((), ())), preferred_element_type=F32)


def _rms_mod(x, g, shift, scale):
    ms = jnp.mean(x * x, axis=-1, keepdims=True)
    y = x * lax.rsqrt(ms + EPS) * g
    return y * (1.0 + scale) + shift


def _round_up(n, m):
    return -(-n // m) * m


def _mod_body(c_ref, w_ref, b_ref, o_ref):
    c = c_ref[...]
    a = (c * jax.nn.sigmoid(c)).astype(BF16)
    o_ref[...] = _dot(a, w_ref[...].astype(BF16)) + b_ref[...]


def mod_vectors(c_all, w_mod, b_mod, layer):
    d = c_all.shape[1]
    depth = w_mod.shape[0]
    return pl.pallas_call(
        _mod_body,
        grid=(N_MOD,),
        in_specs=[pl.BlockSpec((8, d), lambda j: (0, 0)),
                  pl.BlockSpec((None, d, d), lambda j: (layer, 0, j)),
                  pl.BlockSpec((None, 1, d), lambda j: (layer, 0, j))],
        out_specs=pl.BlockSpec((8, d), lambda j: (0, j)),
        out_shape=jax.ShapeDtypeStruct((8, N_MOD * d), F32),
        compiler_params=_cp(("arbitrary",)),
        name="mod_vectors",
    )(c_all, w_mod, b_mod.reshape(depth, 1, -1))


def _inproj_body(x_ref, g_ref, mod_ref, wa_ref, wb_ref, wg_ref, bg_ref, oa_ref, ob_ref, og_ref, oh_ref):
    mod = mod_ref[0]
    h = _rms_mod(x_ref[...], g_ref[...], mod[0:1], mod[1:2]).astype(BF16)
    oh_ref[...] = h
    oa_ref[...] = _dot(h, wa_ref[...])
    ob_ref[...] = _dot(h, wb_ref[...]).astype(BF16)
    og_ref[...] = jax.nn.sigmoid(_dot(h, wg_ref[...]) + bg_ref[...]).astype(BF16)


def in_projection(x_all, g, modtab, wa, wb, wg, bg, n_lat_rows, n_batch):
    r, d = x_all.shape
    tm = ROW_TILE
    per_b = n_lat_rows // n_batch // tm

    def mod_idx(i):
        return (jnp.minimum(i // per_b, n_batch), 0, 0)

    const = lambda i: (0, 0)
    return pl.pallas_call(
        _inproj_body,
        grid=(r // tm,),
        in_specs=[pl.BlockSpec((tm, d), lambda i: (i, 0)),
                  pl.BlockSpec((1, d), const),
                  pl.BlockSpec((1, N_MOD, d), mod_idx),
                  pl.BlockSpec(wa.shape, const, pipeline_mode=pl.Buffered(1)),
                  pl.BlockSpec(wb.shape, const, pipeline_mode=pl.Buffered(1)),
                  pl.BlockSpec(wg.shape, const, pipeline_mode=pl.Buffered(1)),
                  pl.BlockSpec((1, wg.shape[1]), const)],
        out_specs=[pl.BlockSpec((tm, wa.shape[1]), lambda i: (i, 0)),
                   pl.BlockSpec((tm, wb.shape[1]), lambda i: (i, 0)),
                   pl.BlockSpec((tm, wg.shape[1]), lambda i: (i, 0)),
                   pl.BlockSpec((tm, d), lambda i: (i, 0))],
        out_shape=[jax.ShapeDtypeStruct((r, wa.shape[1]), F32),
                   jax.ShapeDtypeStruct((r, wb.shape[1]), BF16),
                   jax.ShapeDtypeStruct((r, wg.shape[1]), BF16),
                   jax.ShapeDtypeStruct((r, d), BF16)],
        compiler_params=_cp(("arbitrary",), VMEM_LIMIT),
        name="in_projection",
    )(x_all, g.reshape(1, d), modtab, wa, wb, wg, bg.reshape(1, -1))


def s5_tables(lam_re, lam_im, log_dt, b_re, b_im, c_re, c_im, d_skip):
    ell, g_n, p_n, c_n = S5_CHUNK, S5_GROUPS, S5_STATE, S5_GROUP
    lam_re, lam_im = lam_re.astype(F32), lam_im.astype(F32)
    b_re, b_im, c_re, c_im = (t.astype(F32) for t in (b_re, b_im, c_re, c_im))
    dt = jnp.exp(log_dt.astype(F32))[..., None]
    mag = jnp.exp(lam_re * dt)
    a_re = mag * jnp.cos(lam_im * dt)
    a_im = mag * jnp.sin(lam_im * dt)
    den = lam_re * lam_re + lam_im * lam_im
    nr = a_re - 1.0
    f_re = (nr * lam_re + a_im * lam_im) / den
    f_im = (a_im * lam_re - nr * lam_im) / den
    bb_re = f_re[..., None] * b_re - f_im[..., None] * b_im
    bb_im = f_re[..., None] * b_im + f_im[..., None] * b_re
    k = jnp.arange(ell + 1, dtype=F32)
    pmag = jnp.exp((lam_re * dt)[..., None] * k)
    ang = (lam_im * dt)[..., None] * k
    pr, pi = pmag * jnp.cos(ang), pmag * jnp.sin(ang)
    ab_re = pr[..., None] * bb_re[:, :, :, None, :] - pi[..., None] * bb_im[:, :, :, None, :]
    ab_im = pr[..., None] * bb_im[:, :, :, None, :] + pi[..., None] * bb_re[:, :, :, None, :]
    flat = lambda t: t.reshape(2 * g_n, p_n, (ell + 1) * c_n)
    kk = (jnp.einsum('bcp,bpn->bcn', c_re.reshape(2 * g_n, c_n, p_n), flat(ab_re), precision=HIGHEST)
          - jnp.einsum('bcp,bpn->bcn', c_im.reshape(2 * g_n, c_n, p_n), flat(ab_im), precision=HIGHEST))
    kk = kk.reshape(2, g_n, c_n, ell + 1, c_n)
    centre = kk[0][:, :, 0] + kk[1][:, :, 0] + d_skip.astype(F32).reshape(g_n, c_n, 1) * jnp.eye(c_n, dtype=F32)
    w = jnp.concatenate([kk[0][:, :, ell - 1:0:-1], centre[:, :, None], kk[1][:, :, 1:ell]], axis=2)
    wf = w.reshape(g_n, c_n, (2 * ell - 1) * c_n)
    toe = jnp.stack([wf[:, :, (ell - 1 - t) * c_n:(2 * ell - 1 - t) * c_n] for t in range(ell)], axis=1)
    tsum_t = toe.astype(BF16).reshape(g_n, ell * c_n, ell * c_n)

    parity = [jnp.asarray(np.arange(g_n) % 2 == q, F32) for q in range(2)]
    parts = []
    for d in range(2):
        for ab in (ab_re, ab_im):
            sel = ab[d][:, :, :ell]
            if d == 0:
                sel = sel[:, :, ::-1]
            sel = sel.reshape(g_n, p_n, ell * c_n)
            parts += [sel * parity[q][:, None, None] for q in range(2)]
    mend_t = jnp.stack(parts, axis=1).reshape(g_n, 8 * p_n, ell * c_n)

    rows = []
    for d in range(2):
        prk, pik = pr[d][:, :, 1:ell + 1], pi[d][:, :, 1:ell + 1]
        if d == 1:
            prk, pik = prk[:, :, ::-1], pik[:, :, ::-1]
        prk = prk.transpose(0, 2, 1)[:, :, None, :]
        pik = pik.transpose(0, 2, 1)[:, :, None, :]
        cr, ci = c_re[d][:, None], c_im[d][:, None]
        for part in (cr * prk - ci * pik, -cr * pik - ci * prk):
            rows += [part * parity[q][:, None, None, None] for q in range(2)]
    wst_t = jnp.concatenate(rows, axis=-1).reshape(g_n, ell * c_n, 8 * p_n)

    al = jnp.stack([pr[0][:, :, ell], pi[0][:, :, ell], pr[1][:, :, ell], pi[1][:, :, ell]], axis=1)
    a_chunk = al.reshape(g_n // 2, 2, 4, p_n).transpose(0, 2, 1, 3).reshape(g_n // 2, 8 * p_n)
    return tsum_t.astype(BF16), mend_t.astype(BF16), wst_t.astype(BF16), a_chunk


def _s5_proj_body(hl_ref, hc_ref, w_ref, o_ref):
    n_lat = hl_ref.shape[0]
    n_pad = o_ref.shape[1] - LANES
    hl = hl_ref[...]
    if n_pad > n_lat:
        hl = jnp.concatenate([hl, jnp.zeros((n_pad - n_lat, hl.shape[1]), BF16)], axis=0)
    o_ref[:, 0:n_pad] = _dot_nt(w_ref[...], hl).astype(BF16)
    hc = hc_ref[...]
    hc = jnp.concatenate([hc, jnp.zeros((LANES - hc.shape[0], hc.shape[1]), BF16)], axis=0)
    o_ref[:, n_pad:] = _dot_nt(w_ref[...], hc).astype(BF16)


def _s5_end_body(u_ref, m_ref, o_ref):
    width = m_ref.shape[1]
    acc = None
    for q in range(2):
        u = u_ref[:, q].reshape(width, u_ref.shape[3])
        term = _dot(m_ref[q], u)
        acc = term if acc is None else acc + term
    o_ref[...] = acc.T


def _s5_scan_body(n_batch, per_b, n_ctx_chunks, lat_pad, e_ref, a_ref, o_ref):
    n_pairs = e_ref.shape[0]
    o_ref[...] = jnp.zeros(o_ref.shape, F32)
    sub = 8
    ctx_groups = n_ctx_chunks // sub
    n_groups = (per_b + n_ctx_chunks) // sub
    coef = [[a_ref[q, :, j * LANES:(j + 1) * LANES] for j in range(4)] for q in range(n_pairs)]

    def group(g, carry):
        new = list(carry)
        for b in range(n_batch):
            ctx = g < ctx_groups
            up = jnp.where(ctx, lat_pad + b * n_ctx_chunks + g * sub, b * per_b + (g - ctx_groups) * sub)
            down = jnp.where(ctx, lat_pad + (b + 1) * n_ctx_chunks - (g + 1) * sub,
                             (b + 1) * per_b - (g - ctx_groups + 1) * sub)
            for q in range(n_pairs):
                for d, base in enumerate((up, down)):
                    base = pl.multiple_of(base, sub)
                    lanes = slice(2 * d * LANES, (2 * d + 2) * LANES)
                    e = e_ref[q, pl.ds(base, sub), lanes]
                    k = ((b * n_pairs + q) * 2 + d) * 2
                    sr, si = new[k], new[k + 1]
                    ar, ai = coef[q][2 * d], coef[q][2 * d + 1]
                    rows = [None] * sub
                    for step in range(sub):
                        r = step if d == 0 else sub - 1 - step
                        rows[r] = jnp.concatenate([sr, si], axis=1)
                        er, ei = e[r:r + 1, :LANES], e[r:r + 1, LANES:]
                        sr, si = ar * sr - ai * si + er, ar * si + ai * sr + ei
                    o_ref[q, pl.ds(base, sub), lanes] = jnp.concatenate(rows, axis=0)
                    new[k], new[k + 1] = sr, si
        return tuple(new)

    z = jnp.zeros((1, LANES), F32)
    lax.fori_loop(0, n_groups, group, tuple(z for _ in range(n_batch * n_pairs * 4)))


def _s5_out_body(u_ref, t_ref, s_ref, w_ref, o_ref):
    width = t_ref.shape[0]
    u = u_ref[...].reshape(width, u_ref.shape[2])
    y = _dot(t_ref[...], u) + _dot_nt(w_ref[...], s_ref[...].astype(BF16))
    o_ref[...] = y.reshape(o_ref.shape)


def _s5_glu_body(n_lat_chunks, n_ctx_chunks, lat_pad, y_ref, w_ref, b_ref, o_ref):
    y = y_ref[...].T
    if lat_pad == n_lat_chunks:
        y = y[:n_lat_chunks + n_ctx_chunks]
    else:
        y = jnp.concatenate([y[:n_lat_chunks], y[lat_pad:lat_pad + n_ctx_chunks]], axis=0)
    z = jax.nn.gelu(y)
    o_ref[...] = (z * jax.nn.sigmoid(_dot(z.astype(BF16), w_ref[...]) + b_ref[...])).astype(o_ref.dtype)


def s5_mixer(h_all, w_u_t, tables, w_glu, b_glu, n_lat, n_ctx, n_batch):
    tsum_t, mend_t, wst_t, a_chunk = tables
    r, d = h_all.shape
    h2 = h_all.reshape(r // S5_CHUNK, S5_CHUNK * d)
    ell, g_n, c_n = S5_CHUNK, S5_GROUPS, S5_GROUP
    width = ell * c_n
    bw = g_n * c_n
    n_lat_chunks = n_batch * n_lat // ell
    n_ctx_chunks = n_batch * n_ctx // ell
    assert n_lat_chunks % n_ctx_chunks == 0 and n_ctx_chunks % 16 == 0 and n_ctx_chunks <= LANES
    lat_pad = _round_up(n_lat_chunks, LANES)
    nch = lat_pad + LANES
    const = lambda t: (0, 0)
    u_t = pl.pallas_call(
        _s5_proj_body,
        grid=(ell,),
        in_specs=[pl.BlockSpec((n_lat_chunks, d), lambda t: (0, t)),
                  pl.BlockSpec((n_ctx_chunks, d), lambda t: (n_lat_chunks // n_ctx_chunks, t)),
                  pl.BlockSpec((bw, d), const)],
        out_specs=pl.BlockSpec((None, bw, nch), lambda t: (t, 0, 0)),
        out_shape=jax.ShapeDtypeStruct((ell, bw, nch), BF16),
        compiler_params=_cp(("arbitrary",)),
        name="s5_projection",
    )(h2, h2, w_u_t)
    u4 = u_t.reshape(ell, g_n, c_n, nch)
    ends = pl.pallas_call(
        _s5_end_body,
        grid=(g_n // 2,),
        in_specs=[pl.BlockSpec((ell, 2, c_n, nch), lambda p: (0, p, 0, 0)),
                  pl.BlockSpec((2, width, width), lambda p: (p, 0, 0))],
        out_specs=pl.BlockSpec((None, nch, width), lambda p: (p, 0, 0)),
        out_shape=jax.ShapeDtypeStruct((g_n // 2, nch, width), F32),
        compiler_params=_cp(("arbitrary",)),
        name="s5_chunk_ends",
    )(u4, mend_t)
    pairs_per_step = 4
    assert (n_ctx // ell) % 8 == 0 and (n_lat // ell) % 8 == 0
    states = pl.pallas_call(
        functools.partial(_s5_scan_body, n_batch, n_lat // ell, n_ctx // ell, lat_pad),
        grid=(g_n // 2 // pairs_per_step,),
        in_specs=[pl.BlockSpec((pairs_per_step, nch, width), lambda j: (j, 0, 0)),
                  pl.BlockSpec((pairs_per_step, 1, width), lambda j: (j, 0, 0))],
        out_specs=pl.BlockSpec((pairs_per_step, nch, width), lambda j: (j, 0, 0)),
        out_shape=jax.ShapeDtypeStruct((g_n // 2, nch, width), F32),
        compiler_params=_cp(("arbitrary",), VMEM_LIMIT),
        name="s5_state_scan",
    )(ends, a_chunk.reshape(g_n // 2, 1, width))
    y_t = pl.pallas_call(
        _s5_out_body,
        grid=(g_n,),
        in_specs=[pl.BlockSpec((ell, None, c_n, nch), lambda gi: (0, gi, 0, 0)),
                  pl.BlockSpec((None, width, width), lambda gi: (gi, 0, 0)),
                  pl.BlockSpec((None, nch, width), lambda gi: (gi // 2, 0, 0)),
                  pl.BlockSpec((None, width, width), lambda gi: (gi, 0, 0))],
        out_specs=pl.BlockSpec((ell, None, c_n, nch), lambda gi: (0, gi, 0, 0)),
        out_shape=jax.ShapeDtypeStruct((ell, g_n, c_n, nch), F32),
        compiler_params=_cp(("arbitrary",)),
        name="s5_outputs",
    )(u4, tsum_t, states, wst_t)
    n_chunks = n_lat_chunks + n_ctx_chunks
    ya = pl.pallas_call(
        functools.partial(_s5_glu_body, n_lat_chunks, n_ctx_chunks, lat_pad),
        grid=(ell,),
        in_specs=[pl.BlockSpec((None, bw, nch), lambda t: (t, 0, 0)),
                  pl.BlockSpec((bw, bw), const),
                  pl.BlockSpec((1, bw), const)],
        out_specs=pl.BlockSpec((n_chunks, bw), lambda t: (0, t)),
        out_shape=jax.ShapeDtypeStruct((n_chunks, ell * bw), BF16),
        compiler_params=_cp(("arbitrary",)),
        name="s5_glu",
    )(y_t.reshape(ell, bw, nch), w_glu, b_glu)
    return ya.reshape(r, bw)


def rope_tables(n_lat):
    half = GQ_HEAD_DIM // 2
    quarter = half // 2
    t = np.arange(n_lat)
    freqs = ROPE_THETA ** (-np.arange(quarter, dtype=np.float64) / quarter)
    ang_r = (t // GRID_W)[:, None] * freqs
    ang_c = (t % GRID_W)[:, None] * freqs
    ang = np.concatenate([ang_r, ang_r, ang_c, ang_c], axis=1)
    sign = np.concatenate([-np.ones(quarter), np.ones(quarter)] * 2)
    cos = np.concatenate([np.cos(ang), np.ones((ROW_TILE, GQ_HEAD_DIM))], axis=0)
    sin = np.concatenate([np.sin(ang) * sign, np.zeros((ROW_TILE, GQ_HEAD_DIM))], axis=0)
    tab = np.concatenate([cos, cos, sin, sin], axis=1)
    return jnp.asarray(tab, F32)


def _group_ones(width, group):
    i = np.arange(width)
    return jnp.asarray((i[:, None] // group) == (i[None, :] // group), BF16)


def _group_mean_sq(x, ones_blk, group):
    sq = x * x
    hi = sq.astype(BF16)
    lo = (sq - hi.astype(F32)).astype(BF16)
    return (_dot(hi, ones_blk) + _dot(lo, ones_blk)) * (1.0 / group)


def _rope(x, cos, sin):
    w = x.shape[-1]
    q = GQ_HEAD_DIM // 4
    lane = lax.broadcasted_iota(I32, x.shape, 1)
    first = (lane % (2 * q)) < q
    partner = jnp.where(first, pltpu.roll(x, w - q, 1), pltpu.roll(x, q, 1))
    return x * cos + partner * sin


def _gq_prep_body(q_ref, k_ref, v_ref, cs_ref, gq_ref, gk_ref, oq_ref, ok_ref, qm_ref, kr_ref, va_ref):
    cs = cs_ref[...]
    cos1, sin1 = cs[:, :LANES], cs[:, LANES:]
    q = q_ref[...].astype(F32)
    qn = q * lax.rsqrt(_group_mean_sq(q, oq_ref[...], GQ_HEAD_DIM) + EPS) * gq_ref[...]
    n_pairs = q.shape[1] // LANES
    qr = _rope(qn, jnp.concatenate([cos1] * n_pairs, axis=1), jnp.concatenate([sin1] * n_pairs, axis=1))
    qr = (qr * (GQ_HEAD_DIM ** -0.5 * LOG2E)).astype(BF16)
    lane = lax.broadcasted_iota(I32, (q.shape[0], LANES), 1)
    heads_per_kv = GQ_HEADS // GQ_KV_HEADS
    for h in range(GQ_HEADS):
        pair = qr[:, (h // 2) * LANES:(h // 2 + 1) * LANES]
        kv = h // heads_per_kv
        if h % 2 != kv:
            pair = pltpu.roll(pair, GQ_HEAD_DIM, 1)
        keep = (lane >= kv * GQ_HEAD_DIM) & (lane < (kv + 1) * GQ_HEAD_DIM)
        qm_ref[h] = jnp.where(keep, pair, jnp.zeros_like(pair))
    k = k_ref[...].astype(F32)
    kn = k * lax.rsqrt(_group_mean_sq(k, ok_ref[...], GQ_HEAD_DIM) + EPS) * gk_ref[...]
    kr_ref[...] = _rope(kn, cos1, sin1).astype(BF16)
    va_ref[...] = jnp.concatenate([v_ref[...], jnp.ones(v_ref.shape, BF16)], axis=1)


def gq_prepare(pb, col_q, col_k, col_v, cs_tab, g_q, g_k, n_lat, n_ctx, n_batch):
    r = pb.shape[0]
    tm = ROW_TILE
    assert n_ctx == tm and n_lat % tm == 0
    nb = n_lat // tm
    n_lat_tiles = n_batch * nb

    def tab_idx(i):
        return (jnp.where(i < n_lat_tiles, i % nb, nb), 0)

    def kv_idx(i):
        lat = (i // nb) * (nb + 1) + i % nb
        ctx = (i - n_lat_tiles) * (nb + 1) + nb
        return (jnp.where(i < n_lat_tiles, lat, ctx), 0)

    qw = GQ_HEADS * GQ_HEAD_DIM
    const = lambda i: (0, 0)
    gq = jnp.tile(g_q.astype(F32), GQ_HEADS).reshape(1, qw)
    gk = jnp.tile(g_k.astype(F32), GQ_KV_HEADS).reshape(1, LANES)
    n_keys = n_batch * (n_lat + n_ctx)
    return pl.pallas_call(
        _gq_prep_body,
        grid=(r // tm,),
        in_specs=[pl.BlockSpec((tm, qw), lambda i: (i, col_q // qw)),
                  pl.BlockSpec((tm, LANES), lambda i: (i, col_k // LANES)),
                  pl.BlockSpec((tm, LANES), lambda i: (i, col_v // LANES)),
                  pl.BlockSpec((tm, 2 * LANES), tab_idx),
                  pl.BlockSpec((1, qw), const),
                  pl.BlockSpec((1, LANES), const),
                  pl.BlockSpec((qw, qw), const),
                  pl.BlockSpec((LANES, LANES), const)],
        out_specs=[pl.BlockSpec((GQ_HEADS, tm, LANES), lambda i: (0, i, 0)),
                   pl.BlockSpec((tm, LANES), kv_idx),
                   pl.BlockSpec((tm, 2 * LANES), kv_idx)],
        out_shape=[jax.ShapeDtypeStruct((GQ_HEADS, r, LANES), BF16),
                   jax.ShapeDtypeStruct((n_keys, LANES), BF16),
                   jax.ShapeDtypeStruct((n_keys, 2 * LANES), BF16)],
        compiler_params=_cp(("arbitrary",)),
        name="gq_prepare",
    )(pb, pb, pb, cs_tab, gq, gk, _group_ones(qw, GQ_HEAD_DIM), _group_ones(LANES, GQ_HEAD_DIM))


def _gq_flash_body(q_ref, k_ref, v_ref, o_ref, m_sc, acc_sc):
    kj = pl.program_id(2)
    n_h, tq, _ = q_ref.shape

    @pl.when(kj == 0)
    def _():
        m_sc[...] = jnp.full(m_sc.shape, -jnp.inf, F32)
        acc_sc[...] = jnp.zeros(acc_sc.shape, F32)

    k = k_ref[...]
    v = v_ref[...]
    hp = 1
    for c in range(n_h // hp):
        rows = slice(c * hp * tq, (c + 1) * hp * tq)
        s = _dot_nt(q_ref[c * hp:(c + 1) * hp].reshape(hp * tq, LANES), k)
        m_prev = m_sc[rows, :]
        m_new = jnp.maximum(m_prev, jnp.max(s, axis=-1, keepdims=True))
        p = jnp.exp2(s - m_new)
        acc_sc[rows, :] = jnp.exp2(m_prev - m_new) * acc_sc[rows, :] + _dot(p.astype(BF16), v)
        m_sc[rows, :] = m_new

    @pl.when(kj == pl.num_programs(2) - 1)
    def _():
        lane = lax.broadcasted_iota(I32, (tq, LANES), 1)
        heads_per_kv = n_h // GQ_KV_HEADS
        for j in range(n_h // 2):
            kv = (2 * j) // heads_per_kv
            halves = []
            for h in (2 * j, 2 * j + 1):
                a = acc_sc[h * tq:(h + 1) * tq, :]
                halves.append(a[:, :LANES] / a[:, LANES:LANES + 1])
            lo, hi = halves
            if kv == 0:
                hi = pltpu.roll(hi, GQ_HEAD_DIM, 1)
            else:
                lo = pltpu.roll(lo, GQ_HEAD_DIM, 1)
            o_ref[:, j * LANES:(j + 1) * LANES] = jnp.where(lane < GQ_HEAD_DIM, lo, hi).astype(o_ref.dtype)


def gq_attention(qm, keys, vals, n_rows_out, tq, tk, q_blk, k_blk, o_blk, n_q, n_k, n_batch):
    n_h = qm.shape[0]
    return pl.pallas_call(
        _gq_flash_body,
        grid=(n_batch, n_q, n_k),
        in_specs=[pl.BlockSpec((n_h, tq, LANES), lambda b, i, j: (0, q_blk(b, i), 0)),
                  pl.BlockSpec((tk, LANES), lambda b, i, j: (k_blk(b, j), 0)),
                  pl.BlockSpec((tk, 2 * LANES), lambda b, i, j: (k_blk(b, j), 0))],
        out_specs=pl.BlockSpec((tq, n_h * GQ_HEAD_DIM), lambda b, i, j: (o_blk(b, i), 0)),
        out_shape=jax.ShapeDtypeStruct((n_rows_out, n_h * GQ_HEAD_DIM), BF16),
        scratch_shapes=[pltpu.VMEM((n_h * tq, 1), F32), pltpu.VMEM((n_h * tq, 2 * LANES), F32)],
        compiler_params=_cp(("arbitrary", "arbitrary", "arbitrary"), VMEM_LIMIT),
        name="gq_attention",
    )(qm, keys, vals)


def _largest_divisor(n, cap):
    return max(d for d in range(1, cap + 1) if n % d == 0)


def gq_mixer(pb, col_q, col_k, col_v, cs_tab, g_q, g_k, n_lat, n_ctx, n_batch, with_ctx):
    qm, keys, vals = gq_prepare(pb, col_q, col_k, col_v, cs_tab, g_q, g_k, n_lat, n_ctx, n_batch)
    tq = 512
    tk = LANES * _largest_divisor((n_lat + n_ctx) // LANES, 22)
    n_q = n_lat // tq
    per_b = (n_lat + n_ctx) // tk
    lat_blk = lambda b, i: b * n_q + i
    y_lat = gq_attention(qm, keys, vals, n_batch * n_lat, tq, tk, lat_blk, lambda b, j: b * per_b + j, lat_blk,
                         n_q, per_b, n_batch)
    if not with_ctx:
        return y_lat, None
    tc = n_ctx
    y_ctx = gq_attention(qm, keys, vals, n_batch * n_ctx, tc, tc, lambda b, i: n_batch * n_lat // tc + b,
                         lambda b, j: b * ((n_lat + n_ctx) // tc) + n_lat // tc, lambda b, i: b, 1, 1, n_batch)
    return y_lat, y_ctx


def na_bias_tables(rpb, n_img_rows, n_ctx):
    tr = NA_TILE_ROWS
    nt = n_img_rows // tr
    assert nt >= 4
    kr = min(NA_ROWS, n_img_rows)
    n_heads = rpb.shape[0]
    qcol = np.arange(GRID_W)[:, None]
    kcol = np.arange(GRID_W)[None, :]
    dc = np.clip(kcol - qcol + NA_COLS - 1, 0, 2 * NA_COLS - 2)
    oh_c = (dc[None] == np.arange(2 * NA_COLS - 1)[:, None, None]).astype(np.float32)
    cstart = np.clip(qcol - NA_COLS // 2, 0, GRID_W - NA_COLS)
    col_ok = (kcol >= cstart) & (kcol < cstart + NA_COLS)
    by_col = jnp.einsum('hrd,dqk->hrqk', rpb.astype(F32), jnp.asarray(oh_c), precision=HIGHEST)
    classes = []
    for i in (0, 1, nt - 1):
        wb = int(np.clip(i - 1, 0, nt - 3))
        qrow = (i * tr + np.arange(tr))[:, None]
        krow = (wb * tr + np.arange(3 * tr))[None, :]
        rs = np.clip(qrow - kr // 2, 0, n_img_rows - kr)
        row_ok = (krow >= rs) & (krow < rs + kr)
        dr = np.clip(krow - qrow + NA_ROWS - 1, 0, 2 * NA_ROWS - 2)
        oh_r = ((dr[None] == np.arange(2 * NA_ROWS - 1)[:, None, None]) & row_ok[None]).astype(np.float32)
        bias = jnp.einsum('hrqk,rab->haqbk', by_col, jnp.asarray(oh_r), precision=HIGHEST)
        ok = row_ok[:, None, :, None] & col_ok[None, :, None, :]
        bias = jnp.where(jnp.asarray(ok)[None], bias * LOG2E, MASK_NEG)
        bias = bias.reshape(n_heads, tr * GRID_W, 3 * tr * GRID_W)
        classes.append(jnp.concatenate([bias, jnp.zeros((n_heads, tr * GRID_W, n_ctx), F32)], axis=-1))
    return jnp.stack(classes, axis=0)


def _pair_attention(q_pair, k_pair, v_pair, bias_fn):
    lane = lax.broadcasted_iota(I32, q_pair.shape, 1)
    v_aug = jnp.concatenate([v_pair, jnp.ones(v_pair.shape, BF16)], axis=1)
    out = None
    for hh in range(2):
        mine = (lane >= hh * NA_HEAD_DIM) & (lane < (hh + 1) * NA_HEAD_DIM)
        qm = jnp.where(mine, q_pair, jnp.zeros_like(q_pair))
        s = _dot_nt(qm, k_pair)
        b = bias_fn(hh)
        if b is not None:
            s = s + b
        m = jnp.max(s, axis=-1, keepdims=True)
        o_aug = _dot(jnp.exp2(s - m).astype(BF16), v_aug)
        o = o_aug[:, :LANES] / o_aug[:, LANES:LANES + 1]
        out = o if out is None else jnp.where(mine, o, out)
    return out


def _na_body(q_ref, k0_ref, k1_ref, k2_ref, kc_ref, v0_ref, v1_ref, v2_ref, vc_ref, b_ref, o_ref):
    for j in range(NA_HEADS // 2):
        sl = slice(j * LANES, (j + 1) * LANES)
        q_pair = q_ref[:, sl]
        k_pair = jnp.concatenate([k0_ref[:, sl], k1_ref[:, sl], k2_ref[:, sl], kc_ref[:, sl]], axis=0)
        v_pair = jnp.concatenate([v0_ref[:, sl], v1_ref[:, sl], v2_ref[:, sl], vc_ref[:, sl]], axis=0)
        o = _pair_attention(q_pair, k_pair, v_pair, lambda hh: b_ref[0, 2 * j + hh])
        o_ref[:, sl] = o.astype(o_ref.dtype)


def na_mixer(pb, col_q, col_k, col_v, bias_tab, n_lat, n_ctx, n_batch):
    tm = NA_TILE_ROWS * GRID_W
    assert n_ctx == tm
    w = NA_HEADS * NA_HEAD_DIM
    nt = n_lat // tm
    n_keys = 3 * tm + n_ctx
    cq, ck, cv = col_q // w, col_k // w, col_v // w
    ctx0 = n_batch * nt

    def win(o):
        return lambda b, i: (b * nt + jnp.clip(i - 1, 0, nt - 3) + o)

    def cls(b, i):
        return (jnp.where(i == 0, 0, jnp.where(i == nt - 1, 2, 1)), 0, 0, 0)

    kspecs = [pl.BlockSpec((tm, w), (lambda b, i, f=win(o): (f(b, i), ck))) for o in range(3)]
    vspecs = [pl.BlockSpec((tm, w), (lambda b, i, f=win(o): (f(b, i), cv))) for o in range(3)]
    return pl.pallas_call(
        _na_body,
        grid=(n_batch, nt),
        in_specs=[pl.BlockSpec((tm, w), lambda b, i: (b * nt + i, cq))] + kspecs
        + [pl.BlockSpec((tm, w), lambda b, i: (ctx0 + b, ck))] + vspecs
        + [pl.BlockSpec((tm, w), lambda b, i: (ctx0 + b, cv)),
           pl.BlockSpec((1, NA_HEADS, tm, n_keys), cls)],
        out_specs=pl.BlockSpec((tm, w), lambda b, i: (b * nt + i, 0)),
        out_shape=jax.ShapeDtypeStruct((n_batch * n_lat, w), BF16),
        compiler_params=_cp(("arbitrary", "arbitrary"), VMEM_LIMIT),
        name="na_attention",
    )(pb, pb, pb, pb, pb, pb, pb, pb, pb, bias_tab)


def _ctx_mha_body(q_ref, k_ref, v_ref, o_ref):
    for j in range(NA_HEADS // 2):
        sl = slice(j * LANES, (j + 1) * LANES)
        o = _pair_attention(q_ref[:, sl], k_ref[:, sl], v_ref[:, sl], lambda hh: None)
        o_ref[:, sl] = o.astype(o_ref.dtype)


def na_ctx_attention(pb, col_q, col_k, col_v, n_lat, n_ctx, n_batch):
    w = NA_HEADS * NA_HEAD_DIM
    ctx0 = n_batch * n_lat // n_ctx
    spec = lambda c: pl.BlockSpec((n_ctx, w), lambda b: (ctx0 + b, c // w))
    return pl.pallas_call(
        _ctx_mha_body,
        grid=(n_batch,),
        in_specs=[spec(col_q), spec(col_k), spec(col_v)],
        out_specs=pl.BlockSpec((n_ctx, w), lambda b: (b, 0)),
        out_shape=jax.ShapeDtypeStruct((n_batch * n_ctx, w), BF16),
        compiler_params=_cp(("arbitrary",)),
        name="na_ctx_attention",
    )(pb, pb, pb)


def _mlstm_body(n_batch, *refs):
    n_in = 8 * n_batch
    ins, bias_ref = refs[:n_in], refs[n_in]
    hf_ref, hb_ref, c_ref, n_ref, m_ref = refs[n_in + 1:]

    @pl.when(pl.program_id(0) == 0)
    def _():
        c_ref[...] = jnp.zeros(c_ref.shape, F32)
        n_ref[...] = jnp.zeros(n_ref.shape, F32)
        m_ref[...] = jnp.full(m_ref.shape, NEG_INIT, F32)

    tok = lax.broadcasted_iota(I32, (ML_CHUNK, ML_CHUNK), 0)
    src = lax.broadcasted_iota(I32, (ML_CHUNK, ML_CHUNK), 1)
    masks = (src <= tok, src >= tok)
    gates = []
    for b in range(n_batch):
        for d in range(2):
            g = ins[(b * 2 + d) * 4 + 3][...] + bias_ref[...]
            lf_cum = jnp.dot(masks[d].astype(F32), jax.nn.log_sigmoid(g), precision=HIGHEST,
                             preferred_element_type=F32)
            gates.append((g, lf_cum))
    gates = [(g, lf_cum, g.T, lf_cum.T) for g, lf_cum in gates]
    chains = []
    for b in range(n_batch):
        for d in range(2):
            g, lf_cum, g_t, lf_cum_t = gates[b * 2 + d]
            for h in range(ML_HEADS):
                ci, cf = d * 2 * ML_HEADS + h, d * 2 * ML_HEADS + ML_HEADS + h
                idx = (b * 2 + d) * ML_HEADS + h
                chains.append(dict(
                    b=b, d=d, h=h, idx=idx, bt_col=lf_cum[:, cf:cf + 1], bt_row=lf_cum_t[cf:cf + 1, :],
                    li_col=g[:, ci:ci + 1], li_row=g_t[ci:ci + 1, :], m_prev=m_ref[idx][:, 0:1],
                    c_prev=c_ref[idx], n_prev=n_ref[idx]))
    for c in chains:
        c['dmat'] = jnp.where(masks[c['d']], c['bt_col'] - c['bt_row'] + c['li_row'], -jnp.inf)
        c['inter'] = c['bt_col'] + c['m_prev']
    for c in chains:
        c['mt'] = jnp.maximum(c['inter'], jnp.max(c['dmat'], axis=-1, keepdims=True))
    for c in chains:
        refs_c = ins[(c['b'] * 2 + c['d']) * 4:(c['b'] * 2 + c['d']) * 4 + 3]
        sl = slice(c['h'] * ML_HEAD_DIM, (c['h'] + 1) * ML_HEAD_DIM)
        c['q'], c['k'], c['v'] = (r[:, sl] for r in refs_c)
        c['s'] = _dot_nt(c['q'], c['k']) * jnp.exp(c['dmat'] - c['mt'])
        c['w_inter'] = jnp.exp(c['inter'] - c['mt'])
    for c in chains:
        num = _dot(c['s'].astype(BF16), c['v']) + c['w_inter'] * _dot_nt(c['q'], c['c_prev'].astype(BF16))
        qn = jnp.sum(c['q'].astype(F32) * c['n_prev'], axis=-1, keepdims=True)
        den = jnp.sum(c['s'], axis=-1, keepdims=True) + c['w_inter'] * qn
        h_out = num / jnp.maximum(jnp.abs(den), jnp.exp(-c['mt']))
        h_ref = hb_ref if c['d'] else hf_ref
        h_ref[c['b'], :, c['h'] * ML_HEAD_DIM:(c['h'] + 1) * ML_HEAD_DIM] = h_out
    new_state = []
    for c in chains:
        b_last = c['bt_col'][0:1, :] if c['d'] else c['bt_col'][ML_CHUNK - 1:ML_CHUNK, :]
        g_col = b_last - c['bt_col'] + c['li_col']
        m_new = jnp.maximum(b_last + c['m_prev'], jnp.max(g_col, axis=0, keepdims=True))
        wg = jnp.exp(g_col - m_new)
        decay = jnp.exp(b_last + c['m_prev'] - m_new)
        c_new = decay * c['c_prev'] + _dot_tn((wg * c['v'].astype(F32)).astype(BF16), c['k'])
        n_new = decay * c['n_prev'] + jnp.sum(wg * c['k'].astype(F32), axis=0, keepdims=True)
        new_state.append((c['idx'], c_new, n_new, m_new))
    for idx, c_new, n_new, m_new in new_state:
        c_ref[idx] = c_new
        n_ref[idx] = n_new
        m_ref[idx] = jnp.broadcast_to(m_new, (1, LANES))


def mlstm_mixer(pb, pa, col_q, col_k, col_v, col_g, bias, n_lat, n_ctx, n_batch):
    w = ML_HEADS * ML_HEAD_DIM
    tc = ML_CHUNK
    nl, nc = n_lat // tc, n_ctx // tc

    def fwd_pos(i):
        return jnp.where(i < nc, nl + i, i - nc)

    def bwd_pos(i):
        return jnp.where(i < nc, nl + nc - 1 - i, nl - 1 - (i - nc))

    def row_blk(b, pos):
        return jnp.where(pos < nl, b * nl + pos, n_batch * nl + b * nc + pos - nl)

    def specs(b, pos_fn):
        return [pl.BlockSpec((tc, w), lambda i, c=c: (row_blk(b, pos_fn(i)), c // w)) for c in (col_q, col_k, col_v)] + [
            pl.BlockSpec((tc, LANES), lambda i: (row_blk(b, pos_fn(i)), col_g // LANES))]

    in_specs, operands = [], []
    for b in range(n_batch):
        for pos_fn in (fwd_pos, bwd_pos):
            in_specs += specs(b, pos_fn)
            operands += [pb, pb, pb, pa]
    n_st = 2 * ML_HEADS * n_batch
    out_shape = jax.ShapeDtypeStruct((n_batch, n_lat + n_ctx, w), F32)
    return pl.pallas_call(
        functools.partial(_mlstm_body, n_batch),
        grid=(nl + nc,),
        in_specs=in_specs + [pl.BlockSpec((1, LANES), lambda i: (0, 0))],
        out_specs=[pl.BlockSpec((n_batch, tc, w), lambda i: (0, fwd_pos(i), 0)),
                   pl.BlockSpec((n_batch, tc, w), lambda i: (0, bwd_pos(i), 0))],
        out_shape=[out_shape, out_shape],
        scratch_shapes=[pltpu.VMEM((n_st, ML_HEAD_DIM, ML_HEAD_DIM), F32),
                        pltpu.VMEM((n_st, 1, ML_HEAD_DIM), F32),
                        pltpu.VMEM((n_st, 1, LANES), F32)],
        compiler_params=_cp(("arbitrary",)),
        name="mlstm_chunks",
    )(*operands, bias)


def _merge_body(n_lat_tiles, ya_ref, ybl_ref, ybc_ref, hf_ref, hb_ref, o_ref, ydl_ref, ydc_ref, gate_ref, x_ref,
                mod_ref, mlg_ref, wbr_ref, wout_ref, g2_ref, wr_ref, xo_ref, h2_ref, st_ref):
    d = x_ref.shape[1]
    is_ctx = pl.program_id(0) >= n_lat_tiles
    yb = jnp.where(is_ctx, ybc_ref[...], ybl_ref[...])
    yd = jnp.where(is_ctx, ydc_ref[...], ydl_ref[...])
    hs = hf_ref[...] + hb_ref[...]
    segs = []
    for h in range(ML_HEADS):
        seg = hs[:, h * ML_HEAD_DIM:(h + 1) * ML_HEAD_DIM]
        segs.append(seg * lax.rsqrt(jnp.mean(seg * seg, axis=-1, keepdims=True) + EPS))
    ym = jnp.concatenate(segs, axis=1) * mlg_ref[...] * jax.nn.sigmoid(o_ref[...].astype(F32))
    ys = (ya_ref[...], yb, ym.astype(BF16), yd)
    merged = None
    for i in range(N_BRANCHES):
        term = gate_ref[:, i * d:(i + 1) * d].astype(F32) * _dot(ys[i], wbr_ref[i])
        merged = term if merged is None else merged + term
    y = _dot(merged.astype(BF16), wout_ref[...])
    mod = mod_ref[0]
    x_new = x_ref[...] + mod[2:3] * y
    xo_ref[...] = x_new
    h2 = _rms_mod(x_new, g2_ref[...], mod[3:4], mod[4:5])
    h2_ref[...] = h2
    st_ref[...] = jax.nn.sigmoid(_dot_nt(wr_ref[...], h2.astype(BF16)))


def merge_layer(ya, yb, yd, hf, hb, pb, col_o, gate, x_all, modtab, mlg, wbr, wout, g2, wr_t,
                n_rows, n_lat_rows, n_batch):
    d = x_all.shape[1]
    tm = ROW_TILE
    w = BRANCH_WIDTH
    per_b = n_lat_rows // n_batch // tm
    row = lambda i: (i, 0)
    const2 = lambda i: (0, 0)
    n_lat_tiles = n_lat_rows // tm
    lat_row = lambda i: (jnp.minimum(i, n_lat_tiles - 1), 0)
    ctx_row = lambda i: (jnp.clip(i - n_lat_tiles, 0, n_batch - 1), 0)

    def seq(i):
        lat = i < n_lat_tiles
        return (jnp.where(lat, i // per_b, i - n_lat_tiles), jnp.where(lat, i % per_b, per_b), 0)

    (yb_lat, yb_ctx), (yd_lat, yd_ctx) = yb, yd
    if yb_ctx is None:
        yb_ctx, yd_ctx = yb_lat, yd_lat
    return pl.pallas_call(
        functools.partial(_merge_body, n_lat_tiles),
        grid=(n_rows // tm,),
        in_specs=[pl.BlockSpec((tm, w), row), pl.BlockSpec((tm, w), lat_row), pl.BlockSpec((tm, w), ctx_row),
                  pl.BlockSpec((None, tm, w), seq), pl.BlockSpec((None, tm, w), seq),
                  pl.BlockSpec((tm, w), lambda i: (i, col_o // w)),
                  pl.BlockSpec((tm, w), lat_row), pl.BlockSpec((tm, w), ctx_row),
                  pl.BlockSpec((tm, N_BRANCHES * d), row),
                  pl.BlockSpec((tm, d), row),
                  pl.BlockSpec((1, N_MOD, d), lambda i: (jnp.minimum(i // per_b, n_batch), 0, 0)),
                  pl.BlockSpec((1, w), const2),
                  pl.BlockSpec((N_BRANCHES, w, d), lambda i: (0, 0, 0)),
                  pl.BlockSpec((d, d), const2), pl.BlockSpec((1, d), const2),
                  pl.BlockSpec((LANES, d), const2)],
        out_specs=[pl.BlockSpec((tm, d), row), pl.BlockSpec((tm, d), row),
                   pl.BlockSpec((LANES, tm), lambda i: (0, i))],
        out_shape=[jax.ShapeDtypeStruct((n_rows, d), F32), jax.ShapeDtypeStruct((n_rows, d), F32),
                   jax.ShapeDtypeStruct((LANES, n_rows), F32)],
        compiler_params=_cp(("arbitrary",), VMEM_LIMIT),
        name="merge_layer",
    )(ya, yb_lat, yb_ctx, hf, hb, pb, yd_lat, yd_ctx, gate, x_all, modtab, mlg, wbr, wout, g2, wr_t)


def _router_body(s_ref, b_ref, e_ref, w_ref, rank_ref, cnt_ref, base_sc):
    @pl.when(pl.program_id(0) == 0)
    def _():
        base_sc[...] = jnp.zeros(base_sc.shape, F32)

    tm = s_ref.shape[1]
    s = s_ref[0:N_EXPERTS, :]
    sel = s + b_ref[0:N_EXPERTS, :]
    row = lambda a, e: a[e:e + 1, :]
    best, grp = None, None
    for g in range(N_EXPERT_GROUPS):
        v = [row(sel, EXPERTS_PER_GROUP * g + k) for k in range(EXPERTS_PER_GROUP)]
        gs = None
        for a in range(EXPERTS_PER_GROUP):
            for c in range(a + 1, EXPERTS_PER_GROUP):
                gs = v[a] + v[c] if gs is None else jnp.maximum(gs, v[a] + v[c])
        if best is None:
            best, grp = gs, jnp.zeros((1, tm), I32)
        else:
            better = gs > best
            grp = jnp.where(better, g, grp)
            best = jnp.where(better, gs, best)
    vals, affs = [], []
    for k in range(EXPERTS_PER_GROUP):
        vk, sk = row(sel, k), row(s, k)
        for g in range(1, N_EXPERT_GROUPS):
            hit = grp == g
            vk = jnp.where(hit, row(sel, EXPERTS_PER_GROUP * g + k), vk)
            sk = jnp.where(hit, row(s, EXPERTS_PER_GROUP * g + k), sk)
        vals.append(vk)
        affs.append(sk)
    i1, b1, w1 = jnp.zeros((1, tm), I32), vals[0], affs[0]
    for k in range(1, EXPERTS_PER_GROUP):
        better = vals[k] > b1
        i1 = jnp.where(better, k, i1)
        w1 = jnp.where(better, affs[k], w1)
        b1 = jnp.where(better, vals[k], b1)
    i2 = jnp.zeros((1, tm), I32)
    b2 = jnp.full((1, tm), -jnp.inf, F32)
    w2 = jnp.zeros((1, tm), F32)
    for k in range(EXPERTS_PER_GROUP):
        cand = (i1 != k) & (vals[k] > b2)
        i2 = jnp.where(cand, k, i2)
        w2 = jnp.where(cand, affs[k], w2)
        b2 = jnp.where(cand, vals[k], b2)
    e1 = grp * EXPERTS_PER_GROUP + i1
    e2 = grp * EXPERTS_PER_GROUP + i2
    tot = w1 + w2
    e_ref[...] = jnp.concatenate([e1, e2], axis=0)
    wpad = jnp.concatenate([w1 / tot, w2 / tot, jnp.zeros((6, tm), F32)], axis=0)
    w_ref[...] = wpad.T
    ids = lax.broadcasted_iota(I32, (N_EXPERTS, tm), 0)
    oh1 = (ids == e1).astype(F32)
    oh2 = (ids == e2).astype(F32)
    oh = oh1 + oh2
    before = (lax.broadcasted_iota(I32, (tm, tm), 0) < lax.broadcasted_iota(I32, (tm, tm), 1)).astype(BF16)
    prior = _dot(oh.astype(BF16), before) + base_sc[...]
    r1 = jnp.sum(oh1 * prior, axis=0, keepdims=True)
    r2 = jnp.sum(oh2 * prior, axis=0, keepdims=True)
    rank_ref[...] = jnp.concatenate([r1, r2], axis=0).astype(I32)
    base = base_sc[...] + jnp.sum(oh, axis=1, keepdims=True)
    base_sc[...] = base
    cnt_ref[...] = jnp.broadcast_to(base, cnt_ref.shape).astype(I32)


def route(s_t, b_router):
    n_rows = s_t.shape[1]
    tm = ROW_TILE
    b_col = jnp.pad(b_router.astype(F32), (0, LANES - N_EXPERTS)).reshape(LANES, 1)
    return pl.pallas_call(
        _router_body,
        grid=(n_rows // tm,),
        in_specs=[pl.BlockSpec((LANES, tm), lambda i: (0, i)), pl.BlockSpec((LANES, 1), lambda i: (0, 0))],
        out_specs=[pl.BlockSpec((TOP_K, tm), lambda i: (0, i)), pl.BlockSpec((tm, 8), lambda i: (i, 0)),
                   pl.BlockSpec((TOP_K, tm), lambda i: (0, i)), pl.BlockSpec((N_EXPERTS, LANES), lambda i: (0, 0))],
        out_shape=[jax.ShapeDtypeStruct((TOP_K, n_rows), I32), jax.ShapeDtypeStruct((n_rows, 8), F32),
                   jax.ShapeDtypeStruct((TOP_K, n_rows), I32), jax.ShapeDtypeStruct((N_EXPERTS, LANES), I32)],
        scratch_shapes=[pltpu.VMEM((N_EXPERTS, 1), F32)],
        compiler_params=_cp(("arbitrary",)),
        name="moe_router",
    )(s_t, b_col)


def _row_copy(src_ref, src_row, dst_ref, dst_row, sem):
    return pltpu.make_async_copy(src_ref.at[pl.ds(src_row, 1), :], dst_ref.at[pl.ds(dst_row, 1), :], sem)


def _dispatch_body(dest_ref, h_ref, buf_in_ref, buf_ref, sem):
    del buf_in_ref
    tm = h_ref.shape[0]

    def issue(t, carry):
        for k in range(TOP_K):
            _row_copy(h_ref, t, buf_ref, dest_ref[0, k, t], sem).start(priority=k)
        return carry

    lax.fori_loop(0, tm, issue, 0, unroll=8)
    for k in range(TOP_K):
        pltpu.make_async_copy(h_ref, buf_ref.at[pl.ds(0, tm), :], sem).wait()


def moe_dispatch(h2, dest3, buf0):
    n_rows, d = h2.shape
    tm = ROW_TILE
    return pl.pallas_call(
        _dispatch_body,
        grid=(n_rows // tm,),
        in_specs=[pl.BlockSpec((1, TOP_K, tm), lambda i: (i, 0, 0), memory_space=pltpu.SMEM),
                  pl.BlockSpec((tm, d), lambda i: (i, 0)),
                  pl.BlockSpec(memory_space=pl.ANY)],
        out_specs=pl.BlockSpec(memory_space=pl.ANY),
        out_shape=jax.ShapeDtypeStruct(buf0.shape, buf0.dtype),
        scratch_shapes=[pltpu.SemaphoreType.DMA(())],
        input_output_aliases={2: 0},
        compiler_params=_cp(("arbitrary",)),
        name="moe_dispatch",
    )(dest3, h2, buf0)


def _expert_body(be_ref, nu_ref, x_ref, w1_ref, w3_ref, w2_ref, o_ref, w1_sc, w3_sc, w2_sc):
    i = pl.program_id(0)
    changed = jnp.logical_or(i == 0, be_ref[i] != be_ref[jnp.maximum(i - 1, 0)])
    used = i < nu_ref[0]

    @pl.when(jnp.logical_and(changed, used))
    def _():
        w1_sc[...] = w1_ref[0].astype(BF16)
        w3_sc[...] = w3_ref[0].astype(BF16)
        w2_sc[...] = w2_ref[0].astype(BF16)

    @pl.when(used)
    def _():
        x = x_ref[...].astype(BF16)
        a = _dot(x, w1_sc[...])
        mid = (a * jax.nn.sigmoid(a)) * _dot(x, w3_sc[...])
        o_ref[...] = _dot(mid.astype(BF16), w2_sc[...])

    @pl.when(jnp.logical_not(used))
    def _():
        o_ref[...] = jnp.zeros(o_ref.shape, o_ref.dtype)


def moe_experts(buf, blk_expert, n_used, w1, w3, w2, layer, blk):
    n_slots, d = buf.shape
    de = w1.shape[3]
    grid_spec = pltpu.PrefetchScalarGridSpec(
        num_scalar_prefetch=2,
        grid=(n_slots // blk,),
        in_specs=[pl.BlockSpec((blk, d), lambda i, be, nu: (i, 0)),
                  pl.BlockSpec((None, 1, d, de), lambda i, be, nu: (layer, be[i], 0, 0)),
                  pl.BlockSpec((None, 1, d, de), lambda i, be, nu: (layer, be[i], 0, 0)),
                  pl.BlockSpec((None, 1, de, d), lambda i, be, nu: (layer, be[i], 0, 0))],
        out_specs=pl.BlockSpec((blk, d), lambda i, be, nu: (i, 0)),
        scratch_shapes=[pltpu.VMEM((d, de), BF16), pltpu.VMEM((d, de), BF16), pltpu.VMEM((de, d), BF16)],
    )
    return pl.pallas_call(
        _expert_body,
        grid_spec=grid_spec,
        out_shape=jax.ShapeDtypeStruct((n_slots, d), F32),
        compiler_params=_cp(("arbitrary",), VMEM_LIMIT),
        name="moe_experts",
    )(blk_expert, n_used, buf, w1, w3, w2)


def _combine_body(final, dest_ref, x_ref, w_ref, mod_ref, gf_ref, y_hbm, o_ref, y0_sc, y1_sc, sem):
    tm = x_ref.shape[0]
    bufs = (y0_sc, y1_sc)

    def issue(t, carry):
        for k in range(TOP_K):
            _row_copy(y_hbm, dest_ref[0, k, t], bufs[k], t, sem).start(priority=k)
        return carry

    lax.fori_loop(0, tm, issue, 0, unroll=8)
    for k in range(TOP_K):
        pltpu.make_async_copy(y_hbm.at[pl.ds(0, tm), :], bufs[k], sem).wait()
    w = w_ref[...]
    f = w[:, 0:1] * y0_sc[...] + w[:, 1:2] * y1_sc[...]
    x_new = x_ref[...] + mod_ref[0][5:6] * f
    if final:
        ms = jnp.mean(x_new * x_new, axis=-1, keepdims=True)
        x_new = x_new * lax.rsqrt(ms + EPS) * gf_ref[...]
    o_ref[...] = x_new


def moe_combine(dest3, x_rows, wts, modtab, g_final, y_slots, n_lat_rows, n_batch, final):
    n_rows, d = x_rows.shape
    tm = ROW_TILE
    per_b = n_lat_rows // n_batch // tm
    return pl.pallas_call(
        functools.partial(_combine_body, final),
        grid=(n_rows // tm,),
        in_specs=[pl.BlockSpec((1, TOP_K, tm), lambda i: (i, 0, 0), memory_space=pltpu.SMEM),
                  pl.BlockSpec((tm, d), lambda i: (i, 0)),
                  pl.BlockSpec((tm, 8), lambda i: (i, 0)),
                  pl.BlockSpec((1, N_MOD, d), lambda i: (jnp.minimum(i // per_b, n_batch), 0, 0)),
                  pl.BlockSpec((1, d), lambda i: (0, 0)),
                  pl.BlockSpec(memory_space=pl.ANY)],
        out_specs=pl.BlockSpec((tm, d), lambda i: (i, 0)),
        out_shape=jax.ShapeDtypeStruct((n_rows, d), F32),
        scratch_shapes=[pltpu.VMEM((tm, d), F32), pltpu.VMEM((tm, d), F32), pltpu.SemaphoreType.DMA(())],
        compiler_params=_cp(("arbitrary",)),
        name="moe_combine",
    )(dest3, x_rows, wts, modtab, g_final.reshape(1, d), y_slots)


def moe_layer(x_rows, h2, s_t, b_router, w1, w3, w2, layer, modtab, g_final, n_lat_rows, n_batch, final):
    n_rows, d = h2.shape
    blk = 2 * MOE_BLOCK
    experts, wts, rank, counts = route(s_t, b_router)
    cnt = counts[:, 0]
    padded = (cnt + blk - 1) // blk * blk
    pend = jnp.cumsum(padded)
    pstart = pend - padded
    hit = experts[..., None] == jnp.arange(N_EXPERTS, dtype=I32)
    dest = jnp.sum(jnp.where(hit, pstart.astype(I32), 0), axis=-1) + rank
    n_blocks = -(-(n_rows * TOP_K) // blk) + N_EXPERTS
    blk_start = jnp.arange(n_blocks, dtype=I32) * blk
    blk_expert = jnp.minimum(jnp.sum((pend[None, :] <= blk_start[:, None]).astype(I32), axis=1), N_EXPERTS - 1)
    dest3 = dest.reshape(TOP_K, n_rows // ROW_TILE, ROW_TILE).transpose(1, 0, 2)
    buf = moe_dispatch(h2, dest3, jnp.zeros((n_blocks * blk, d), F32))
    n_used = (pend[-1:] // blk).astype(I32)
    y_slots = moe_experts(buf, blk_expert, n_used, w1, w3, w2, layer, blk)
    return moe_combine(dest3, x_rows, wts, modtab, g_final, y_slots, n_lat_rows, n_batch, final)


_COL = dict(na_q=0, na_k=512, na_v=1024, ml_q=1536, ml_k=2048, ml_v=2560, ml_o=3072,
            gq_q=3584, gq_k=4096, gq_v=4224)


def _split_w_in(w_in):
    sizes = (BRANCH_WIDTH,) * 8 + (4 * ML_HEADS, BRANCH_WIDTH, GQ_KV_HEADS * GQ_HEAD_DIM, GQ_KV_HEADS * GQ_HEAD_DIM)
    idx = np.cumsum(sizes)[:-1].tolist()
    (s5_u, na_q, na_k, na_v, ml_q, ml_k, ml_v, ml_o, ml_gt, gq_q, gq_k, gq_v) = jnp.split(w_in, idx, axis=-1)
    wa = jnp.pad(ml_gt, ((0, 0), (0, LANES - 4 * ML_HEADS)))
    wb = jnp.concatenate([na_q * (NA_HEAD_DIM ** -0.5 * LOG2E), na_k, na_v, ml_q, ml_k * (ML_HEAD_DIM ** -0.5),
                          ml_v, ml_o, gq_q, gq_k, gq_v], axis=1)
    return s5_u.T.astype(BF16), wa.astype(BF16), wb.astype(BF16)


def kernel(x, c, ctx, c_ctx, w_mod, b_mod, g_norm1, g_norm2, w_in, s5_lam_re, s5_lam_im, s5_log_dt, s5_b_re,
           s5_b_im, s5_c_re, s5_c_im, s5_d, s5_w_glu, s5_b_glu, na_rpb, ml_b_gates, ml_norm, gq_qnorm, gq_knorm,
           w_branch, w_gate, b_gate, w_out, w_router, b_router, moe_w1, moe_w3, moe_w2, g_final):
    b, n_lat, dm = x.shape
    n_ctx = ctx.shape[1]
    depth = w_in.shape[0]
    bn, bc = b * n_lat, b * n_ctx
    x_all = jnp.concatenate([x.reshape(bn, dm), ctx.reshape(bc, dm)], axis=0).astype(F32)
    c_all = jnp.concatenate([c.astype(F32), c_ctx.astype(F32)[None], jnp.zeros((8 - b - 1, dm), F32)], axis=0)
    cs_tab = rope_tables(n_lat)
    wr_t = jnp.pad(w_router.astype(BF16).T, ((0, LANES - N_EXPERTS), (0, 0)))
    out = None
    for l in range(depth):
        last = l == depth - 1
        with_ctx = not last
        modtab = mod_vectors(c_all, w_mod, b_mod, l)[:b + 1].reshape(b + 1, N_MOD, dm)
        w_u_t, wa, wb = _split_w_in(w_in[l])
        pa, pb, gate, h_all = in_projection(x_all, g_norm1[l], modtab, wa, wb, w_gate[l].astype(BF16), b_gate[l],
                                            bn, b)

        tables = s5_tables(s5_lam_re[l], s5_lam_im[l], s5_log_dt[l], s5_b_re[l], s5_b_im[l],
                           s5_c_re[l], s5_c_im[l], s5_d[l])
        ya = s5_mixer(h_all, w_u_t, tables, s5_w_glu[l].astype(BF16), s5_b_glu[l].astype(F32).reshape(1, -1),
                      n_lat, n_ctx, b)

        bias_tab = na_bias_tables(na_rpb[l], n_lat // GRID_W, n_ctx)
        n_rows = bn + bc if with_ctx else bn
        na_cols = (_COL['na_q'], _COL['na_k'], _COL['na_v'])
        yb = (na_mixer(pb, *na_cols, bias_tab, n_lat, n_ctx, b),
              na_ctx_attention(pb, *na_cols, n_lat, n_ctx, b) if with_ctx else None)

        ml_bias = jnp.pad(ml_b_gates[l].astype(F32), (0, LANES - 4 * ML_HEADS)).reshape(1, LANES)
        hf, hb = mlstm_mixer(pb, pa, _COL['ml_q'], _COL['ml_k'], _COL['ml_v'], 0, ml_bias, n_lat, n_ctx, b)

        yd = gq_mixer(pb, _COL['gq_q'], _COL['gq_k'], _COL['gq_v'], cs_tab, gq_qnorm[l], gq_knorm[l],
                      n_lat, n_ctx, b, with_ctx)
        x_mid, h2, s_t = merge_layer(
            ya, yb, yd, hf, hb, pb, _COL['ml_o'], gate, x_all, modtab, ml_norm[l].astype(F32).reshape(1, -1),
            w_branch[l].astype(BF16), w_out[l].astype(BF16), g_norm2[l].astype(F32).reshape(1, -1), wr_t,
            n_rows, bn, b)
        x_next = moe_layer(x_mid, h2, s_t, b_router, moe_w1, moe_w3, moe_w2, l, modtab, g_final, bn, b, last)
        if last:
            out = x_next.reshape(b, n_lat, dm).astype(x.dtype)
        else:
            x_all = x_next
    return out
```

```python
import functools
import math

import numpy as np
import jax
import jax.numpy as jnp
from jax import lax
from jax.experimental import pallas as pl
from jax.experimental.pallas import tpu as pltpu

F32 = jnp.float32
BF16 = jnp.bfloat16
I32 = jnp.int32

GRID_W = 64
N_MOD = 6
BRANCH_WIDTH = 512
N_BRANCHES = 4
S5_GROUP = 16
S5_GROUPS = BRANCH_WIDTH // S5_GROUP
S5_STATE = 64
NA_HEADS = 8
NA_HEAD_DIM = 64
NA_ROWS = 8
NA_COLS = 16
ML_HEADS = 4
ML_HEAD_DIM = 128
ML_CHUNK = 128
GQ_HEADS = 8
GQ_KV_HEADS = 2
GQ_HEAD_DIM = 64
ROPE_THETA = 10000.0
N_EXPERTS = 32
N_EXPERT_GROUPS = 8
EXPERTS_PER_GROUP = 4
TOP_K = 2
D_EXPERT = 512
MOE_BLOCK = 128
EPS = 1e-6
NEG_INIT = -1e30
MASK_NEG = -1e30
LOG2E = 1.4426950408889634

LANES = 128
ROW_TILE = 256
S5_CHUNK = 16
NA_TILE_ROWS = 4
VMEM_LIMIT = 56 * 1024 * 1024

HIGHEST = lax.Precision.HIGHEST


def _cp(sem, vmem=None):
    return pltpu.CompilerParams(dimension_semantics=sem, vmem_limit_bytes=vmem)


def _dot(a, b):
    return jnp.dot(a, b, preferred_element_type=F32)


def _dot_nt(a, b):
    return lax.dot_general(a, b, (((1,), (1,)), ((), ())), preferred_element_type=F32)


def _dot_tn(a, b):
    return lax.dot_general(a, b, (((0,), (0,)), ((), ())), preferred_element_type=F32)


def _rms_mod(x, g, shift, scale):
    ms = jnp.mean(x * x, axis=-1, keepdims=True)
    y = x * lax.rsqrt(ms + EPS) * g
    return y * (1.0 + scale) + shift


def _round_up(n, m):
    return -(-n // m) * m


def _mod_body(c_ref, w_ref, b_ref, o_ref):
    c = c_ref[...]
    a = (c * jax.nn.sigmoid(c)).astype(BF16)
    o_ref[...] = _dot(a, w_ref[...].astype(BF16)) + b_ref[...]


def mod_vectors(c_all, w_mod, b_mod, layer):
    d = c_all.shape[1]
    depth = w_mod.shape[0]
    return pl.pallas_call(
        _mod_body,
        grid=(N_MOD,),
        in_specs=[pl.BlockSpec((8, d), lambda j: (0, 0)),
                  pl.BlockSpec((None, d, d), lambda j: (layer, 0, j)),
                  pl.BlockSpec((None, 1, d), lambda j: (layer, 0, j))],
        out_specs=pl.BlockSpec((8, d), lambda j: (0, j)),
        out_shape=jax.ShapeDtypeStruct((8, N_MOD * d), F32),
        compiler_params=_cp(("arbitrary",)),
        name="mod_vectors",
    )(c_all, w_mod, b_mod.reshape(depth, 1, -1))


def _inproj_body(x_ref, g_ref, mod_ref, wa_ref, wb_ref, wg_ref, bg_ref, oa_ref, ob_ref, og_ref, oh_ref):
    mod = mod_ref[0]
    h = _rms_mod(x_ref[...], g_ref[...], mod[0:1], mod[1:2]).astype(BF16)
    oh_ref[...] = h
    oa_ref[...] = _dot(h, wa_ref[...])
    ob_ref[...] = _dot(h, wb_ref[...]).astype(BF16)
    og_ref[...] = jax.nn.sigmoid(_dot(h, wg_ref[...]) + bg_ref[...]).astype(BF16)


def in_projection(x_all, g, modtab, wa, wb, wg, bg, n_lat_rows, n_batch):
    r, d = x_all.shape
    tm = ROW_TILE
    per_b = n_lat_rows // n_batch // tm

    def mod_idx(i):
        return (jnp.minimum(i // per_b, n_batch), 0, 0)

    const = lambda i: (0, 0)
    return pl.pallas_call(
        _inproj_body,
        grid=(r // tm,),
        in_specs=[pl.BlockSpec((tm, d), lambda i: (i, 0)),
                  pl.BlockSpec((1, d), const),
                  pl.BlockSpec((1, N_MOD, d), mod_idx),
                  pl.BlockSpec(wa.shape, const, pipeline_mode=pl.Buffered(1)),
                  pl.BlockSpec(wb.shape, const, pipeline_mode=pl.Buffered(1)),
                  pl.BlockSpec(wg.shape, const, pipeline_mode=pl.Buffered(1)),
                  pl.BlockSpec((1, wg.shape[1]), const)],
        out_specs=[pl.BlockSpec((tm, wa.shape[1]), lambda i: (i, 0)),
                   pl.BlockSpec((tm, wb.shape[1]), lambda i: (i, 0)),
                   pl.BlockSpec((tm, wg.shape[1]), lambda i: (i, 0)),
                   pl.BlockSpec((tm, d), lambda i: (i, 0))],
        out_shape=[jax.ShapeDtypeStruct((r, wa.shape[1]), F32),
                   jax.ShapeDtypeStruct((r, wb.shape[1]), BF16),
                   jax.ShapeDtypeStruct((r, wg.shape[1]), BF16),
                   jax.ShapeDtypeStruct((r, d), BF16)],
        compiler_params=_cp(("arbitrary",), VMEM_LIMIT),
        name="in_projection",
    )(x_all, g.reshape(1, d), modtab, wa, wb, wg, bg.reshape(1, -1))


def s5_tables(lam_re, lam_im, log_dt, b_re, b_im, c_re, c_im, d_skip):
    ell, g_n, p_n, c_n = S5_CHUNK, S5_GROUPS, S5_STATE, S5_GROUP
    lam_re, lam_im = lam_re.astype(F32), lam_im.astype(F32)
    b_re, b_im, c_re, c_im = (t.astype(F32) for t in (b_re, b_im, c_re, c_im))
    dt = jnp.exp(log_dt.astype(F32))[..., None]
    mag = jnp.exp(lam_re * dt)
    a_re = mag * jnp.cos(lam_im * dt)
    a_im = mag * jnp.sin(lam_im * dt)
    den = lam_re * lam_re + lam_im * lam_im
    nr = a_re - 1.0
    f_re = (nr * lam_re + a_im * lam_im) / den
    f_im = (a_im * lam_re - nr * lam_im) / den
    bb_re = f_re[..., None] * b_re - f_im[..., None] * b_im
    bb_im = f_re[..., None] * b_im + f_im[..., None] * b_re
    k = jnp.arange(ell + 1, dtype=F32)
    pmag = jnp.exp((lam_re * dt)[..., None] * k)
    ang = (lam_im * dt)[..., None] * k
    pr, pi = pmag * jnp.cos(ang), pmag * jnp.sin(ang)
    ab_re = pr[..., None] * bb_re[:, :, :, None, :] - pi[..., None] * bb_im[:, :, :, None, :]
    ab_im = pr[..., None] * bb_im[:, :, :, None, :] + pi[..., None] * bb_re[:, :, :, None, :]
    flat = lambda t: t.reshape(2 * g_n, p_n, (ell + 1) * c_n)
    kk = (jnp.einsum('bcp,bpn->bcn', c_re.reshape(2 * g_n, c_n, p_n), flat(ab_re), precision=HIGHEST)
          - jnp.einsum('bcp,bpn->bcn', c_im.reshape(2 * g_n, c_n, p_n), flat(ab_im), precision=HIGHEST))
    kk = kk.reshape(2, g_n, c_n, ell + 1, c_n)
    centre = kk[0][:, :, 0] + kk[1][:, :, 0] + d_skip.astype(F32).reshape(g_n, c_n, 1) * jnp.eye(c_n, dtype=F32)
    w = jnp.concatenate([kk[0][:, :, ell - 1:0:-1], centre[:, :, None], kk[1][:, :, 1:ell]], axis=2)
    wf = w.reshape(g_n, c_n, (2 * ell - 1) * c_n)
    toe = jnp.stack([wf[:, :, (ell - 1 - t) * c_n:(2 * ell - 1 - t) * c_n] for t in range(ell)], axis=1)
    tsum_t = toe.astype(BF16).reshape(g_n, ell * c_n, ell * c_n)

    parity = [jnp.asarray(np.arange(g_n) % 2 == q, F32) for q in range(2)]
    parts = []
    for d in range(2):
        for ab in (ab_re, ab_im):
            sel = ab[d][:, :, :ell]
            if d == 0:
                sel = sel[:, :, ::-1]
            sel = sel.reshape(g_n, p_n, ell * c_n)
            parts += [sel * parity[q][:, None, None] for q in range(2)]
    mend_t = jnp.stack(parts, axis=1).reshape(g_n, 8 * p_n, ell * c_n)

    rows = []
    for d in range(2):
        prk, pik = pr[d][:, :, 1:ell + 1], pi[d][:, :, 1:ell + 1]
        if d == 1:
            prk, pik = prk[:, :, ::-1], pik[:, :, ::-1]
        prk = prk.transpose(0, 2, 1)[:, :, None, :]
        pik = pik.transpose(0, 2, 1)[:, :, None, :]
        cr, ci = c_re[d][:, None], c_im[d][:, None]
        for part in (cr * prk - ci * pik, -cr * pik - ci * prk):
            rows += [part * parity[q][:, None, None, None] for q in range(2)]
    wst_t = jnp.concatenate(rows, axis=-1).reshape(g_n, ell * c_n, 8 * p_n)

    al = jnp.stack([pr[0][:, :, ell], pi[0][:, :, ell], pr[1][:, :, ell], pi[1][:, :, ell]], axis=1)
    a_chunk = al.reshape(g_n // 2, 2, 4, p_n).transpose(0, 2, 1, 3).reshape(g_n // 2, 8 * p_n)
    return tsum_t.astype(BF16), mend_t.astype(BF16), wst_t.astype(BF16), a_chunk


def _s5_proj_body(hl_ref, hc_ref, w_ref, o_ref):
    n_lat = hl_ref.shape[0]
    n_pad = o_ref.shape[1] - LANES
    hl = hl_ref[...]
    if n_pad > n_lat:
        hl = jnp.concatenate([hl, jnp.zeros((n_pad - n_lat, hl.shape[1]), BF16)], axis=0)
    o_ref[:, 0:n_pad] = _dot_nt(w_ref[...], hl).astype(BF16)
    hc = hc_ref[...]
    hc = jnp.concatenate([hc, jnp.zeros((LANES - hc.shape[0], hc.shape[1]), BF16)], axis=0)
    o_ref[:, n_pad:] = _dot_nt(w_ref[...], hc).astype(BF16)


def _s5_end_body(u_ref, m_ref, o_ref):
    width = m_ref.shape[2]
    acc = None
    for q in range(2):
        u = u_ref[:, q].reshape(width, u_ref.shape[3])
        term = _dot(m_ref[q], u)
        acc = term if acc is None else acc + term
    o_ref[...] = acc.T


def _s5_scan_body(n_batch, per_b, n_ctx_chunks, lat_pad, e_ref, a_ref, o_ref):
    n_pairs = e_ref.shape[0]
    o_ref[...] = jnp.zeros(o_ref.shape, F32)
    sub = 8
    ctx_groups = n_ctx_chunks // sub
    n_groups = (per_b + n_ctx_chunks) // sub
    coef = [[a_ref[q, :, j * LANES:(j + 1) * LANES] for j in range(4)] for q in range(n_pairs)]

    def group(g, carry):
        new = list(carry)
        for b in range(n_batch):
            ctx = g < ctx_groups
            up = jnp.where(ctx, lat_pad + b * n_ctx_chunks + g * sub, b * per_b + (g - ctx_groups) * sub)
            down = jnp.where(ctx, lat_pad + (b + 1) * n_ctx_chunks - (g + 1) * sub,
                             (b + 1) * per_b - (g - ctx_groups + 1) * sub)
            for q in range(n_pairs):
                for d, base in enumerate((up, down)):
                    base = pl.multiple_of(base, sub)
                    lanes = slice(2 * d * LANES, (2 * d + 2) * LANES)
                    e = e_ref[q, pl.ds(base, sub), lanes]
                    k = ((b * n_pairs + q) * 2 + d) * 2
                    sr, si = new[k], new[k + 1]
                    ar, ai = coef[q][2 * d], coef[q][2 * d + 1]
                    rows = [None] * sub
                    for step in range(sub):
                        r = step if d == 0 else sub - 1 - step
                        rows[r] = jnp.concatenate([sr, si], axis=1)
                        er, ei = e[r:r + 1, :LANES], e[r:r + 1, LANES:]
                        sr, si = ar * sr - ai * si + er, ar * si + ai * sr + ei
                    o_ref[q, pl.ds(base, sub), lanes] = jnp.concatenate(rows, axis=0)
                    new[k], new[k + 1] = sr, si
        return tuple(new)

    z = jnp.zeros((1, LANES), F32)
    lax.fori_loop(0, n_groups, group, tuple(z for _ in range(n_batch * n_pairs * 4)))


def _s5_out_body(u_ref, t_ref, s_ref, w_ref, o_ref):
    width = t_ref.shape[0]
    u = u_ref[...].reshape(width, u_ref.shape[2])
    y = _dot(t_ref[...], u) + _dot_nt(w_ref[...], s_ref[...].astype(BF16))
    o_ref[...] = y.reshape(o_ref.shape)


def _s5_glu_body(n_lat_chunks, n_ctx_chunks, lat_pad, y_ref, w_ref, b_ref, o_ref):
    y = y_ref[...].T
    if lat_pad == n_lat_chunks:
        y = y[:n_lat_chunks + n_ctx_chunks]
    else:
        y = jnp.concatenate([y[:n_lat_chunks], y[lat_pad:lat_pad + n_ctx_chunks]], axis=0)
    z = jax.nn.gelu(y)
    o_ref[...] = (z * jax.nn.sigmoid(_dot(z.astype(BF16), w_ref[...]) + b_ref[...])).astype(o_ref.dtype)


def s5_mixer(h_all, w_u_t, tables, w_glu, b_glu, n_lat, n_ctx, n_batch):
    tsum_t, mend_t, wst_t, a_chunk = tables
    r, d = h_all.shape
    h2 = h_all.reshape(r // S5_CHUNK, S5_CHUNK * d)
    ell, g_n, c_n = S5_CHUNK, S5_GROUPS, S5_GROUP
    width = ell * c_n
    sw = 8 * S5_STATE
    bw = g_n * c_n
    n_lat_chunks = n_batch * n_lat // ell
    n_ctx_chunks = n_batch * n_ctx // ell
    assert n_lat_chunks % n_ctx_chunks == 0 and n_ctx_chunks % 16 == 0 and n_ctx_chunks <= LANES
    lat_pad = _round_up(n_lat_chunks, LANES)
    nch = lat_pad + LANES
    const = lambda t: (0, 0)
    u_t = pl.pallas_call(
        _s5_proj_body,
        grid=(ell,),
        in_specs=[pl.BlockSpec((n_lat_chunks, d), lambda t: (0, t)),
                  pl.BlockSpec((n_ctx_chunks, d), lambda t: (n_lat_chunks // n_ctx_chunks, t)),
                  pl.BlockSpec((bw, d), const)],
        out_specs=pl.BlockSpec((None, bw, nch), lambda t: (t, 0, 0)),
        out_shape=jax.ShapeDtypeStruct((ell, bw, nch), BF16),
        compiler_params=_cp(("arbitrary",)),
        name="s5_projection",
    )(h2, h2, w_u_t)
    u4 = u_t.reshape(ell, g_n, c_n, nch)
    ends = pl.pallas_call(
        _s5_end_body,
        grid=(g_n // 2,),
        in_specs=[pl.BlockSpec((ell, 2, c_n, nch), lambda p: (0, p, 0, 0)),
                  pl.BlockSpec((2, sw, width), lambda p: (p, 0, 0))],
        out_specs=pl.BlockSpec((None, nch, sw), lambda p: (p, 0, 0)),
        out_shape=jax.ShapeDtypeStruct((g_n // 2, nch, sw), F32),
        compiler_params=_cp(("arbitrary",)),
        name="s5_chunk_ends",
    )(u4, mend_t)
    pairs_per_step = 4
    assert (n_ctx // ell) % 8 == 0 and (n_lat // ell) % 8 == 0
    states = pl.pallas_call(
        functools.partial(_s5_scan_body, n_batch, n_lat // ell, n_ctx // ell, lat_pad),
        grid=(g_n // 2 // pairs_per_step,),
        in_specs=[pl.BlockSpec((pairs_per_step, nch, sw), lambda j: (j, 0, 0)),
                  pl.BlockSpec((pairs_per_step, 1, sw), lambda j: (j, 0, 0))],
        out_specs=pl.BlockSpec((pairs_per_step, nch, sw), lambda j: (j, 0, 0)),
        out_shape=jax.ShapeDtypeStruct((g_n // 2, nch, sw), F32),
        compiler_params=_cp(("arbitrary",), VMEM_LIMIT),
        name="s5_state_scan",
    )(ends, a_chunk.reshape(g_n // 2, 1, sw))
    y_t = pl.pallas_call(
        _s5_out_body,
        grid=(g_n,),
        in_specs=[pl.BlockSpec((ell, None, c_n, nch), lambda gi: (0, gi, 0, 0)),
                  pl.BlockSpec((None, width, width), lambda gi: (gi, 0, 0)),
                  pl.BlockSpec((None, nch, sw), lambda gi: (gi // 2, 0, 0)),
                  pl.BlockSpec((None, width, sw), lambda gi: (gi, 0, 0))],
        out_specs=pl.BlockSpec((ell, None, c_n, nch), lambda gi: (0, gi, 0, 0)),
        out_shape=jax.ShapeDtypeStruct((ell, g_n, c_n, nch), F32),
        compiler_params=_cp(("arbitrary",)),
        name="s5_outputs",
    )(u4, tsum_t, states, wst_t)
    n_chunks = n_lat_chunks + n_ctx_chunks
    ya = pl.pallas_call(
        functools.partial(_s5_glu_body, n_lat_chunks, n_ctx_chunks, lat_pad),
        grid=(ell,),
        in_specs=[pl.BlockSpec((None, bw, nch), lambda t: (t, 0, 0)),
                  pl.BlockSpec((bw, bw), const),
                  pl.BlockSpec((1, bw), const)],
        out_specs=pl.BlockSpec((n_chunks, bw), lambda t: (0, t)),
        out_shape=jax.ShapeDtypeStruct((n_chunks, ell * bw), BF16),
        compiler_params=_cp(("arbitrary",)),
        name="s5_glu",
    )(y_t.reshape(ell, bw, nch), w_glu, b_glu)
    return ya.reshape(r, bw)


def rope_tables(n_lat):
    half = GQ_HEAD_DIM // 2
    quarter = half // 2
    t = np.arange(n_lat)
    freqs = ROPE_THETA ** (-np.arange(quarter, dtype=np.float64) / quarter)
    ang_r = (t // GRID_W)[:, None] * freqs
    ang_c = (t % GRID_W)[:, None] * freqs
    ang = np.concatenate([ang_r, ang_r, ang_c, ang_c], axis=1)
    sign = np.concatenate([-np.ones(quarter), np.ones(quarter)] * 2)
    cos = np.concatenate([np.cos(ang), np.ones((ROW_TILE, GQ_HEAD_DIM))], axis=0)
    sin = np.concatenate([np.sin(ang) * sign, np.zeros((ROW_TILE, GQ_HEAD_DIM))], axis=0)
    tab = np.concatenate([cos, cos, sin, sin], axis=1)
    return jnp.asarray(tab, F32)


def _group_ones(width, group):
    i = np.arange(width)
    return jnp.asarray((i[:, None] // group) == (i[None, :] // group), BF16)


def _group_mean_sq(x, ones_blk, group):
    sq = x * x
    hi = sq.astype(BF16)
    lo = (sq - hi.astype(F32)).astype(BF16)
    return (_dot(hi, ones_blk) + _dot(lo, ones_blk)) * (1.0 / group)


def _rope(x, cos, sin):
    w = x.shape[-1]
    q = GQ_HEAD_DIM // 4
    lane = lax.broadcasted_iota(I32, x.shape, 1)
    first = (lane % (2 * q)) < q
    partner = jnp.where(first, pltpu.roll(x, w - q, 1), pltpu.roll(x, q, 1))
    return x * cos + partner * sin


def _gq_prep_body(q_ref, k_ref, v_ref, cs_ref, gq_ref, gk_ref, oq_ref, ok_ref, qm_ref, kr_ref, va_ref):
    cs = cs_ref[...]
    cos1, sin1 = cs[:, :LANES], cs[:, LANES:]
    q = q_ref[...].astype(F32)
    qn = q * lax.rsqrt(_group_mean_sq(q, oq_ref[...], GQ_HEAD_DIM) + EPS) * gq_ref[...]
    n_pairs = q.shape[1] // LANES
    qr = _rope(qn, jnp.concatenate([cos1] * n_pairs, axis=1), jnp.concatenate([sin1] * n_pairs, axis=1))
    qr = (qr * (GQ_HEAD_DIM ** -0.5 * LOG2E)).astype(BF16)
    lane = lax.broadcasted_iota(I32, (q.shape[0], LANES), 1)
    heads_per_kv = GQ_HEADS // GQ_KV_HEADS
    for h in range(GQ_HEADS):
        pair = qr[:, (h // 2) * LANES:(h // 2 + 1) * LANES]
        kv = h // heads_per_kv
        if h % 2 != kv:
            pair = pltpu.roll(pair, GQ_HEAD_DIM, 1)
        keep = (lane >= kv * GQ_HEAD_DIM) & (lane < (kv + 1) * GQ_HEAD_DIM)
        qm_ref[h] = jnp.where(keep, pair, jnp.zeros_like(pair))
    k = k_ref[...].astype(F32)
    kn = k * lax.rsqrt(_group_mean_sq(k, ok_ref[...], GQ_HEAD_DIM) + EPS) * gk_ref[...]
    kr_ref[...] = _rope(kn, cos1, sin1).astype(BF16)
    va_ref[...] = jnp.concatenate([v_ref[...], jnp.ones(v_ref.shape, BF16)], axis=1)


def gq_prepare(pb, col_q, col_k, col_v, cs_tab, g_q, g_k, n_lat, n_ctx, n_batch):
    r = pb.shape[0]
    tm = ROW_TILE
    assert n_ctx == tm and n_lat % tm == 0
    nb = n_lat // tm
    n_lat_tiles = n_batch * nb

    def tab_idx(i):
        return (jnp.where(i < n_lat_tiles, i % nb, nb), 0)

    def kv_idx(i):
        lat = (i // nb) * (nb + 1) + i % nb
        ctx = (i - n_lat_tiles) * (nb + 1) + nb
        return (jnp.where(i < n_lat_tiles, lat, ctx), 0)

    qw = GQ_HEADS * GQ_HEAD_DIM
    const = lambda i: (0, 0)
    gq = jnp.tile(g_q.astype(F32), GQ_HEADS).reshape(1, qw)
    gk = jnp.tile(g_k.astype(F32), GQ_KV_HEADS).reshape(1, LANES)
    n_keys = n_batch * (n_lat + n_ctx)
    return pl.pallas_call(
        _gq_prep_body,
        grid=(r // tm,),
        in_specs=[pl.BlockSpec((tm, qw), lambda i: (i, col_q // qw)),
                  pl.BlockSpec((tm, LANES), lambda i: (i, col_k // LANES)),
                  pl.BlockSpec((tm, LANES), lambda i: (i, col_v // LANES)),
                  pl.BlockSpec((tm, 2 * LANES), tab_idx),
                  pl.BlockSpec((1, qw), const),
                  pl.BlockSpec((1, LANES), const),
                  pl.BlockSpec((qw, qw), const),
                  pl.BlockSpec((LANES, LANES), const)],
        out_specs=[pl.BlockSpec((GQ_HEADS, tm, LANES), lambda i: (0, i, 0)),
                   pl.BlockSpec((tm, LANES), kv_idx),
                   pl.BlockSpec((tm, 2 * LANES), kv_idx)],
        out_shape=[jax.ShapeDtypeStruct((GQ_HEADS, r, LANES), BF16),
                   jax.ShapeDtypeStruct((n_keys, LANES), BF16),
                   jax.ShapeDtypeStruct((n_keys, 2 * LANES), BF16)],
        compiler_params=_cp(("arbitrary",)),
        name="gq_prepare",
    )(pb, pb, pb, cs_tab, gq, gk, _group_ones(qw, GQ_HEAD_DIM), _group_ones(LANES, GQ_HEAD_DIM))


def _gq_flash_body(q_ref, k_ref, v_ref, o_ref, m_sc, acc_sc):
    kj = pl.program_id(2)
    n_h, tq, _ = q_ref.shape

    @pl.when(kj == 0)
    def _():
        m_sc[...] = jnp.full(m_sc.shape, -jnp.inf, F32)
        acc_sc[...] = jnp.zeros(acc_sc.shape, F32)

    k = k_ref[...]
    v = v_ref[...]
    hp = 1
    for c in range(n_h // hp):
        rows = slice(c * hp * tq, (c + 1) * hp * tq)
        s = _dot_nt(q_ref[c * hp:(c + 1) * hp].reshape(hp * tq, LANES), k)
        m_prev = m_sc[rows, :]
        m_new = jnp.maximum(m_prev, jnp.max(s, axis=-1, keepdims=True))
        p = jnp.exp2(s - m_new)
        acc_sc[rows, :] = jnp.exp2(m_prev - m_new) * acc_sc[rows, :] + _dot(p.astype(BF16), v)
        m_sc[rows, :] = m_new

    @pl.when(kj == pl.num_programs(2) - 1)
    def _():
        lane = lax.broadcasted_iota(I32, (tq, LANES), 1)
        heads_per_kv = n_h // GQ_KV_HEADS
        for j in range(n_h // 2):
            kv = (2 * j) // heads_per_kv
            halves = []
            for h in (2 * j, 2 * j + 1):
                a = acc_sc[h * tq:(h + 1) * tq, :]
                halves.append(a[:, :LANES] / a[:, LANES:LANES + 1])
            lo, hi = halves
            if kv == 0:
                hi = pltpu.roll(hi, GQ_HEAD_DIM, 1)
            else:
                lo = pltpu.roll(lo, GQ_HEAD_DIM, 1)
            o_ref[:, j * LANES:(j + 1) * LANES] = jnp.where(lane < GQ_HEAD_DIM, lo, hi).astype(o_ref.dtype)


def gq_attention(qm, keys, vals, n_rows_out, tq, tk, q_blk, k_blk, o_blk, n_q, n_k, n_batch):
    n_h = qm.shape[0]
    return pl.pallas_call(
        _gq_flash_body,
        grid=(n_batch, n_q, n_k),
        in_specs=[pl.BlockSpec((n_h, tq, LANES), lambda b, i, j: (0, q_blk(b, i), 0)),
                  pl.BlockSpec((tk, LANES), lambda b, i, j: (k_blk(b, j), 0)),
                  pl.BlockSpec((tk, 2 * LANES), lambda b, i, j: (k_blk(b, j), 0))],
        out_specs=pl.BlockSpec((tq, n_h * GQ_HEAD_DIM), lambda b, i, j: (o_blk(b, i), 0)),
        out_shape=jax.ShapeDtypeStruct((n_rows_out, n_h * GQ_HEAD_DIM), BF16),
        scratch_shapes=[pltpu.VMEM((n_h * tq, 1), F32), pltpu.VMEM((n_h * tq, 2 * LANES), F32)],
        compiler_params=_cp(("arbitrary", "arbitrary", "arbitrary"), VMEM_LIMIT),
        name="gq_attention",
    )(qm, keys, vals)


def _largest_divisor(n, cap):
    return max(d for d in range(1, cap + 1) if n % d == 0)


def gq_mixer(pb, col_q, col_k, col_v, cs_tab, g_q, g_k, n_lat, n_ctx, n_batch, with_ctx):
    qm, keys, vals = gq_prepare(pb, col_q, col_k, col_v, cs_tab, g_q, g_k, n_lat, n_ctx, n_batch)
    tq = 512
    tk = LANES * _largest_divisor((n_lat + n_ctx) // LANES, 22)
    n_q = n_lat // tq
    per_b = (n_lat + n_ctx) // tk
    lat_blk = lambda b, i: b * n_q + i
    y_lat = gq_attention(qm, keys, vals, n_batch * n_lat, tq, tk, lat_blk, lambda b, j: b * per_b + j, lat_blk,
                         n_q, per_b, n_batch)
    if not with_ctx:
        return y_lat, None
    tc = n_ctx
    y_ctx = gq_attention(qm, keys, vals, n_batch * n_ctx, tc, tc, lambda b, i: n_batch * n_lat // tc + b,
                         lambda b, j: b * ((n_lat + n_ctx) // tc) + n_lat // tc, lambda b, i: b, 1, 1, n_batch)
    return y_lat, y_ctx


def na_bias_tables(rpb, n_img_rows, n_ctx):
    tr = NA_TILE_ROWS
    nt = n_img_rows // tr
    assert nt >= 4
    kr = min(NA_ROWS, n_img_rows)
    n_heads = rpb.shape[0]
    qcol = np.arange(GRID_W)[:, None]
    kcol = np.arange(GRID_W)[None, :]
    dc = np.clip(kcol - qcol + NA_COLS - 1, 0, 2 * NA_COLS - 2)
    oh_c = (dc[None] == np.arange(2 * NA_COLS - 1)[:, None, None]).astype(np.float32)
    cstart = np.clip(qcol - NA_COLS // 2, 0, GRID_W - NA_COLS)
    col_ok = (kcol >= cstart) & (kcol < cstart + NA_COLS)
    by_col = jnp.einsum('hrd,dqk->hrqk', rpb.astype(F32), jnp.asarray(oh_c), precision=HIGHEST)
    classes = []
    for i in (0, 1, nt - 1):
        wb = int(np.clip(i - 1, 0, nt - 3))
        qrow = (i * tr + np.arange(tr))[:, None]
        krow = (wb * tr + np.arange(3 * tr))[None, :]
        rs = np.clip(qrow - kr // 2, 0, n_img_rows - kr)
        row_ok = (krow >= rs) & (krow < rs + kr)
        dr = np.clip(krow - qrow + NA_ROWS - 1, 0, 2 * NA_ROWS - 2)
        oh_r = ((dr[None] == np.arange(2 * NA_ROWS - 1)[:, None, None]) & row_ok[None]).astype(np.float32)
        bias = jnp.einsum('hrqk,rab->haqbk', by_col, jnp.asarray(oh_r), precision=HIGHEST)
        ok = row_ok[:, None, :, None] & col_ok[None, :, None, :]
        bias = jnp.where(jnp.asarray(ok)[None], bias * LOG2E, MASK_NEG)
        bias = bias.reshape(n_heads, tr * GRID_W, 3 * tr * GRID_W)
        classes.append(jnp.concatenate([bias, jnp.zeros((n_heads, tr * GRID_W, n_ctx), F32)], axis=-1))
    return jnp.stack(classes, axis=0)


def _pair_attention(q_pair, k_pair, v_pair, bias_fn):
    lane = lax.broadcasted_iota(I32, q_pair.shape, 1)
    v_aug = jnp.concatenate([v_pair, jnp.ones(v_pair.shape, BF16)], axis=1)
    out = None
    for hh in range(2):
        mine = (lane >= hh * NA_HEAD_DIM) & (lane < (hh + 1) * NA_HEAD_DIM)
        qm = jnp.where(mine, q_pair, jnp.zeros_like(q_pair))
        s = _dot_nt(qm, k_pair)
        b = bias_fn(hh)
        if b is not None:
            s = s + b
        m = jnp.max(s, axis=-1, keepdims=True)
        o_aug = _dot(jnp.exp2(s - m).astype(BF16), v_aug)
        o = o_aug[:, :LANES] / o_aug[:, LANES:LANES + 1]
        out = o if out is None else jnp.where(mine, o, out)
    return out


def _na_body(q_ref, k0_ref, k1_ref, k2_ref, kc_ref, v0_ref, v1_ref, v2_ref, vc_ref, b_ref, o_ref):
    for j in range(NA_HEADS // 2):
        sl = slice(j * LANES, (j + 1) * LANES)
        q_pair = q_ref[:, sl]
        k_pair = jnp.concatenate([k0_ref[:, sl], k1_ref[:, sl], k2_ref[:, sl], kc_ref[:, sl]], axis=0)
        v_pair = jnp.concatenate([v0_ref[:, sl], v1_ref[:, sl], v2_ref[:, sl], vc_ref[:, sl]], axis=0)
        o = _pair_attention(q_pair, k_pair, v_pair, lambda hh: b_ref[0, 2 * j + hh])
        o_ref[:, sl] = o.astype(o_ref.dtype)


def na_mixer(pb, col_q, col_k, col_v, bias_tab, n_lat, n_ctx, n_batch):
    tm = NA_TILE_ROWS * GRID_W
    assert n_ctx == tm
    w = NA_HEADS * NA_HEAD_DIM
    nt = n_lat // tm
    n_keys = 3 * tm + n_ctx
    cq, ck, cv = col_q // w, col_k // w, col_v // w
    ctx0 = n_batch * nt

    def win(o):
        return lambda b, i: (b * nt + jnp.clip(i - 1, 0, nt - 3) + o)

    def cls(b, i):
        return (jnp.where(i == 0, 0, jnp.where(i == nt - 1, 2, 1)), 0, 0, 0)

    kspecs = [pl.BlockSpec((tm, w), (lambda b, i, f=win(o): (f(b, i), ck))) for o in range(3)]
    vspecs = [pl.BlockSpec((tm, w), (lambda b, i, f=win(o): (f(b, i), cv))) for o in range(3)]
    return pl.pallas_call(
        _na_body,
        grid=(n_batch, nt),
        in_specs=[pl.BlockSpec((tm, w), lambda b, i: (b * nt + i, cq))] + kspecs
        + [pl.BlockSpec((tm, w), lambda b, i: (ctx0 + b, ck))] + vspecs
        + [pl.BlockSpec((tm, w), lambda b, i: (ctx0 + b, cv)),
           pl.BlockSpec((1, NA_HEADS, tm, n_keys), cls)],
        out_specs=pl.BlockSpec((tm, w), lambda b, i: (b * nt + i, 0)),
        out_shape=jax.ShapeDtypeStruct((n_batch * n_lat, w), BF16),
        compiler_params=_cp(("arbitrary", "arbitrary"), VMEM_LIMIT),
        name="na_attention",
    )(pb, pb, pb, pb, pb, pb, pb, pb, pb, bias_tab)


def _ctx_mha_body(q_ref, k_ref, v_ref, o_ref):
    for j in range(NA_HEADS // 2):
        sl = slice(j * LANES, (j + 1) * LANES)
        o = _pair_attention(q_ref[:, sl], k_ref[:, sl], v_ref[:, sl], lambda hh: None)
        o_ref[:, sl] = o.astype(o_ref.dtype)


def na_ctx_attention(pb, col_q, col_k, col_v, n_lat, n_ctx, n_batch):
    w = NA_HEADS * NA_HEAD_DIM
    ctx0 = n_batch * n_lat // n_ctx
    spec = lambda c: pl.BlockSpec((n_ctx, w), lambda b: (ctx0 + b, c // w))
    return pl.pallas_call(
        _ctx_mha_body,
        grid=(n_batch,),
        in_specs=[spec(col_q), spec(col_k), spec(col_v)],
        out_specs=pl.BlockSpec((n_ctx, w), lambda b: (b, 0)),
        out_shape=jax.ShapeDtypeStruct((n_batch * n_ctx, w), BF16),
        compiler_params=_cp(("arbitrary",)),
        name="na_ctx_attention",
    )(pb, pb, pb)


def _mlstm_body(n_batch, *refs):
    n_in = 8 * n_batch
    ins, bias_ref = refs[:n_in], refs[n_in]
    hf_ref, hb_ref, c_ref, n_ref, m_ref = refs[n_in + 1:]

    @pl.when(pl.program_id(0) == 0)
    def _():
        c_ref[...] = jnp.zeros(c_ref.shape, F32)
        n_ref[...] = jnp.zeros(n_ref.shape, F32)
        m_ref[...] = jnp.full(m_ref.shape, NEG_INIT, F32)

    tok = lax.broadcasted_iota(I32, (ML_CHUNK, ML_CHUNK), 0)
    src = lax.broadcasted_iota(I32, (ML_CHUNK, ML_CHUNK), 1)
    masks = (src <= tok, src >= tok)
    gates = []
    for b in range(n_batch):
        for d in range(2):
            g = ins[(b * 2 + d) * 4 + 3][...] + bias_ref[...]
            lf_cum = jnp.dot(masks[d].astype(F32), jax.nn.log_sigmoid(g), precision=HIGHEST,
                             preferred_element_type=F32)
            gates.append((g, lf_cum))
    gates = [(g, lf_cum, g.T, lf_cum.T) for g, lf_cum in gates]
    chains = []
    for b in range(n_batch):
        for d in range(2):
            g, lf_cum, g_t, lf_cum_t = gates[b * 2 + d]
            for h in range(ML_HEADS):
                ci, cf = d * 2 * ML_HEADS + h, d * 2 * ML_HEADS + ML_HEADS + h
                idx = (b * 2 + d) * ML_HEADS + h
                chains.append(dict(
                    b=b, d=d, h=h, idx=idx, bt_col=lf_cum[:, cf:cf + 1], bt_row=lf_cum_t[cf:cf + 1, :],
                    li_col=g[:, ci:ci + 1], li_row=g_t[ci:ci + 1, :], m_prev=m_ref[idx][:, 0:1],
                    c_prev=c_ref[idx], n_prev=n_ref[idx]))
    for c in chains:
        c['dmat'] = jnp.where(masks[c['d']], c['bt_col'] - c['bt_row'] + c['li_row'], -jnp.inf)
        c['inter'] = c['bt_col'] + c['m_prev']
    for c in chains:
        c['mt'] = jnp.maximum(c['inter'], jnp.max(c['dmat'], axis=-1, keepdims=True))
    for c in chains:
        refs_c = ins[(c['b'] * 2 + c['d']) * 4:(c['b'] * 2 + c['d']) * 4 + 3]
        sl = slice(c['h'] * ML_HEAD_DIM, (c['h'] + 1) * ML_HEAD_DIM)
        c['q'], c['k'], c['v'] = (r[:, sl] for r in refs_c)
        c['s'] = _dot_nt(c['q'], c['k']) * jnp.exp(c['dmat'] - c['mt'])
        c['w_inter'] = jnp.exp(c['inter'] - c['mt'])
    for c in chains:
        num = _dot(c['s'].astype(BF16), c['v']) + c['w_inter'] * _dot_nt(c['q'], c['c_prev'].astype(BF16))
        qn = jnp.sum(c['q'].astype(F32) * c['n_prev'], axis=-1, keepdims=True)
        den = jnp.sum(c['s'], axis=-1, keepdims=True) + c['w_inter'] * qn
        h_out = num / jnp.maximum(jnp.abs(den), jnp.exp(-c['mt']))
        h_ref = hb_ref if c['d'] else hf_ref
        h_ref[c['b'], :, c['h'] * ML_HEAD_DIM:(c['h'] + 1) * ML_HEAD_DIM] = h_out
    new_state = []
    for c in chains:
        b_last = c['bt_col'][0:1, :] if c['d'] else c['bt_col'][ML_CHUNK - 1:ML_CHUNK, :]
        g_col = b_last - c['bt_col'] + c['li_col']
        m_new = jnp.maximum(b_last + c['m_prev'], jnp.max(g_col, axis=0, keepdims=True))
        wg = jnp.exp(g_col - m_new)
        decay = jnp.exp(b_last + c['m_prev'] - m_new)
        c_new = decay * c['c_prev'] + _dot_tn((wg * c['v'].astype(F32)).astype(BF16), c['k'])
        n_new = decay * c['n_prev'] + jnp.sum(wg * c['k'].astype(F32), axis=0, keepdims=True)
        new_state.append((c['idx'], c_new, n_new, m_new))
    for idx, c_new, n_new, m_new in new_state:
        c_ref[idx] = c_new
        n_ref[idx] = n_new
        m_ref[idx] = jnp.broadcast_to(m_new, (1, LANES))


def mlstm_mixer(pb, pa, col_q, col_k, col_v, col_g, bias, n_lat, n_ctx, n_batch):
    w = ML_HEADS * ML_HEAD_DIM
    tc = ML_CHUNK
    nl, nc = n_lat // tc, n_ctx // tc

    def fwd_pos(i):
        return jnp.where(i < nc, nl + i, i - nc)

    def bwd_pos(i):
        return jnp.where(i < nc, nl + nc - 1 - i, nl - 1 - (i - nc))

    def row_blk(b, pos):
        return jnp.where(pos < nl, b * nl + pos, n_batch * nl + b * nc + pos - nl)

    def specs(b, pos_fn):
        return [pl.BlockSpec((tc, w), lambda i, c=c: (row_blk(b, pos_fn(i)), c // w)) for c in (col_q, col_k, col_v)] + [
            pl.BlockSpec((tc, LANES), lambda i: (row_blk(b, pos_fn(i)), col_g // LANES))]

    in_specs, operands = [], []
    for b in range(n_batch):
        for pos_fn in (fwd_pos, bwd_pos):
            in_specs += specs(b, pos_fn)
            operands += [pb, pb, pb, pa]
    n_st = 2 * ML_HEADS * n_batch
    out_shape = jax.ShapeDtypeStruct((n_batch, n_lat + n_ctx, w), F32)
    return pl.pallas_call(
        functools.partial(_mlstm_body, n_batch),
        grid=(nl + nc,),
        in_specs=in_specs + [pl.BlockSpec((1, LANES), lambda i: (0, 0))],
        out_specs=[pl.BlockSpec((n_batch, tc, w), lambda i: (0, fwd_pos(i), 0)),
                   pl.BlockSpec((n_batch, tc, w), lambda i: (0, bwd_pos(i), 0))],
        out_shape=[out_shape, out_shape],
        scratch_shapes=[pltpu.VMEM((n_st, ML_HEAD_DIM, ML_HEAD_DIM), F32),
                        pltpu.VMEM((n_st, 1, ML_HEAD_DIM), F32),
                        pltpu.VMEM((n_st, 1, LANES), F32)],
        compiler_params=_cp(("arbitrary",)),
        name="mlstm_chunks",
    )(*operands, bias)


def _merge_body(n_lat_tiles, ya_ref, ybl_ref, ybc_ref, hf_ref, hb_ref, o_ref, ydl_ref, ydc_ref, gate_ref, x_ref,
                mod_ref, mlg_ref, wbr_ref, wout_ref, g2_ref, wr_ref, xo_ref, h2_ref, st_ref):
    d = x_ref.shape[1]
    is_ctx = pl.program_id(0) >= n_lat_tiles
    yb = jnp.where(is_ctx, ybc_ref[...], ybl_ref[...])
    yd = jnp.where(is_ctx, ydc_ref[...], ydl_ref[...])
    hs = hf_ref[...] + hb_ref[...]
    segs = []
    for h in range(ML_HEADS):
        seg = hs[:, h * ML_HEAD_DIM:(h + 1) * ML_HEAD_DIM]
        segs.append(seg * lax.rsqrt(jnp.mean(seg * seg, axis=-1, keepdims=True) + EPS))
    ym = jnp.concatenate(segs, axis=1) * mlg_ref[...] * jax.nn.sigmoid(o_ref[...].astype(F32))
    ys = (ya_ref[...], yb, ym.astype(BF16), yd)
    merged = None
    for i in range(N_BRANCHES):
        term = gate_ref[:, i * d:(i + 1) * d].astype(F32) * _dot(ys[i], wbr_ref[i])
        merged = term if merged is None else merged + term
    y = _dot(merged.astype(BF16), wout_ref[...])
    mod = mod_ref[0]
    x_new = x_ref[...] + mod[2:3] * y
    xo_ref[...] = x_new
    h2 = _rms_mod(x_new, g2_ref[...], mod[3:4], mod[4:5])
    h2_ref[...] = h2
    st_ref[...] = jax.nn.sigmoid(_dot_nt(wr_ref[...], h2.astype(BF16)))


def merge_layer(ya, yb, yd, hf, hb, pb, col_o, gate, x_all, modtab, mlg, wbr, wout, g2, wr_t,
                n_rows, n_lat_rows, n_batch):
    d = x_all.shape[1]
    tm = ROW_TILE
    w = BRANCH_WIDTH
    per_b = n_lat_rows // n_batch // tm
    row = lambda i: (i, 0)
    const2 = lambda i: (0, 0)
    n_lat_tiles = n_lat_rows // tm
    lat_row = lambda i: (jnp.minimum(i, n_lat_tiles - 1), 0)
    ctx_row = lambda i: (jnp.clip(i - n_lat_tiles, 0, n_batch - 1), 0)

    def seq(i):
        lat = i < n_lat_tiles
        return (jnp.where(lat, i // per_b, i - n_lat_tiles), jnp.where(lat, i % per_b, per_b), 0)

    (yb_lat, yb_ctx), (yd_lat, yd_ctx) = yb, yd
    if yb_ctx is None:
        yb_ctx, yd_ctx = yb_lat, yd_lat
    return pl.pallas_call(
        functools.partial(_merge_body, n_lat_tiles),
        grid=(n_rows // tm,),
        in_specs=[pl.BlockSpec((tm, w), row), pl.BlockSpec((tm, w), lat_row), pl.BlockSpec((tm, w), ctx_row),
                  pl.BlockSpec((None, tm, w), seq), pl.BlockSpec((None, tm, w), seq),
                  pl.BlockSpec((tm, w), lambda i: (i, col_o // w)),
                  pl.BlockSpec((tm, w), lat_row), pl.BlockSpec((tm, w), ctx_row),
                  pl.BlockSpec((tm, N_BRANCHES * d), row),
                  pl.BlockSpec((tm, d), row),
                  pl.BlockSpec((1, N_MOD, d), lambda i: (jnp.minimum(i // per_b, n_batch), 0, 0)),
                  pl.BlockSpec((1, w), const2),
                  pl.BlockSpec((N_BRANCHES, w, d), lambda i: (0, 0, 0)),
                  pl.BlockSpec((d, d), const2), pl.BlockSpec((1, d), const2),
                  pl.BlockSpec((LANES, d), const2)],
        out_specs=[pl.BlockSpec((tm, d), row), pl.BlockSpec((tm, d), row),
                   pl.BlockSpec((LANES, tm), lambda i: (0, i))],
        out_shape=[jax.ShapeDtypeStruct((n_rows, d), F32), jax.ShapeDtypeStruct((n_rows, d), F32),
                   jax.ShapeDtypeStruct((LANES, n_rows), F32)],
        compiler_params=_cp(("arbitrary",), VMEM_LIMIT),
        name="merge_layer",
    )(ya, yb_lat, yb_ctx, hf, hb, pb, yd_lat, yd_ctx, gate, x_all, modtab, mlg, wbr, wout, g2, wr_t)


def _router_body(s_ref, b_ref, e_ref, w_ref, rank_ref, cnt_ref, base_sc):
    @pl.when(pl.program_id(0) == 0)
    def _():
        base_sc[...] = jnp.zeros(base_sc.shape, F32)

    tm = s_ref.shape[1]
    s = s_ref[0:N_EXPERTS, :]
    sel = s + b_ref[0:N_EXPERTS, :]
    row = lambda a, e: a[e:e + 1, :]
    best, grp = None, None
    for g in range(N_EXPERT_GROUPS):
        v = [row(sel, EXPERTS_PER_GROUP * g + k) for k in range(EXPERTS_PER_GROUP)]
        gs = None
        for a in range(EXPERTS_PER_GROUP):
            for c in range(a + 1, EXPERTS_PER_GROUP):
                gs = v[a] + v[c] if gs is None else jnp.maximum(gs, v[a] + v[c])
        if best is None:
            best, grp = gs, jnp.zeros((1, tm), I32)
        else:
            better = gs > best
            grp = jnp.where(better, g, grp)
            best = jnp.where(better, gs, best)
    vals, affs = [], []
    for k in range(EXPERTS_PER_GROUP):
        vk, sk = row(sel, k), row(s, k)
        for g in range(1, N_EXPERT_GROUPS):
            hit = grp == g
            vk = jnp.where(hit, row(sel, EXPERTS_PER_GROUP * g + k), vk)
            sk = jnp.where(hit, row(s, EXPERTS_PER_GROUP * g + k), sk)
        vals.append(vk)
        affs.append(sk)
    i1, b1, w1 = jnp.zeros((1, tm), I32), vals[0], affs[0]
    for k in range(1, EXPERTS_PER_GROUP):
        better = vals[k] > b1
        i1 = jnp.where(better, k, i1)
        w1 = jnp.where(better, affs[k], w1)
        b1 = jnp.where(better, vals[k], b1)
    i2 = jnp.zeros((1, tm), I32)
    b2 = jnp.full((1, tm), -jnp.inf, F32)
    w2 = jnp.zeros((1, tm), F32)
    for k in range(EXPERTS_PER_GROUP):
        cand = (i1 != k) & (vals[k] > b2)
        i2 = jnp.where(cand, k, i2)
        w2 = jnp.where(cand, affs[k], w2)
        b2 = jnp.where(cand, vals[k], b2)
    e1 = grp * EXPERTS_PER_GROUP + i1
    e2 = grp * EXPERTS_PER_GROUP + i2
    tot = w1 + w2
    e_ref[...] = jnp.concatenate([e1, e2], axis=0)
    wpad = jnp.concatenate([w1 / tot, w2 / tot, jnp.zeros((6, tm), F32)], axis=0)
    w_ref[...] = wpad.T
    ids = lax.broadcasted_iota(I32, (N_EXPERTS, tm), 0)
    oh1 = (ids == e1).astype(F32)
    oh2 = (ids == e2).astype(F32)
    oh = oh1 + oh2
    before = (lax.broadcasted_iota(I32, (tm, tm), 0) < lax.broadcasted_iota(I32, (tm, tm), 1)).astype(BF16)
    prior = _dot(oh.astype(BF16), before) + base_sc[...]
    r1 = jnp.sum(oh1 * prior, axis=0, keepdims=True)
    r2 = jnp.sum(oh2 * prior, axis=0, keepdims=True)
    rank_ref[...] = jnp.concatenate([r1, r2], axis=0).astype(I32)
    base = base_sc[...] + jnp.sum(oh, axis=1, keepdims=True)
    base_sc[...] = base
    cnt_ref[...] = jnp.broadcast_to(base, cnt_ref.shape).astype(I32)


def route(s_t, b_router):
    n_rows = s_t.shape[1]
    tm = ROW_TILE
    b_col = jnp.pad(b_router.astype(F32), (0, LANES - N_EXPERTS)).reshape(LANES, 1)
    return pl.pallas_call(
        _router_body,
        grid=(n_rows // tm,),
        in_specs=[pl.BlockSpec((LANES, tm), lambda i: (0, i)), pl.BlockSpec((LANES, 1), lambda i: (0, 0))],
        out_specs=[pl.BlockSpec((TOP_K, tm), lambda i: (0, i)), pl.BlockSpec((tm, 8), lambda i: (i, 0)),
                   pl.BlockSpec((TOP_K, tm), lambda i: (0, i)), pl.BlockSpec((N_EXPERTS, LANES), lambda i: (0, 0))],
        out_shape=[jax.ShapeDtypeStruct((TOP_K, n_rows), I32), jax.ShapeDtypeStruct((n_rows, 8), F32),
                   jax.ShapeDtypeStruct((TOP_K, n_rows), I32), jax.ShapeDtypeStruct((N_EXPERTS, LANES), I32)],
        scratch_shapes=[pltpu.VMEM((N_EXPERTS, 1), F32)],
        compiler_params=_cp(("arbitrary",)),
        name="moe_router",
    )(s_t, b_col)


def _row_copy(src_ref, src_row, dst_ref, dst_row, sem):
    return pltpu.make_async_copy(src_ref.at[pl.ds(src_row, 1), :], dst_ref.at[pl.ds(dst_row, 1), :], sem)


def _dispatch_body(dest_ref, h_ref, buf_in_ref, buf_ref, sem):
    del buf_in_ref
    tm = h_ref.shape[0]

    def issue(t, carry):
        for k in range(TOP_K):
            _row_copy(h_ref, t, buf_ref, dest_ref[0, k, t], sem).start(priority=k)
        return carry

    lax.fori_loop(0, tm, issue, 0, unroll=8)
    for k in range(TOP_K):
        pltpu.make_async_copy(h_ref, buf_ref.at[pl.ds(0, tm), :], sem).wait()


def moe_dispatch(h2, dest3, buf0):
    n_rows, d = h2.shape
    tm = ROW_TILE
    return pl.pallas_call(
        _dispatch_body,
        grid=(n_rows // tm,),
        in_specs=[pl.BlockSpec((1, TOP_K, tm), lambda i: (i, 0, 0), memory_space=pltpu.SMEM),
                  pl.BlockSpec((tm, d), lambda i: (i, 0)),
                  pl.BlockSpec(memory_space=pl.ANY)],
        out_specs=pl.BlockSpec(memory_space=pl.ANY),
        out_shape=jax.ShapeDtypeStruct(buf0.shape, buf0.dtype),
        scratch_shapes=[pltpu.SemaphoreType.DMA(())],
        input_output_aliases={2: 0},
        compiler_params=_cp(("arbitrary",)),
        name="moe_dispatch",
    )(dest3, h2, buf0)


def _expert_body(be_ref, nu_ref, x_ref, w1_ref, w3_ref, w2_ref, o_ref, w1_sc, w3_sc, w2_sc):
    i = pl.program_id(0)
    changed = jnp.logical_or(i == 0, be_ref[i] != be_ref[jnp.maximum(i - 1, 0)])
    used = i < nu_ref[0]

    @pl.when(jnp.logical_and(changed, used))
    def _():
        w1_sc[...] = w1_ref[0].astype(BF16)
        w3_sc[...] = w3_ref[0].astype(BF16)
        w2_sc[...] = w2_ref[0].astype(BF16)

    @pl.when(used)
    def _():
        x = x_ref[...].astype(BF16)
        a = _dot(x, w1_sc[...])
        mid = (a * jax.nn.sigmoid(a)) * _dot(x, w3_sc[...])
        o_ref[...] = _dot(mid.astype(BF16), w2_sc[...])

    @pl.when(jnp.logical_not(used))
    def _():
        o_ref[...] = jnp.zeros(o_ref.shape, o_ref.dtype)


def moe_experts(buf, blk_expert, n_used, w1, w3, w2, layer, blk):
    n_slots, d = buf.shape
    de = w1.shape[3]
    grid_spec = pltpu.PrefetchScalarGridSpec(
        num_scalar_prefetch=2,
        grid=(n_slots // blk,),
        in_specs=[pl.BlockSpec((blk, d), lambda i, be, nu: (i, 0)),
                  pl.BlockSpec((None, 1, d, de), lambda i, be, nu: (layer, be[i], 0, 0)),
                  pl.BlockSpec((None, 1, d, de), lambda i, be, nu: (layer, be[i], 0, 0)),
                  pl.BlockSpec((None, 1, de, d), lambda i, be, nu: (layer, be[i], 0, 0))],
        out_specs=pl.BlockSpec((blk, d), lambda i, be, nu: (i, 0)),
        scratch_shapes=[pltpu.VMEM((d, de), BF16), pltpu.VMEM((d, de), BF16), pltpu.VMEM((de, d), BF16)],
    )
    return pl.pallas_call(
        _expert_body,
        grid_spec=grid_spec,
        out_shape=jax.ShapeDtypeStruct((n_slots, d), F32),
        compiler_params=_cp(("arbitrary",), VMEM_LIMIT),
        name="moe_experts",
    )(blk_expert, n_used, buf, w1, w3, w2)


def _combine_body(final, dest_ref, x_ref, w_ref, mod_ref, gf_ref, y_hbm, o_ref, y0_sc, y1_sc, sem):
    tm = x_ref.shape[0]
    bufs = (y0_sc, y1_sc)

    def issue(t, carry):
        for k in range(TOP_K):
            _row_copy(y_hbm, dest_ref[0, k, t], bufs[k], t, sem).start(priority=k)
        return carry

    lax.fori_loop(0, tm, issue, 0, unroll=8)
    for k in range(TOP_K):
        pltpu.make_async_copy(y_hbm.at[pl.ds(0, tm), :], bufs[k], sem).wait()
    w = w_ref[...]
    f = w[:, 0:1] * y0_sc[...] + w[:, 1:2] * y1_sc[...]
    x_new = x_ref[...] + mod_ref[0][5:6] * f
    if final:
        ms = jnp.mean(x_new * x_new, axis=-1, keepdims=True)
        x_new = x_new * lax.rsqrt(ms + EPS) * gf_ref[...]
    o_ref[...] = x_new


def moe_combine(dest3, x_rows, wts, modtab, g_final, y_slots, n_lat_rows, n_batch, final):
    n_rows, d = x_rows.shape
    tm = ROW_TILE
    per_b = n_lat_rows // n_batch // tm
    return pl.pallas_call(
        functools.partial(_combine_body, final),
        grid=(n_rows // tm,),
        in_specs=[pl.BlockSpec((1, TOP_K, tm), lambda i: (i, 0, 0), memory_space=pltpu.SMEM),
                  pl.BlockSpec((tm, d), lambda i: (i, 0)),
                  pl.BlockSpec((tm, 8), lambda i: (i, 0)),
                  pl.BlockSpec((1, N_MOD, d), lambda i: (jnp.minimum(i // per_b, n_batch), 0, 0)),
                  pl.BlockSpec((1, d), lambda i: (0, 0)),
                  pl.BlockSpec(memory_space=pl.ANY)],
        out_specs=pl.BlockSpec((tm, d), lambda i: (i, 0)),
        out_shape=jax.ShapeDtypeStruct((n_rows, d), F32),
        scratch_shapes=[pltpu.VMEM((tm, d), F32), pltpu.VMEM((tm, d), F32), pltpu.SemaphoreType.DMA(())],
        compiler_params=_cp(("arbitrary",)),
        name="moe_combine",
    )(dest3, x_rows, wts, modtab, g_final.reshape(1, d), y_slots)


def moe_layer(x_rows, h2, s_t, b_router, w1, w3, w2, layer, modtab, g_final, n_lat_rows, n_batch, final):
    n_rows, d = h2.shape
    blk = 2 * MOE_BLOCK
    experts, wts, rank, counts = route(s_t, b_router)
    cnt = counts[:, 0]
    padded = (cnt + blk - 1) // blk * blk
    pend = jnp.cumsum(padded)
    pstart = pend - padded
    hit = experts[..., None] == jnp.arange(N_EXPERTS, dtype=I32)
    dest = jnp.sum(jnp.where(hit, pstart.astype(I32), 0), axis=-1) + rank
    n_blocks = -(-(n_rows * TOP_K) // blk) + N_EXPERTS
    blk_start = jnp.arange(n_blocks, dtype=I32) * blk
    blk_expert = jnp.minimum(jnp.sum((pend[None, :] <= blk_start[:, None]).astype(I32), axis=1), N_EXPERTS - 1)
    dest3 = dest.reshape(TOP_K, n_rows // ROW_TILE, ROW_TILE).transpose(1, 0, 2)
    buf = moe_dispatch(h2, dest3, jnp.zeros((n_blocks * blk, d), F32))
    n_used = (pend[-1:] // blk).astype(I32)
    y_slots = moe_experts(buf, blk_expert, n_used, w1, w3, w2, layer, blk)
    return moe_combine(dest3, x_rows, wts, modtab, g_final, y_slots, n_lat_rows, n_batch, final)


_COL = dict(na_q=0, na_k=512, na_v=1024, ml_q=1536, ml_k=2048, ml_v=2560, ml_o=3072,
            gq_q=3584, gq_k=4096, gq_v=4224)


def _split_w_in(w_in):
    sizes = (BRANCH_WIDTH,) * 8 + (4 * ML_HEADS, BRANCH_WIDTH, GQ_KV_HEADS * GQ_HEAD_DIM, GQ_KV_HEADS * GQ_HEAD_DIM)
    idx = np.cumsum(sizes)[:-1].tolist()
    (s5_u, na_q, na_k, na_v, ml_q, ml_k, ml_v, ml_o, ml_gt, gq_q, gq_k, gq_v) = jnp.split(w_in, idx, axis=-1)
    wa = jnp.pad(ml_gt, ((0, 0), (0, LANES - 4 * ML_HEADS)))
    wb = jnp.concatenate([na_q * (NA_HEAD_DIM ** -0.5 * LOG2E), na_k, na_v, ml_q, ml_k * (ML_HEAD_DIM ** -0.5),
                          ml_v, ml_o, gq_q, gq_k, gq_v], axis=1)
    return s5_u.T.astype(BF16), wa.astype(BF16), wb.astype(BF16)


def kernel(x, c, ctx, c_ctx, w_mod, b_mod, g_norm1, g_norm2, w_in, s5_lam_re, s5_lam_im, s5_log_dt, s5_b_re,
           s5_b_im, s5_c_re, s5_c_im, s5_d, s5_w_glu, s5_b_glu, na_rpb, ml_b_gates, ml_norm, gq_qnorm, gq_knorm,
           w_branch, w_gate, b_gate, w_out, w_router, b_router, moe_w1, moe_w3, moe_w2, g_final):
    b, n_lat, dm = x.shape
    n_ctx = ctx.shape[1]
    depth = w_in.shape[0]
    bn, bc = b * n_lat, b * n_ctx
    x_all = jnp.concatenate([x.reshape(bn, dm), ctx.reshape(bc, dm)], axis=0).astype(F32)
    c_all = jnp.concatenate([c.astype(F32), c_ctx.astype(F32)[None], jnp.zeros((8 - b - 1, dm), F32)], axis=0)
    cs_tab = rope_tables(n_lat)
    wr_t = jnp.pad(w_router.astype(BF16).T, ((0, LANES - N_EXPERTS), (0, 0)))
    out = None
    for l in range(depth):
        last = l == depth - 1
        with_ctx = not last
        modtab = mod_vectors(c_all, w_mod, b_mod, l)[:b + 1].reshape(b + 1, N_MOD, dm)
        w_u_t, wa, wb = _split_w_in(w_in[l])
        pa, pb, gate, h_all = in_projection(x_all, g_norm1[l], modtab, wa, wb, w_gate[l].astype(BF16), b_gate[l],
                                            bn, b)

        tables = s5_tables(s5_lam_re[l], s5_lam_im[l], s5_log_dt[l], s5_b_re[l], s5_b_im[l],
                           s5_c_re[l], s5_c_im[l], s5_d[l])
        ya = s5_mixer(h_all, w_u_t, tables, s5_w_glu[l].astype(BF16), s5_b_glu[l].astype(F32).reshape(1, -1),
                      n_lat, n_ctx, b)

        bias_tab = na_bias_tables(na_rpb[l], n_lat // GRID_W, n_ctx)
        n_rows = bn + bc if with_ctx else bn
        na_cols = (_COL['na_q'], _COL['na_k'], _COL['na_v'])
        yb = (na_mixer(pb, *na_cols, bias_tab, n_lat, n_ctx, b),
              na_ctx_attention(pb, *na_cols, n_lat, n_ctx, b) if with_ctx else None)

        ml_bias = jnp.pad(ml_b_gates[l].astype(F32), (0, LANES - 4 * ML_HEADS)).reshape(1, LANES)
        hf, hb = mlstm_mixer(pb, pa, _COL['ml_q'], _COL['ml_k'], _COL['ml_v'], 0, ml_bias, n_lat, n_ctx, b)

        yd = gq_mixer(pb, _COL['gq_q'], _COL['gq_k'], _COL['gq_v'], cs_tab, gq_qnorm[l], gq_knorm[l],
                      n_lat, n_ctx, b, with_ctx)
        x_mid, h2, s_t = merge_layer(
            ya, yb, yd, hf, hb, pb, _COL['ml_o'], gate, x_all, modtab, ml_norm[l].astype(F32).reshape(1, -1),
            w_branch[l].astype(BF16), w_out[l].astype(BF16), g_norm2[l].astype(F32).reshape(1, -1), wr_t,
            n_rows, bn, b)
        x_next = moe_layer(x_mid, h2, s_t, b_router, moe_w1, moe_w3, moe_w2, l, modtab, g_final, bn, b, last)
        if last:
            out = x_next.reshape(b, n_lat, dm).astype(x.dtype)
        else:
            x_all = x_next
    return out
```

```python
import functools
import math

import numpy as np
import jax
import jax.numpy as jnp
from jax import lax
from jax.experimental import pallas as pl
from jax.experimental.pallas import tpu as pltpu

F32 = jnp.float32
BF16 = jnp.bfloat16
I32 = jnp.int32

GRID_W = 64
N_MOD = 6
BRANCH_WIDTH = 512
N_BRANCHES = 4
S5_GROUP = 16
S5_GROUPS = BRANCH_WIDTH // S5_GROUP
S5_STATE = 64
NA_HEADS = 8
NA_HEAD_DIM = 64
NA_ROWS = 8
NA_COLS = 16
ML_HEADS = 4
ML_HEAD_DIM = 128
ML_CHUNK = 128
GQ_HEADS = 8
GQ_KV_HEADS = 2
GQ_HEAD_DIM = 64
ROPE_THETA = 10000.0
N_EXPERTS = 32
N_EXPERT_GROUPS = 8
EXPERTS_PER_GROUP = 4
TOP_K = 2
D_EXPERT = 512
MOE_BLOCK = 128
EPS = 1e-6
NEG_INIT = -1e30
MASK_NEG = -1e30
LOG2E = 1.4426950408889634

LANES = 128
ROW_TILE = 256
S5_CHUNK = 16
NA_TILE_ROWS = 4
VMEM_LIMIT = 56 * 1024 * 1024

HIGHEST = lax.Precision.HIGHEST


def _cp(sem, vmem=None):
    return pltpu.CompilerParams(dimension_semantics=sem, vmem_limit_bytes=vmem)


def _dot(a, b):
    return jnp.dot(a, b, preferred_element_type=F32)


def _dot_nt(a, b):
    return lax.dot_general(a, b, (((1,), (1,)), ((), ())), preferred_element_type=F32)


def _dot_tn(a, b):
    return lax.dot_general(a, b, (((0,), (0,)), ((), ())), preferred_element_type=F32)


def _rms_mod(x, g, shift, scale):
    ms = jnp.mean(x * x, axis=-1, keepdims=True)
    y = x * lax.rsqrt(ms + EPS) * g
    return y * (1.0 + scale) + shift


def _round_up(n, m):
    return -(-n // m) * m


def _mod_body(c_ref, w_ref, b_ref, o_ref):
    c = c_ref[...]
    a = (c * jax.nn.sigmoid(c)).astype(BF16)
    o_ref[...] = _dot(a, w_ref[...].astype(BF16)) + b_ref[...]


def mod_vectors(c_all, w_mod, b_mod, layer):
    d = c_all.shape[1]
    depth = w_mod.shape[0]
    return pl.pallas_call(
        _mod_body,
        grid=(N_MOD,),
        in_specs=[pl.BlockSpec((8, d), lambda j: (0, 0)),
                  pl.BlockSpec((None, d, d), lambda j: (layer, 0, j)),
                  pl.BlockSpec((None, 1, d), lambda j: (layer, 0, j))],
        out_specs=pl.BlockSpec((8, d), lambda j: (0, j)),
        out_shape=jax.ShapeDtypeStruct((8, N_MOD * d), F32),
        compiler_params=_cp(("arbitrary",)),
        name="mod_vectors",
    )(c_all, w_mod, b_mod.reshape(depth, 1, -1))


def _inproj_body(x_ref, g_ref, mod_ref, wa_ref, wb_ref, wg_ref, bg_ref, oa_ref, ob_ref, og_ref, oh_ref):
    mod = mod_ref[0]
    h = _rms_mod(x_ref[...], g_ref[...], mod[0:1], mod[1:2]).astype(BF16)
    oh_ref[...] = h
    oa_ref[...] = _dot(h, wa_ref[...])
    ob_ref[...] = _dot(h, wb_ref[...]).astype(BF16)
    og_ref[...] = jax.nn.sigmoid(_dot(h, wg_ref[...]) + bg_ref[...]).astype(BF16)


def in_projection(x_all, g, modtab, wa, wb, wg, bg, n_lat_rows, n_batch):
    r, d = x_all.shape
    tm = ROW_TILE
    per_b = n_lat_rows // n_batch // tm

    def mod_idx(i):
        return (jnp.minimum(i // per_b, n_batch), 0, 0)

    const = lambda i: (0, 0)
    return pl.pallas_call(
        _inproj_body,
        grid=(r // tm,),
        in_specs=[pl.BlockSpec((tm, d), lambda i: (i, 0)),
                  pl.BlockSpec((1, d), const),
                  pl.BlockSpec((1, N_MOD, d), mod_idx),
                  pl.BlockSpec(wa.shape, const, pipeline_mode=pl.Buffered(1)),
                  pl.BlockSpec(wb.shape, const, pipeline_mode=pl.Buffered(1)),
                  pl.BlockSpec(wg.shape, const, pipeline_mode=pl.Buffered(1)),
                  pl.BlockSpec((1, wg.shape[1]), const)],
        out_specs=[pl.BlockSpec((tm, wa.shape[1]), lambda i: (i, 0)),
                   pl.BlockSpec((tm, wb.shape[1]), lambda i: (i, 0)),
                   pl.BlockSpec((tm, wg.shape[1]), lambda i: (i, 0)),
                   pl.BlockSpec((tm, d), lambda i: (i, 0))],
        out_shape=[jax.ShapeDtypeStruct((r, wa.shape[1]), F32),
                   jax.ShapeDtypeStruct((r, wb.shape[1]), BF16),
                   jax.ShapeDtypeStruct((r, wg.shape[1]), BF16),
                   jax.ShapeDtypeStruct((r, d), BF16)],
        compiler_params=_cp(("arbitrary",), VMEM_LIMIT),
        name="in_projection",
    )(x_all, g.reshape(1, d), modtab, wa, wb, wg, bg.reshape(1, -1))


def s5_tables(lam_re, lam_im, log_dt, b_re, b_im, c_re, c_im, d_skip):
    ell, g_n, p_n, c_n = S5_CHUNK, S5_GROUPS, S5_STATE, S5_GROUP
    lam_re, lam_im = lam_re.astype(F32), lam_im.astype(F32)
    b_re, b_im, c_re, c_im = (t.astype(F32) for t in (b_re, b_im, c_re, c_im))
    dt = jnp.exp(log_dt.astype(F32))[..., None]
    mag = jnp.exp(lam_re * dt)
    a_re = mag * jnp.cos(lam_im * dt)
    a_im = mag * jnp.sin(lam_im * dt)
    den = lam_re * lam_re + lam_im * lam_im
    nr = a_re - 1.0
    f_re = (nr * lam_re + a_im * lam_im) / den
    f_im = (a_im * lam_re - nr * lam_im) / den
    bb_re = f_re[..., None] * b_re - f_im[..., None] * b_im
    bb_im = f_re[..., None] * b_im + f_im[..., None] * b_re
    k = jnp.arange(ell + 1, dtype=F32)
    pmag = jnp.exp((lam_re * dt)[..., None] * k)
    ang = (lam_im * dt)[..., None] * k
    pr, pi = pmag * jnp.cos(ang), pmag * jnp.sin(ang)
    ab_re = pr[..., None] * bb_re[:, :, :, None, :] - pi[..., None] * bb_im[:, :, :, None, :]
    ab_im = pr[..., None] * bb_im[:, :, :, None, :] + pi[..., None] * bb_re[:, :, :, None, :]
    flat = lambda t: t.reshape(2 * g_n, p_n, (ell + 1) * c_n)
    kk = (jnp.einsum('bcp,bpn->bcn', c_re.reshape(2 * g_n, c_n, p_n), flat(ab_re), precision=HIGHEST)
          - jnp.einsum('bcp,bpn->bcn', c_im.reshape(2 * g_n, c_n, p_n), flat(ab_im), precision=HIGHEST))
    kk = kk.reshape(2, g_n, c_n, ell + 1, c_n)
    centre = kk[0][:, :, 0] + kk[1][:, :, 0] + d_skip.astype(F32).reshape(g_n, c_n, 1) * jnp.eye(c_n, dtype=F32)
    w = jnp.concatenate([kk[0][:, :, ell - 1:0:-1], centre[:, :, None], kk[1][:, :, 1:ell]], axis=2)
    wf = w.reshape(g_n, c_n, (2 * ell - 1) * c_n)
    toe = jnp.stack([wf[:, :, (ell - 1 - t) * c_n:(2 * ell - 1 - t) * c_n] for t in range(ell)], axis=1)
    tsum_t = toe.astype(BF16).reshape(g_n, ell * c_n, ell * c_n)

    parity = [jnp.asarray(np.arange(g_n) % 2 == q, F32) for q in range(2)]
    parts = []
    for d in range(2):
        for ab in (ab_re, ab_im):
            sel = ab[d][:, :, :ell]
            if d == 0:
                sel = sel[:, :, ::-1]
            sel = sel.reshape(g_n, p_n, ell * c_n)
            parts += [sel * parity[q][:, None, None] for q in range(2)]
    mend_t = jnp.stack(parts, axis=1).reshape(g_n, 8 * p_n, ell * c_n)

    rows = []
    for d in range(2):
        prk, pik = pr[d][:, :, 1:ell + 1], pi[d][:, :, 1:ell + 1]
        if d == 1:
            prk, pik = prk[:, :, ::-1], pik[:, :, ::-1]
        prk = prk.transpose(0, 2, 1)[:, :, None, :]
        pik = pik.transpose(0, 2, 1)[:, :, None, :]
        cr, ci = c_re[d][:, None], c_im[d][:, None]
        for part in (cr * prk - ci * pik, -cr * pik - ci * prk):
            rows += [part * parity[q][:, None, None, None] for q in range(2)]
    wst_t = jnp.concatenate(rows, axis=-1).reshape(g_n, ell * c_n, 8 * p_n)

    al = jnp.stack([pr[0][:, :, ell], pi[0][:, :, ell], pr[1][:, :, ell], pi[1][:, :, ell]], axis=1)
    a_chunk = al.reshape(g_n // 2, 2, 4, p_n).transpose(0, 2, 1, 3).reshape(g_n // 2, 8 * p_n)
    return tsum_t.astype(BF16), mend_t.astype(BF16), wst_t.astype(BF16), a_chunk


def _s5_proj_body(hl_ref, hc_ref, w_ref, o_ref):
    n_lat = hl_ref.shape[0]
    n_pad = o_ref.shape[1] - LANES
    hl = hl_ref[...]
    if n_pad > n_lat:
        hl = jnp.concatenate([hl, jnp.zeros((n_pad - n_lat, hl.shape[1]), BF16)], axis=0)
    o_ref[:, 0:n_pad] = _dot_nt(w_ref[...], hl).astype(BF16)
    hc = hc_ref[...]
    hc = jnp.concatenate([hc, jnp.zeros((LANES - hc.shape[0], hc.shape[1]), BF16)], axis=0)
    o_ref[:, n_pad:] = _dot_nt(w_ref[...], hc).astype(BF16)


def _s5_end_body(u_ref, m_ref, o_ref):
    width = m_ref.shape[2]
    acc = None
    for q in range(2):
        u = u_ref[:, q].reshape(width, u_ref.shape[3])
        term = _dot(m_ref[q], u)
        acc = term if acc is None else acc + term
    o_ref[...] = acc.T


def _s5_scan_body(n_batch, per_b, n_ctx_chunks, lat_pad, e_ref, a_ref, o_ref):
    n_pairs = e_ref.shape[0]
    o_ref[...] = jnp.zeros(o_ref.shape, F32)
    sub = 8
    ctx_groups = n_ctx_chunks // sub
    n_groups = (per_b + n_ctx_chunks) // sub
    coef = [[a_ref[q, :, j * LANES:(j + 1) * LANES] for j in range(4)] for q in range(n_pairs)]

    def group(g, carry):
        new = list(carry)
        for b in range(n_batch):
            ctx = g < ctx_groups
            up = jnp.where(ctx, lat_pad + b * n_ctx_chunks + g * sub, b * per_b + (g - ctx_groups) * sub)
            down = jnp.where(ctx, lat_pad + (b + 1) * n_ctx_chunks - (g + 1) * sub,
                             (b + 1) * per_b - (g - ctx_groups + 1) * sub)
            for q in range(n_pairs):
                for d, base in enumerate((up, down)):
                    base = pl.multiple_of(base, sub)
                    lanes = slice(2 * d * LANES, (2 * d + 2) * LANES)
                    e = e_ref[q, pl.ds(base, sub), lanes]
                    k = ((b * n_pairs + q) * 2 + d) * 2
                    sr, si = new[k], new[k + 1]
                    ar, ai = coef[q][2 * d], coef[q][2 * d + 1]
                    rows = [None] * sub
                    for step in range(sub):
                        r = step if d == 0 else sub - 1 - step
                        rows[r] = jnp.concatenate([sr, si], axis=1)
                        er, ei = e[r:r + 1, :LANES], e[r:r + 1, LANES:]
                        sr, si = ar * sr - ai * si + er, ar * si + ai * sr + ei
                    o_ref[q, pl.ds(base, sub), lanes] = jnp.concatenate(rows, axis=0)
                    new[k], new[k + 1] = sr, si
        return tuple(new)

    z = jnp.zeros((1, LANES), F32)
    lax.fori_loop(0, n_groups, group, tuple(z for _ in range(n_batch * n_pairs * 4)))


def _s5_out_body(u_ref, t_ref, s_ref, w_ref, o_ref):
    width = t_ref.shape[0]
    u = u_ref[...].reshape(width, u_ref.shape[2])
    y = _dot(t_ref[...], u) + _dot_nt(w_ref[...], s_ref[...].astype(BF16))
    o_ref[...] = y.reshape(o_ref.shape)


def _s5_glu_body(n_lat_chunks, n_ctx_chunks, lat_pad, y_ref, w_ref, b_ref, o_ref):
    y = y_ref[...].T
    if lat_pad == n_lat_chunks:
        y = y[:n_lat_chunks + n_ctx_chunks]
    else:
        y = jnp.concatenate([y[:n_lat_chunks], y[lat_pad:lat_pad + n_ctx_chunks]], axis=0)
    z = jax.nn.gelu(y)
    o_ref[...] = (z * jax.nn.sigmoid(_dot(z.astype(BF16), w_ref[...]) + b_ref[...])).astype(o_ref.dtype)


def s5_mixer(h_all, w_u_t, tables, w_glu, b_glu, n_lat, n_ctx, n_batch):
    tsum_t, mend_t, wst_t, a_chunk = tables
    r, d = h_all.shape
    h2 = h_all.reshape(r // S5_CHUNK, S5_CHUNK * d)
    ell, g_n, c_n = S5_CHUNK, S5_GROUPS, S5_GROUP
    width = ell * c_n
    sw = 8 * S5_STATE
    bw = g_n * c_n
    n_lat_chunks = n_batch * n_lat // ell
    n_ctx_chunks = n_batch * n_ctx // ell
    assert n_lat_chunks % n_ctx_chunks == 0 and n_ctx_chunks % 16 == 0 and n_ctx_chunks <= LANES
    lat_pad = _round_up(n_lat_chunks, LANES)
    nch = lat_pad + LANES
    const = lambda t: (0, 0)
    u_t = pl.pallas_call(
        _s5_proj_body,
        grid=(ell,),
        in_specs=[pl.BlockSpec((n_lat_chunks, d), lambda t: (0, t)),
                  pl.BlockSpec((n_ctx_chunks, d), lambda t: (n_lat_chunks // n_ctx_chunks, t)),
                  pl.BlockSpec((bw, d), const)],
        out_specs=pl.BlockSpec((None, bw, nch), lambda t: (t, 0, 0)),
        out_shape=jax.ShapeDtypeStruct((ell, bw, nch), BF16),
        compiler_params=_cp(("arbitrary",)),
        name="s5_projection",
    )(h2, h2, w_u_t)
    u4 = u_t.reshape(ell, g_n, c_n, nch)
    ends = pl.pallas_call(
        _s5_end_body,
        grid=(g_n // 2,),
        in_specs=[pl.BlockSpec((ell, 2, c_n, nch), lambda p: (0, p, 0, 0)),
                  pl.BlockSpec((2, sw, width), lambda p: (p, 0, 0))],
        out_specs=pl.BlockSpec((None, nch, sw), lambda p: (p, 0, 0)),
        out_shape=jax.ShapeDtypeStruct((g_n // 2, nch, sw), F32),
        compiler_params=_cp(("arbitrary",)),
        name="s5_chunk_ends",
    )(u4, mend_t)
    pairs_per_step = 4
    assert (n_ctx // ell) % 8 == 0 and (n_lat // ell) % 8 == 0
    states = pl.pallas_call(
        functools.partial(_s5_scan_body, n_batch, n_lat // ell, n_ctx // ell, lat_pad),
        grid=(g_n // 2 // pairs_per_step,),
        in_specs=[pl.BlockSpec((pairs_per_step, nch, sw), lambda j: (j, 0, 0)),
                  pl.BlockSpec((pairs_per_step, 1, sw), lambda j: (j, 0, 0))],
        out_specs=pl.BlockSpec((pairs_per_step, nch, sw), lambda j: (j, 0, 0)),
        out_shape=jax.ShapeDtypeStruct((g_n // 2, nch, sw), F32),
        compiler_params=_cp(("arbitrary",), VMEM_LIMIT),
        name="s5_state_scan",
    )(ends, a_chunk.reshape(g_n // 2, 1, sw))
    y_t = pl.pallas_call(
        _s5_out_body,
        grid=(g_n,),
        in_specs=[pl.BlockSpec((ell, None, c_n, nch), lambda gi: (0, gi, 0, 0)),
                  pl.BlockSpec((None, width, width), lambda gi: (gi, 0, 0)),
                  pl.BlockSpec((None, nch, sw), lambda gi: (gi // 2, 0, 0)),
                  pl.BlockSpec((None, width, sw), lambda gi: (gi, 0, 0))],
        out_specs=pl.BlockSpec((ell, None, c_n, nch), lambda gi: (0, gi, 0, 0)),
        out_shape=jax.ShapeDtypeStruct((ell, g_n, c_n, nch), F32),
        compiler_params=_cp(("arbitrary",)),
        name="s5_outputs",
    )(u4, tsum_t, states, wst_t)
    n_chunks = n_lat_chunks + n_ctx_chunks
    ya = pl.pallas_call(
        functools.partial(_s5_glu_body, n_lat_chunks, n_ctx_chunks, lat_pad),
        grid=(ell,),
        in_specs=[pl.BlockSpec((None, bw, nch), lambda t: (t, 0, 0)),
                  pl.BlockSpec((bw, bw), const),
                  pl.BlockSpec((1, bw), const)],
        out_specs=pl.BlockSpec((n_chunks, bw), lambda t: (0, t)),
        out_shape=jax.ShapeDtypeStruct((n_chunks, ell * bw), BF16),
        compiler_params=_cp(("arbitrary",)),
        name="s5_glu",
    )(y_t.reshape(ell, bw, nch), w_glu, b_glu)
    return ya.reshape(r, bw)


def rope_tables(n_lat):
    half = GQ_HEAD_DIM // 2
    quarter = half // 2
    t = np.arange(n_lat)
    freqs = ROPE_THETA ** (-np.arange(quarter, dtype=np.float64) / quarter)
    ang_r = (t // GRID_W)[:, None] * freqs
    ang_c = (t % GRID_W)[:, None] * freqs
    ang = np.concatenate([ang_r, ang_r, ang_c, ang_c], axis=1)
    sign = np.concatenate([-np.ones(quarter), np.ones(quarter)] * 2)
    cos = np.concatenate([np.cos(ang), np.ones((ROW_TILE, GQ_HEAD_DIM))], axis=0)
    sin = np.concatenate([np.sin(ang) * sign, np.zeros((ROW_TILE, GQ_HEAD_DIM))], axis=0)
    tab = np.concatenate([cos, cos, sin, sin], axis=1)
    return jnp.asarray(tab, F32)


def _group_ones(width, group):
    i = np.arange(width)
    return jnp.asarray((i[:, None] // group) == (i[None, :] // group), BF16)


def _group_mean_sq(x, ones_blk, group):
    sq = x * x
    hi = sq.astype(BF16)
    lo = (sq - hi.astype(F32)).astype(BF16)
    return (_dot(hi, ones_blk) + _dot(lo, ones_blk)) * (1.0 / group)


def _rope(x, cos, sin):
    w = x.shape[-1]
    q = GQ_HEAD_DIM // 4
    lane = lax.broadcasted_iota(I32, x.shape, 1)
    first = (lane % (2 * q)) < q
    partner = jnp.where(first, pltpu.roll(x, w - q, 1), pltpu.roll(x, q, 1))
    return x * cos + partner * sin


def _gq_prep_body(q_ref, k_ref, v_ref, cs_ref, gq_ref, gk_ref, oq_ref, ok_ref, qm_ref, kr_ref, va_ref):
    cs = cs_ref[...]
    cos1, sin1 = cs[:, :LANES], cs[:, LANES:]
    q = q_ref[...].astype(F32)
    qn = q * lax.rsqrt(_group_mean_sq(q, oq_ref[...], GQ_HEAD_DIM) + EPS) * gq_ref[...]
    n_pairs = q.shape[1] // LANES
    qr = _rope(qn, jnp.concatenate([cos1] * n_pairs, axis=1), jnp.concatenate([sin1] * n_pairs, axis=1))
    qr = (qr * (GQ_HEAD_DIM ** -0.5 * LOG2E)).astype(BF16)
    lane = lax.broadcasted_iota(I32, (q.shape[0], LANES), 1)
    heads_per_kv = GQ_HEADS // GQ_KV_HEADS
    for h in range(GQ_HEADS):
        pair = qr[:, (h // 2) * LANES:(h // 2 + 1) * LANES]
        kv = h // heads_per_kv
        if h % 2 != kv:
            pair = pltpu.roll(pair, GQ_HEAD_DIM, 1)
        keep = (lane >= kv * GQ_HEAD_DIM) & (lane < (kv + 1) * GQ_HEAD_DIM)
        qm_ref[h] = jnp.where(keep, pair, jnp.zeros_like(pair))
    k = k_ref[...].astype(F32)
    kn = k * lax.rsqrt(_group_mean_sq(k, ok_ref[...], GQ_HEAD_DIM) + EPS) * gk_ref[...]
    kr_ref[...] = _rope(kn, cos1, sin1).astype(BF16)
    va_ref[...] = jnp.concatenate([v_ref[...], jnp.ones(v_ref.shape, BF16)], axis=1)


def gq_prepare(pb, col_q, col_k, col_v, cs_tab, g_q, g_k, n_lat, n_ctx, n_batch):
    r = pb.shape[0]
    tm = ROW_TILE
    assert n_ctx == tm and n_lat % tm == 0
    nb = n_lat // tm
    n_lat_tiles = n_batch * nb

    def tab_idx(i):
        return (jnp.where(i < n_lat_tiles, i % nb, nb), 0)

    def kv_idx(i):
        lat = (i // nb) * (nb + 1) + i % nb
        ctx = (i - n_lat_tiles) * (nb + 1) + nb
        return (jnp.where(i < n_lat_tiles, lat, ctx), 0)

    qw = GQ_HEADS * GQ_HEAD_DIM
    const = lambda i: (0, 0)
    gq = jnp.tile(g_q.astype(F32), GQ_HEADS).reshape(1, qw)
    gk = jnp.tile(g_k.astype(F32), GQ_KV_HEADS).reshape(1, LANES)
    n_keys = n_batch * (n_lat + n_ctx)
    return pl.pallas_call(
        _gq_prep_body,
        grid=(r // tm,),
        in_specs=[pl.BlockSpec((tm, qw), lambda i: (i, col_q // qw)),
                  pl.BlockSpec((tm, LANES), lambda i: (i, col_k // LANES)),
                  pl.BlockSpec((tm, LANES), lambda i: (i, col_v // LANES)),
                  pl.BlockSpec((tm, 2 * LANES), tab_idx),
                  pl.BlockSpec((1, qw), const),
                  pl.BlockSpec((1, LANES), const),
                  pl.BlockSpec((qw, qw), const),
                  pl.BlockSpec((LANES, LANES), const)],
        out_specs=[pl.BlockSpec((GQ_HEADS, tm, LANES), lambda i: (0, i, 0)),
                   pl.BlockSpec((tm, LANES), kv_idx),
                   pl.BlockSpec((tm, 2 * LANES), kv_idx)],
        out_shape=[jax.ShapeDtypeStruct((GQ_HEADS, r, LANES), BF16),
                   jax.ShapeDtypeStruct((n_keys, LANES), BF16),
                   jax.ShapeDtypeStruct((n_keys, 2 * LANES), BF16)],
        compiler_params=_cp(("arbitrary",)),
        name="gq_prepare",
    )(pb, pb, pb, cs_tab, gq, gk, _group_ones(qw, GQ_HEAD_DIM), _group_ones(LANES, GQ_HEAD_DIM))


def _gq_flash_body(q_ref, k_ref, v_ref, o_ref, m_sc, acc_sc):
    kj = pl.program_id(2)
    n_h, tq, _ = q_ref.shape

    @pl.when(kj == 0)
    def _():
        m_sc[...] = jnp.full(m_sc.shape, -jnp.inf, F32)
        acc_sc[...] = jnp.zeros(acc_sc.shape, F32)

    k = k_ref[...]
    v = v_ref[...]
    hp = 1
    for c in range(n_h // hp):
        rows = slice(c * hp * tq, (c + 1) * hp * tq)
        s = _dot_nt(q_ref[c * hp:(c + 1) * hp].reshape(hp * tq, LANES), k)
        m_prev = m_sc[rows, :]
        m_new = jnp.maximum(m_prev, jnp.max(s, axis=-1, keepdims=True))
        p = jnp.exp2(s - m_new)
        acc_sc[rows, :] = jnp.exp2(m_prev - m_new) * acc_sc[rows, :] + _dot(p.astype(BF16), v)
        m_sc[rows, :] = m_new

    @pl.when(kj == pl.num_programs(2) - 1)
    def _():
        lane = lax.broadcasted_iota(I32, (tq, LANES), 1)
        heads_per_kv = n_h // GQ_KV_HEADS
        for j in range(n_h // 2):
            kv = (2 * j) // heads_per_kv
            halves = []
            for h in (2 * j, 2 * j + 1):
                a = acc_sc[h * tq:(h + 1) * tq, :]
                halves.append(a[:, :LANES] / a[:, LANES:LANES + 1])
            lo, hi = halves
            if kv == 0:
                hi = pltpu.roll(hi, GQ_HEAD_DIM, 1)
            else:
                lo = pltpu.roll(lo, GQ_HEAD_DIM, 1)
            o_ref[:, j * LANES:(j + 1) * LANES] = jnp.where(lane < GQ_HEAD_DIM, lo, hi).astype(o_ref.dtype)


def gq_attention(qm, keys, vals, n_rows_out, tq, tk, q_blk, k_blk, o_blk, n_q, n_k, n_batch):
    n_h = qm.shape[0]
    return pl.pallas_call(
        _gq_flash_body,
        grid=(n_batch, n_q, n_k),
        in_specs=[pl.BlockSpec((n_h, tq, LANES), lambda b, i, j: (0, q_blk(b, i), 0)),
                  pl.BlockSpec((tk, LANES), lambda b, i, j: (k_blk(b, j), 0)),
                  pl.BlockSpec((tk, 2 * LANES), lambda b, i, j: (k_blk(b, j), 0))],
        out_specs=pl.BlockSpec((tq, n_h * GQ_HEAD_DIM), lambda b, i, j: (o_blk(b, i), 0)),
        out_shape=jax.ShapeDtypeStruct((n_rows_out, n_h * GQ_HEAD_DIM), BF16),
        scratch_shapes=[pltpu.VMEM((n_h * tq, 1), F32), pltpu.VMEM((n_h * tq, 2 * LANES), F32)],
        compiler_params=_cp(("arbitrary", "arbitrary", "arbitrary"), VMEM_LIMIT),
        name="gq_attention",
    )(qm, keys, vals)


def _largest_divisor(n, cap):
    return max(d for d in range(1, cap + 1) if n % d == 0)


def gq_mixer(pb, col_q, col_k, col_v, cs_tab, g_q, g_k, n_lat, n_ctx, n_batch, with_ctx):
    qm, keys, vals = gq_prepare(pb, col_q, col_k, col_v, cs_tab, g_q, g_k, n_lat, n_ctx, n_batch)
    tq = 512
    tk = LANES * _largest_divisor((n_lat + n_ctx) // LANES, 22)
    n_q = n_lat // tq
    per_b = (n_lat + n_ctx) // tk
    lat_blk = lambda b, i: b * n_q + i
    y_lat = gq_attention(qm, keys, vals, n_batch * n_lat, tq, tk, lat_blk, lambda b, j: b * per_b + j, lat_blk,
                         n_q, per_b, n_batch)
    if not with_ctx:
        return y_lat, None
    tc = n_ctx
    y_ctx = gq_attention(qm, keys, vals, n_batch * n_ctx, tc, tc, lambda b, i: n_batch * n_lat // tc + b,
                         lambda b, j: b * ((n_lat + n_ctx) // tc) + n_lat // tc, lambda b, i: b, 1, 1, n_batch)
    return y_lat, y_ctx


def na_bias_tables(rpb, n_img_rows, n_ctx):
    tr = NA_TILE_ROWS
    nt = n_img_rows // tr
    assert nt >= 4
    kr = min(NA_ROWS, n_img_rows)
    n_heads = rpb.shape[0]
    qcol = np.arange(GRID_W)[:, None]
    kcol = np.arange(GRID_W)[None, :]
    dc = np.clip(kcol - qcol + NA_COLS - 1, 0, 2 * NA_COLS - 2)
    oh_c = (dc[None] == np.arange(2 * NA_COLS - 1)[:, None, None]).astype(np.float32)
    cstart = np.clip(qcol - NA_COLS // 2, 0, GRID_W - NA_COLS)
    col_ok = (kcol >= cstart) & (kcol < cstart + NA_COLS)
    by_col = jnp.einsum('hrd,dqk->hrqk', rpb.astype(F32), jnp.asarray(oh_c), precision=HIGHEST)
    by_col = jnp.where(jnp.asarray(col_ok)[None, None], by_col * LOG2E, MASK_NEG)
    masked = jnp.full((n_heads, GRID_W, GRID_W), MASK_NEG, F32)
    ctx_cols = jnp.zeros((n_heads, GRID_W, n_ctx), F32)
    classes = []
    for i in (0, 1, nt - 1):
        wb = int(np.clip(i - 1, 0, nt - 3))
        qrow = (i * tr + np.arange(tr))[:, None]
        krow = (wb * tr + np.arange(3 * tr))[None, :]
        rs = np.clip(qrow - kr // 2, 0, n_img_rows - kr)
        row_ok = (krow >= rs) & (krow < rs + kr)
        dr = np.clip(krow - qrow + NA_ROWS - 1, 0, 2 * NA_ROWS - 2)
        q_rows = []
        for a in range(tr):
            blocks = [by_col[:, int(dr[a, b])] if row_ok[a, b] else masked for b in range(3 * tr)]
            q_rows.append(jnp.concatenate(blocks + [ctx_cols], axis=-1))
        classes.append(jnp.concatenate(q_rows, axis=1))
    return jnp.stack(classes, axis=0)


def _pair_attention(q_pair, k_pair, v_pair, bias_fn):
    lane = lax.broadcasted_iota(I32, q_pair.shape, 1)
    v_aug = jnp.concatenate([v_pair, jnp.ones(v_pair.shape, BF16)], axis=1)
    out = None
    for hh in range(2):
        mine = (lane >= hh * NA_HEAD_DIM) & (lane < (hh + 1) * NA_HEAD_DIM)
        qm = jnp.where(mine, q_pair, jnp.zeros_like(q_pair))
        s = _dot_nt(qm, k_pair)
        b = bias_fn(hh)
        if b is not None:
            s = s + b
        m = jnp.max(s, axis=-1, keepdims=True)
        o_aug = _dot(jnp.exp2(s - m).astype(BF16), v_aug)
        o = o_aug[:, :LANES] / o_aug[:, LANES:LANES + 1]
        out = o if out is None else jnp.where(mine, o, out)
    return out


def _na_body(q_ref, k0_ref, k1_ref, k2_ref, kc_ref, v0_ref, v1_ref, v2_ref, vc_ref, b_ref, o_ref):
    for j in range(NA_HEADS // 2):
        sl = slice(j * LANES, (j + 1) * LANES)
        q_pair = q_ref[:, sl]
        k_pair = jnp.concatenate([k0_ref[:, sl], k1_ref[:, sl], k2_ref[:, sl], kc_ref[:, sl]], axis=0)
        v_pair = jnp.concatenate([v0_ref[:, sl], v1_ref[:, sl], v2_ref[:, sl], vc_ref[:, sl]], axis=0)
        o = _pair_attention(q_pair, k_pair, v_pair, lambda hh: b_ref[0, 2 * j + hh])
        o_ref[:, sl] = o.astype(o_ref.dtype)


def na_mixer(pb, col_q, col_k, col_v, bias_tab, n_lat, n_ctx, n_batch):
    tm = NA_TILE_ROWS * GRID_W
    assert n_ctx == tm
    w = NA_HEADS * NA_HEAD_DIM
    nt = n_lat // tm
    n_keys = 3 * tm + n_ctx
    cq, ck, cv = col_q // w, col_k // w, col_v // w
    ctx0 = n_batch * nt

    def win(o):
        return lambda b, i: (b * nt + jnp.clip(i - 1, 0, nt - 3) + o)

    def cls(b, i):
        return (jnp.where(i == 0, 0, jnp.where(i == nt - 1, 2, 1)), 0, 0, 0)

    kspecs = [pl.BlockSpec((tm, w), (lambda b, i, f=win(o): (f(b, i), ck))) for o in range(3)]
    vspecs = [pl.BlockSpec((tm, w), (lambda b, i, f=win(o): (f(b, i), cv))) for o in range(3)]
    return pl.pallas_call(
        _na_body,
        grid=(n_batch, nt),
        in_specs=[pl.BlockSpec((tm, w), lambda b, i: (b * nt + i, cq))] + kspecs
        + [pl.BlockSpec((tm, w), lambda b, i: (ctx0 + b, ck))] + vspecs
        + [pl.BlockSpec((tm, w), lambda b, i: (ctx0 + b, cv)),
           pl.BlockSpec((1, NA_HEADS, tm, n_keys), cls)],
        out_specs=pl.BlockSpec((tm, w), lambda b, i: (b * nt + i, 0)),
        out_shape=jax.ShapeDtypeStruct((n_batch * n_lat, w), BF16),
        compiler_params=_cp(("arbitrary", "arbitrary"), VMEM_LIMIT),
        name="na_attention",
    )(pb, pb, pb, pb, pb, pb, pb, pb, pb, bias_tab)


def _ctx_mha_body(q_ref, k_ref, v_ref, o_ref):
    for j in range(NA_HEADS // 2):
        sl = slice(j * LANES, (j + 1) * LANES)
        o = _pair_attention(q_ref[:, sl], k_ref[:, sl], v_ref[:, sl], lambda hh: None)
        o_ref[:, sl] = o.astype(o_ref.dtype)


def na_ctx_attention(pb, col_q, col_k, col_v, n_lat, n_ctx, n_batch):
    w = NA_HEADS * NA_HEAD_DIM
    ctx0 = n_batch * n_lat // n_ctx
    spec = lambda c: pl.BlockSpec((n_ctx, w), lambda b: (ctx0 + b, c // w))
    return pl.pallas_call(
        _ctx_mha_body,
        grid=(n_batch,),
        in_specs=[spec(col_q), spec(col_k), spec(col_v)],
        out_specs=pl.BlockSpec((n_ctx, w), lambda b: (b, 0)),
        out_shape=jax.ShapeDtypeStruct((n_batch * n_ctx, w), BF16),
        compiler_params=_cp(("arbitrary",)),
        name="na_ctx_attention",
    )(pb, pb, pb)


def _mlstm_body(n_batch, *refs):
    n_in = 8 * n_batch
    ins, bias_ref = refs[:n_in], refs[n_in]
    hf_ref, hb_ref, c_ref, n_ref, m_ref = refs[n_in + 1:]

    @pl.when(pl.program_id(0) == 0)
    def _():
        c_ref[...] = jnp.zeros(c_ref.shape, F32)
        n_ref[...] = jnp.zeros(n_ref.shape, F32)
        m_ref[...] = jnp.full(m_ref.shape, NEG_INIT, F32)

    tok = lax.broadcasted_iota(I32, (ML_CHUNK, ML_CHUNK), 0)
    src = lax.broadcasted_iota(I32, (ML_CHUNK, ML_CHUNK), 1)
    masks = (src <= tok, src >= tok)
    gates = []
    for b in range(n_batch):
        for d in range(2):
            g = ins[(b * 2 + d) * 4 + 3][...] + bias_ref[...]
            lf_cum = jnp.dot(masks[d].astype(F32), jax.nn.log_sigmoid(g), precision=HIGHEST,
                             preferred_element_type=F32)
            gates.append((g, lf_cum))
    gates = [(g, lf_cum, g.T, lf_cum.T) for g, lf_cum in gates]
    chains = []
    for b in range(n_batch):
        for d in range(2):
            g, lf_cum, g_t, lf_cum_t = gates[b * 2 + d]
            for h in range(ML_HEADS):
                ci, cf = d * 2 * ML_HEADS + h, d * 2 * ML_HEADS + ML_HEADS + h
                idx = (b * 2 + d) * ML_HEADS + h
                chains.append(dict(
                    b=b, d=d, h=h, idx=idx, bt_col=lf_cum[:, cf:cf + 1], bt_row=lf_cum_t[cf:cf + 1, :],
                    li_col=g[:, ci:ci + 1], li_row=g_t[ci:ci + 1, :], m_prev=m_ref[idx][:, 0:1],
                    c_prev=c_ref[idx], n_prev=n_ref[idx]))
    for c in chains:
        c['dmat'] = jnp.where(masks[c['d']], c['bt_col'] - c['bt_row'] + c['li_row'], -jnp.inf)
        c['inter'] = c['bt_col'] + c['m_prev']
    for c in chains:
        c['mt'] = jnp.maximum(c['inter'], jnp.max(c['dmat'], axis=-1, keepdims=True))
    for c in chains:
        refs_c = ins[(c['b'] * 2 + c['d']) * 4:(c['b'] * 2 + c['d']) * 4 + 3]
        sl = slice(c['h'] * ML_HEAD_DIM, (c['h'] + 1) * ML_HEAD_DIM)
        c['q'], c['k'], c['v'] = (r[:, sl] for r in refs_c)
        c['s'] = _dot_nt(c['q'], c['k']) * jnp.exp(c['dmat'] - c['mt'])
        c['w_inter'] = jnp.exp(c['inter'] - c['mt'])
    for c in chains:
        num = _dot(c['s'].astype(BF16), c['v']) + c['w_inter'] * _dot_nt(c['q'], c['c_prev'].astype(BF16))
        qn = jnp.sum(c['q'].astype(F32) * c['n_prev'], axis=-1, keepdims=True)
        den = jnp.sum(c['s'], axis=-1, keepdims=True) + c['w_inter'] * qn
        h_out = num / jnp.maximum(jnp.abs(den), jnp.exp(-c['mt']))
        h_ref = hb_ref if c['d'] else hf_ref
        h_ref[c['b'], :, c['h'] * ML_HEAD_DIM:(c['h'] + 1) * ML_HEAD_DIM] = h_out
    new_state = []
    for c in chains:
        b_last = c['bt_col'][0:1, :] if c['d'] else c['bt_col'][ML_CHUNK - 1:ML_CHUNK, :]
        g_col = b_last - c['bt_col'] + c['li_col']
        m_new = jnp.maximum(b_last + c['m_prev'], jnp.max(g_col, axis=0, keepdims=True))
        wg = jnp.exp(g_col - m_new)
        decay = jnp.exp(b_last + c['m_prev'] - m_new)
        c_new = decay * c['c_prev'] + _dot_tn((wg * c['v'].astype(F32)).astype(BF16), c['k'])
        n_new = decay * c['n_prev'] + jnp.sum(wg * c['k'].astype(F32), axis=0, keepdims=True)
        new_state.append((c['idx'], c_new, n_new, m_new))
    for idx, c_new, n_new, m_new in new_state:
        c_ref[idx] = c_new
        n_ref[idx] = n_new
        m_ref[idx] = jnp.broadcast_to(m_new, (1, LANES))


def mlstm_mixer(pb, pa, col_q, col_k, col_v, col_g, bias, n_lat, n_ctx, n_batch):
    w = ML_HEADS * ML_HEAD_DIM
    tc = ML_CHUNK
    nl, nc = n_lat // tc, n_ctx // tc

    def fwd_pos(i):
        return jnp.where(i < nc, nl + i, i - nc)

    def bwd_pos(i):
        return jnp.where(i < nc, nl + nc - 1 - i, nl - 1 - (i - nc))

    def row_blk(b, pos):
        return jnp.where(pos < nl, b * nl + pos, n_batch * nl + b * nc + pos - nl)

    def specs(b, pos_fn):
        return [pl.BlockSpec((tc, w), lambda i, c=c: (row_blk(b, pos_fn(i)), c // w)) for c in (col_q, col_k, col_v)] + [
            pl.BlockSpec((tc, LANES), lambda i: (row_blk(b, pos_fn(i)), col_g // LANES))]

    in_specs, operands = [], []
    for b in range(n_batch):
        for pos_fn in (fwd_pos, bwd_pos):
            in_specs += specs(b, pos_fn)
            operands += [pb, pb, pb, pa]
    n_st = 2 * ML_HEADS * n_batch
    out_shape = jax.ShapeDtypeStruct((n_batch, n_lat + n_ctx, w), F32)
    return pl.pallas_call(
        functools.partial(_mlstm_body, n_batch),
        grid=(nl + nc,),
        in_specs=in_specs + [pl.BlockSpec((1, LANES), lambda i: (0, 0))],
        out_specs=[pl.BlockSpec((n_batch, tc, w), lambda i: (0, fwd_pos(i), 0)),
                   pl.BlockSpec((n_batch, tc, w), lambda i: (0, bwd_pos(i), 0))],
        out_shape=[out_shape, out_shape],
        scratch_shapes=[pltpu.VMEM((n_st, ML_HEAD_DIM, ML_HEAD_DIM), F32),
                        pltpu.VMEM((n_st, 1, ML_HEAD_DIM), F32),
                        pltpu.VMEM((n_st, 1, LANES), F32)],
        compiler_params=_cp(("arbitrary",)),
        name="mlstm_chunks",
    )(*operands, bias)


def _merge_body(n_lat_tiles, ya_ref, ybl_ref, ybc_ref, hf_ref, hb_ref, o_ref, ydl_ref, ydc_ref, gate_ref, x_ref,
                mod_ref, mlg_ref, wbr_ref, wout_ref, g2_ref, wr_ref, xo_ref, h2_ref, st_ref):
    d = x_ref.shape[1]
    is_ctx = pl.program_id(0) >= n_lat_tiles
    yb = jnp.where(is_ctx, ybc_ref[...], ybl_ref[...])
    yd = jnp.where(is_ctx, ydc_ref[...], ydl_ref[...])
    hs = hf_ref[...] + hb_ref[...]
    segs = []
    for h in range(ML_HEADS):
        seg = hs[:, h * ML_HEAD_DIM:(h + 1) * ML_HEAD_DIM]
        segs.append(seg * lax.rsqrt(jnp.mean(seg * seg, axis=-1, keepdims=True) + EPS))
    ym = jnp.concatenate(segs, axis=1) * mlg_ref[...] * jax.nn.sigmoid(o_ref[...].astype(F32))
    ys = (ya_ref[...], yb, ym.astype(BF16), yd)
    merged = None
    for i in range(N_BRANCHES):
        term = gate_ref[:, i * d:(i + 1) * d].astype(F32) * _dot(ys[i], wbr_ref[i])
        merged = term if merged is None else merged + term
    y = _dot(merged.astype(BF16), wout_ref[...])
    mod = mod_ref[0]
    x_new = x_ref[...] + mod[2:3] * y
    xo_ref[...] = x_new
    h2 = _rms_mod(x_new, g2_ref[...], mod[3:4], mod[4:5])
    h2_ref[...] = h2
    st_ref[...] = jax.nn.sigmoid(_dot_nt(wr_ref[...], h2.astype(BF16)))


def merge_layer(ya, yb, yd, hf, hb, pb, col_o, gate, x_all, modtab, mlg, wbr, wout, g2, wr_t,
                n_rows, n_lat_rows, n_batch):
    d = x_all.shape[1]
    tm = ROW_TILE
    w = BRANCH_WIDTH
    per_b = n_lat_rows // n_batch // tm
    row = lambda i: (i, 0)
    const2 = lambda i: (0, 0)
    n_lat_tiles = n_lat_rows // tm
    lat_row = lambda i: (jnp.minimum(i, n_lat_tiles - 1), 0)
    ctx_row = lambda i: (jnp.clip(i - n_lat_tiles, 0, n_batch - 1), 0)

    def seq(i):
        lat = i < n_lat_tiles
        return (jnp.where(lat, i // per_b, i - n_lat_tiles), jnp.where(lat, i % per_b, per_b), 0)

    (yb_lat, yb_ctx), (yd_lat, yd_ctx) = yb, yd
    if yb_ctx is None:
        yb_ctx, yd_ctx = yb_lat, yd_lat
    return pl.pallas_call(
        functools.partial(_merge_body, n_lat_tiles),
        grid=(n_rows // tm,),
        in_specs=[pl.BlockSpec((tm, w), row), pl.BlockSpec((tm, w), lat_row), pl.BlockSpec((tm, w), ctx_row),
                  pl.BlockSpec((None, tm, w), seq), pl.BlockSpec((None, tm, w), seq),
                  pl.BlockSpec((tm, w), lambda i: (i, col_o // w)),
                  pl.BlockSpec((tm, w), lat_row), pl.BlockSpec((tm, w), ctx_row),
                  pl.BlockSpec((tm, N_BRANCHES * d), row),
                  pl.BlockSpec((tm, d), row),
                  pl.BlockSpec((1, N_MOD, d), lambda i: (jnp.minimum(i // per_b, n_batch), 0, 0)),
                  pl.BlockSpec((1, w), const2),
                  pl.BlockSpec((N_BRANCHES, w, d), lambda i: (0, 0, 0)),
                  pl.BlockSpec((d, d), const2), pl.BlockSpec((1, d), const2),
                  pl.BlockSpec((LANES, d), const2)],
        out_specs=[pl.BlockSpec((tm, d), row), pl.BlockSpec((tm, d), row),
                   pl.BlockSpec((LANES, tm), lambda i: (0, i))],
        out_shape=[jax.ShapeDtypeStruct((n_rows, d), F32), jax.ShapeDtypeStruct((n_rows, d), F32),
                   jax.ShapeDtypeStruct((LANES, n_rows), F32)],
        compiler_params=_cp(("arbitrary",), VMEM_LIMIT),
        name="merge_layer",
    )(ya, yb_lat, yb_ctx, hf, hb, pb, yd_lat, yd_ctx, gate, x_all, modtab, mlg, wbr, wout, g2, wr_t)


def _router_body(s_ref, b_ref, e_ref, w_ref, rank_ref, cnt_ref, base_sc):
    @pl.when(pl.program_id(0) == 0)
    def _():
        base_sc[...] = jnp.zeros(base_sc.shape, F32)

    tm = s_ref.shape[1]
    s = s_ref[0:N_EXPERTS, :]
    sel = s + b_ref[0:N_EXPERTS, :]
    row = lambda a, e: a[e:e + 1, :]
    best, grp = None, None
    for g in range(N_EXPERT_GROUPS):
        v = [row(sel, EXPERTS_PER_GROUP * g + k) for k in range(EXPERTS_PER_GROUP)]
        gs = None
        for a in range(EXPERTS_PER_GROUP):
            for c in range(a + 1, EXPERTS_PER_GROUP):
                gs = v[a] + v[c] if gs is None else jnp.maximum(gs, v[a] + v[c])
        if best is None:
            best, grp = gs, jnp.zeros((1, tm), I32)
        else:
            better = gs > best
            grp = jnp.where(better, g, grp)
            best = jnp.where(better, gs, best)
    vals, affs = [], []
    for k in range(EXPERTS_PER_GROUP):
        vk, sk = row(sel, k), row(s, k)
        for g in range(1, N_EXPERT_GROUPS):
            hit = grp == g
            vk = jnp.where(hit, row(sel, EXPERTS_PER_GROUP * g + k), vk)
            sk = jnp.where(hit, row(s, EXPERTS_PER_GROUP * g + k), sk)
        vals.append(vk)
        affs.append(sk)
    i1, b1, w1 = jnp.zeros((1, tm), I32), vals[0], affs[0]
    for k in range(1, EXPERTS_PER_GROUP):
        better = vals[k] > b1
        i1 = jnp.where(better, k, i1)
        w1 = jnp.where(better, affs[k], w1)
        b1 = jnp.where(better, vals[k], b1)
    i2 = jnp.zeros((1, tm), I32)
    b2 = jnp.full((1, tm), -jnp.inf, F32)
    w2 = jnp.zeros((1, tm), F32)
    for k in range(EXPERTS_PER_GROUP):
        cand = (i1 != k) & (vals[k] > b2)
        i2 = jnp.where(cand, k, i2)
        w2 = jnp.where(cand, affs[k], w2)
        b2 = jnp.where(cand, vals[k], b2)
    e1 = grp * EXPERTS_PER_GROUP + i1
    e2 = grp * EXPERTS_PER_GROUP + i2
    tot = w1 + w2
    e_ref[...] = jnp.concatenate([e1, e2], axis=0)
    wpad = jnp.concatenate([w1 / tot, w2 / tot, jnp.zeros((6, tm), F32)], axis=0)
    w_ref[...] = wpad.T
    ids = lax.broadcasted_iota(I32, (N_EXPERTS, tm), 0)
    oh1 = (ids == e1).astype(F32)
    oh2 = (ids == e2).astype(F32)
    oh = oh1 + oh2
    before = (lax.broadcasted_iota(I32, (tm, tm), 0) < lax.broadcasted_iota(I32, (tm, tm), 1)).astype(BF16)
    prior = _dot(oh.astype(BF16), before) + base_sc[...]
    r1 = jnp.sum(oh1 * prior, axis=0, keepdims=True)
    r2 = jnp.sum(oh2 * prior, axis=0, keepdims=True)
    rank_ref[...] = jnp.concatenate([r1, r2], axis=0).astype(I32)
    base = base_sc[...] + jnp.sum(oh, axis=1, keepdims=True)
    base_sc[...] = base
    cnt_ref[...] = jnp.broadcast_to(base, cnt_ref.shape).astype(I32)


def route(s_t, b_router):
    n_rows = s_t.shape[1]
    tm = ROW_TILE
    b_col = jnp.pad(b_router.astype(F32), (0, LANES - N_EXPERTS)).reshape(LANES, 1)
    return pl.pallas_call(
        _router_body,
        grid=(n_rows // tm,),
        in_specs=[pl.BlockSpec((LANES, tm), lambda i: (0, i)), pl.BlockSpec((LANES, 1), lambda i: (0, 0))],
        out_specs=[pl.BlockSpec((TOP_K, tm), lambda i: (0, i)), pl.BlockSpec((tm, 8), lambda i: (i, 0)),
                   pl.BlockSpec((TOP_K, tm), lambda i: (0, i)), pl.BlockSpec((N_EXPERTS, LANES), lambda i: (0, 0))],
        out_shape=[jax.ShapeDtypeStruct((TOP_K, n_rows), I32), jax.ShapeDtypeStruct((n_rows, 8), F32),
                   jax.ShapeDtypeStruct((TOP_K, n_rows), I32), jax.ShapeDtypeStruct((N_EXPERTS, LANES), I32)],
        scratch_shapes=[pltpu.VMEM((N_EXPERTS, 1), F32)],
        compiler_params=_cp(("arbitrary",)),
        name="moe_router",
    )(s_t, b_col)


def _row_copy(src_ref, src_row, dst_ref, dst_row, sem):
    return pltpu.make_async_copy(src_ref.at[pl.ds(src_row, 1), :], dst_ref.at[pl.ds(dst_row, 1), :], sem)


def _dispatch_body(dest_ref, h_ref, buf_in_ref, buf_ref, sem):
    del buf_in_ref
    tm = h_ref.shape[0]

    def issue(t, carry):
        for k in range(TOP_K):
            _row_copy(h_ref, t, buf_ref, dest_ref[0, k, t], sem).start(priority=k)
        return carry

    lax.fori_loop(0, tm, issue, 0, unroll=8)
    for k in range(TOP_K):
        pltpu.make_async_copy(h_ref, buf_ref.at[pl.ds(0, tm), :], sem).wait()


def moe_dispatch(h2, dest3, buf0):
    n_rows, d = h2.shape
    tm = ROW_TILE
    return pl.pallas_call(
        _dispatch_body,
        grid=(n_rows // tm,),
        in_specs=[pl.BlockSpec((1, TOP_K, tm), lambda i: (i, 0, 0), memory_space=pltpu.SMEM),
                  pl.BlockSpec((tm, d), lambda i: (i, 0)),
                  pl.BlockSpec(memory_space=pl.ANY)],
        out_specs=pl.BlockSpec(memory_space=pl.ANY),
        out_shape=jax.ShapeDtypeStruct(buf0.shape, buf0.dtype),
        scratch_shapes=[pltpu.SemaphoreType.DMA(())],
        input_output_aliases={2: 0},
        compiler_params=_cp(("arbitrary",)),
        name="moe_dispatch",
    )(dest3, h2, buf0)


def _expert_body(be_ref, nu_ref, x_ref, w1_ref, w3_ref, w2_ref, o_ref, w1_sc, w3_sc, w2_sc):
    i = pl.program_id(0)
    changed = jnp.logical_or(i == 0, be_ref[i] != be_ref[jnp.maximum(i - 1, 0)])
    used = i < nu_ref[0]

    @pl.when(jnp.logical_and(changed, used))
    def _():
        w1_sc[...] = w1_ref[0].astype(BF16)
        w3_sc[...] = w3_ref[0].astype(BF16)
        w2_sc[...] = w2_ref[0].astype(BF16)

    @pl.when(used)
    def _():
        x = x_ref[...].astype(BF16)
        a = _dot(x, w1_sc[...])
        mid = (a * jax.nn.sigmoid(a)) * _dot(x, w3_sc[...])
        o_ref[...] = _dot(mid.astype(BF16), w2_sc[...])

    @pl.when(jnp.logical_not(used))
    def _():
        o_ref[...] = jnp.zeros(o_ref.shape, o_ref.dtype)


def moe_experts(buf, blk_expert, n_used, w1, w3, w2, layer, blk):
    n_slots, d = buf.shape
    de = w1.shape[3]
    grid_spec = pltpu.PrefetchScalarGridSpec(
        num_scalar_prefetch=2,
        grid=(n_slots // blk,),
        in_specs=[pl.BlockSpec((blk, d), lambda i, be, nu: (i, 0)),
                  pl.BlockSpec((None, 1, d, de), lambda i, be, nu: (layer, be[i], 0, 0)),
                  pl.BlockSpec((None, 1, d, de), lambda i, be, nu: (layer, be[i], 0, 0)),
                  pl.BlockSpec((None, 1, de, d), lambda i, be, nu: (layer, be[i], 0, 0))],
        out_specs=pl.BlockSpec((blk, d), lambda i, be, nu: (i, 0)),
        scratch_shapes=[pltpu.VMEM((d, de), BF16), pltpu.VMEM((d, de), BF16), pltpu.VMEM((de, d), BF16)],
    )
    return pl.pallas_call(
        _expert_body,
        grid_spec=grid_spec,
        out_shape=jax.ShapeDtypeStruct((n_slots, d), F32),
        compiler_params=_cp(("arbitrary",), VMEM_LIMIT),
        name="moe_experts",
    )(blk_expert, n_used, buf, w1, w3, w2)


def _combine_body(final, dest_ref, x_ref, w_ref, mod_ref, gf_ref, y_hbm, o_ref, y0_sc, y1_sc, sem):
    tm = x_ref.shape[0]
    bufs = (y0_sc, y1_sc)

    def issue(t, carry):
        for k in range(TOP_K):
            _row_copy(y_hbm, dest_ref[0, k, t], bufs[k], t, sem).start(priority=k)
        return carry

    lax.fori_loop(0, tm, issue, 0, unroll=8)
    for k in range(TOP_K):
        pltpu.make_async_copy(y_hbm.at[pl.ds(0, tm), :], bufs[k], sem).wait()
    w = w_ref[...]
    f = w[:, 0:1] * y0_sc[...] + w[:, 1:2] * y1_sc[...]
    x_new = x_ref[...] + mod_ref[0][5:6] * f
    if final:
        ms = jnp.mean(x_new * x_new, axis=-1, keepdims=True)
        x_new = x_new * lax.rsqrt(ms + EPS) * gf_ref[...]
    o_ref[...] = x_new


def moe_combine(dest3, x_rows, wts, modtab, g_final, y_slots, n_lat_rows, n_batch, final):
    n_rows, d = x_rows.shape
    tm = ROW_TILE
    per_b = n_lat_rows // n_batch // tm
    return pl.pallas_call(
        functools.partial(_combine_body, final),
        grid=(n_rows // tm,),
        in_specs=[pl.BlockSpec((1, TOP_K, tm), lambda i: (i, 0, 0), memory_space=pltpu.SMEM),
                  pl.BlockSpec((tm, d), lambda i: (i, 0)),
                  pl.BlockSpec((tm, 8), lambda i: (i, 0)),
                  pl.BlockSpec((1, N_MOD, d), lambda i: (jnp.minimum(i // per_b, n_batch), 0, 0)),
                  pl.BlockSpec((1, d), lambda i: (0, 0)),
                  pl.BlockSpec(memory_space=pl.ANY)],
        out_specs=pl.BlockSpec((tm, d), lambda i: (i, 0)),
        out_shape=jax.ShapeDtypeStruct((n_rows, d), F32),
        scratch_shapes=[pltpu.VMEM((tm, d), F32), pltpu.VMEM((tm, d), F32), pltpu.SemaphoreType.DMA(())],
        compiler_params=_cp(("arbitrary",)),
        name="moe_combine",
    )(dest3, x_rows, wts, modtab, g_final.reshape(1, d), y_slots)


def moe_layer(x_rows, h2, s_t, b_router, w1, w3, w2, layer, modtab, g_final, n_lat_rows, n_batch, final):
    n_rows, d = h2.shape
    blk = 2 * MOE_BLOCK
    experts, wts, rank, counts = route(s_t, b_router)
    cnt = counts[:, 0]
    padded = (cnt + blk - 1) // blk * blk
    pend = jnp.cumsum(padded)
    pstart = pend - padded
    hit = experts[..., None] == jnp.arange(N_EXPERTS, dtype=I32)
    dest = jnp.sum(jnp.where(hit, pstart.astype(I32), 0), axis=-1) + rank
    n_blocks = -(-(n_rows * TOP_K) // blk) + N_EXPERTS
    blk_start = jnp.arange(n_blocks, dtype=I32) * blk
    blk_expert = jnp.minimum(jnp.sum((pend[None, :] <= blk_start[:, None]).astype(I32), axis=1), N_EXPERTS - 1)
    dest3 = dest.reshape(TOP_K, n_rows // ROW_TILE, ROW_TILE).transpose(1, 0, 2)
    buf = moe_dispatch(h2, dest3, jnp.zeros((n_blocks * blk, d), F32))
    n_used = (pend[-1:] // blk).astype(I32)
    y_slots = moe_experts(buf, blk_expert, n_used, w1, w3, w2, layer, blk)
    return moe_combine(dest3, x_rows, wts, modtab, g_final, y_slots, n_lat_rows, n_batch, final)


_COL = dict(na_q=0, na_k=512, na_v=1024, ml_q=1536, ml_k=2048, ml_v=2560, ml_o=3072,
            gq_q=3584, gq_k=4096, gq_v=4224)


def _split_w_in(w_in):
    sizes = (BRANCH_WIDTH,) * 8 + (4 * ML_HEADS, BRANCH_WIDTH, GQ_KV_HEADS * GQ_HEAD_DIM, GQ_KV_HEADS * GQ_HEAD_DIM)
    idx = np.cumsum(sizes)[:-1].tolist()
    (s5_u, na_q, na_k, na_v, ml_q, ml_k, ml_v, ml_o, ml_gt, gq_q, gq_k, gq_v) = jnp.split(w_in, idx, axis=-1)
    wa = jnp.pad(ml_gt, ((0, 0), (0, LANES - 4 * ML_HEADS)))
    wb = jnp.concatenate([na_q * (NA_HEAD_DIM ** -0.5 * LOG2E), na_k, na_v, ml_q, ml_k * (ML_HEAD_DIM ** -0.5),
                          ml_v, ml_o, gq_q, gq_k, gq_v], axis=1)
    return s5_u.T.astype(BF16), wa.astype(BF16), wb.astype(BF16)


def kernel(x, c, ctx, c_ctx, w_mod, b_mod, g_norm1, g_norm2, w_in, s5_lam_re, s5_lam_im, s5_log_dt, s5_b_re,
           s5_b_im, s5_c_re, s5_c_im, s5_d, s5_w_glu, s5_b_glu, na_rpb, ml_b_gates, ml_norm, gq_qnorm, gq_knorm,
           w_branch, w_gate, b_gate, w_out, w_router, b_router, moe_w1, moe_w3, moe_w2, g_final):
    b, n_lat, dm = x.shape
    n_ctx = ctx.shape[1]
    depth = w_in.shape[0]
    bn, bc = b * n_lat, b * n_ctx
    x_all = jnp.concatenate([x.reshape(bn, dm), ctx.reshape(bc, dm)], axis=0).astype(F32)
    c_all = jnp.concatenate([c.astype(F32), c_ctx.astype(F32)[None], jnp.zeros((8 - b - 1, dm), F32)], axis=0)
    cs_tab = rope_tables(n_lat)
    wr_t = jnp.pad(w_router.astype(BF16).T, ((0, LANES - N_EXPERTS), (0, 0)))
    out = None
    for l in range(depth):
        last = l == depth - 1
        with_ctx = not last
        modtab = mod_vectors(c_all, w_mod, b_mod, l)[:b + 1].reshape(b + 1, N_MOD, dm)
        w_u_t, wa, wb = _split_w_in(w_in[l])
        pa, pb, gate, h_all = in_projection(x_all, g_norm1[l], modtab, wa, wb, w_gate[l].astype(BF16), b_gate[l],
                                            bn, b)

        tables = s5_tables(s5_lam_re[l], s5_lam_im[l], s5_log_dt[l], s5_b_re[l], s5_b_im[l],
                           s5_c_re[l], s5_c_im[l], s5_d[l])
        ya = s5_mixer(h_all, w_u_t, tables, s5_w_glu[l].astype(BF16), s5_b_glu[l].astype(F32).reshape(1, -1),
                      n_lat, n_ctx, b)

        bias_tab = na_bias_tables(na_rpb[l], n_lat // GRID_W, n_ctx)
        n_rows = bn + bc if with_ctx else bn
        na_cols = (_COL['na_q'], _COL['na_k'], _COL['na_v'])
        yb = (na_mixer(pb, *na_cols, bias_tab, n_lat, n_ctx, b),
              na_ctx_attention(pb, *na_cols, n_lat, n_ctx, b) if with_ctx else None)

        ml_bias = jnp.pad(ml_b_gates[l].astype(F32), (0, LANES - 4 * ML_HEADS)).reshape(1, LANES)
        hf, hb = mlstm_mixer(pb, pa, _COL['ml_q'], _COL['ml_k'], _COL['ml_v'], 0, ml_bias, n_lat, n_ctx, b)

        yd = gq_mixer(pb, _COL['gq_q'], _COL['gq_k'], _COL['gq_v'], cs_tab, gq_qnorm[l], gq_knorm[l],
                      n_lat, n_ctx, b, with_ctx)
        x_mid, h2, s_t = merge_layer(
            ya, yb, yd, hf, hb, pb, _COL['ml_o'], gate, x_all, modtab, ml_norm[l].astype(F32).reshape(1, -1),
            w_branch[l].astype(BF16), w_out[l].astype(BF16), g_norm2[l].astype(F32).reshape(1, -1), wr_t,
            n_rows, bn, b)
        x_next = moe_layer(x_mid, h2, s_t, b_router, moe_w1, moe_w3, moe_w2, l, modtab, g_final, bn, b, last)
        if last:
            out = x_next.reshape(b, n_lat, dm).astype(x.dtype)
        else:
            x_all = x_next
    return out
```

```python
import functools
import math

import numpy as np
import jax
import jax.numpy as jnp
from jax import lax
from jax.experimental import pallas as pl
from jax.experimental.pallas import tpu as pltpu

F32 = jnp.float32
BF16 = jnp.bfloat16
I32 = jnp.int32

GRID_W = 64
N_MOD = 6
BRANCH_WIDTH = 512
N_BRANCHES = 4
S5_GROUP = 16
S5_GROUPS = BRANCH_WIDTH // S5_GROUP
S5_STATE = 64
NA_HEADS = 8
NA_HEAD_DIM = 64
NA_ROWS = 8
NA_COLS = 16
ML_HEADS = 4
ML_HEAD_DIM = 128
ML_CHUNK = 128
GQ_HEADS = 8
GQ_KV_HEADS = 2
GQ_HEAD_DIM = 64
ROPE_THETA = 10000.0
N_EXPERTS = 32
N_EXPERT_GROUPS = 8
EXPERTS_PER_GROUP = 4
TOP_K = 2
D_EXPERT = 512
MOE_BLOCK = 128
EPS = 1e-6
NEG_INIT = -1e30
MASK_NEG = -1e30
LOG2E = 1.4426950408889634

LANES = 128
ROW_TILE = 256
S5_CHUNK = 16
NA_TILE_ROWS = 4
VMEM_LIMIT = 56 * 1024 * 1024

HIGHEST = lax.Precision.HIGHEST


def _cp(sem, vmem=None):
    return pltpu.CompilerParams(dimension_semantics=sem, vmem_limit_bytes=vmem)


def _dot(a, b):
    return jnp.dot(a, b, preferred_element_type=F32)


def _dot_nt(a, b):
    return lax.dot_general(a, b, (((1,), (1,)), ((), ())), preferred_element_type=F32)


def _dot_tn(a, b):
    return lax.dot_general(a, b, (((0,), (0,)), ((), ())), preferred_element_type=F32)


def _rms_mod(x, g, shift, scale):
    ms = jnp.mean(x * x, axis=-1, keepdims=True)
    y = x * lax.rsqrt(ms + EPS) * g
    return y * (1.0 + scale) + shift


def _round_up(n, m):
    return -(-n // m) * m


def _mod_body(c_ref, w_ref, b_ref, o_ref):
    c = c_ref[...]
    a = (c * jax.nn.sigmoid(c)).astype(BF16)
    o_ref[...] = _dot(a, w_ref[...].astype(BF16)) + b_ref[...]


def mod_vectors(c_all, w_mod, b_mod, layer):
    d = c_all.shape[1]
    depth = w_mod.shape[0]
    return pl.pallas_call(
        _mod_body,
        grid=(N_MOD,),
        in_specs=[pl.BlockSpec((8, d), lambda j: (0, 0)),
                  pl.BlockSpec((None, d, d), lambda j: (layer, 0, j)),
                  pl.BlockSpec((None, 1, d), lambda j: (layer, 0, j))],
        out_specs=pl.BlockSpec((8, d), lambda j: (0, j)),
        out_shape=jax.ShapeDtypeStruct((8, N_MOD * d), F32),
        compiler_params=_cp(("arbitrary",)),
        name="mod_vectors",
    )(c_all, w_mod, b_mod.reshape(depth, 1, -1))


def _inproj_body(x_ref, g_ref, mod_ref, wa_ref, wb_ref, wg_ref, bg_ref, oa_ref, ob_ref, og_ref, oh_ref):
    mod = mod_ref[0]
    h = _rms_mod(x_ref[...], g_ref[...], mod[0:1], mod[1:2]).astype(BF16)
    oh_ref[...] = h
    oa_ref[...] = _dot(h, wa_ref[...])
    ob_ref[...] = _dot(h, wb_ref[...]).astype(BF16)
    og_ref[...] = jax.nn.sigmoid(_dot(h, wg_ref[...]) + bg_ref[...]).astype(BF16)


def in_projection(x_all, g, modtab, wa, wb, wg, bg, n_lat_rows, n_batch):
    r, d = x_all.shape
    tm = ROW_TILE
    per_b = n_lat_rows // n_batch // tm

    def mod_idx(i):
        return (jnp.minimum(i // per_b, n_batch), 0, 0)

    const = lambda i: (0, 0)
    return pl.pallas_call(
        _inproj_body,
        grid=(r // tm,),
        in_specs=[pl.BlockSpec((tm, d), lambda i: (i, 0)),
                  pl.BlockSpec((1, d), const),
                  pl.BlockSpec((1, N_MOD, d), mod_idx),
                  pl.BlockSpec(wa.shape, const, pipeline_mode=pl.Buffered(1)),
                  pl.BlockSpec(wb.shape, const, pipeline_mode=pl.Buffered(1)),
                  pl.BlockSpec(wg.shape, const, pipeline_mode=pl.Buffered(1)),
                  pl.BlockSpec((1, wg.shape[1]), const)],
        out_specs=[pl.BlockSpec((tm, wa.shape[1]), lambda i: (i, 0)),
                   pl.BlockSpec((tm, wb.shape[1]), lambda i: (i, 0)),
                   pl.BlockSpec((tm, wg.shape[1]), lambda i: (i, 0)),
                   pl.BlockSpec((tm, d), lambda i: (i, 0))],
        out_shape=[jax.ShapeDtypeStruct((r, wa.shape[1]), F32),
                   jax.ShapeDtypeStruct((r, wb.shape[1]), BF16),
                   jax.ShapeDtypeStruct((r, wg.shape[1]), BF16),
                   jax.ShapeDtypeStruct((r, d), BF16)],
        compiler_params=_cp(("arbitrary",), VMEM_LIMIT),
        name="in_projection",
    )(x_all, g.reshape(1, d), modtab, wa, wb, wg, bg.reshape(1, -1))


def s5_tables(lam_re, lam_im, log_dt, b_re, b_im, c_re, c_im, d_skip):
    ell, g_n, p_n, c_n = S5_CHUNK, S5_GROUPS, S5_STATE, S5_GROUP
    lam_re, lam_im = lam_re.astype(F32), lam_im.astype(F32)
    b_re, b_im, c_re, c_im = (t.astype(F32) for t in (b_re, b_im, c_re, c_im))
    dt = jnp.exp(log_dt.astype(F32))[..., None]
    mag = jnp.exp(lam_re * dt)
    a_re = mag * jnp.cos(lam_im * dt)
    a_im = mag * jnp.sin(lam_im * dt)
    den = lam_re * lam_re + lam_im * lam_im
    nr = a_re - 1.0
    f_re = (nr * lam_re + a_im * lam_im) / den
    f_im = (a_im * lam_re - nr * lam_im) / den
    bb_re = f_re[..., None] * b_re - f_im[..., None] * b_im
    bb_im = f_re[..., None] * b_im + f_im[..., None] * b_re
    k = jnp.arange(ell + 1, dtype=F32)
    pmag = jnp.exp((lam_re * dt)[..., None] * k)
    ang = (lam_im * dt)[..., None] * k
    pr, pi = pmag * jnp.cos(ang), pmag * jnp.sin(ang)
    ab_re = pr[..., None] * bb_re[:, :, :, None, :] - pi[..., None] * bb_im[:, :, :, None, :]
    ab_im = pr[..., None] * bb_im[:, :, :, None, :] + pi[..., None] * bb_re[:, :, :, None, :]
    flat = lambda t: t.reshape(2 * g_n, p_n, (ell + 1) * c_n)
    kk = (jnp.einsum('bcp,bpn->bcn', c_re.reshape(2 * g_n, c_n, p_n), flat(ab_re), precision=HIGHEST)
          - jnp.einsum('bcp,bpn->bcn', c_im.reshape(2 * g_n, c_n, p_n), flat(ab_im), precision=HIGHEST))
    kk = kk.reshape(2, g_n, c_n, ell + 1, c_n)
    centre = kk[0][:, :, 0] + kk[1][:, :, 0] + d_skip.astype(F32).reshape(g_n, c_n, 1) * jnp.eye(c_n, dtype=F32)
    w = jnp.concatenate([kk[0][:, :, ell - 1:0:-1], centre[:, :, None], kk[1][:, :, 1:ell]], axis=2)
    wf = w.reshape(g_n, c_n, (2 * ell - 1) * c_n)
    toe = jnp.stack([wf[:, :, (ell - 1 - t) * c_n:(2 * ell - 1 - t) * c_n] for t in range(ell)], axis=1)
    tsum_t = toe.astype(BF16).reshape(g_n, ell * c_n, ell * c_n)

    parity = [jnp.asarray(np.arange(g_n) % 2 == q, F32) for q in range(2)]
    parts = []
    for d in range(2):
        for ab in (ab_re, ab_im):
            sel = ab[d][:, :, :ell]
            if d == 0:
                sel = sel[:, :, ::-1]
            sel = sel.reshape(g_n, p_n, ell * c_n)
            parts += [sel * parity[q][:, None, None] for q in range(2)]
    mend_t = jnp.stack(parts, axis=1).reshape(g_n, 8 * p_n, ell * c_n)

    rows = []
    for d in range(2):
        prk, pik = pr[d][:, :, 1:ell + 1], pi[d][:, :, 1:ell + 1]
        if d == 1:
            prk, pik = prk[:, :, ::-1], pik[:, :, ::-1]
        prk = prk.transpose(0, 2, 1)[:, :, None, :]
        pik = pik.transpose(0, 2, 1)[:, :, None, :]
        cr, ci = c_re[d][:, None], c_im[d][:, None]
        for part in (cr * prk - ci * pik, -cr * pik - ci * prk):
            rows += [part * parity[q][:, None, None, None] for q in range(2)]
    wst_t = jnp.concatenate(rows, axis=-1).reshape(g_n, ell * c_n, 8 * p_n)

    al = jnp.stack([pr[0][:, :, ell], pi[0][:, :, ell], pr[1][:, :, ell], pi[1][:, :, ell]], axis=1)
    a_chunk = al.reshape(g_n // 2, 2, 4, p_n).transpose(0, 2, 1, 3).reshape(g_n // 2, 8 * p_n)
    return tsum_t.astype(BF16), mend_t.astype(BF16), wst_t.astype(BF16), a_chunk


def _s5_proj_body(hl_ref, hc_ref, w_ref, o_ref):
    n_lat = hl_ref.shape[0]
    n_pad = o_ref.shape[1] - LANES
    hl = hl_ref[...]
    if n_pad > n_lat:
        hl = jnp.concatenate([hl, jnp.zeros((n_pad - n_lat, hl.shape[1]), BF16)], axis=0)
    o_ref[:, 0:n_pad] = _dot_nt(w_ref[...], hl).astype(BF16)
    hc = hc_ref[...]
    hc = jnp.concatenate([hc, jnp.zeros((LANES - hc.shape[0], hc.shape[1]), BF16)], axis=0)
    o_ref[:, n_pad:] = _dot_nt(w_ref[...], hc).astype(BF16)


def _s5_end_body(u_ref, m_ref, o_ref):
    width = m_ref.shape[2]
    acc = None
    for q in range(2):
        u = u_ref[:, q].reshape(width, u_ref.shape[3])
        term = _dot(m_ref[q], u)
        acc = term if acc is None else acc + term
    o_ref[...] = acc.T


def _s5_scan_body(n_batch, per_b, n_ctx_chunks, lat_pad, e_ref, a_ref, o_ref):
    n_pairs = e_ref.shape[0]
    o_ref[...] = jnp.zeros(o_ref.shape, F32)
    sub = 8
    ctx_groups = n_ctx_chunks // sub
    n_groups = (per_b + n_ctx_chunks) // sub
    coef = [[a_ref[q, :, j * LANES:(j + 1) * LANES] for j in range(4)] for q in range(n_pairs)]

    def group(g, carry):
        new = list(carry)
        for b in range(n_batch):
            ctx = g < ctx_groups
            up = jnp.where(ctx, lat_pad + b * n_ctx_chunks + g * sub, b * per_b + (g - ctx_groups) * sub)
            down = jnp.where(ctx, lat_pad + (b + 1) * n_ctx_chunks - (g + 1) * sub,
                             (b + 1) * per_b - (g - ctx_groups + 1) * sub)
            for q in range(n_pairs):
                for d, base in enumerate((up, down)):
                    base = pl.multiple_of(base, sub)
                    lanes = slice(2 * d * LANES, (2 * d + 2) * LANES)
                    e = e_ref[q, pl.ds(base, sub), lanes]
                    k = ((b * n_pairs + q) * 2 + d) * 2
                    sr, si = new[k], new[k + 1]
                    ar, ai = coef[q][2 * d], coef[q][2 * d + 1]
                    rows = [None] * sub
                    for step in range(sub):
                        r = step if d == 0 else sub - 1 - step
                        rows[r] = jnp.concatenate([sr, si], axis=1)
                        er, ei = e[r:r + 1, :LANES], e[r:r + 1, LANES:]
                        sr, si = ar * sr - ai * si + er, ar * si + ai * sr + ei
                    o_ref[q, pl.ds(base, sub), lanes] = jnp.concatenate(rows, axis=0)
                    new[k], new[k + 1] = sr, si
        return tuple(new)

    z = jnp.zeros((1, LANES), F32)
    lax.fori_loop(0, n_groups, group, tuple(z for _ in range(n_batch * n_pairs * 4)))


def _s5_out_body(u_ref, t_ref, s_ref, w_ref, o_ref):
    width = t_ref.shape[0]
    u = u_ref[...].reshape(width, u_ref.shape[2])
    y = _dot(t_ref[...], u) + _dot_nt(w_ref[...], s_ref[...].astype(BF16))
    o_ref[...] = y.reshape(o_ref.shape)


def _s5_glu_body(n_lat_chunks, n_ctx_chunks, lat_pad, y_ref, w_ref, b_ref, o_ref):
    y = y_ref[...].T
    if lat_pad == n_lat_chunks:
        y = y[:n_lat_chunks + n_ctx_chunks]
    else:
        y = jnp.concatenate([y[:n_lat_chunks], y[lat_pad:lat_pad + n_ctx_chunks]], axis=0)
    z = jax.nn.gelu(y)
    o_ref[...] = (z * jax.nn.sigmoid(_dot(z.astype(BF16), w_ref[...]) + b_ref[...])).astype(o_ref.dtype)


def s5_mixer(h_all, w_u_t, tables, w_glu, b_glu, n_lat, n_ctx, n_batch):
    tsum_t, mend_t, wst_t, a_chunk = tables
    r, d = h_all.shape
    h2 = h_all.reshape(r // S5_CHUNK, S5_CHUNK * d)
    ell, g_n, c_n = S5_CHUNK, S5_GROUPS, S5_GROUP
    width = ell * c_n
    sw = 8 * S5_STATE
    bw = g_n * c_n
    n_lat_chunks = n_batch * n_lat // ell
    n_ctx_chunks = n_batch * n_ctx // ell
    assert n_lat_chunks % n_ctx_chunks == 0 and n_ctx_chunks % 16 == 0 and n_ctx_chunks <= LANES
    lat_pad = _round_up(n_lat_chunks, LANES)
    nch = lat_pad + LANES
    const = lambda t: (0, 0)
    u_t = pl.pallas_call(
        _s5_proj_body,
        grid=(ell,),
        in_specs=[pl.BlockSpec((n_lat_chunks, d), lambda t: (0, t)),
                  pl.BlockSpec((n_ctx_chunks, d), lambda t: (n_lat_chunks // n_ctx_chunks, t)),
                  pl.BlockSpec((bw, d), const)],
        out_specs=pl.BlockSpec((None, bw, nch), lambda t: (t, 0, 0)),
        out_shape=jax.ShapeDtypeStruct((ell, bw, nch), BF16),
        compiler_params=_cp(("arbitrary",)),
        name="s5_projection",
    )(h2, h2, w_u_t)
    u4 = u_t.reshape(ell, g_n, c_n, nch)
    ends = pl.pallas_call(
        _s5_end_body,
        grid=(g_n // 2,),
        in_specs=[pl.BlockSpec((ell, 2, c_n, nch), lambda p: (0, p, 0, 0)),
                  pl.BlockSpec((2, sw, width), lambda p: (p, 0, 0))],
        out_specs=pl.BlockSpec((None, nch, sw), lambda p: (p, 0, 0)),
        out_shape=jax.ShapeDtypeStruct((g_n // 2, nch, sw), F32),
        compiler_params=_cp(("arbitrary",)),
        name="s5_chunk_ends",
    )(u4, mend_t)
    pairs_per_step = 4
    assert (n_ctx // ell) % 8 == 0 and (n_lat // ell) % 8 == 0
    states = pl.pallas_call(
        functools.partial(_s5_scan_body, n_batch, n_lat // ell, n_ctx // ell, lat_pad),
        grid=(g_n // 2 // pairs_per_step,),
        in_specs=[pl.BlockSpec((pairs_per_step, nch, sw), lambda j: (j, 0, 0)),
                  pl.BlockSpec((pairs_per_step, 1, sw), lambda j: (j, 0, 0))],
        out_specs=pl.BlockSpec((pairs_per_step, nch, sw), lambda j: (j, 0, 0)),
        out_shape=jax.ShapeDtypeStruct((g_n // 2, nch, sw), F32),
        compiler_params=_cp(("arbitrary",), VMEM_LIMIT),
        name="s5_state_scan",
    )(ends, a_chunk.reshape(g_n // 2, 1, sw))
    y_t = pl.pallas_call(
        _s5_out_body,
        grid=(g_n,),
        in_specs=[pl.BlockSpec((ell, None, c_n, nch), lambda gi: (0, gi, 0, 0)),
                  pl.BlockSpec((None, width, width), lambda gi: (gi, 0, 0)),
                  pl.BlockSpec((None, nch, sw), lambda gi: (gi // 2, 0, 0)),
                  pl.BlockSpec((None, width, sw), lambda gi: (gi, 0, 0))],
        out_specs=pl.BlockSpec((ell, None, c_n, nch), lambda gi: (0, gi, 0, 0)),
        out_shape=jax.ShapeDtypeStruct((ell, g_n, c_n, nch), F32),
        compiler_params=_cp(("arbitrary",)),
        name="s5_outputs",
    )(u4, tsum_t, states, wst_t)
    n_chunks = n_lat_chunks + n_ctx_chunks
    ya = pl.pallas_call(
        functools.partial(_s5_glu_body, n_lat_chunks, n_ctx_chunks, lat_pad),
        grid=(ell,),
        in_specs=[pl.BlockSpec((None, bw, nch), lambda t: (t, 0, 0)),
                  pl.BlockSpec((bw, bw), const),
                  pl.BlockSpec((1, bw), const)],
        out_specs=pl.BlockSpec((n_chunks, bw), lambda t: (0, t)),
        out_shape=jax.ShapeDtypeStruct((n_chunks, ell * bw), BF16),
        compiler_params=_cp(("arbitrary",)),
        name="s5_glu",
    )(y_t.reshape(ell, bw, nch), w_glu, b_glu)
    return ya.reshape(r, bw)


def rope_tables(n_lat):
    half = GQ_HEAD_DIM // 2
    quarter = half // 2
    t = np.arange(n_lat)
    freqs = ROPE_THETA ** (-np.arange(quarter, dtype=np.float64) / quarter)
    ang_r = (t // GRID_W)[:, None] * freqs
    ang_c = (t % GRID_W)[:, None] * freqs
    ang = np.concatenate([ang_r, ang_r, ang_c, ang_c], axis=1)
    sign = np.concatenate([-np.ones(quarter), np.ones(quarter)] * 2)
    cos = np.concatenate([np.cos(ang), np.ones((ROW_TILE, GQ_HEAD_DIM))], axis=0)
    sin = np.concatenate([np.sin(ang) * sign, np.zeros((ROW_TILE, GQ_HEAD_DIM))], axis=0)
    tab = np.concatenate([cos, cos, sin, sin], axis=1)
    return jnp.asarray(tab, F32)


def _group_ones(width, group):
    i = np.arange(width)
    return jnp.asarray((i[:, None] // group) == (i[None, :] // group), BF16)


def _group_mean_sq(x, ones_blk, group):
    sq = x * x
    hi = sq.astype(BF16)
    lo = (sq - hi.astype(F32)).astype(BF16)
    return (_dot(hi, ones_blk) + _dot(lo, ones_blk)) * (1.0 / group)


def _rope(x, cos, sin):
    w = x.shape[-1]
    q = GQ_HEAD_DIM // 4
    lane = lax.broadcasted_iota(I32, x.shape, 1)
    first = (lane % (2 * q)) < q
    partner = jnp.where(first, pltpu.roll(x, w - q, 1), pltpu.roll(x, q, 1))
    return x * cos + partner * sin


def _gq_prep_body(q_ref, k_ref, v_ref, cs_ref, gq_ref, gk_ref, oq_ref, ok_ref, qm_ref, kr_ref, va_ref):
    cs = cs_ref[...]
    cos1, sin1 = cs[:, :LANES], cs[:, LANES:]
    q = q_ref[...].astype(F32)
    qn = q * lax.rsqrt(_group_mean_sq(q, oq_ref[...], GQ_HEAD_DIM) + EPS) * gq_ref[...]
    n_pairs = q.shape[1] // LANES
    qr = _rope(qn, jnp.concatenate([cos1] * n_pairs, axis=1), jnp.concatenate([sin1] * n_pairs, axis=1))
    qr = (qr * (GQ_HEAD_DIM ** -0.5 * LOG2E)).astype(BF16)
    lane = lax.broadcasted_iota(I32, (q.shape[0], LANES), 1)
    heads_per_kv = GQ_HEADS // GQ_KV_HEADS
    for h in range(GQ_HEADS):
        pair = qr[:, (h // 2) * LANES:(h // 2 + 1) * LANES]
        kv = h // heads_per_kv
        if h % 2 != kv:
            pair = pltpu.roll(pair, GQ_HEAD_DIM, 1)
        keep = (lane >= kv * GQ_HEAD_DIM) & (lane < (kv + 1) * GQ_HEAD_DIM)
        qm_ref[h] = jnp.where(keep, pair, jnp.zeros_like(pair))
    k = k_ref[...].astype(F32)
    kn = k * lax.rsqrt(_group_mean_sq(k, ok_ref[...], GQ_HEAD_DIM) + EPS) * gk_ref[...]
    kr_ref[...] = _rope(kn, cos1, sin1).astype(BF16)
    va_ref[...] = jnp.concatenate([v_ref[...], jnp.ones(v_ref.shape, BF16)], axis=1)


def gq_prepare(pb, col_q, col_k, col_v, cs_tab, g_q, g_k, n_lat, n_ctx, n_batch):
    r = pb.shape[0]
    tm = ROW_TILE
    assert n_ctx == tm and n_lat % tm == 0
    nb = n_lat // tm
    n_lat_tiles = n_batch * nb

    def tab_idx(i):
        return (jnp.where(i < n_lat_tiles, i % nb, nb), 0)

    def kv_idx(i):
        lat = (i // nb) * (nb + 1) + i % nb
        ctx = (i - n_lat_tiles) * (nb + 1) + nb
        return (jnp.where(i < n_lat_tiles, lat, ctx), 0)

    qw = GQ_HEADS * GQ_HEAD_DIM
    const = lambda i: (0, 0)
    gq = jnp.tile(g_q.astype(F32), GQ_HEADS).reshape(1, qw)
    gk = jnp.tile(g_k.astype(F32), GQ_KV_HEADS).reshape(1, LANES)
    n_keys = n_batch * (n_lat + n_ctx)
    return pl.pallas_call(
        _gq_prep_body,
        grid=(r // tm,),
        in_specs=[pl.BlockSpec((tm, qw), lambda i: (i, col_q // qw)),
                  pl.BlockSpec((tm, LANES), lambda i: (i, col_k // LANES)),
                  pl.BlockSpec((tm, LANES), lambda i: (i, col_v // LANES)),
                  pl.BlockSpec((tm, 2 * LANES), tab_idx),
                  pl.BlockSpec((1, qw), const),
                  pl.BlockSpec((1, LANES), const),
                  pl.BlockSpec((qw, qw), const),
                  pl.BlockSpec((LANES, LANES), const)],
        out_specs=[pl.BlockSpec((GQ_HEADS, tm, LANES), lambda i: (0, i, 0)),
                   pl.BlockSpec((tm, LANES), kv_idx),
                   pl.BlockSpec((tm, 2 * LANES), kv_idx)],
        out_shape=[jax.ShapeDtypeStruct((GQ_HEADS, r, LANES), BF16),
                   jax.ShapeDtypeStruct((n_keys, LANES), BF16),
                   jax.ShapeDtypeStruct((n_keys, 2 * LANES), BF16)],
        compiler_params=_cp(("arbitrary",)),
        name="gq_prepare",
    )(pb, pb, pb, cs_tab, gq, gk, _group_ones(qw, GQ_HEAD_DIM), _group_ones(LANES, GQ_HEAD_DIM))


def _gq_flash_body(q_ref, k_ref, v_ref, o_ref, m_sc, acc_sc):
    kj = pl.program_id(2)
    n_h, tq, _ = q_ref.shape

    @pl.when(kj == 0)
    def _():
        m_sc[...] = jnp.full(m_sc.shape, -jnp.inf, F32)
        acc_sc[...] = jnp.zeros(acc_sc.shape, F32)

    k = k_ref[...]
    v = v_ref[...]
    sub = min(tq, 256)
    for c in range(n_h * tq // sub):
        rows = slice(c * sub, (c + 1) * sub)
        h, r0 = divmod(c * sub, tq)
        s = _dot_nt(q_ref[h, r0:r0 + sub, :], k)
        m_prev = m_sc[rows, :]
        m_new = jnp.maximum(m_prev, jnp.max(s, axis=-1, keepdims=True))
        p = jnp.exp2(s - m_new)
        acc_sc[rows, :] = jnp.exp2(m_prev - m_new) * acc_sc[rows, :] + _dot(p.astype(BF16), v)
        m_sc[rows, :] = m_new

    @pl.when(kj == pl.num_programs(2) - 1)
    def _():
        lane = lax.broadcasted_iota(I32, (tq, LANES), 1)
        heads_per_kv = n_h // GQ_KV_HEADS
        for j in range(n_h // 2):
            kv = (2 * j) // heads_per_kv
            halves = []
            for h in (2 * j, 2 * j + 1):
                a = acc_sc[h * tq:(h + 1) * tq, :]
                halves.append(a[:, :LANES] / a[:, LANES:LANES + 1])
            lo, hi = halves
            if kv == 0:
                hi = pltpu.roll(hi, GQ_HEAD_DIM, 1)
            else:
                lo = pltpu.roll(lo, GQ_HEAD_DIM, 1)
            o_ref[:, j * LANES:(j + 1) * LANES] = jnp.where(lane < GQ_HEAD_DIM, lo, hi).astype(o_ref.dtype)


def gq_attention(qm, keys, vals, n_rows_out, tq, tk, q_blk, k_blk, o_blk, n_q, n_k, n_batch):
    n_h = qm.shape[0]
    return pl.pallas_call(
        _gq_flash_body,
        grid=(n_batch, n_q, n_k),
        in_specs=[pl.BlockSpec((n_h, tq, LANES), lambda b, i, j: (0, q_blk(b, i), 0)),
                  pl.BlockSpec((tk, LANES), lambda b, i, j: (k_blk(b, j), 0)),
                  pl.BlockSpec((tk, 2 * LANES), lambda b, i, j: (k_blk(b, j), 0))],
        out_specs=pl.BlockSpec((tq, n_h * GQ_HEAD_DIM), lambda b, i, j: (o_blk(b, i), 0)),
        out_shape=jax.ShapeDtypeStruct((n_rows_out, n_h * GQ_HEAD_DIM), BF16),
        scratch_shapes=[pltpu.VMEM((n_h * tq, 1), F32), pltpu.VMEM((n_h * tq, 2 * LANES), F32)],
        compiler_params=_cp(("arbitrary", "arbitrary", "arbitrary"), VMEM_LIMIT),
        name="gq_attention",
    )(qm, keys, vals)


def _largest_divisor(n, cap):
    return max(d for d in range(1, cap + 1) if n % d == 0)


def gq_mixer(pb, col_q, col_k, col_v, cs_tab, g_q, g_k, n_lat, n_ctx, n_batch, with_ctx):
    qm, keys, vals = gq_prepare(pb, col_q, col_k, col_v, cs_tab, g_q, g_k, n_lat, n_ctx, n_batch)
    tq = 512
    tk = LANES * _largest_divisor((n_lat + n_ctx) // LANES, 22)
    n_q = n_lat // tq
    per_b = (n_lat + n_ctx) // tk
    lat_blk = lambda b, i: b * n_q + i
    y_lat = gq_attention(qm, keys, vals, n_batch * n_lat, tq, tk, lat_blk, lambda b, j: b * per_b + j, lat_blk,
                         n_q, per_b, n_batch)
    if not with_ctx:
        return y_lat, None
    tc = n_ctx
    y_ctx = gq_attention(qm, keys, vals, n_batch * n_ctx, tc, tc, lambda b, i: n_batch * n_lat // tc + b,
                         lambda b, j: b * ((n_lat + n_ctx) // tc) + n_lat // tc, lambda b, i: b, 1, 1, n_batch)
    return y_lat, y_ctx


def na_bias_tables(rpb, n_img_rows, n_ctx):
    tr = NA_TILE_ROWS
    nt = n_img_rows // tr
    assert nt >= 4
    kr = min(NA_ROWS, n_img_rows)
    n_heads = rpb.shape[0]
    qcol = np.arange(GRID_W)[:, None]
    kcol = np.arange(GRID_W)[None, :]
    dc = np.clip(kcol - qcol + NA_COLS - 1, 0, 2 * NA_COLS - 2)
    oh_c = (dc[None] == np.arange(2 * NA_COLS - 1)[:, None, None]).astype(np.float32)
    cstart = np.clip(qcol - NA_COLS // 2, 0, GRID_W - NA_COLS)
    col_ok = (kcol >= cstart) & (kcol < cstart + NA_COLS)
    by_col = jnp.einsum('hrd,dqk->hrqk', rpb.astype(F32), jnp.asarray(oh_c), precision=HIGHEST)
    by_col = jnp.where(jnp.asarray(col_ok)[None, None], by_col * LOG2E, MASK_NEG)
    masked = jnp.full((n_heads, GRID_W, GRID_W), MASK_NEG, F32)
    ctx_cols = jnp.zeros((n_heads, GRID_W, n_ctx), F32)
    classes = []
    for i in (0, 1, nt - 1):
        wb = int(np.clip(i - 1, 0, nt - 3))
        qrow = (i * tr + np.arange(tr))[:, None]
        krow = (wb * tr + np.arange(3 * tr))[None, :]
        rs = np.clip(qrow - kr // 2, 0, n_img_rows - kr)
        row_ok = (krow >= rs) & (krow < rs + kr)
        dr = np.clip(krow - qrow + NA_ROWS - 1, 0, 2 * NA_ROWS - 2)
        q_rows = []
        for a in range(tr):
            blocks = [by_col[:, int(dr[a, b])] if row_ok[a, b] else masked for b in range(3 * tr)]
            q_rows.append(jnp.concatenate(blocks + [ctx_cols], axis=-1))
        classes.append(jnp.concatenate(q_rows, axis=1))
    return jnp.stack(classes, axis=0)


def _pair_attention(q_pair, k_pair, v_pair, bias_fn):
    lane = lax.broadcasted_iota(I32, q_pair.shape, 1)
    v_aug = jnp.concatenate([v_pair, jnp.ones(v_pair.shape, BF16)], axis=1)
    out = None
    for hh in range(2):
        mine = (lane >= hh * NA_HEAD_DIM) & (lane < (hh + 1) * NA_HEAD_DIM)
        qm = jnp.where(mine, q_pair, jnp.zeros_like(q_pair))
        s = _dot_nt(qm, k_pair)
        b = bias_fn(hh)
        if b is not None:
            s = s + b
        m = jnp.max(s, axis=-1, keepdims=True)
        o_aug = _dot(jnp.exp2(s - m).astype(BF16), v_aug)
        o = o_aug[:, :LANES] / o_aug[:, LANES:LANES + 1]
        out = o if out is None else jnp.where(mine, o, out)
    return out


def _na_body(q_ref, k0_ref, k1_ref, k2_ref, kc_ref, v0_ref, v1_ref, v2_ref, vc_ref, b_ref, o_ref):
    for j in range(NA_HEADS // 2):
        sl = slice(j * LANES, (j + 1) * LANES)
        q_pair = q_ref[:, sl]
        k_pair = jnp.concatenate([k0_ref[:, sl], k1_ref[:, sl], k2_ref[:, sl], kc_ref[:, sl]], axis=0)
        v_pair = jnp.concatenate([v0_ref[:, sl], v1_ref[:, sl], v2_ref[:, sl], vc_ref[:, sl]], axis=0)
        o = _pair_attention(q_pair, k_pair, v_pair, lambda hh: b_ref[0, 2 * j + hh])
        o_ref[:, sl] = o.astype(o_ref.dtype)


def na_mixer(pb, col_q, col_k, col_v, bias_tab, n_lat, n_ctx, n_batch):
    tm = NA_TILE_ROWS * GRID_W
    assert n_ctx == tm
    w = NA_HEADS * NA_HEAD_DIM
    nt = n_lat // tm
    n_keys = 3 * tm + n_ctx
    cq, ck, cv = col_q // w, col_k // w, col_v // w
    ctx0 = n_batch * nt

    def win(o):
        return lambda b, i: (b * nt + jnp.clip(i - 1, 0, nt - 3) + o)

    def cls(b, i):
        return (jnp.where(i == 0, 0, jnp.where(i == nt - 1, 2, 1)), 0, 0, 0)

    kspecs = [pl.BlockSpec((tm, w), (lambda b, i, f=win(o): (f(b, i), ck))) for o in range(3)]
    vspecs = [pl.BlockSpec((tm, w), (lambda b, i, f=win(o): (f(b, i), cv))) for o in range(3)]
    return pl.pallas_call(
        _na_body,
        grid=(n_batch, nt),
        in_specs=[pl.BlockSpec((tm, w), lambda b, i: (b * nt + i, cq))] + kspecs
        + [pl.BlockSpec((tm, w), lambda b, i: (ctx0 + b, ck))] + vspecs
        + [pl.BlockSpec((tm, w), lambda b, i: (ctx0 + b, cv)),
           pl.BlockSpec((1, NA_HEADS, tm, n_keys), cls)],
        out_specs=pl.BlockSpec((tm, w), lambda b, i: (b * nt + i, 0)),
        out_shape=jax.ShapeDtypeStruct((n_batch * n_lat, w), BF16),
        compiler_params=_cp(("arbitrary", "arbitrary"), VMEM_LIMIT),
        name="na_attention",
    )(pb, pb, pb, pb, pb, pb, pb, pb, pb, bias_tab)


def _ctx_mha_body(q_ref, k_ref, v_ref, o_ref):
    for j in range(NA_HEADS // 2):
        sl = slice(j * LANES, (j + 1) * LANES)
        o = _pair_attention(q_ref[:, sl], k_ref[:, sl], v_ref[:, sl], lambda hh: None)
        o_ref[:, sl] = o.astype(o_ref.dtype)


def na_ctx_attention(pb, col_q, col_k, col_v, n_lat, n_ctx, n_batch):
    w = NA_HEADS * NA_HEAD_DIM
    ctx0 = n_batch * n_lat // n_ctx
    spec = lambda c: pl.BlockSpec((n_ctx, w), lambda b: (ctx0 + b, c // w))
    return pl.pallas_call(
        _ctx_mha_body,
        grid=(n_batch,),
        in_specs=[spec(col_q), spec(col_k), spec(col_v)],
        out_specs=pl.BlockSpec((n_ctx, w), lambda b: (b, 0)),
        out_shape=jax.ShapeDtypeStruct((n_batch * n_ctx, w), BF16),
        compiler_params=_cp(("arbitrary",)),
        name="na_ctx_attention",
    )(pb, pb, pb)


def _mlstm_body(n_batch, *refs):
    n_in = 8 * n_batch
    ins, bias_ref = refs[:n_in], refs[n_in]
    hf_ref, hb_ref, c_ref, n_ref, m_ref = refs[n_in + 1:]

    @pl.when(pl.program_id(0) == 0)
    def _():
        c_ref[...] = jnp.zeros(c_ref.shape, F32)
        n_ref[...] = jnp.zeros(n_ref.shape, F32)
        m_ref[...] = jnp.full(m_ref.shape, NEG_INIT, F32)

    tok = lax.broadcasted_iota(I32, (ML_CHUNK, ML_CHUNK), 0)
    src = lax.broadcasted_iota(I32, (ML_CHUNK, ML_CHUNK), 1)
    masks = (src <= tok, src >= tok)
    gates = []
    for b in range(n_batch):
        for d in range(2):
            g = ins[(b * 2 + d) * 4 + 3][...] + bias_ref[...]
            lf_cum = jnp.dot(masks[d].astype(F32), jax.nn.log_sigmoid(g), precision=HIGHEST,
                             preferred_element_type=F32)
            gates.append((g, lf_cum))
    gates = [(g, lf_cum, g.T, lf_cum.T) for g, lf_cum in gates]
    chains = []
    for b in range(n_batch):
        for d in range(2):
            g, lf_cum, g_t, lf_cum_t = gates[b * 2 + d]
            for h in range(ML_HEADS):
                ci, cf = d * 2 * ML_HEADS + h, d * 2 * ML_HEADS + ML_HEADS + h
                idx = (b * 2 + d) * ML_HEADS + h
                chains.append(dict(
                    b=b, d=d, h=h, idx=idx, bt_col=lf_cum[:, cf:cf + 1], bt_row=lf_cum_t[cf:cf + 1, :],
                    li_col=g[:, ci:ci + 1], li_row=g_t[ci:ci + 1, :], m_prev=m_ref[idx][:, 0:1],
                    c_prev=c_ref[idx], n_prev=n_ref[idx]))
    for c in chains:
        c['dmat'] = jnp.where(masks[c['d']], c['bt_col'] - c['bt_row'] + c['li_row'], -jnp.inf)
        c['inter'] = c['bt_col'] + c['m_prev']
    for c in chains:
        c['mt'] = jnp.maximum(c['inter'], jnp.max(c['dmat'], axis=-1, keepdims=True))
    for c in chains:
        refs_c = ins[(c['b'] * 2 + c['d']) * 4:(c['b'] * 2 + c['d']) * 4 + 3]
        sl = slice(c['h'] * ML_HEAD_DIM, (c['h'] + 1) * ML_HEAD_DIM)
        c['q'], c['k'], c['v'] = (r[:, sl] for r in refs_c)
        c['s'] = _dot_nt(c['q'], c['k']) * jnp.exp(c['dmat'] - c['mt'])
        c['w_inter'] = jnp.exp(c['inter'] - c['mt'])
    for c in chains:
        num = _dot(c['s'].astype(BF16), c['v']) + c['w_inter'] * _dot_nt(c['q'], c['c_prev'].astype(BF16))
        qn = jnp.sum(c['q'].astype(F32) * c['n_prev'], axis=-1, keepdims=True)
        den = jnp.sum(c['s'], axis=-1, keepdims=True) + c['w_inter'] * qn
        h_out = num / jnp.maximum(jnp.abs(den), jnp.exp(-c['mt']))
        h_ref = hb_ref if c['d'] else hf_ref
        h_ref[c['b'], :, c['h'] * ML_HEAD_DIM:(c['h'] + 1) * ML_HEAD_DIM] = h_out
    new_state = []
    for c in chains:
        b_last = c['bt_col'][0:1, :] if c['d'] else c['bt_col'][ML_CHUNK - 1:ML_CHUNK, :]
        g_col = b_last - c['bt_col'] + c['li_col']
        m_new = jnp.maximum(b_last + c['m_prev'], jnp.max(g_col, axis=0, keepdims=True))
        wg = jnp.exp(g_col - m_new)
        decay = jnp.exp(b_last + c['m_prev'] - m_new)
        c_new = decay * c['c_prev'] + _dot_tn((wg * c['v'].astype(F32)).astype(BF16), c['k'])
        n_new = decay * c['n_prev'] + jnp.sum(wg * c['k'].astype(F32), axis=0, keepdims=True)
        new_state.append((c['idx'], c_new, n_new, m_new))
    for idx, c_new, n_new, m_new in new_state:
        c_ref[idx] = c_new
        n_ref[idx] = n_new
        m_ref[idx] = jnp.broadcast_to(m_new, (1, LANES))


def mlstm_mixer(pb, pa, col_q, col_k, col_v, col_g, bias, n_lat, n_ctx, n_batch):
    w = ML_HEADS * ML_HEAD_DIM
    tc = ML_CHUNK
    nl, nc = n_lat // tc, n_ctx // tc

    def fwd_pos(i):
        return jnp.where(i < nc, nl + i, i - nc)

    def bwd_pos(i):
        return jnp.where(i < nc, nl + nc - 1 - i, nl - 1 - (i - nc))

    def row_blk(b, pos):
        return jnp.where(pos < nl, b * nl + pos, n_batch * nl + b * nc + pos - nl)

    def specs(b, pos_fn):
        return [pl.BlockSpec((tc, w), lambda i, c=c: (row_blk(b, pos_fn(i)), c // w)) for c in (col_q, col_k, col_v)] + [
            pl.BlockSpec((tc, LANES), lambda i: (row_blk(b, pos_fn(i)), col_g // LANES))]

    in_specs, operands = [], []
    for b in range(n_batch):
        for pos_fn in (fwd_pos, bwd_pos):
            in_specs += specs(b, pos_fn)
            operands += [pb, pb, pb, pa]
    n_st = 2 * ML_HEADS * n_batch
    out_shape = jax.ShapeDtypeStruct((n_batch, n_lat + n_ctx, w), F32)
    return pl.pallas_call(
        functools.partial(_mlstm_body, n_batch),
        grid=(nl + nc,),
        in_specs=in_specs + [pl.BlockSpec((1, LANES), lambda i: (0, 0))],
        out_specs=[pl.BlockSpec((n_batch, tc, w), lambda i: (0, fwd_pos(i), 0)),
                   pl.BlockSpec((n_batch, tc, w), lambda i: (0, bwd_pos(i), 0))],
        out_shape=[out_shape, out_shape],
        scratch_shapes=[pltpu.VMEM((n_st, ML_HEAD_DIM, ML_HEAD_DIM), F32),
                        pltpu.VMEM((n_st, 1, ML_HEAD_DIM), F32),
                        pltpu.VMEM((n_st, 1, LANES), F32)],
        compiler_params=_cp(("arbitrary",)),
        name="mlstm_chunks",
    )(*operands, bias)


def _merge_body(n_lat_tiles, ya_ref, ybl_ref, ybc_ref, hf_ref, hb_ref, o_ref, ydl_ref, ydc_ref, gate_ref, x_ref,
                mod_ref, mlg_ref, wbr_ref, wout_ref, g2_ref, wr_ref, xo_ref, h2_ref, st_ref):
    d = x_ref.shape[1]
    is_ctx = pl.program_id(0) >= n_lat_tiles
    yb = jnp.where(is_ctx, ybc_ref[...], ybl_ref[...])
    yd = jnp.where(is_ctx, ydc_ref[...], ydl_ref[...])
    hs = hf_ref[...] + hb_ref[...]
    segs = []
    for h in range(ML_HEADS):
        seg = hs[:, h * ML_HEAD_DIM:(h + 1) * ML_HEAD_DIM]
        segs.append(seg * lax.rsqrt(jnp.mean(seg * seg, axis=-1, keepdims=True) + EPS))
    ym = jnp.concatenate(segs, axis=1) * mlg_ref[...] * jax.nn.sigmoid(o_ref[...].astype(F32))
    ys = (ya_ref[...], yb, ym.astype(BF16), yd)
    merged = None
    for i in range(N_BRANCHES):
        term = gate_ref[:, i * d:(i + 1) * d].astype(F32) * _dot(ys[i], wbr_ref[i])
        merged = term if merged is None else merged + term
    y = _dot(merged.astype(BF16), wout_ref[...])
    mod = mod_ref[0]
    x_new = x_ref[...] + mod[2:3] * y
    xo_ref[...] = x_new
    h2 = _rms_mod(x_new, g2_ref[...], mod[3:4], mod[4:5])
    h2_ref[...] = h2
    st_ref[...] = jax.nn.sigmoid(_dot_nt(wr_ref[...], h2.astype(BF16)))


def merge_layer(ya, yb, yd, hf, hb, pb, col_o, gate, x_all, modtab, mlg, wbr, wout, g2, wr_t,
                n_rows, n_lat_rows, n_batch):
    d = x_all.shape[1]
    tm = ROW_TILE
    w = BRANCH_WIDTH
    per_b = n_lat_rows // n_batch // tm
    row = lambda i: (i, 0)
    const2 = lambda i: (0, 0)
    n_lat_tiles = n_lat_rows // tm
    lat_row = lambda i: (jnp.minimum(i, n_lat_tiles - 1), 0)
    ctx_row = lambda i: (jnp.clip(i - n_lat_tiles, 0, n_batch - 1), 0)

    def seq(i):
        lat = i < n_lat_tiles
        return (jnp.where(lat, i // per_b, i - n_lat_tiles), jnp.where(lat, i % per_b, per_b), 0)

    (yb_lat, yb_ctx), (yd_lat, yd_ctx) = yb, yd
    if yb_ctx is None:
        yb_ctx, yd_ctx = yb_lat, yd_lat
    return pl.pallas_call(
        functools.partial(_merge_body, n_lat_tiles),
        grid=(n_rows // tm,),
        in_specs=[pl.BlockSpec((tm, w), row), pl.BlockSpec((tm, w), lat_row), pl.BlockSpec((tm, w), ctx_row),
                  pl.BlockSpec((None, tm, w), seq), pl.BlockSpec((None, tm, w), seq),
                  pl.BlockSpec((tm, w), lambda i: (i, col_o // w)),
                  pl.BlockSpec((tm, w), lat_row), pl.BlockSpec((tm, w), ctx_row),
                  pl.BlockSpec((tm, N_BRANCHES * d), row),
                  pl.BlockSpec((tm, d), row),
                  pl.BlockSpec((1, N_MOD, d), lambda i: (jnp.minimum(i // per_b, n_batch), 0, 0)),
                  pl.BlockSpec((1, w), const2),
                  pl.BlockSpec((N_BRANCHES, w, d), lambda i: (0, 0, 0)),
                  pl.BlockSpec((d, d), const2), pl.BlockSpec((1, d), const2),
                  pl.BlockSpec((LANES, d), const2)],
        out_specs=[pl.BlockSpec((tm, d), row), pl.BlockSpec((tm, d), row),
                   pl.BlockSpec((LANES, tm), lambda i: (0, i))],
        out_shape=[jax.ShapeDtypeStruct((n_rows, d), F32), jax.ShapeDtypeStruct((n_rows, d), F32),
                   jax.ShapeDtypeStruct((LANES, n_rows), F32)],
        compiler_params=_cp(("arbitrary",), VMEM_LIMIT),
        name="merge_layer",
    )(ya, yb_lat, yb_ctx, hf, hb, pb, yd_lat, yd_ctx, gate, x_all, modtab, mlg, wbr, wout, g2, wr_t)


def _router_body(s_ref, b_ref, e_ref, w_ref, rank_ref, cnt_ref, base_sc):
    @pl.when(pl.program_id(0) == 0)
    def _():
        base_sc[...] = jnp.zeros(base_sc.shape, F32)

    tm = s_ref.shape[1]
    s = s_ref[0:N_EXPERTS, :]
    sel = s + b_ref[0:N_EXPERTS, :]
    row = lambda a, e: a[e:e + 1, :]
    best, grp = None, None
    for g in range(N_EXPERT_GROUPS):
        v = [row(sel, EXPERTS_PER_GROUP * g + k) for k in range(EXPERTS_PER_GROUP)]
        gs = None
        for a in range(EXPERTS_PER_GROUP):
            for c in range(a + 1, EXPERTS_PER_GROUP):
                gs = v[a] + v[c] if gs is None else jnp.maximum(gs, v[a] + v[c])
        if best is None:
            best, grp = gs, jnp.zeros((1, tm), I32)
        else:
            better = gs > best
            grp = jnp.where(better, g, grp)
            best = jnp.where(better, gs, best)
    vals, affs = [], []
    for k in range(EXPERTS_PER_GROUP):
        vk, sk = row(sel, k), row(s, k)
        for g in range(1, N_EXPERT_GROUPS):
            hit = grp == g
            vk = jnp.where(hit, row(sel, EXPERTS_PER_GROUP * g + k), vk)
            sk = jnp.where(hit, row(s, EXPERTS_PER_GROUP * g + k), sk)
        vals.append(vk)
        affs.append(sk)
    i1, b1, w1 = jnp.zeros((1, tm), I32), vals[0], affs[0]
    for k in range(1, EXPERTS_PER_GROUP):
        better = vals[k] > b1
        i1 = jnp.where(better, k, i1)
        w1 = jnp.where(better, affs[k], w1)
        b1 = jnp.where(better, vals[k], b1)
    i2 = jnp.zeros((1, tm), I32)
    b2 = jnp.full((1, tm), -jnp.inf, F32)
    w2 = jnp.zeros((1, tm), F32)
    for k in range(EXPERTS_PER_GROUP):
        cand = (i1 != k) & (vals[k] > b2)
        i2 = jnp.where(cand, k, i2)
        w2 = jnp.where(cand, affs[k], w2)
        b2 = jnp.where(cand, vals[k], b2)
    e1 = grp * EXPERTS_PER_GROUP + i1
    e2 = grp * EXPERTS_PER_GROUP + i2
    tot = w1 + w2
    e_ref[...] = jnp.concatenate([e1, e2], axis=0)
    wpad = jnp.concatenate([w1 / tot, w2 / tot, jnp.zeros((6, tm), F32)], axis=0)
    w_ref[...] = wpad.T
    ids = lax.broadcasted_iota(I32, (N_EXPERTS, tm), 0)
    oh1 = (ids == e1).astype(F32)
    oh2 = (ids == e2).astype(F32)
    oh = oh1 + oh2
    before = (lax.broadcasted_iota(I32, (tm, tm), 0) < lax.broadcasted_iota(I32, (tm, tm), 1)).astype(BF16)
    prior = _dot(oh.astype(BF16), before) + base_sc[...]
    r1 = jnp.sum(oh1 * prior, axis=0, keepdims=True)
    r2 = jnp.sum(oh2 * prior, axis=0, keepdims=True)
    rank_ref[...] = jnp.concatenate([r1, r2], axis=0).astype(I32)
    base = base_sc[...] + jnp.sum(oh, axis=1, keepdims=True)
    base_sc[...] = base
    cnt_ref[...] = jnp.broadcast_to(base, cnt_ref.shape).astype(I32)


def route(s_t, b_router):
    n_rows = s_t.shape[1]
    tm = ROW_TILE
    b_col = jnp.pad(b_router.astype(F32), (0, LANES - N_EXPERTS)).reshape(LANES, 1)
    return pl.pallas_call(
        _router_body,
        grid=(n_rows // tm,),
        in_specs=[pl.BlockSpec((LANES, tm), lambda i: (0, i)), pl.BlockSpec((LANES, 1), lambda i: (0, 0))],
        out_specs=[pl.BlockSpec((TOP_K, tm), lambda i: (0, i)), pl.BlockSpec((tm, 8), lambda i: (i, 0)),
                   pl.BlockSpec((TOP_K, tm), lambda i: (0, i)), pl.BlockSpec((N_EXPERTS, LANES), lambda i: (0, 0))],
        out_shape=[jax.ShapeDtypeStruct((TOP_K, n_rows), I32), jax.ShapeDtypeStruct((n_rows, 8), F32),
                   jax.ShapeDtypeStruct((TOP_K, n_rows), I32), jax.ShapeDtypeStruct((N_EXPERTS, LANES), I32)],
        scratch_shapes=[pltpu.VMEM((N_EXPERTS, 1), F32)],
        compiler_params=_cp(("arbitrary",)),
        name="moe_router",
    )(s_t, b_col)


def _row_copy(src_ref, src_row, dst_ref, dst_row, sem):
    return pltpu.make_async_copy(src_ref.at[pl.ds(src_row, 1), :], dst_ref.at[pl.ds(dst_row, 1), :], sem)


def _dispatch_body(dest_ref, h_ref, buf_in_ref, buf_ref, sem):
    del buf_in_ref
    tm = h_ref.shape[0]

    def issue(t, carry):
        for k in range(TOP_K):
            _row_copy(h_ref, t, buf_ref, dest_ref[0, k, t], sem).start(priority=k)
        return carry

    lax.fori_loop(0, tm, issue, 0, unroll=8)
    for k in range(TOP_K):
        pltpu.make_async_copy(h_ref, buf_ref.at[pl.ds(0, tm), :], sem).wait()


def moe_dispatch(h2, dest3, buf0):
    n_rows, d = h2.shape
    tm = ROW_TILE
    return pl.pallas_call(
        _dispatch_body,
        grid=(n_rows // tm,),
        in_specs=[pl.BlockSpec((1, TOP_K, tm), lambda i: (i, 0, 0), memory_space=pltpu.SMEM),
                  pl.BlockSpec((tm, d), lambda i: (i, 0)),
                  pl.BlockSpec(memory_space=pl.ANY)],
        out_specs=pl.BlockSpec(memory_space=pl.ANY),
        out_shape=jax.ShapeDtypeStruct(buf0.shape, buf0.dtype),
        scratch_shapes=[pltpu.SemaphoreType.DMA(())],
        input_output_aliases={2: 0},
        compiler_params=_cp(("arbitrary",)),
        name="moe_dispatch",
    )(dest3, h2, buf0)


def _expert_body(be_ref, nu_ref, x_ref, w1_ref, w3_ref, w2_ref, o_ref, w1_sc, w3_sc, w2_sc):
    i = pl.program_id(0)
    changed = jnp.logical_or(i == 0, be_ref[i] != be_ref[jnp.maximum(i - 1, 0)])
    used = i < nu_ref[0]

    @pl.when(jnp.logical_and(changed, used))
    def _():
        w1_sc[...] = w1_ref[0].astype(BF16)
        w3_sc[...] = w3_ref[0].astype(BF16)
        w2_sc[...] = w2_ref[0].astype(BF16)

    @pl.when(used)
    def _():
        x = x_ref[...].astype(BF16)
        a = _dot(x, w1_sc[...])
        mid = (a * jax.nn.sigmoid(a)) * _dot(x, w3_sc[...])
        o_ref[...] = _dot(mid.astype(BF16), w2_sc[...])

    @pl.when(jnp.logical_not(used))
    def _():
        o_ref[...] = jnp.zeros(o_ref.shape, o_ref.dtype)


def moe_experts(buf, blk_expert, n_used, w1, w3, w2, layer, blk):
    n_slots, d = buf.shape
    de = w1.shape[3]
    grid_spec = pltpu.PrefetchScalarGridSpec(
        num_scalar_prefetch=2,
        grid=(n_slots // blk,),
        in_specs=[pl.BlockSpec((blk, d), lambda i, be, nu: (i, 0)),
                  pl.BlockSpec((None, 1, d, de), lambda i, be, nu: (layer, be[i], 0, 0)),
                  pl.BlockSpec((None, 1, d, de), lambda i, be, nu: (layer, be[i], 0, 0)),
                  pl.BlockSpec((None, 1, de, d), lambda i, be, nu: (layer, be[i], 0, 0))],
        out_specs=pl.BlockSpec((blk, d), lambda i, be, nu: (i, 0)),
        scratch_shapes=[pltpu.VMEM((d, de), BF16), pltpu.VMEM((d, de), BF16), pltpu.VMEM((de, d), BF16)],
    )
    return pl.pallas_call(
        _expert_body,
        grid_spec=grid_spec,
        out_shape=jax.ShapeDtypeStruct((n_slots, d), F32),
        compiler_params=_cp(("arbitrary",), VMEM_LIMIT),
        name="moe_experts",
    )(blk_expert, n_used, buf, w1, w3, w2)


def _combine_body(final, dest_ref, x_ref, w_ref, mod_ref, gf_ref, y_hbm, o_ref, y0_sc, y1_sc, sem):
    tm = x_ref.shape[0]
    bufs = (y0_sc, y1_sc)

    def issue(t, carry):
        for k in range(TOP_K):
            _row_copy(y_hbm, dest_ref[0, k, t], bufs[k], t, sem).start(priority=k)
        return carry

    lax.fori_loop(0, tm, issue, 0, unroll=8)
    for k in range(TOP_K):
        pltpu.make_async_copy(y_hbm.at[pl.ds(0, tm), :], bufs[k], sem).wait()
    w = w_ref[...]
    f = w[:, 0:1] * y0_sc[...] + w[:, 1:2] * y1_sc[...]
    x_new = x_ref[...] + mod_ref[0][5:6] * f
    if final:
        ms = jnp.mean(x_new * x_new, axis=-1, keepdims=True)
        x_new = x_new * lax.rsqrt(ms + EPS) * gf_ref[...]
    o_ref[...] = x_new


def moe_combine(dest3, x_rows, wts, modtab, g_final, y_slots, n_lat_rows, n_batch, final):
    n_rows, d = x_rows.shape
    tm = ROW_TILE
    per_b = n_lat_rows // n_batch // tm
    return pl.pallas_call(
        functools.partial(_combine_body, final),
        grid=(n_rows // tm,),
        in_specs=[pl.BlockSpec((1, TOP_K, tm), lambda i: (i, 0, 0), memory_space=pltpu.SMEM),
                  pl.BlockSpec((tm, d), lambda i: (i, 0)),
                  pl.BlockSpec((tm, 8), lambda i: (i, 0)),
                  pl.BlockSpec((1, N_MOD, d), lambda i: (jnp.minimum(i // per_b, n_batch), 0, 0)),
                  pl.BlockSpec((1, d), lambda i: (0, 0)),
                  pl.BlockSpec(memory_space=pl.ANY)],
        out_specs=pl.BlockSpec((tm, d), lambda i: (i, 0)),
        out_shape=jax.ShapeDtypeStruct((n_rows, d), F32),
        scratch_shapes=[pltpu.VMEM((tm, d), F32), pltpu.VMEM((tm, d), F32), pltpu.SemaphoreType.DMA(())],
        compiler_params=_cp(("arbitrary",)),
        name="moe_combine",
    )(dest3, x_rows, wts, modtab, g_final.reshape(1, d), y_slots)


def moe_layer(x_rows, h2, s_t, b_router, w1, w3, w2, layer, modtab, g_final, n_lat_rows, n_batch, final):
    n_rows, d = h2.shape
    blk = 2 * MOE_BLOCK
    experts, wts, rank, counts = route(s_t, b_router)
    cnt = counts[:, 0]
    padded = (cnt + blk - 1) // blk * blk
    pend = jnp.cumsum(padded)
    pstart = pend - padded
    hit = experts[..., None] == jnp.arange(N_EXPERTS, dtype=I32)
    dest = jnp.sum(jnp.where(hit, pstart.astype(I32), 0), axis=-1) + rank
    n_blocks = -(-(n_rows * TOP_K) // blk) + N_EXPERTS
    blk_start = jnp.arange(n_blocks, dtype=I32) * blk
    blk_expert = jnp.minimum(jnp.sum((pend[None, :] <= blk_start[:, None]).astype(I32), axis=1), N_EXPERTS - 1)
    dest3 = dest.reshape(TOP_K, n_rows // ROW_TILE, ROW_TILE).transpose(1, 0, 2)
    buf = moe_dispatch(h2, dest3, jnp.zeros((n_blocks * blk, d), F32))
    n_used = (pend[-1:] // blk).astype(I32)
    y_slots = moe_experts(buf, blk_expert, n_used, w1, w3, w2, layer, blk)
    return moe_combine(dest3, x_rows, wts, modtab, g_final, y_slots, n_lat_rows, n_batch, final)


_COL = dict(na_q=0, na_k=512, na_v=1024, ml_q=1536, ml_k=2048, ml_v=2560, ml_o=3072,
            gq_q=3584, gq_k=4096, gq_v=4224)


def _split_w_in(w_in):
    sizes = (BRANCH_WIDTH,) * 8 + (4 * ML_HEADS, BRANCH_WIDTH, GQ_KV_HEADS * GQ_HEAD_DIM, GQ_KV_HEADS * GQ_HEAD_DIM)
    idx = np.cumsum(sizes)[:-1].tolist()
    (s5_u, na_q, na_k, na_v, ml_q, ml_k, ml_v, ml_o, ml_gt, gq_q, gq_k, gq_v) = jnp.split(w_in, idx, axis=-1)
    wa = jnp.pad(ml_gt, ((0, 0), (0, LANES - 4 * ML_HEADS)))
    wb = jnp.concatenate([na_q * (NA_HEAD_DIM ** -0.5 * LOG2E), na_k, na_v, ml_q, ml_k * (ML_HEAD_DIM ** -0.5),
                          ml_v, ml_o, gq_q, gq_k, gq_v], axis=1)
    return s5_u.T.astype(BF16), wa.astype(BF16), wb.astype(BF16)


def kernel(x, c, ctx, c_ctx, w_mod, b_mod, g_norm1, g_norm2, w_in, s5_lam_re, s5_lam_im, s5_log_dt, s5_b_re,
           s5_b_im, s5_c_re, s5_c_im, s5_d, s5_w_glu, s5_b_glu, na_rpb, ml_b_gates, ml_norm, gq_qnorm, gq_knorm,
           w_branch, w_gate, b_gate, w_out, w_router, b_router, moe_w1, moe_w3, moe_w2, g_final):
    b, n_lat, dm = x.shape
    n_ctx = ctx.shape[1]
    depth = w_in.shape[0]
    bn, bc = b * n_lat, b * n_ctx
    x_all = jnp.concatenate([x.reshape(bn, dm), ctx.reshape(bc, dm)], axis=0).astype(F32)
    c_all = jnp.concatenate([c.astype(F32), c_ctx.astype(F32)[None], jnp.zeros((8 - b - 1, dm), F32)], axis=0)
    cs_tab = rope_tables(n_lat)
    wr_t = jnp.pad(w_router.astype(BF16).T, ((0, LANES - N_EXPERTS), (0, 0)))
    out = None
    for l in range(depth):
        last = l == depth - 1
        with_ctx = not last
        modtab = mod_vectors(c_all, w_mod, b_mod, l)[:b + 1].reshape(b + 1, N_MOD, dm)
        w_u_t, wa, wb = _split_w_in(w_in[l])
        pa, pb, gate, h_all = in_projection(x_all, g_norm1[l], modtab, wa, wb, w_gate[l].astype(BF16), b_gate[l],
                                            bn, b)

        tables = s5_tables(s5_lam_re[l], s5_lam_im[l], s5_log_dt[l], s5_b_re[l], s5_b_im[l],
                           s5_c_re[l], s5_c_im[l], s5_d[l])
        ya = s5_mixer(h_all, w_u_t, tables, s5_w_glu[l].astype(BF16), s5_b_glu[l].astype(F32).reshape(1, -1),
                      n_lat, n_ctx, b)

        bias_tab = na_bias_tables(na_rpb[l], n_lat // GRID_W, n_ctx)
        n_rows = bn + bc if with_ctx else bn
        na_cols = (_COL['na_q'], _COL['na_k'], _COL['na_v'])
        yb = (na_mixer(pb, *na_cols, bias_tab, n_lat, n_ctx, b),
              na_ctx_attention(pb, *na_cols, n_lat, n_ctx, b) if with_ctx else None)

        ml_bias = jnp.pad(ml_b_gates[l].astype(F32), (0, LANES - 4 * ML_HEADS)).reshape(1, LANES)
        hf, hb = mlstm_mixer(pb, pa, _COL['ml_q'], _COL['ml_k'], _COL['ml_v'], 0, ml_bias, n_lat, n_ctx, b)

        yd = gq_mixer(pb, _COL['gq_q'], _COL['gq_k'], _COL['gq_v'], cs_tab, gq_qnorm[l], gq_knorm[l],
                      n_lat, n_ctx, b, with_ctx)
        x_mid, h2, s_t = merge_layer(
            ya, yb, yd, hf, hb, pb, _COL['ml_o'], gate, x_all, modtab, ml_norm[l].astype(F32).reshape(1, -1),
            w_branch[l].astype(BF16), w_out[l].astype(BF16), g_norm2[l].astype(F32).reshape(1, -1), wr_t,
            n_rows, bn, b)
        x_next = moe_layer(x_mid, h2, s_t, b_router, moe_w1, moe_w3, moe_w2, l, modtab, g_final, bn, b, last)
        if last:
            out = x_next.reshape(b, n_lat, dm).astype(x.dtype)
        else:
            x_all = x_next
    return out
```

```python
import functools
import math

import numpy as np
import jax
import jax.numpy as jnp
from jax import lax
from jax.experimental import pallas as pl
from jax.experimental.pallas import tpu as pltpu

F32 = jnp.float32
BF16 = jnp.bfloat16
I32 = jnp.int32

GRID_W = 64
N_MOD = 6
BRANCH_WIDTH = 512
N_BRANCHES = 4
S5_GROUP = 16
S5_GROUPS = BRANCH_WIDTH // S5_GROUP
S5_STATE = 64
NA_HEADS = 8
NA_HEAD_DIM = 64
NA_ROWS = 8
NA_COLS = 16
ML_HEADS = 4
ML_HEAD_DIM = 128
ML_CHUNK = 128
GQ_HEADS = 8
GQ_KV_HEADS = 2
GQ_HEAD_DIM = 64
ROPE_THETA = 10000.0
N_EXPERTS = 32
N_EXPERT_GROUPS = 8
EXPERTS_PER_GROUP = 4
TOP_K = 2
D_EXPERT = 512
MOE_BLOCK = 128
EPS = 1e-6
NEG_INIT = -1e30
MASK_NEG = -1e30
LOG2E = 1.4426950408889634

LANES = 128
ROW_TILE = 256
S5_CHUNK = 16
NA_TILE_ROWS = 4
VMEM_LIMIT = 56 * 1024 * 1024

HIGHEST = lax.Precision.HIGHEST


def _cp(sem, vmem=None):
    return pltpu.CompilerParams(dimension_semantics=sem, vmem_limit_bytes=vmem)


def _dot(a, b):
    return jnp.dot(a, b, preferred_element_type=F32)


def _dot_nt(a, b):
    return lax.dot_general(a, b, (((1,), (1,)), ((), ())), preferred_element_type=F32)


def _dot_tn(a, b):
    return lax.dot_general(a, b, (((0,), (0,)), ((), ())), preferred_element_type=F32)


def _rms_mod(x, g, shift, scale):
    ms = jnp.mean(x * x, axis=-1, keepdims=True)
    y = x * lax.rsqrt(ms + EPS) * g
    return y * (1.0 + scale) + shift


def _round_up(n, m):
    return -(-n // m) * m


def _mod_body(c_ref, w_ref, b_ref, o_ref):
    c = c_ref[...]
    a = (c * jax.nn.sigmoid(c)).astype(BF16)
    o_ref[...] = _dot(a, w_ref[...].astype(BF16)) + b_ref[...]


def mod_vectors(c_all, w_mod, b_mod, layer):
    d = c_all.shape[1]
    depth = w_mod.shape[0]
    return pl.pallas_call(
        _mod_body,
        grid=(N_MOD,),
        in_specs=[pl.BlockSpec((8, d), lambda j: (0, 0)),
                  pl.BlockSpec((None, d, d), lambda j: (layer, 0, j)),
                  pl.BlockSpec((None, 1, d), lambda j: (layer, 0, j))],
        out_specs=pl.BlockSpec((8, d), lambda j: (0, j)),
        out_shape=jax.ShapeDtypeStruct((8, N_MOD * d), F32),
        compiler_params=_cp(("arbitrary",)),
        name="mod_vectors",
    )(c_all, w_mod, b_mod.reshape(depth, 1, -1))


def _inproj_body(x_ref, g_ref, mod_ref, wa_ref, wb_ref, wg_ref, bg_ref, oa_ref, ob_ref, og_ref, oh_ref):
    mod = mod_ref[0]
    h = _rms_mod(x_ref[...], g_ref[...], mod[0:1], mod[1:2]).astype(BF16)
    oh_ref[...] = h
    oa_ref[...] = _dot(h, wa_ref[...])
    ob_ref[...] = _dot(h, wb_ref[...]).astype(BF16)
    og_ref[...] = jax.nn.sigmoid(_dot(h, wg_ref[...]) + bg_ref[...]).astype(BF16)


def in_projection(x_all, g, modtab, wa, wb, wg, bg, n_lat_rows, n_batch):
    r, d = x_all.shape
    tm = ROW_TILE
    per_b = n_lat_rows // n_batch // tm

    def mod_idx(i):
        return (jnp.minimum(i // per_b, n_batch), 0, 0)

    const = lambda i: (0, 0)
    return pl.pallas_call(
        _inproj_body,
        grid=(r // tm,),
        in_specs=[pl.BlockSpec((tm, d), lambda i: (i, 0)),
                  pl.BlockSpec((1, d), const),
                  pl.BlockSpec((1, N_MOD, d), mod_idx),
                  pl.BlockSpec(wa.shape, const, pipeline_mode=pl.Buffered(1)),
                  pl.BlockSpec(wb.shape, const, pipeline_mode=pl.Buffered(1)),
                  pl.BlockSpec(wg.shape, const, pipeline_mode=pl.Buffered(1)),
                  pl.BlockSpec((1, wg.shape[1]), const)],
        out_specs=[pl.BlockSpec((tm, wa.shape[1]), lambda i: (i, 0)),
                   pl.BlockSpec((tm, wb.shape[1]), lambda i: (i, 0)),
                   pl.BlockSpec((tm, wg.shape[1]), lambda i: (i, 0)),
                   pl.BlockSpec((tm, d), lambda i: (i, 0))],
        out_shape=[jax.ShapeDtypeStruct((r, wa.shape[1]), F32),
                   jax.ShapeDtypeStruct((r, wb.shape[1]), BF16),
                   jax.ShapeDtypeStruct((r, wg.shape[1]), BF16),
                   jax.ShapeDtypeStruct((r, d), BF16)],
        compiler_params=_cp(("arbitrary",), VMEM_LIMIT),
        name="in_projection",
    )(x_all, g.reshape(1, d), modtab, wa, wb, wg, bg.reshape(1, -1))


def s5_tables(lam_re, lam_im, log_dt, b_re, b_im, c_re, c_im, d_skip):
    ell, g_n, p_n, c_n = S5_CHUNK, S5_GROUPS, S5_STATE, S5_GROUP
    lam_re, lam_im = lam_re.astype(F32), lam_im.astype(F32)
    b_re, b_im, c_re, c_im = (t.astype(F32) for t in (b_re, b_im, c_re, c_im))
    dt = jnp.exp(log_dt.astype(F32))[..., None]
    mag = jnp.exp(lam_re * dt)
    a_re = mag * jnp.cos(lam_im * dt)
    a_im = mag * jnp.sin(lam_im * dt)
    den = lam_re * lam_re + lam_im * lam_im
    nr = a_re - 1.0
    f_re = (nr * lam_re + a_im * lam_im) / den
    f_im = (a_im * lam_re - nr * lam_im) / den
    bb_re = f_re[..., None] * b_re - f_im[..., None] * b_im
    bb_im = f_re[..., None] * b_im + f_im[..., None] * b_re
    k = jnp.arange(ell + 1, dtype=F32)
    pmag = jnp.exp((lam_re * dt)[..., None] * k)
    ang = (lam_im * dt)[..., None] * k
    pr, pi = pmag * jnp.cos(ang), pmag * jnp.sin(ang)
    ab_re = pr[..., None] * bb_re[:, :, :, None, :] - pi[..., None] * bb_im[:, :, :, None, :]
    ab_im = pr[..., None] * bb_im[:, :, :, None, :] + pi[..., None] * bb_re[:, :, :, None, :]
    flat = lambda t: t.reshape(2 * g_n, p_n, (ell + 1) * c_n)
    kk = (jnp.einsum('bcp,bpn->bcn', c_re.reshape(2 * g_n, c_n, p_n), flat(ab_re), precision=HIGHEST)
          - jnp.einsum('bcp,bpn->bcn', c_im.reshape(2 * g_n, c_n, p_n), flat(ab_im), precision=HIGHEST))
    kk = kk.reshape(2, g_n, c_n, ell + 1, c_n)
    centre = kk[0][:, :, 0] + kk[1][:, :, 0] + d_skip.astype(F32).reshape(g_n, c_n, 1) * jnp.eye(c_n, dtype=F32)
    w = jnp.concatenate([kk[0][:, :, ell - 1:0:-1], centre[:, :, None], kk[1][:, :, 1:ell]], axis=2)
    wf = w.reshape(g_n, c_n, (2 * ell - 1) * c_n)
    toe = jnp.stack([wf[:, :, (ell - 1 - t) * c_n:(2 * ell - 1 - t) * c_n] for t in range(ell)], axis=1)
    tsum_t = toe.astype(BF16).reshape(g_n, ell * c_n, ell * c_n)

    parity = [jnp.asarray(np.arange(g_n) % 2 == q, F32) for q in range(2)]
    parts = []
    for d in range(2):
        for ab in (ab_re, ab_im):
            sel = ab[d][:, :, :ell]
            if d == 0:
                sel = sel[:, :, ::-1]
            sel = sel.reshape(g_n, p_n, ell * c_n)
            parts += [sel * parity[q][:, None, None] for q in range(2)]
    mend_t = jnp.stack(parts, axis=1).reshape(g_n, 8 * p_n, ell * c_n)

    rows = []
    for d in range(2):
        prk, pik = pr[d][:, :, 1:ell + 1], pi[d][:, :, 1:ell + 1]
        if d == 1:
            prk, pik = prk[:, :, ::-1], pik[:, :, ::-1]
        prk = prk.transpose(0, 2, 1)[:, :, None, :]
        pik = pik.transpose(0, 2, 1)[:, :, None, :]
        cr, ci = c_re[d][:, None], c_im[d][:, None]
        for part in (cr * prk - ci * pik, -cr * pik - ci * prk):
            rows += [part * parity[q][:, None, None, None] for q in range(2)]
    wst_t = jnp.concatenate(rows, axis=-1).reshape(g_n, ell * c_n, 8 * p_n)

    al = jnp.stack([pr[0][:, :, ell], pi[0][:, :, ell], pr[1][:, :, ell], pi[1][:, :, ell]], axis=1)
    a_chunk = al.reshape(g_n // 2, 2, 4, p_n).transpose(0, 2, 1, 3).reshape(g_n // 2, 8 * p_n)
    return tsum_t.astype(BF16), mend_t.astype(BF16), wst_t.astype(BF16), a_chunk


def _s5_proj_body(hl_ref, hc_ref, w_ref, o_ref):
    n_lat = hl_ref.shape[0]
    n_pad = o_ref.shape[1] - LANES
    hl = hl_ref[...]
    if n_pad > n_lat:
        hl = jnp.concatenate([hl, jnp.zeros((n_pad - n_lat, hl.shape[1]), BF16)], axis=0)
    o_ref[:, 0:n_pad] = _dot_nt(w_ref[...], hl).astype(BF16)
    hc = hc_ref[...]
    hc = jnp.concatenate([hc, jnp.zeros((LANES - hc.shape[0], hc.shape[1]), BF16)], axis=0)
    o_ref[:, n_pad:] = _dot_nt(w_ref[...], hc).astype(BF16)


def _s5_end_body(u_ref, m_ref, o_ref):
    width = m_ref.shape[2]
    acc = None
    for q in range(2):
        u = u_ref[:, q].reshape(width, u_ref.shape[3])
        term = _dot(m_ref[q], u)
        acc = term if acc is None else acc + term
    o_ref[...] = acc.T


def _s5_scan_body(n_batch, per_b, n_ctx_chunks, lat_pad, e_ref, a_ref, o_ref):
    n_pairs = e_ref.shape[0]
    o_ref[...] = jnp.zeros(o_ref.shape, F32)
    sub = 8
    ctx_groups = n_ctx_chunks // sub
    n_groups = (per_b + n_ctx_chunks) // sub
    coef = [[a_ref[q, :, j * LANES:(j + 1) * LANES] for j in range(4)] for q in range(n_pairs)]

    def group(g, carry):
        new = list(carry)
        for b in range(n_batch):
            ctx = g < ctx_groups
            up = jnp.where(ctx, lat_pad + b * n_ctx_chunks + g * sub, b * per_b + (g - ctx_groups) * sub)
            down = jnp.where(ctx, lat_pad + (b + 1) * n_ctx_chunks - (g + 1) * sub,
                             (b + 1) * per_b - (g - ctx_groups + 1) * sub)
            for q in range(n_pairs):
                for d, base in enumerate((up, down)):
                    base = pl.multiple_of(base, sub)
                    lanes = slice(2 * d * LANES, (2 * d + 2) * LANES)
                    e = e_ref[q, pl.ds(base, sub), lanes]
                    k = ((b * n_pairs + q) * 2 + d) * 2
                    sr, si = new[k], new[k + 1]
                    ar, ai = coef[q][2 * d], coef[q][2 * d + 1]
                    rows = [None] * sub
                    for step in range(sub):
                        r = step if d == 0 else sub - 1 - step
                        rows[r] = jnp.concatenate([sr, si], axis=1)
                        er, ei = e[r:r + 1, :LANES], e[r:r + 1, LANES:]
                        sr, si = ar * sr - ai * si + er, ar * si + ai * sr + ei
                    o_ref[q, pl.ds(base, sub), lanes] = jnp.concatenate(rows, axis=0)
                    new[k], new[k + 1] = sr, si
        return tuple(new)

    z = jnp.zeros((1, LANES), F32)
    lax.fori_loop(0, n_groups, group, tuple(z for _ in range(n_batch * n_pairs * 4)))


def _s5_out_body(u_ref, t_ref, s_ref, w_ref, o_ref):
    width = t_ref.shape[0]
    u = u_ref[...].reshape(width, u_ref.shape[2])
    y = _dot(t_ref[...], u) + _dot_nt(w_ref[...], s_ref[...].astype(BF16))
    o_ref[...] = y.reshape(o_ref.shape)


def _s5_glu_body(n_lat_chunks, n_ctx_chunks, lat_pad, y_ref, w_ref, b_ref, o_ref):
    y = y_ref[...].T
    if lat_pad == n_lat_chunks:
        y = y[:n_lat_chunks + n_ctx_chunks]
    else:
        y = jnp.concatenate([y[:n_lat_chunks], y[lat_pad:lat_pad + n_ctx_chunks]], axis=0)
    z = jax.nn.gelu(y)
    o_ref[...] = (z * jax.nn.sigmoid(_dot(z.astype(BF16), w_ref[...]) + b_ref[...])).astype(o_ref.dtype)


def s5_mixer(h_all, w_u_t, tables, w_glu, b_glu, n_lat, n_ctx, n_batch):
    tsum_t, mend_t, wst_t, a_chunk = tables
    r, d = h_all.shape
    h2 = h_all.reshape(r // S5_CHUNK, S5_CHUNK * d)
    ell, g_n, c_n = S5_CHUNK, S5_GROUPS, S5_GROUP
    width = ell * c_n
    sw = 8 * S5_STATE
    bw = g_n * c_n
    n_lat_chunks = n_batch * n_lat // ell
    n_ctx_chunks = n_batch * n_ctx // ell
    assert n_lat_chunks % n_ctx_chunks == 0 and n_ctx_chunks % 16 == 0 and n_ctx_chunks <= LANES
    lat_pad = _round_up(n_lat_chunks, LANES)
    nch = lat_pad + LANES
    const = lambda t: (0, 0)
    u_t = pl.pallas_call(
        _s5_proj_body,
        grid=(ell,),
        in_specs=[pl.BlockSpec((n_lat_chunks, d), lambda t: (0, t)),
                  pl.BlockSpec((n_ctx_chunks, d), lambda t: (n_lat_chunks // n_ctx_chunks, t)),
                  pl.BlockSpec((bw, d), const)],
        out_specs=pl.BlockSpec((None, bw, nch), lambda t: (t, 0, 0)),
        out_shape=jax.ShapeDtypeStruct((ell, bw, nch), BF16),
        compiler_params=_cp(("arbitrary",)),
        name="s5_projection",
    )(h2, h2, w_u_t)
    u4 = u_t.reshape(ell, g_n, c_n, nch)
    ends = pl.pallas_call(
        _s5_end_body,
        grid=(g_n // 2,),
        in_specs=[pl.BlockSpec((ell, 2, c_n, nch), lambda p: (0, p, 0, 0)),
                  pl.BlockSpec((2, sw, width), lambda p: (p, 0, 0))],
        out_specs=pl.BlockSpec((None, nch, sw), lambda p: (p, 0, 0)),
        out_shape=jax.ShapeDtypeStruct((g_n // 2, nch, sw), F32),
        compiler_params=_cp(("arbitrary",)),
        name="s5_chunk_ends",
    )(u4, mend_t)
    pairs_per_step = 4
    assert (n_ctx // ell) % 8 == 0 and (n_lat // ell) % 8 == 0
    states = pl.pallas_call(
        functools.partial(_s5_scan_body, n_batch, n_lat // ell, n_ctx // ell, lat_pad),
        grid=(g_n // 2 // pairs_per_step,),
        in_specs=[pl.BlockSpec((pairs_per_step, nch, sw), lambda j: (j, 0, 0)),
                  pl.BlockSpec((pairs_per_step, 1, sw), lambda j: (j, 0, 0))],
        out_specs=pl.BlockSpec((pairs_per_step, nch, sw), lambda j: (j, 0, 0)),
        out_shape=jax.ShapeDtypeStruct((g_n // 2, nch, sw), F32),
        compiler_params=_cp(("arbitrary",), VMEM_LIMIT),
        name="s5_state_scan",
    )(ends, a_chunk.reshape(g_n // 2, 1, sw))
    y_t = pl.pallas_call(
        _s5_out_body,
        grid=(g_n,),
        in_specs=[pl.BlockSpec((ell, None, c_n, nch), lambda gi: (0, gi, 0, 0)),
                  pl.BlockSpec((None, width, width), lambda gi: (gi, 0, 0)),
                  pl.BlockSpec((None, nch, sw), lambda gi: (gi // 2, 0, 0)),
                  pl.BlockSpec((None, width, sw), lambda gi: (gi, 0, 0))],
        out_specs=pl.BlockSpec((ell, None, c_n, nch), lambda gi: (0, gi, 0, 0)),
        out_shape=jax.ShapeDtypeStruct((ell, g_n, c_n, nch), F32),
        compiler_params=_cp(("arbitrary",)),
        name="s5_outputs",
    )(u4, tsum_t, states, wst_t)
    n_chunks = n_lat_chunks + n_ctx_chunks
    ya = pl.pallas_call(
        functools.partial(_s5_glu_body, n_lat_chunks, n_ctx_chunks, lat_pad),
        grid=(ell,),
        in_specs=[pl.BlockSpec((None, bw, nch), lambda t: (t, 0, 0)),
                  pl.BlockSpec((bw, bw), const),
                  pl.BlockSpec((1, bw), const)],
        out_specs=pl.BlockSpec((n_chunks, bw), lambda t: (0, t)),
        out_shape=jax.ShapeDtypeStruct((n_chunks, ell * bw), BF16),
        compiler_params=_cp(("arbitrary",)),
        name="s5_glu",
    )(y_t.reshape(ell, bw, nch), w_glu, b_glu)
    return ya.reshape(r, bw)


def rope_tables(n_lat):
    half = GQ_HEAD_DIM // 2
    quarter = half // 2
    t = np.arange(n_lat)
    freqs = ROPE_THETA ** (-np.arange(quarter, dtype=np.float64) / quarter)
    ang_r = (t // GRID_W)[:, None] * freqs
    ang_c = (t % GRID_W)[:, None] * freqs
    ang = np.concatenate([ang_r, ang_r, ang_c, ang_c], axis=1)
    sign = np.concatenate([-np.ones(quarter), np.ones(quarter)] * 2)
    cos = np.concatenate([np.cos(ang), np.ones((ROW_TILE, GQ_HEAD_DIM))], axis=0)
    sin = np.concatenate([np.sin(ang) * sign, np.zeros((ROW_TILE, GQ_HEAD_DIM))], axis=0)
    tab = np.concatenate([cos, cos, sin, sin], axis=1)
    return jnp.asarray(tab, F32)


def _group_ones(width, group):
    i = np.arange(width)
    return jnp.asarray((i[:, None] // group) == (i[None, :] // group), BF16)


def _group_mean_sq(x, ones_blk, group):
    sq = x * x
    hi = sq.astype(BF16)
    lo = (sq - hi.astype(F32)).astype(BF16)
    return (_dot(hi, ones_blk) + _dot(lo, ones_blk)) * (1.0 / group)


def _rope(x, cos, sin):
    w = x.shape[-1]
    q = GQ_HEAD_DIM // 4
    lane = lax.broadcasted_iota(I32, x.shape, 1)
    first = (lane % (2 * q)) < q
    partner = jnp.where(first, pltpu.roll(x, w - q, 1), pltpu.roll(x, q, 1))
    return x * cos + partner * sin


def _gq_prep_body(q_ref, k_ref, v_ref, cs_ref, gq_ref, gk_ref, oq_ref, ok_ref, qm_ref, kr_ref, va_ref):
    cs = cs_ref[...]
    cos1, sin1 = cs[:, :LANES], cs[:, LANES:]
    q = q_ref[...].astype(F32)
    qn = q * lax.rsqrt(_group_mean_sq(q, oq_ref[...], GQ_HEAD_DIM) + EPS) * gq_ref[...]
    n_pairs = q.shape[1] // LANES
    qr = _rope(qn, jnp.concatenate([cos1] * n_pairs, axis=1), jnp.concatenate([sin1] * n_pairs, axis=1))
    qr = (qr * (GQ_HEAD_DIM ** -0.5 * LOG2E)).astype(BF16)
    lane = lax.broadcasted_iota(I32, (q.shape[0], LANES), 1)
    heads_per_kv = GQ_HEADS // GQ_KV_HEADS
    for h in range(GQ_HEADS):
        pair = qr[:, (h // 2) * LANES:(h // 2 + 1) * LANES]
        kv = h // heads_per_kv
        if h % 2 != kv:
            pair = pltpu.roll(pair, GQ_HEAD_DIM, 1)
        keep = (lane >= kv * GQ_HEAD_DIM) & (lane < (kv + 1) * GQ_HEAD_DIM)
        qm_ref[h] = jnp.where(keep, pair, jnp.zeros_like(pair))
    k = k_ref[...].astype(F32)
    kn = k * lax.rsqrt(_group_mean_sq(k, ok_ref[...], GQ_HEAD_DIM) + EPS) * gk_ref[...]
    kr_ref[...] = _rope(kn, cos1, sin1).astype(BF16)
    va_ref[...] = jnp.concatenate([v_ref[...], jnp.ones(v_ref.shape, BF16)], axis=1)


def gq_prepare(pb, col_q, col_k, col_v, cs_tab, g_q, g_k, n_lat, n_ctx, n_batch):
    r = pb.shape[0]
    tm = ROW_TILE
    assert n_ctx == tm and n_lat % tm == 0
    nb = n_lat // tm
    n_lat_tiles = n_batch * nb

    def tab_idx(i):
        return (jnp.where(i < n_lat_tiles, i % nb, nb), 0)

    def kv_idx(i):
        lat = (i // nb) * (nb + 1) + i % nb
        ctx = (i - n_lat_tiles) * (nb + 1) + nb
        return (jnp.where(i < n_lat_tiles, lat, ctx), 0)

    qw = GQ_HEADS * GQ_HEAD_DIM
    const = lambda i: (0, 0)
    gq = jnp.tile(g_q.astype(F32), GQ_HEADS).reshape(1, qw)
    gk = jnp.tile(g_k.astype(F32), GQ_KV_HEADS).reshape(1, LANES)
    n_keys = n_batch * (n_lat + n_ctx)
    return pl.pallas_call(
        _gq_prep_body,
        grid=(r // tm,),
        in_specs=[pl.BlockSpec((tm, qw), lambda i: (i, col_q // qw)),
                  pl.BlockSpec((tm, LANES), lambda i: (i, col_k // LANES)),
                  pl.BlockSpec((tm, LANES), lambda i: (i, col_v // LANES)),
                  pl.BlockSpec((tm, 2 * LANES), tab_idx),
                  pl.BlockSpec((1, qw), const),
                  pl.BlockSpec((1, LANES), const),
                  pl.BlockSpec((qw, qw), const),
                  pl.BlockSpec((LANES, LANES), const)],
        out_specs=[pl.BlockSpec((GQ_HEADS, tm, LANES), lambda i: (0, i, 0)),
                   pl.BlockSpec((tm, LANES), kv_idx),
                   pl.BlockSpec((tm, 2 * LANES), kv_idx)],
        out_shape=[jax.ShapeDtypeStruct((GQ_HEADS, r, LANES), BF16),
                   jax.ShapeDtypeStruct((n_keys, LANES), BF16),
                   jax.ShapeDtypeStruct((n_keys, 2 * LANES), BF16)],
        compiler_params=_cp(("arbitrary",)),
        name="gq_prepare",
    )(pb, pb, pb, cs_tab, gq, gk, _group_ones(qw, GQ_HEAD_DIM), _group_ones(LANES, GQ_HEAD_DIM))


def _gq_flash_body(q_ref, k_ref, v_ref, o_ref, m_sc, acc_sc):
    kj = pl.program_id(2)
    n_h, tq, _ = q_ref.shape

    @pl.when(kj == 0)
    def _():
        m_sc[...] = jnp.full(m_sc.shape, -jnp.inf, F32)
        acc_sc[...] = jnp.zeros(acc_sc.shape, F32)

    k = k_ref[...]
    v = v_ref[...]
    sub = min(tq, 256)
    for c in range(n_h * tq // sub):
        rows = slice(c * sub, (c + 1) * sub)
        h, r0 = divmod(c * sub, tq)
        s = _dot_nt(q_ref[h, r0:r0 + sub, :], k)
        m_prev = m_sc[rows, :]
        m_new = jnp.maximum(m_prev, jnp.max(s, axis=-1, keepdims=True))
        p = jnp.exp2(s - m_new)
        acc_sc[rows, :] = jnp.exp2(m_prev - m_new) * acc_sc[rows, :] + _dot(p.astype(BF16), v)
        m_sc[rows, :] = m_new

    @pl.when(kj == pl.num_programs(2) - 1)
    def _():
        lane = lax.broadcasted_iota(I32, (tq, LANES), 1)
        heads_per_kv = n_h // GQ_KV_HEADS
        for j in range(n_h // 2):
            kv = (2 * j) // heads_per_kv
            halves = []
            for h in (2 * j, 2 * j + 1):
                a = acc_sc[h * tq:(h + 1) * tq, :]
                halves.append(a[:, :LANES] / a[:, LANES:LANES + 1])
            lo, hi = halves
            if kv == 0:
                hi = pltpu.roll(hi, GQ_HEAD_DIM, 1)
            else:
                lo = pltpu.roll(lo, GQ_HEAD_DIM, 1)
            o_ref[:, j * LANES:(j + 1) * LANES] = jnp.where(lane < GQ_HEAD_DIM, lo, hi).astype(o_ref.dtype)


def gq_attention(qm, keys, vals, n_rows_out, tq, tk, q_blk, k_blk, o_blk, n_q, n_k, n_batch):
    n_h = qm.shape[0]
    return pl.pallas_call(
        _gq_flash_body,
        grid=(n_batch, n_q, n_k),
        in_specs=[pl.BlockSpec((n_h, tq, LANES), lambda b, i, j: (0, q_blk(b, i), 0)),
                  pl.BlockSpec((tk, LANES), lambda b, i, j: (k_blk(b, j), 0)),
                  pl.BlockSpec((tk, 2 * LANES), lambda b, i, j: (k_blk(b, j), 0))],
        out_specs=pl.BlockSpec((tq, n_h * GQ_HEAD_DIM), lambda b, i, j: (o_blk(b, i), 0)),
        out_shape=jax.ShapeDtypeStruct((n_rows_out, n_h * GQ_HEAD_DIM), BF16),
        scratch_shapes=[pltpu.VMEM((n_h * tq, 1), F32), pltpu.VMEM((n_h * tq, 2 * LANES), F32)],
        compiler_params=_cp(("arbitrary", "arbitrary", "arbitrary"), VMEM_LIMIT),
        name="gq_attention",
    )(qm, keys, vals)


def _largest_divisor(n, cap):
    return max(d for d in range(1, cap + 1) if n % d == 0)


def gq_mixer(pb, col_q, col_k, col_v, cs_tab, g_q, g_k, n_lat, n_ctx, n_batch, with_ctx):
    qm, keys, vals = gq_prepare(pb, col_q, col_k, col_v, cs_tab, g_q, g_k, n_lat, n_ctx, n_batch)
    tq = 512
    tk = LANES * _largest_divisor((n_lat + n_ctx) // LANES, 22)
    n_q = n_lat // tq
    per_b = (n_lat + n_ctx) // tk
    lat_blk = lambda b, i: b * n_q + i
    y_lat = gq_attention(qm, keys, vals, n_batch * n_lat, tq, tk, lat_blk, lambda b, j: b * per_b + j, lat_blk,
                         n_q, per_b, n_batch)
    if not with_ctx:
        return y_lat, None
    tc = n_ctx
    y_ctx = gq_attention(qm, keys, vals, n_batch * n_ctx, tc, tc, lambda b, i: n_batch * n_lat // tc + b,
                         lambda b, j: b * ((n_lat + n_ctx) // tc) + n_lat // tc, lambda b, i: b, 1, 1, n_batch)
    return y_lat, y_ctx


def na_bias_tables(rpb, n_img_rows, n_ctx):
    tr = NA_TILE_ROWS
    nt = n_img_rows // tr
    assert nt >= 4
    kr = min(NA_ROWS, n_img_rows)
    n_heads = rpb.shape[0]
    qcol = np.arange(GRID_W)[:, None]
    kcol = np.arange(GRID_W)[None, :]
    dc = np.clip(kcol - qcol + NA_COLS - 1, 0, 2 * NA_COLS - 2)
    oh_c = (dc[None] == np.arange(2 * NA_COLS - 1)[:, None, None]).astype(np.float32)
    cstart = np.clip(qcol - NA_COLS // 2, 0, GRID_W - NA_COLS)
    col_ok = (kcol >= cstart) & (kcol < cstart + NA_COLS)
    by_col = jnp.einsum('hrd,dqk->hrqk', rpb.astype(F32), jnp.asarray(oh_c), precision=HIGHEST)
    by_col = jnp.where(jnp.asarray(col_ok)[None, None], by_col * LOG2E, MASK_NEG)
    masked = jnp.full((n_heads, GRID_W, GRID_W), MASK_NEG, F32)
    ctx_cols = jnp.zeros((n_heads, GRID_W, n_ctx), F32)
    classes = []
    for i in (0, 1, nt - 1):
        wb = int(np.clip(i - 1, 0, nt - 3))
        qrow = (i * tr + np.arange(tr))[:, None]
        krow = (wb * tr + np.arange(3 * tr))[None, :]
        rs = np.clip(qrow - kr // 2, 0, n_img_rows - kr)
        row_ok = (krow >= rs) & (krow < rs + kr)
        dr = np.clip(krow - qrow + NA_ROWS - 1, 0, 2 * NA_ROWS - 2)
        q_rows = []
        for a in range(tr):
            blocks = [by_col[:, int(dr[a, b])] if row_ok[a, b] else masked for b in range(3 * tr)]
            q_rows.append(jnp.concatenate(blocks + [ctx_cols], axis=-1))
        classes.append(jnp.concatenate(q_rows, axis=1))
    return jnp.stack(classes, axis=0)


def _pair_attention(q_pair, k_pair, v_pair, bias_fn):
    lane = lax.broadcasted_iota(I32, q_pair.shape, 1)
    v_aug = jnp.concatenate([v_pair, jnp.ones(v_pair.shape, BF16)], axis=1)
    out = None
    for hh in range(2):
        mine = (lane >= hh * NA_HEAD_DIM) & (lane < (hh + 1) * NA_HEAD_DIM)
        qm = jnp.where(mine, q_pair, jnp.zeros_like(q_pair))
        s = _dot_nt(qm, k_pair)
        b = bias_fn(hh)
        if b is not None:
            s = s + b
        m = jnp.max(s, axis=-1, keepdims=True)
        o_aug = _dot(jnp.exp2(s - m).astype(BF16), v_aug)
        o = o_aug[:, :LANES] / o_aug[:, LANES:LANES + 1]
        out = o if out is None else jnp.where(mine, o, out)
    return out


def _na_body(q_ref, k0_ref, k1_ref, k2_ref, kc_ref, v0_ref, v1_ref, v2_ref, vc_ref, b_ref, o_ref):
    for j in range(NA_HEADS // 2):
        sl = slice(j * LANES, (j + 1) * LANES)
        q_pair = q_ref[:, sl]
        k_pair = jnp.concatenate([k0_ref[:, sl], k1_ref[:, sl], k2_ref[:, sl], kc_ref[:, sl]], axis=0)
        v_pair = jnp.concatenate([v0_ref[:, sl], v1_ref[:, sl], v2_ref[:, sl], vc_ref[:, sl]], axis=0)
        o = _pair_attention(q_pair, k_pair, v_pair, lambda hh: b_ref[0, 2 * j + hh])
        o_ref[:, sl] = o.astype(o_ref.dtype)


def na_mixer(pb, col_q, col_k, col_v, bias_tab, n_lat, n_ctx, n_batch):
    tm = NA_TILE_ROWS * GRID_W
    assert n_ctx == tm
    w = NA_HEADS * NA_HEAD_DIM
    nt = n_lat // tm
    n_keys = 3 * tm + n_ctx
    cq, ck, cv = col_q // w, col_k // w, col_v // w
    ctx0 = n_batch * nt

    def win(o):
        return lambda b, i: (b * nt + jnp.clip(i - 1, 0, nt - 3) + o)

    def cls(b, i):
        return (jnp.where(i == 0, 0, jnp.where(i == nt - 1, 2, 1)), 0, 0, 0)

    kspecs = [pl.BlockSpec((tm, w), (lambda b, i, f=win(o): (f(b, i), ck))) for o in range(3)]
    vspecs = [pl.BlockSpec((tm, w), (lambda b, i, f=win(o): (f(b, i), cv))) for o in range(3)]
    return pl.pallas_call(
        _na_body,
        grid=(n_batch, nt),
        in_specs=[pl.BlockSpec((tm, w), lambda b, i: (b * nt + i, cq))] + kspecs
        + [pl.BlockSpec((tm, w), lambda b, i: (ctx0 + b, ck))] + vspecs
        + [pl.BlockSpec((tm, w), lambda b, i: (ctx0 + b, cv)),
           pl.BlockSpec((1, NA_HEADS, tm, n_keys), cls)],
        out_specs=pl.BlockSpec((tm, w), lambda b, i: (b * nt + i, 0)),
        out_shape=jax.ShapeDtypeStruct((n_batch * n_lat, w), BF16),
        compiler_params=_cp(("arbitrary", "arbitrary"), VMEM_LIMIT),
        name="na_attention",
    )(pb, pb, pb, pb, pb, pb, pb, pb, pb, bias_tab)


def _ctx_mha_body(q_ref, k_ref, v_ref, o_ref):
    for j in range(NA_HEADS // 2):
        sl = slice(j * LANES, (j + 1) * LANES)
        o = _pair_attention(q_ref[:, sl], k_ref[:, sl], v_ref[:, sl], lambda hh: None)
        o_ref[:, sl] = o.astype(o_ref.dtype)


def na_ctx_attention(pb, col_q, col_k, col_v, n_lat, n_ctx, n_batch):
    w = NA_HEADS * NA_HEAD_DIM
    ctx0 = n_batch * n_lat // n_ctx
    spec = lambda c: pl.BlockSpec((n_ctx, w), lambda b: (ctx0 + b, c // w))
    return pl.pallas_call(
        _ctx_mha_body,
        grid=(n_batch,),
        in_specs=[spec(col_q), spec(col_k), spec(col_v)],
        out_specs=pl.BlockSpec((n_ctx, w), lambda b: (b, 0)),
        out_shape=jax.ShapeDtypeStruct((n_batch * n_ctx, w), BF16),
        compiler_params=_cp(("arbitrary",)),
        name="na_ctx_attention",
    )(pb, pb, pb)


def _mlstm_body(n_batch, *refs):
    n_in = 8 * n_batch
    ins, bias_ref = refs[:n_in], refs[n_in]
    hf_ref, hb_ref, c_ref, n_ref, m_ref = refs[n_in + 1:]

    @pl.when(pl.program_id(0) == 0)
    def _():
        c_ref[...] = jnp.zeros(c_ref.shape, F32)
        n_ref[...] = jnp.zeros(n_ref.shape, F32)
        m_ref[...] = jnp.full(m_ref.shape, NEG_INIT, F32)

    tok = lax.broadcasted_iota(I32, (ML_CHUNK, ML_CHUNK), 0)
    src = lax.broadcasted_iota(I32, (ML_CHUNK, ML_CHUNK), 1)
    masks = (src <= tok, src >= tok)
    gates = []
    for b in range(n_batch):
        for d in range(2):
            g = ins[(b * 2 + d) * 4 + 3][...] + bias_ref[...]
            lf_cum = jnp.dot(masks[d].astype(F32), jax.nn.log_sigmoid(g), precision=HIGHEST,
                             preferred_element_type=F32)
            gates.append((g, lf_cum))
    gates = [(g, lf_cum, g.T, lf_cum.T) for g, lf_cum in gates]
    chains = []
    for b in range(n_batch):
        for d in range(2):
            g, lf_cum, g_t, lf_cum_t = gates[b * 2 + d]
            for h in range(ML_HEADS):
                ci, cf = d * 2 * ML_HEADS + h, d * 2 * ML_HEADS + ML_HEADS + h
                idx = (b * 2 + d) * ML_HEADS + h
                chains.append(dict(
                    b=b, d=d, h=h, idx=idx, bt_col=lf_cum[:, cf:cf + 1], bt_row=lf_cum_t[cf:cf + 1, :],
                    li_col=g[:, ci:ci + 1], li_row=g_t[ci:ci + 1, :], m_prev=m_ref[idx][:, 0:1],
                    c_prev=c_ref[idx], n_prev=n_ref[idx]))
    for c in chains:
        c['dmat'] = jnp.where(masks[c['d']], c['bt_col'] - c['bt_row'] + c['li_row'], -jnp.inf)
        c['inter'] = c['bt_col'] + c['m_prev']
    for c in chains:
        c['mt'] = jnp.maximum(c['inter'], jnp.max(c['dmat'], axis=-1, keepdims=True))
    for c in chains:
        refs_c = ins[(c['b'] * 2 + c['d']) * 4:(c['b'] * 2 + c['d']) * 4 + 3]
        sl = slice(c['h'] * ML_HEAD_DIM, (c['h'] + 1) * ML_HEAD_DIM)
        c['q'], c['k'], c['v'] = (r[:, sl] for r in refs_c)
        c['s'] = _dot_nt(c['q'], c['k']) * jnp.exp(c['dmat'] - c['mt'])
        c['w_inter'] = jnp.exp(c['inter'] - c['mt'])
    for c in chains:
        num = _dot(c['s'].astype(BF16), c['v']) + c['w_inter'] * _dot_nt(c['q'], c['c_prev'].astype(BF16))
        qn = jnp.sum(c['q'].astype(F32) * c['n_prev'], axis=-1, keepdims=True)
        den = jnp.sum(c['s'], axis=-1, keepdims=True) + c['w_inter'] * qn
        h_out = num / jnp.maximum(jnp.abs(den), jnp.exp(-c['mt']))
        h_ref = hb_ref if c['d'] else hf_ref
        h_ref[c['b'], :, c['h'] * ML_HEAD_DIM:(c['h'] + 1) * ML_HEAD_DIM] = h_out
    new_state = []
    for c in chains:
        b_last = c['bt_col'][0:1, :] if c['d'] else c['bt_col'][ML_CHUNK - 1:ML_CHUNK, :]
        g_col = b_last - c['bt_col'] + c['li_col']
        m_new = jnp.maximum(b_last + c['m_prev'], jnp.max(g_col, axis=0, keepdims=True))
        wg = jnp.exp(g_col - m_new)
        decay = jnp.exp(b_last + c['m_prev'] - m_new)
        c_new = decay * c['c_prev'] + _dot_tn((wg * c['v'].astype(F32)).astype(BF16), c['k'])
        n_new = decay * c['n_prev'] + jnp.sum(wg * c['k'].astype(F32), axis=0, keepdims=True)
        new_state.append((c['idx'], c_new, n_new, m_new))
    for idx, c_new, n_new, m_new in new_state:
        c_ref[idx] = c_new
        n_ref[idx] = n_new
        m_ref[idx] = jnp.broadcast_to(m_new, (1, LANES))


def mlstm_mixer(pb, pa, col_q, col_k, col_v, col_g, bias, n_lat, n_ctx, n_batch):
    w = ML_HEADS * ML_HEAD_DIM
    tc = ML_CHUNK
    nl, nc = n_lat // tc, n_ctx // tc

    def fwd_pos(i):
        return jnp.where(i < nc, nl + i, i - nc)

    def bwd_pos(i):
        return jnp.where(i < nc, nl + nc - 1 - i, nl - 1 - (i - nc))

    def row_blk(b, pos):
        return jnp.where(pos < nl, b * nl + pos, n_batch * nl + b * nc + pos - nl)

    def specs(b, pos_fn):
        return [pl.BlockSpec((tc, w), lambda i, c=c: (row_blk(b, pos_fn(i)), c // w)) for c in (col_q, col_k, col_v)] + [
            pl.BlockSpec((tc, LANES), lambda i: (row_blk(b, pos_fn(i)), col_g // LANES))]

    in_specs, operands = [], []
    for b in range(n_batch):
        for pos_fn in (fwd_pos, bwd_pos):
            in_specs += specs(b, pos_fn)
            operands += [pb, pb, pb, pa]
    n_st = 2 * ML_HEADS * n_batch
    out_shape = jax.ShapeDtypeStruct((n_batch, n_lat + n_ctx, w), F32)
    return pl.pallas_call(
        functools.partial(_mlstm_body, n_batch),
        grid=(nl + nc,),
        in_specs=in_specs + [pl.BlockSpec((1, LANES), lambda i: (0, 0))],
        out_specs=[pl.BlockSpec((n_batch, tc, w), lambda i: (0, fwd_pos(i), 0)),
                   pl.BlockSpec((n_batch, tc, w), lambda i: (0, bwd_pos(i), 0))],
        out_shape=[out_shape, out_shape],
        scratch_shapes=[pltpu.VMEM((n_st, ML_HEAD_DIM, ML_HEAD_DIM), F32),
                        pltpu.VMEM((n_st, 1, ML_HEAD_DIM), F32),
                        pltpu.VMEM((n_st, 1, LANES), F32)],
        compiler_params=_cp(("arbitrary",)),
        name="mlstm_chunks",
    )(*operands, bias)


def _merge_body(n_lat_tiles, ya_ref, ybl_ref, ybc_ref, hf_ref, hb_ref, o_ref, ydl_ref, ydc_ref, gate_ref, x_ref,
                mod_ref, mlg_ref, wbr_ref, wout_ref, g2_ref, wr_ref, xo_ref, h2_ref, st_ref):
    d = x_ref.shape[1]
    is_ctx = pl.program_id(0) >= n_lat_tiles
    yb = jnp.where(is_ctx, ybc_ref[...], ybl_ref[...])
    yd = jnp.where(is_ctx, ydc_ref[...], ydl_ref[...])
    hs = hf_ref[...] + hb_ref[...]
    segs = []
    for h in range(ML_HEADS):
        seg = hs[:, h * ML_HEAD_DIM:(h + 1) * ML_HEAD_DIM]
        segs.append(seg * lax.rsqrt(jnp.mean(seg * seg, axis=-1, keepdims=True) + EPS))
    ym = jnp.concatenate(segs, axis=1) * mlg_ref[...] * jax.nn.sigmoid(o_ref[...].astype(F32))
    ys = (ya_ref[...], yb, ym.astype(BF16), yd)
    merged = None
    for i in range(N_BRANCHES):
        term = gate_ref[:, i * d:(i + 1) * d].astype(F32) * _dot(ys[i], wbr_ref[i])
        merged = term if merged is None else merged + term
    y = _dot(merged.astype(BF16), wout_ref[...])
    mod = mod_ref[0]
    x_new = x_ref[...] + mod[2:3] * y
    xo_ref[...] = x_new
    h2 = _rms_mod(x_new, g2_ref[...], mod[3:4], mod[4:5])
    h2_ref[...] = h2
    st_ref[...] = jax.nn.sigmoid(_dot_nt(wr_ref[...], h2.astype(BF16)))


def merge_layer(ya, yb, yd, hf, hb, pb, col_o, gate, x_all, modtab, mlg, wbr, wout, g2, wr_t,
                n_rows, n_lat_rows, n_batch):
    d = x_all.shape[1]
    tm = ROW_TILE
    w = BRANCH_WIDTH
    per_b = n_lat_rows // n_batch // tm
    row = lambda i: (i, 0)
    const2 = lambda i: (0, 0)
    n_lat_tiles = n_lat_rows // tm
    lat_row = lambda i: (jnp.minimum(i, n_lat_tiles - 1), 0)
    ctx_row = lambda i: (jnp.clip(i - n_lat_tiles, 0, n_batch - 1), 0)

    def seq(i):
        lat = i < n_lat_tiles
        return (jnp.where(lat, i // per_b, i - n_lat_tiles), jnp.where(lat, i % per_b, per_b), 0)

    (yb_lat, yb_ctx), (yd_lat, yd_ctx) = yb, yd
    if yb_ctx is None:
        yb_ctx, yd_ctx = yb_lat, yd_lat
    return pl.pallas_call(
        functools.partial(_merge_body, n_lat_tiles),
        grid=(n_rows // tm,),
        in_specs=[pl.BlockSpec((tm, w), row), pl.BlockSpec((tm, w), lat_row), pl.BlockSpec((tm, w), ctx_row),
                  pl.BlockSpec((None, tm, w), seq), pl.BlockSpec((None, tm, w), seq),
                  pl.BlockSpec((tm, w), lambda i: (i, col_o // w)),
                  pl.BlockSpec((tm, w), lat_row), pl.BlockSpec((tm, w), ctx_row),
                  pl.BlockSpec((tm, N_BRANCHES * d), row),
                  pl.BlockSpec((tm, d), row),
                  pl.BlockSpec((1, N_MOD, d), lambda i: (jnp.minimum(i // per_b, n_batch), 0, 0)),
                  pl.BlockSpec((1, w), const2),
                  pl.BlockSpec((N_BRANCHES, w, d), lambda i: (0, 0, 0)),
                  pl.BlockSpec((d, d), const2), pl.BlockSpec((1, d), const2),
                  pl.BlockSpec((LANES, d), const2)],
        out_specs=[pl.BlockSpec((tm, d), row), pl.BlockSpec((tm, d), row),
                   pl.BlockSpec((LANES, tm), lambda i: (0, i))],
        out_shape=[jax.ShapeDtypeStruct((n_rows, d), F32), jax.ShapeDtypeStruct((n_rows, d), F32),
                   jax.ShapeDtypeStruct((LANES, n_rows), F32)],
        compiler_params=_cp(("arbitrary",), VMEM_LIMIT),
        name="merge_layer",
    )(ya, yb_lat, yb_ctx, hf, hb, pb, yd_lat, yd_ctx, gate, x_all, modtab, mlg, wbr, wout, g2, wr_t)


def _router_body(s_ref, b_ref, e_ref, w_ref, rank_ref, cnt_ref, base_sc):
    @pl.when(pl.program_id(0) == 0)
    def _():
        base_sc[...] = jnp.zeros(base_sc.shape, F32)

    tm = s_ref.shape[1]
    s = s_ref[0:N_EXPERTS, :]
    sel = s + b_ref[0:N_EXPERTS, :]
    row = lambda a, e: a[e:e + 1, :]
    best, grp = None, None
    for g in range(N_EXPERT_GROUPS):
        v = [row(sel, EXPERTS_PER_GROUP * g + k) for k in range(EXPERTS_PER_GROUP)]
        gs = None
        for a in range(EXPERTS_PER_GROUP):
            for c in range(a + 1, EXPERTS_PER_GROUP):
                gs = v[a] + v[c] if gs is None else jnp.maximum(gs, v[a] + v[c])
        if best is None:
            best, grp = gs, jnp.zeros((1, tm), I32)
        else:
            better = gs > best
            grp = jnp.where(better, g, grp)
            best = jnp.where(better, gs, best)
    vals, affs = [], []
    for k in range(EXPERTS_PER_GROUP):
        vk, sk = row(sel, k), row(s, k)
        for g in range(1, N_EXPERT_GROUPS):
            hit = grp == g
            vk = jnp.where(hit, row(sel, EXPERTS_PER_GROUP * g + k), vk)
            sk = jnp.where(hit, row(s, EXPERTS_PER_GROUP * g + k), sk)
        vals.append(vk)
        affs.append(sk)
    i1, b1, w1 = jnp.zeros((1, tm), I32), vals[0], affs[0]
    for k in range(1, EXPERTS_PER_GROUP):
        better = vals[k] > b1
        i1 = jnp.where(better, k, i1)
        w1 = jnp.where(better, affs[k], w1)
        b1 = jnp.where(better, vals[k], b1)
    i2 = jnp.zeros((1, tm), I32)
    b2 = jnp.full((1, tm), -jnp.inf, F32)
    w2 = jnp.zeros((1, tm), F32)
    for k in range(EXPERTS_PER_GROUP):
        cand = (i1 != k) & (vals[k] > b2)
        i2 = jnp.where(cand, k, i2)
        w2 = jnp.where(cand, affs[k], w2)
        b2 = jnp.where(cand, vals[k], b2)
    e1 = grp * EXPERTS_PER_GROUP + i1
    e2 = grp * EXPERTS_PER_GROUP + i2
    tot = w1 + w2
    e_ref[...] = jnp.concatenate([e1, e2], axis=0)
    wpad = jnp.concatenate([w1 / tot, w2 / tot, jnp.zeros((6, tm), F32)], axis=0)
    w_ref[...] = wpad.T
    ids = lax.broadcasted_iota(I32, (N_EXPERTS, tm), 0)
    oh1 = (ids == e1).astype(F32)
    oh2 = (ids == e2).astype(F32)
    oh = oh1 + oh2
    before = (lax.broadcasted_iota(I32, (tm, tm), 0) < lax.broadcasted_iota(I32, (tm, tm), 1)).astype(BF16)
    prior = _dot(oh.astype(BF16), before) + base_sc[...]
    r1 = jnp.sum(oh1 * prior, axis=0, keepdims=True)
    r2 = jnp.sum(oh2 * prior, axis=0, keepdims=True)
    rank_ref[...] = jnp.concatenate([r1, r2], axis=0).astype(I32)
    base = base_sc[...] + jnp.sum(oh, axis=1, keepdims=True)
    base_sc[...] = base
    cnt_ref[...] = jnp.broadcast_to(base, cnt_ref.shape).astype(I32)


def route(s_t, b_router):
    n_rows = s_t.shape[1]
    tm = ROW_TILE
    b_col = jnp.pad(b_router.astype(F32), (0, LANES - N_EXPERTS)).reshape(LANES, 1)
    return pl.pallas_call(
        _router_body,
        grid=(n_rows // tm,),
        in_specs=[pl.BlockSpec((LANES, tm), lambda i: (0, i)), pl.BlockSpec((LANES, 1), lambda i: (0, 0))],
        out_specs=[pl.BlockSpec((TOP_K, tm), lambda i: (0, i)), pl.BlockSpec((tm, 8), lambda i: (i, 0)),
                   pl.BlockSpec((TOP_K, tm), lambda i: (0, i)), pl.BlockSpec((N_EXPERTS, LANES), lambda i: (0, 0))],
        out_shape=[jax.ShapeDtypeStruct((TOP_K, n_rows), I32), jax.ShapeDtypeStruct((n_rows, 8), F32),
                   jax.ShapeDtypeStruct((TOP_K, n_rows), I32), jax.ShapeDtypeStruct((N_EXPERTS, LANES), I32)],
        scratch_shapes=[pltpu.VMEM((N_EXPERTS, 1), F32)],
        compiler_params=_cp(("arbitrary",)),
        name="moe_router",
    )(s_t, b_col)


def _row_copy(src_ref, src_row, dst_ref, dst_row, sem):
    return pltpu.make_async_copy(src_ref.at[pl.ds(src_row, 1), :], dst_ref.at[pl.ds(dst_row, 1), :], sem)


def _dispatch_body(dest_ref, h_ref, buf_in_ref, buf_ref, sem):
    del buf_in_ref
    tm = h_ref.shape[0]

    def issue(t, carry):
        for k in range(TOP_K):
            _row_copy(h_ref, t, buf_ref, dest_ref[0, k, t], sem).start(priority=k)
        return carry

    lax.fori_loop(0, tm, issue, 0, unroll=8)
    for k in range(TOP_K):
        pltpu.make_async_copy(h_ref, buf_ref.at[pl.ds(0, tm), :], sem).wait()


def moe_dispatch(h2, dest3, buf0):
    n_rows, d = h2.shape
    tm = ROW_TILE
    return pl.pallas_call(
        _dispatch_body,
        grid=(n_rows // tm,),
        in_specs=[pl.BlockSpec((1, TOP_K, tm), lambda i: (i, 0, 0), memory_space=pltpu.SMEM),
                  pl.BlockSpec((tm, d), lambda i: (i, 0)),
                  pl.BlockSpec(memory_space=pl.ANY)],
        out_specs=pl.BlockSpec(memory_space=pl.ANY),
        out_shape=jax.ShapeDtypeStruct(buf0.shape, buf0.dtype),
        scratch_shapes=[pltpu.SemaphoreType.DMA(())],
        input_output_aliases={2: 0},
        compiler_params=_cp(("arbitrary",)),
        name="moe_dispatch",
    )(dest3, h2, buf0)


def _expert_body(be_ref, nu_ref, x_ref, w1_ref, w3_ref, w2_ref, o_ref, w1_sc, w3_sc, w2_sc):
    i = pl.program_id(0)
    changed = jnp.logical_or(i == 0, be_ref[i] != be_ref[jnp.maximum(i - 1, 0)])
    used = i < nu_ref[0]

    @pl.when(jnp.logical_and(changed, used))
    def _():
        w1_sc[...] = w1_ref[0].astype(BF16)
        w3_sc[...] = w3_ref[0].astype(BF16)
        w2_sc[...] = w2_ref[0].astype(BF16)

    @pl.when(used)
    def _():
        x = x_ref[...].astype(BF16)
        a = _dot(x, w1_sc[...])
        mid = (a * jax.nn.sigmoid(a)) * _dot(x, w3_sc[...])
        o_ref[...] = _dot(mid.astype(BF16), w2_sc[...])

    @pl.when(jnp.logical_not(used))
    def _():
        o_ref[...] = jnp.zeros(o_ref.shape, o_ref.dtype)


def moe_experts(buf, blk_expert, n_used, w1, w3, w2, layer, blk):
    n_slots, d = buf.shape
    de = w1.shape[3]
    grid_spec = pltpu.PrefetchScalarGridSpec(
        num_scalar_prefetch=2,
        grid=(n_slots // blk,),
        in_specs=[pl.BlockSpec((blk, d), lambda i, be, nu: (i, 0)),
                  pl.BlockSpec((None, 1, d, de), lambda i, be, nu: (layer, be[i], 0, 0)),
                  pl.BlockSpec((None, 1, d, de), lambda i, be, nu: (layer, be[i], 0, 0)),
                  pl.BlockSpec((None, 1, de, d), lambda i, be, nu: (layer, be[i], 0, 0))],
        out_specs=pl.BlockSpec((blk, d), lambda i, be, nu: (i, 0)),
        scratch_shapes=[pltpu.VMEM((d, de), BF16), pltpu.VMEM((d, de), BF16), pltpu.VMEM((de, d), BF16)],
    )
    return pl.pallas_call(
        _expert_body,
        grid_spec=grid_spec,
        out_shape=jax.ShapeDtypeStruct((n_slots, d), F32),
        compiler_params=_cp(("arbitrary",), VMEM_LIMIT),
        name="moe_experts",
    )(blk_expert, n_used, buf, w1, w3, w2)


def _combine_body(final, dest_ref, dnext_ref, x_ref, w_ref, mod_ref, gf_ref, y_hbm, o_ref, y_sc, sem):
    i = pl.program_id(0)
    n = pl.num_programs(0)
    tm = x_ref.shape[0]

    def gather(d_ref, slot):
        def issue(t, carry):
            for k in range(TOP_K):
                _row_copy(y_hbm, d_ref[0, k, t], y_sc.at[slot, k], t, sem.at[slot]).start(priority=k)
            return carry

        lax.fori_loop(0, tm, issue, 0, unroll=8)

    def step(slot):
        if slot == 0:
            @pl.when(i == 0)
            def _():
                gather(dest_ref, 0)

        @pl.when(i + 1 < n)
        def _():
            gather(dnext_ref, 1 - slot)

        for k in range(TOP_K):
            pltpu.make_async_copy(y_hbm.at[pl.ds(0, tm), :], y_sc.at[slot, k], sem.at[slot]).wait()
        w = w_ref[...]
        f = w[:, 0:1] * y_sc[slot, 0] + w[:, 1:2] * y_sc[slot, 1]
        x_new = x_ref[...] + mod_ref[0][5:6] * f
        if final:
            ms = jnp.mean(x_new * x_new, axis=-1, keepdims=True)
            x_new = x_new * lax.rsqrt(ms + EPS) * gf_ref[...]
        o_ref[...] = x_new

    for slot in range(2):
        pl.when(i % 2 == slot)(functools.partial(step, slot))


def moe_combine(dest3, x_rows, wts, modtab, g_final, y_slots, n_lat_rows, n_batch, final):
    n_rows, d = x_rows.shape
    tm = ROW_TILE
    per_b = n_lat_rows // n_batch // tm
    return pl.pallas_call(
        functools.partial(_combine_body, final),
        grid=(n_rows // tm,),
        in_specs=[pl.BlockSpec((1, TOP_K, tm), lambda i: (i, 0, 0), memory_space=pltpu.SMEM),
                  pl.BlockSpec((1, TOP_K, tm), lambda i: (jnp.minimum(i + 1, n_rows // tm - 1), 0, 0),
                               memory_space=pltpu.SMEM),
                  pl.BlockSpec((tm, d), lambda i: (i, 0)),
                  pl.BlockSpec((tm, 8), lambda i: (i, 0)),
                  pl.BlockSpec((1, N_MOD, d), lambda i: (jnp.minimum(i // per_b, n_batch), 0, 0)),
                  pl.BlockSpec((1, d), lambda i: (0, 0)),
                  pl.BlockSpec(memory_space=pl.ANY)],
        out_specs=pl.BlockSpec((tm, d), lambda i: (i, 0)),
        out_shape=jax.ShapeDtypeStruct((n_rows, d), F32),
        scratch_shapes=[pltpu.VMEM((2, TOP_K, tm, d), F32), pltpu.SemaphoreType.DMA((2,))],
        compiler_params=_cp(("arbitrary",)),
        name="moe_combine",
    )(dest3, dest3, x_rows, wts, modtab, g_final.reshape(1, d), y_slots)


def moe_layer(x_rows, h2, s_t, b_router, w1, w3, w2, layer, modtab, g_final, n_lat_rows, n_batch, final):
    n_rows, d = h2.shape
    blk = 2 * MOE_BLOCK
    experts, wts, rank, counts = route(s_t, b_router)
    cnt = counts[:, 0]
    padded = (cnt + blk - 1) // blk * blk
    pend = jnp.cumsum(padded)
    pstart = pend - padded
    hit = experts[..., None] == jnp.arange(N_EXPERTS, dtype=I32)
    dest = jnp.sum(jnp.where(hit, pstart.astype(I32), 0), axis=-1) + rank
    n_blocks = -(-(n_rows * TOP_K) // blk) + N_EXPERTS
    blk_start = jnp.arange(n_blocks, dtype=I32) * blk
    blk_expert = jnp.minimum(jnp.sum((pend[None, :] <= blk_start[:, None]).astype(I32), axis=1), N_EXPERTS - 1)
    dest3 = dest.reshape(TOP_K, n_rows // ROW_TILE, ROW_TILE).transpose(1, 0, 2)
    buf = moe_dispatch(h2, dest3, jnp.zeros((n_blocks * blk, d), F32))
    n_used = (pend[-1:] // blk).astype(I32)
    y_slots = moe_experts(buf, blk_expert, n_used, w1, w3, w2, layer, blk)
    return moe_combine(dest3, x_rows, wts, modtab, g_final, y_slots, n_lat_rows, n_batch, final)


_COL = dict(na_q=0, na_k=512, na_v=1024, ml_q=1536, ml_k=2048, ml_v=2560, ml_o=3072,
            gq_q=3584, gq_k=4096, gq_v=4224)


def _split_w_in(w_in):
    sizes = (BRANCH_WIDTH,) * 8 + (4 * ML_HEADS, BRANCH_WIDTH, GQ_KV_HEADS * GQ_HEAD_DIM, GQ_KV_HEADS * GQ_HEAD_DIM)
    idx = np.cumsum(sizes)[:-1].tolist()
    (s5_u, na_q, na_k, na_v, ml_q, ml_k, ml_v, ml_o, ml_gt, gq_q, gq_k, gq_v) = jnp.split(w_in, idx, axis=-1)
    wa = jnp.pad(ml_gt, ((0, 0), (0, LANES - 4 * ML_HEADS)))
    wb = jnp.concatenate([na_q * (NA_HEAD_DIM ** -0.5 * LOG2E), na_k, na_v, ml_q, ml_k * (ML_HEAD_DIM ** -0.5),
                          ml_v, ml_o, gq_q, gq_k, gq_v], axis=1)
    return s5_u.T.astype(BF16), wa.astype(BF16), wb.astype(BF16)


def kernel(x, c, ctx, c_ctx, w_mod, b_mod, g_norm1, g_norm2, w_in, s5_lam_re, s5_lam_im, s5_log_dt, s5_b_re,
           s5_b_im, s5_c_re, s5_c_im, s5_d, s5_w_glu, s5_b_glu, na_rpb, ml_b_gates, ml_norm, gq_qnorm, gq_knorm,
           w_branch, w_gate, b_gate, w_out, w_router, b_router, moe_w1, moe_w3, moe_w2, g_final):
    b, n_lat, dm = x.shape
    n_ctx = ctx.shape[1]
    depth = w_in.shape[0]
    bn, bc = b * n_lat, b * n_ctx
    x_all = jnp.concatenate([x.reshape(bn, dm), ctx.reshape(bc, dm)], axis=0).astype(F32)
    c_all = jnp.concatenate([c.astype(F32), c_ctx.astype(F32)[None], jnp.zeros((8 - b - 1, dm), F32)], axis=0)
    cs_tab = rope_tables(n_lat)
    wr_t = jnp.pad(w_router.astype(BF16).T, ((0, LANES - N_EXPERTS), (0, 0)))
    out = None
    for l in range(depth):
        last = l == depth - 1
        with_ctx = not last
        modtab = mod_vectors(c_all, w_mod, b_mod, l)[:b + 1].reshape(b + 1, N_MOD, dm)
        w_u_t, wa, wb = _split_w_in(w_in[l])
        pa, pb, gate, h_all = in_projection(x_all, g_norm1[l], modtab, wa, wb, w_gate[l].astype(BF16), b_gate[l],
                                            bn, b)

        tables = s5_tables(s5_lam_re[l], s5_lam_im[l], s5_log_dt[l], s5_b_re[l], s5_b_im[l],
                           s5_c_re[l], s5_c_im[l], s5_d[l])
        ya = s5_mixer(h_all, w_u_t, tables, s5_w_glu[l].astype(BF16), s5_b_glu[l].astype(F32).reshape(1, -1),
                      n_lat, n_ctx, b)

        bias_tab = na_bias_tables(na_rpb[l], n_lat // GRID_W, n_ctx)
        n_rows = bn + bc if with_ctx else bn
        na_cols = (_COL['na_q'], _COL['na_k'], _COL['na_v'])
        yb = (na_mixer(pb, *na_cols, bias_tab, n_lat, n_ctx, b),
              na_ctx_attention(pb, *na_cols, n_lat, n_ctx, b) if with_ctx else None)

        ml_bias = jnp.pad(ml_b_gates[l].astype(F32), (0, LANES - 4 * ML_HEADS)).reshape(1, LANES)
        hf, hb = mlstm_mixer(pb, pa, _COL['ml_q'], _COL['ml_k'], _COL['ml_v'], 0, ml_bias, n_lat, n_ctx, b)

        yd = gq_mixer(pb, _COL['gq_q'], _COL['gq_k'], _COL['gq_v'], cs_tab, gq_qnorm[l], gq_knorm[l],
                      n_lat, n_ctx, b, with_ctx)
        x_mid, h2, s_t = merge_layer(
            ya, yb, yd, hf, hb, pb, _COL['ml_o'], gate, x_all, modtab, ml_norm[l].astype(F32).reshape(1, -1),
            w_branch[l].astype(BF16), w_out[l].astype(BF16), g_norm2[l].astype(F32).reshape(1, -1), wr_t,
            n_rows, bn, b)
        x_next = moe_layer(x_mid, h2, s_t, b_router, moe_w1, moe_w3, moe_w2, l, modtab, g_final, bn, b, last)
        if last:
            out = x_next.reshape(b, n_lat, dm).astype(x.dtype)
        else:
            x_all = x_next
    return out
```
